```python
import math
import jax, jax.numpy as jnp
from jax import lax
import numpy as np

D_MODEL = 1024
BATCH = 16
SEQ = 4096
DEPTH = 4
DEC_BATCH = 32
DEC_SEQ = 16
PAST_LEN = 2048

CHUNK = 64
N_BRANCH = 4
BRANCH_W = 512
D_FF = 2816
P_DIM = 256
EPS = 1e-6
NEG_INF = -1e30
A_HEADS = 8
A_KV_HEADS = 2
A_HD = 64
A_GROUP = A_HEADS // A_KV_HEADS
WINDOW = 128
WIN_CHUNKS = WINDOW // CHUNK
N_BUCKETS = 32
MAX_DIST = 128
B_HEADS = 8
B_HD = 64
B_W_RANK = 64
B_A_RANK = 64
B_G_RANK = 128
RWKV_GN_EPS = 64e-5
C_HEADS = 4
C_DK = 64
C_DV = 128
C_G_RANK = 16
C_TAU = 16.0
D_BLOCKS = 8
D_BW = BRANCH_W // D_BLOCKS
CONV_W = 4
LRU_C = 8.0

A_COLS = A_HEADS * A_HD + 2 * A_KV_HEADS * A_HD
B_COLS = 3 * BRANCH_W + B_W_RANK + B_A_RANK + B_G_RANK
C_COLS = 2 * C_HEADS * C_DK + C_HEADS * C_DV + C_G_RANK + BRANCH_W
D_COLS = 2 * BRANCH_W
IN_COLS = A_COLS + B_COLS + C_COLS + D_COLS
IN_SPLITS = (A_COLS, A_COLS + B_COLS, A_COLS + B_COLS + C_COLS)
A_SPLITS = (A_HEADS * A_HD, A_HEADS * A_HD + A_KV_HEADS * A_HD)
B_SPLITS = (BRANCH_W, 2 * BRANCH_W, 3 * BRANCH_W, 3 * BRANCH_W + B_W_RANK, 3 * BRANCH_W + B_W_RANK + B_A_RANK)
C_SPLITS = (C_HEADS * C_DK, 2 * C_HEADS * C_DK, 2 * C_HEADS * C_DK + C_HEADS * C_DV,
            2 * C_HEADS * C_DK + C_HEADS * C_DV + C_G_RANK)

kernel_name = 'hybrid_streaming_encoder_step'


def rmsnorm(x, g):
    xf = x.astype(jnp.float32)
    y = xf * lax.rsqrt(jnp.mean(xf * xf, axis=-1, keepdims=True) + EPS)
    return (y * g.astype(jnp.float32)).astype(x.dtype)


def swiglu(h, w_gate, w_up, w_down):
    return (jax.nn.silu(h @ w_gate) * (h @ w_up)) @ w_down


def t5_bucket(rel):
    half = N_BUCKETS // 2
    max_exact = half // 2
    ret = jnp.where(rel > 0, half, 0)
    n = jnp.abs(rel)
    nf = jnp.maximum(n, 1).astype(jnp.float32)
    large = max_exact + (jnp.log(nf / max_exact) / math.log(MAX_DIST / max_exact)
                         * (half - max_exact)).astype(jnp.int32)
    large = jnp.minimum(large, half - 1)
    return ret + jnp.where(n < max_exact, n, large)


def rel_bias(table, n_q, n_k):
    rel = jnp.arange(n_k)[None, :] - WINDOW - jnp.arange(n_q)[:, None]
    b = table[t5_bucket(rel)].astype(jnp.float32)
    return jnp.transpose(b, (2, 0, 1)).reshape(A_KV_HEADS, A_GROUP, n_q, n_k)


def sink_softmax(s, sink, mask=None):
    sk = sink.astype(jnp.float32).reshape(A_KV_HEADS, A_GROUP, 1, 1)
    if mask is not None:
        s = jnp.where(mask, s, NEG_INF)
    m = jnp.maximum(jnp.max(s, axis=-1, keepdims=True), sk)
    e = jnp.exp(s - m)
    return e / (jnp.sum(e, axis=-1, keepdims=True) + jnp.exp(sk - m))


def window_attn_prompt(q, k, v, sink, table):
    Bn, T = q.shape[:2]
    nC = T // CHUNK
    band = (WIN_CHUNKS + 1) * CHUNK
    qc = q.reshape(Bn, nC, CHUNK, A_KV_HEADS, A_GROUP, A_HD)
    pad = ((0, 0), (WINDOW, 0), (0, 0), (0, 0))
    kc = jnp.pad(k, pad).reshape(Bn, nC + WIN_CHUNKS, CHUNK, A_KV_HEADS, A_HD)
    vc = jnp.pad(v, pad).reshape(Bn, nC + WIN_CHUNKS, CHUNK, A_KV_HEADS, A_HD)
    kb = jnp.concatenate([kc[:, j:j + nC] for j in range(WIN_CHUNKS + 1)], axis=2)
    vb = jnp.concatenate([vc[:, j:j + nC] for j in range(WIN_CHUNKS + 1)], axis=2)
    s = jnp.einsum('bcqgrd,bckgd->bcgrqk', qc, kb).astype(jnp.float32) * (A_HD ** -0.5)
    s = s + rel_bias(table, CHUNK, band)
    kpos = jnp.arange(nC)[:, None] * CHUNK + jnp.arange(band)[None, :] - WINDOW
    mask = (kpos >= 0)[None, :, None, None, None, :]
    p = sink_softmax(s, sink, mask).astype(v.dtype)
    o = jnp.einsum('bcgrqk,bckgd->bcqgrd', p, vb)
    return o.reshape(Bn, T, A_HEADS * A_HD)


def window_attn_sample(q, k, v, cache_k, cache_v, sink, table):
    Bn, S = q.shape[:2]
    kk = jnp.concatenate([cache_k.astype(k.dtype), k], axis=1)
    vv = jnp.concatenate([cache_v.astype(v.dtype), v], axis=1)
    qg = q.reshape(Bn, S, A_KV_HEADS, A_GROUP, A_HD)
    s = jnp.einsum('bqgrd,bkgd->bgrqk', qg, kk).astype(jnp.float32) * (A_HD ** -0.5)
    s = s + rel_bias(table, S, WINDOW + S)
    p = sink_softmax(s, sink).astype(vv.dtype)
    o = jnp.einsum('bgrqk,bkgd->bqgrd', p, vv)
    return o.reshape(Bn, S, A_HEADS * A_HD), kk[:, -WINDOW:], vv[:, -WINDOW:]


def rwkv7_mix(P, shift0, S0, W, i):
    Bn, T = P.shape[:2]
    f32 = jnp.float32
    Pf = P.astype(f32)
    prev = jnp.concatenate([shift0.astype(f32)[:, None], Pf[:, :-1]], axis=1)
    xs = Pf + (prev - Pf) * W['rwkv_mu'][i]
    r, k, v, wl, al, gl = jnp.split(xs, B_SPLITS, axis=-1)
    w_log = -jax.nn.softplus(-(W['rwkv_w0'][i] + jnp.tanh(wl) @ W['rwkv_w2'][i])) - 0.5
    decay = jnp.exp(-jnp.exp(w_log))
    a = jax.nn.sigmoid(W['rwkv_a0'][i] + al @ W['rwkv_a2'][i])
    g = jax.nn.sigmoid(gl) @ W['rwkv_g2'][i]
    heads = lambda t: t.reshape(Bn, T, B_HEADS, B_HD)
    kk = heads(k * W['rwkv_k_k'][i])
    kk = kk / jnp.maximum(jnp.sqrt(jnp.sum(kk * kk, axis=-1, keepdims=True)), 1e-12)
    k = k * (1.0 + (a - 1.0) * W['rwkv_k_a'][i])
    r, k, v, decay, a = heads(r), heads(k), heads(v), heads(decay), heads(a)

    def step(S, inp):
        r_t, w_t, k_t, v_t, kk_t, a_t = inp
        sa = jnp.einsum('bhij,bhj->bhi', S, -kk_t)
        S = (S * w_t[:, :, None, :] + sa[..., None] * (kk_t * a_t)[:, :, None, :]
             + v_t[..., None] * k_t[:, :, None, :])
        return S, jnp.einsum('bhij,bhj->bhi', S, r_t)

    seq = tuple(jnp.moveaxis(t, 1, 0) for t in (r, decay, k, v, kk, a))
    S_T, y = lax.scan(step, S0.astype(f32), seq)
    y = jnp.moveaxis(y, 0, 1)
    mu = jnp.mean(y, axis=-1, keepdims=True)
    var = jnp.mean(jnp.square(y - mu), axis=-1, keepdims=True)
    yn = ((y - mu) * lax.rsqrt(var + RWKV_GN_EPS)).reshape(Bn, T, BRANCH_W)
    yn = yn * W['rwkv_ln_g'][i] + W['rwkv_ln_b'][i]
    bonus = jnp.sum(r * k * W['rwkv_r_k'][i], axis=-1, keepdims=True) * v
    out = (yn + bonus.reshape(Bn, T, BRANCH_W)) * g
    return out.astype(P.dtype), P[:, -1], S_T.astype(S0.dtype)


def gla_block(S, blk):
    q, k, v, g = blk
    L = q.shape[1]
    b = jnp.cumsum(g, axis=1)
    qe = q * jnp.exp(b)
    ke = k * jnp.exp(-b)
    A = jnp.einsum('blhd,bmhd->bhlm', qe, ke)
    A = jnp.where(jnp.tril(jnp.ones((L, L), dtype=bool)), A, 0.0)
    o = jnp.einsum('blhd,bhde->blhe', qe, S) + jnp.einsum('bhlm,bmhe->blhe', A, v)
    bL = b[:, -1]
    kd = k * jnp.exp(bL[:, None] - b)
    S = S * jnp.exp(bL)[..., None] + jnp.einsum('blhd,blhe->bhde', kd, v)
    return S, o


def gla_mix(P, S0, W, i):
    Bn, T = P.shape[:2]
    Pf = P.astype(jnp.float32)
    q, k, v, gl, og = jnp.split(Pf, C_SPLITS, axis=-1)
    gk = jax.nn.log_sigmoid(gl @ W['gla_g2'][i] + W['gla_gb'][i]) / C_TAU
    q = q.reshape(Bn, T, C_HEADS, C_DK) * (C_DK ** -0.5)
    k = k.reshape(Bn, T, C_HEADS, C_DK)
    gk = gk.reshape(Bn, T, C_HEADS, C_DK)
    v = v.reshape(Bn, T, C_HEADS, C_DV)
    L = min(T, CHUNK)
    nb = T // L
    blk = lambda t: jnp.moveaxis(t.reshape(Bn, nb, L, *t.shape[2:]), 1, 0)
    S_T, o = lax.scan(gla_block, S0.astype(jnp.float32), (blk(q), blk(k), blk(v), blk(gk)))
    o = jnp.moveaxis(o, 0, 1).reshape(Bn, T, C_HEADS, C_DV)
    o = rmsnorm(o, W['gla_norm'][i]).reshape(Bn, T, BRANCH_W) * jax.nn.silu(og)
    return o.astype(P.dtype), S_T.astype(S0.dtype)


def linear_scan(a, u, h0):
    u = u.at[:, 0].add(a[:, 0] * h0)

    def combine(left, right):
        a_l, u_l = left
        a_r, u_r = right
        return a_l * a_r, a_r * u_l + u_r

    _, h = lax.associative_scan(combine, (a, u), axis=1)
    return h


def rglru_mix(P, conv0, h0, W, i):
    Bn, T = P.shape[:2]
    Pf = P.astype(jnp.float32)
    gate, xr = jnp.split(Pf, [BRANCH_W], axis=-1)
    xp = jnp.concatenate([conv0.astype(jnp.float32), xr], axis=1)
    wc = W['lru_conv_w'][i]
    xc = W['lru_conv_b'][i] + xp[:, 0:T] * wc[0]
    for j in range(1, CONV_W):
        xc = xc + xp[:, j:j + T] * wc[j]
    xb = xc.reshape(Bn, T, D_BLOCKS, D_BW)
    r = jax.nn.sigmoid(jnp.einsum('bthi,hij->bthj', xb, W['lru_wa'][i]) + W['lru_ba'][i])
    gi = jax.nn.sigmoid(jnp.einsum('bthi,hij->bthj', xb, W['lru_wx'][i]) + W['lru_bx'][i])
    log_a = -LRU_C * r * jax.nn.softplus(-W['lru_lambda'][i])
    a = jnp.exp(log_a)
    u = jnp.sqrt(-jnp.expm1(2.0 * log_a)) * (gi * xb)
    h = linear_scan(a, u, h0.astype(jnp.float32).reshape(Bn, D_BLOCKS, D_BW))
    y = h.reshape(Bn, T, BRANCH_W) * jax.nn.gelu(gate)
    new_conv = xp[:, -(CONV_W - 1):].astype(conv0.dtype)
    return y.astype(P.dtype), new_conv, h[:, -1].reshape(Bn, BRANCH_W).astype(h0.dtype)


def token_mixing(h, st, W, i, table, first_chunk):
    ck, cv, shift0, s_rwkv0, s_gla0, conv0, lru0 = st
    Bn, T = h.shape[:2]
    P = h @ W['w_in'][i]
    pa, pb, pc, pd = jnp.split(P, IN_SPLITS, axis=-1)
    q, k, v = jnp.split(pa, A_SPLITS, axis=-1)
    q = rmsnorm(q.reshape(Bn, T, A_HEADS, A_HD), W['q_norm'][i])
    k = rmsnorm(k.reshape(Bn, T, A_KV_HEADS, A_HD), W['k_norm'][i])
    v = v.reshape(Bn, T, A_KV_HEADS, A_HD)
    sink = W['attn_sink'][i]
    if first_chunk:
        o_a = window_attn_prompt(q, k, v, sink, table)
        k_win, v_win = k[:, -WINDOW:], v[:, -WINDOW:]
    else:
        o_a, k_win, v_win = window_attn_sample(q, k, v, ck, cv, sink, table)
    o_b, shift1, s_rwkv1 = rwkv7_mix(pb, shift0, s_rwkv0, W, i)
    o_c, s_gla1 = gla_mix(pc, s_gla0, W, i)
    o_d, conv1, lru1 = rglru_mix(pd, conv0, lru0, W, i)
    y = None
    for n, o in enumerate((o_a, o_b, o_c, o_d)):
        t = jax.nn.sigmoid(h @ W['w_merge_gate'][i, n]) * (o @ W['w_branch'][i, n])
        y = t if y is None else y + t
    return y @ W['w_out'][i], (k_win, v_win, shift1, s_rwkv1, s_gla1, conv1, lru1)


def trunk_layer(x, pe, st, W, i, table, first_chunk):
    h = rmsnorm(x, W['g_ffn1'][i])
    x = x + 0.5 * swiglu(h, W['w_ffn1_gate'][i], W['w_ffn1_up'][i], W['w_ffn1_down'][i])
    h = rmsnorm(x, W['g_mix'][i])
    mix, new_st = token_mixing(h, st, W, i, table, first_chunk)
    x = x + mix
    h = rmsnorm(x, W['g_ffn2'][i])
    x = x + 0.5 * swiglu(h, W['w_ffn2_gate'][i], W['w_ffn2_up'][i], W['w_ffn2_down'][i])
    h = rmsnorm(x, W['g_ple'][i])
    x = x + jax.nn.sigmoid(h @ W['w_ple_gate'][i]) * (pe @ W['w_ple_proj'][i])
    return x, new_st


def setup_inputs(seed: int = 0) -> dict:
    key = jax.random.key(seed)
    ks = iter(jax.random.split(key, 64))
    f32 = jnp.float32

    def nrm(shape, scale=1.0):
        return jax.random.normal(next(ks), shape, f32) * scale

    def gain(shape):
        return 1.0 + nrm(shape, 0.02)

    u = jax.random.uniform(next(ks), (DEPTH, D_BLOCKS, D_BW), f32, 0.9, 0.999)
    s = u ** (1.0 / LRU_C)
    lam = jnp.log(s) - jnp.log1p(-s)
    w0 = jax.random.uniform(next(ks), (DEPTH, BRANCH_W), f32, -6.5, -1.5)
    mu = jax.random.uniform(next(ks), (DEPTH, B_COLS), f32)
    return {
        'x_prompt': nrm((BATCH, SEQ, D_MODEL)),
        'x_sample': nrm((DEC_BATCH, DEC_SEQ, D_MODEL)),
        'cache_attn_k': nrm((DEPTH, DEC_BATCH, WINDOW, A_KV_HEADS, A_HD)),
        'cache_attn_v': nrm((DEPTH, DEC_BATCH, WINDOW, A_KV_HEADS, A_HD)),
        'state_rwkv_shift': nrm((DEPTH, DEC_BATCH, B_COLS)),
        'state_rwkv': nrm((DEPTH, DEC_BATCH, B_HEADS, B_HD, B_HD), 0.5),
        'state_gla': nrm((DEPTH, DEC_BATCH, C_HEADS, C_DK, C_DV), 0.5),
        'state_lru_conv': nrm((DEPTH, DEC_BATCH, CONV_W - 1, BRANCH_W)),
        'state_lru': nrm((DEPTH, DEC_BATCH, BRANCH_W), 0.5),
        'p_prompt': nrm((DEPTH, BATCH, SEQ, P_DIM)),
        'p_sample': nrm((DEPTH, DEC_BATCH, DEC_SEQ, P_DIM)),
        'rel_bias_table': nrm((N_BUCKETS, A_HEADS), 0.2),
        'g_ffn1': gain((DEPTH, D_MODEL)),
        'w_ffn1_gate': nrm((DEPTH, D_MODEL, D_FF), D_MODEL ** -0.5),
        'w_ffn1_up': nrm((DEPTH, D_MODEL, D_FF), D_MODEL ** -0.5),
        'w_ffn1_down': nrm((DEPTH, D_FF, D_MODEL), D_FF ** -0.5),
        'g_mix': gain((DEPTH, D_MODEL)),
        'w_in': nrm((DEPTH, D_MODEL, IN_COLS), D_MODEL ** -0.5),
        'q_norm': gain((DEPTH, A_HD)),
        'k_norm': gain((DEPTH, A_HD)),
        'attn_sink': nrm((DEPTH, A_HEADS)),
        'rwkv_mu': mu,
        'rwkv_w0': w0,
        'rwkv_w2': nrm((DEPTH, B_W_RANK, BRANCH_W), B_W_RANK ** -0.5),
        'rwkv_a0': nrm((DEPTH, BRANCH_W), 0.1),
        'rwkv_a2': nrm((DEPTH, B_A_RANK, BRANCH_W), B_A_RANK ** -0.5),
        'rwkv_g2': nrm((DEPTH, B_G_RANK, BRANCH_W), B_G_RANK ** -0.5),
        'rwkv_k_k': 0.85 + nrm((DEPTH, BRANCH_W), 0.05),
        'rwkv_k_a': 1.0 + nrm((DEPTH, BRANCH_W), 0.05),
        'rwkv_r_k': nrm((DEPTH, B_HEADS, B_HD), 0.1),
        'rwkv_ln_g': gain((DEPTH, BRANCH_W)),
        'rwkv_ln_b': nrm((DEPTH, BRANCH_W), 0.02),
        'gla_g2': nrm((DEPTH, C_G_RANK, C_HEADS * C_DK), C_G_RANK ** -0.5),
        'gla_gb': nrm((DEPTH, C_HEADS * C_DK), 0.1),
        'gla_norm': gain((DEPTH, C_DV)),
        'lru_conv_w': nrm((DEPTH, CONV_W, BRANCH_W), CONV_W ** -0.5),
        'lru_conv_b': nrm((DEPTH, BRANCH_W), 0.02),
        'lru_wa': nrm((DEPTH, D_BLOCKS, D_BW, D_BW), D_BW ** -0.5),
        'lru_ba': nrm((DEPTH, D_BLOCKS, D_BW), 0.02),
        'lru_wx': nrm((DEPTH, D_BLOCKS, D_BW, D_BW), D_BW ** -0.5),
        'lru_bx': nrm((DEPTH, D_BLOCKS, D_BW), 0.02),
        'lru_lambda': lam,
        'w_merge_gate': nrm((DEPTH, N_BRANCH, D_MODEL, D_MODEL), D_MODEL ** -0.5),
        'w_branch': nrm((DEPTH, N_BRANCH, BRANCH_W, D_MODEL), BRANCH_W ** -0.5),
        'w_out': nrm((DEPTH, D_MODEL, D_MODEL), D_MODEL ** -0.5),
        'g_ffn2': gain((DEPTH, D_MODEL)),
        'w_ffn2_gate': nrm((DEPTH, D_MODEL, D_FF), D_MODEL ** -0.5),
        'w_ffn2_up': nrm((DEPTH, D_MODEL, D_FF), D_MODEL ** -0.5),
        'w_ffn2_down': nrm((DEPTH, D_FF, D_MODEL), D_FF ** -0.5),
        'g_ple': gain((DEPTH, D_MODEL)),
        'w_ple_gate': nrm((DEPTH, D_MODEL, D_MODEL), D_MODEL ** -0.5),
        'w_ple_proj': nrm((DEPTH, P_DIM, D_MODEL), P_DIM ** -0.5),
    }


def reference(x_prompt, x_sample, cache_attn_k, cache_attn_v, state_rwkv_shift, state_rwkv,
              state_gla, state_lru_conv, state_lru, p_prompt, p_sample, rel_bias_table,
              g_ffn1, w_ffn1_gate, w_ffn1_up, w_ffn1_down, g_mix, w_in, q_norm, k_norm, attn_sink,
              rwkv_mu, rwkv_w0, rwkv_w2, rwkv_a0, rwkv_a2, rwkv_g2, rwkv_k_k, rwkv_k_a, rwkv_r_k,
              rwkv_ln_g, rwkv_ln_b, gla_g2, gla_gb, gla_norm, lru_conv_w, lru_conv_b, lru_wa, lru_ba,
              lru_wx, lru_bx, lru_lambda, w_merge_gate, w_branch, w_out, g_ffn2, w_ffn2_gate,
              w_ffn2_up, w_ffn2_down, g_ple, w_ple_gate, w_ple_proj):
    W = dict(g_ffn1=g_ffn1, w_ffn1_gate=w_ffn1_gate, w_ffn1_up=w_ffn1_up, w_ffn1_down=w_ffn1_down,
             g_mix=g_mix, w_in=w_in, q_norm=q_norm, k_norm=k_norm, attn_sink=attn_sink,
             rwkv_mu=rwkv_mu, rwkv_w0=rwkv_w0, rwkv_w2=rwkv_w2, rwkv_a0=rwkv_a0, rwkv_a2=rwkv_a2,
             rwkv_g2=rwkv_g2, rwkv_k_k=rwkv_k_k, rwkv_k_a=rwkv_k_a, rwkv_r_k=rwkv_r_k,
             rwkv_ln_g=rwkv_ln_g, rwkv_ln_b=rwkv_ln_b, gla_g2=gla_g2, gla_gb=gla_gb,
             gla_norm=gla_norm, lru_conv_w=lru_conv_w, lru_conv_b=lru_conv_b, lru_wa=lru_wa,
             lru_ba=lru_ba, lru_wx=lru_wx, lru_bx=lru_bx, lru_lambda=lru_lambda,
             w_merge_gate=w_merge_gate, w_branch=w_branch, w_out=w_out, g_ffn2=g_ffn2,
             w_ffn2_gate=w_ffn2_gate, w_ffn2_up=w_ffn2_up, w_ffn2_down=w_ffn2_down,
             g_ple=g_ple, w_ple_gate=w_ple_gate, w_ple_proj=w_ple_proj)
    dt = x_prompt.dtype
    Bp = x_prompt.shape[0]
    yp, ys = x_prompt, x_sample
    st_p, st_s = [], []
    for i in range(DEPTH):
        zero_st = (None, None,
                   jnp.zeros((Bp, B_COLS), dt),
                   jnp.zeros((Bp, B_HEADS, B_HD, B_HD), dt),
                   jnp.zeros((Bp, C_HEADS, C_DK, C_DV), dt),
                   jnp.zeros((Bp, CONV_W - 1, BRANCH_W), dt),
                   jnp.zeros((Bp, BRANCH_W), dt))
        yp, sp = trunk_layer(yp, p_prompt[i], zero_st, W, i, rel_bias_table, True)
        cache_st = (cache_attn_k[i], cache_attn_v[i], state_rwkv_shift[i], state_rwkv[i],
                    state_gla[i], state_lru_conv[i], state_lru[i])
        ys, ss = trunk_layer(ys, p_sample[i], cache_st, W, i, rel_bias_table, False)
        st_p.append(sp)
        st_s.append(ss)

    def stack(states, j):
        return jnp.stack([s[j] for s in states])

    new_attn_k_p, new_attn_v_p, new_rwkv_shift_p, new_rwkv_p, new_gla_p, new_lru_conv_p, new_lru_p = (
        stack(st_p, j) for j in range(7))
    new_attn_k_s, new_attn_v_s, new_rwkv_shift_s, new_rwkv_s, new_gla_s, new_lru_conv_s, new_lru_s = (
        stack(st_s, j) for j in range(7))
    return (yp, ys,
            new_attn_k_p, new_attn_v_p, new_rwkv_shift_p, new_rwkv_p, new_gla_p, new_lru_conv_p, new_lru_p,
            new_attn_k_s, new_attn_v_s, new_rwkv_shift_s, new_rwkv_s, new_gla_s, new_lru_conv_s, new_lru_s)
```

```python
import functools
import math

import numpy as np
import jax
import jax.numpy as jnp
from jax import lax
from jax.experimental import pallas as pl
from jax.experimental.pallas import tpu as pltpu

F32 = jnp.float32
BF16 = jnp.bfloat16

V7X_VMEM_BYTES = 64 * 1024 * 1024
VMEM_LIMIT = V7X_VMEM_BYTES - 8 * 1024 * 1024
SUBLANES = 8

EPS = 1e-6
NEG_INF = -1e30
CHUNK = 64
WINDOW = 128
N_BUCKETS = 32
MAX_DIST = 128
A_HEADS, A_KV_HEADS, A_HD = 8, 2, 64
A_GROUP = A_HEADS // A_KV_HEADS
B_HEADS, B_HD = 8, 64
B_W_RANK, B_A_RANK, B_G_RANK = 64, 64, 128
RWKV_GN_EPS = 64e-5
C_HEADS, C_DK, C_DV = 4, 64, 128
C_G_RANK = 16
C_TAU = 16.0
D_BLOCKS = 8
CONV_W = 4
LRU_C = 8.0
BRANCH_W = 512
HG = 4
HGW = HG * B_HD


def _cparams(*sem):
    return pltpu.CompilerParams(dimension_semantics=sem, vmem_limit_bytes=VMEM_LIMIT)


def _resident(shape):
    nd = len(shape)
    return pl.BlockSpec(shape, lambda *_: (0,) * nd, pipeline_mode=pl.Buffered(1))


def _row_tile(n, cap):
    t = min(n, cap)
    while n % t:
        t //= 2
    return t


def _dot(a, b):
    return jnp.dot(a.astype(BF16), b.astype(BF16), preferred_element_type=F32)


def _dot_nt(a, b):
    return lax.dot_general(a.astype(BF16), b.astype(BF16), (((1,), (1,)), ((), ())),
                           preferred_element_type=F32)


def _dot_tn(a, b):
    return lax.dot_general(a.astype(BF16), b.astype(BF16), (((0,), (0,)), ((), ())),
                           preferred_element_type=F32)


def _rms(x, g):
    return x * lax.rsqrt(jnp.mean(x * x, axis=-1, keepdims=True) + EPS) * g


def _sigmoid(x):
    return 1.0 / (1.0 + jnp.exp(-x))


def _softplus(x):
    return jnp.maximum(x, 0.0) + jnp.log1p(jnp.exp(-jnp.abs(x)))


def _split_dot(e_lhs, x, terms):
    acc = None
    rem = x
    for n in range(terms):
        piece = rem.astype(BF16)
        d = jnp.dot(e_lhs, piece, preferred_element_type=F32)
        acc = d if acc is None else acc + d
        if n + 1 < terms:
            rem = rem - piece.astype(F32)
    return acc


def _segsum(x, e):
    hi = x.astype(BF16)
    lo = (x - hi.astype(F32)).astype(BF16)
    return (jnp.dot(hi, e, preferred_element_type=F32)
            + jnp.dot(lo, e, preferred_element_type=F32))


def _tril_ones(n):
    r = lax.broadcasted_iota(jnp.int32, (n, n), 0)
    c = lax.broadcasted_iota(jnp.int32, (n, n), 1)
    return jnp.where(r >= c, 1.0, 0.0).astype(BF16)


def _bd_rows(x, blk, nblk):
    lane_blk = lax.broadcasted_iota(jnp.int32, x.shape, 1) // blk
    return jnp.concatenate([jnp.where(lane_blk == h, x, 0.0) for h in range(nblk)], axis=0)


def _seg_matrix(width, seg):
    i = np.arange(width)
    return jnp.asarray((i[:, None] // seg) == (i[None, :] // seg), dtype=BF16)


def _ffn_body(x_ref, g_ref, wg_ref, wu_ref, wd_ref, o_ref, *, fchunk):
    x = x_ref[...]
    h = _rms(x, g_ref[...]).astype(BF16)
    acc = None
    for c in range(wg_ref.shape[1] // fchunk):
        sl = pl.ds(c * fchunk, fchunk)
        gt = jnp.dot(h, wg_ref[:, sl], preferred_element_type=F32)
        up = jnp.dot(h, wu_ref[:, sl], preferred_element_type=F32)
        act = (gt * _sigmoid(gt) * up).astype(BF16)
        d = jnp.dot(act, wd_ref[sl, :], preferred_element_type=F32)
        acc = d if acc is None else acc + d
    o_ref[...] = x + 0.5 * acc


def _ffn(x, g, wg, wu, wd):
    n, d = x.shape
    f = wg.shape[1]
    tm = _row_tile(n, 512)
    fchunk = f // 2 if (f // 2) % 128 == 0 else f
    return pl.pallas_call(
        functools.partial(_ffn_body, fchunk=fchunk),
        grid=(n // tm,),
        in_specs=[pl.BlockSpec((tm, d), lambda i: (i, 0)),
                  _resident((1, d)), _resident((d, f)), _resident((d, f)), _resident((f, d))],
        out_specs=pl.BlockSpec((tm, d), lambda i: (i, 0)),
        out_shape=jax.ShapeDtypeStruct((n, d), F32),
        compiler_params=_cparams("parallel"),
        name="ffn",
    )(x, g, wg, wu, wd)


def _inproj_body(x_ref, g_ref, wa_ref, wb_ref, wc_ref, wd_ref, pa_ref, pb_ref, pc_ref, pd_ref):
    h = _rms(x_ref[...], g_ref[...]).astype(BF16)
    for w_ref, p_ref in ((wa_ref, pa_ref), (wb_ref, pb_ref), (wc_ref, pc_ref), (wd_ref, pd_ref)):
        p_ref[...] = jnp.dot(h, w_ref[...], preferred_element_type=F32)


def _inproj(x, g, ws):
    n, d = x.shape
    tm = _row_tile(n, 512)
    widths = [w.shape[1] for w in ws]
    return pl.pallas_call(
        _inproj_body,
        grid=(n // tm,),
        in_specs=[pl.BlockSpec((tm, d), lambda i: (i, 0)), _resident((1, d))]
                 + [_resident((d, w)) for w in widths],
        out_specs=[pl.BlockSpec((tm, w), lambda i: (i, 0)) for w in widths],
        out_shape=[jax.ShapeDtypeStruct((n, w), F32) for w in widths],
        compiler_params=_cparams("parallel"),
        name="inproj",
    )(x, g, *ws)


def _merge_body(x_ref, oa_ref, ob_ref, oc_ref, od_ref, g_ref, wmg_ref, wb_ref, wo_ref, o_ref):
    x = x_ref[...]
    h = _rms(x, g_ref[...]).astype(BF16)
    y = None
    for n, b_ref in enumerate((oa_ref, ob_ref, oc_ref, od_ref)):
        gate = _sigmoid(jnp.dot(h, wmg_ref[n], preferred_element_type=F32))
        t = gate * jnp.dot(b_ref[...].astype(BF16), wb_ref[n], preferred_element_type=F32)
        y = t if y is None else y + t
    o_ref[...] = x + jnp.dot(y.astype(BF16), wo_ref[...], preferred_element_type=F32)


def _merge(x, outs, g, wmg, wb, wo):
    n, d = x.shape
    bw = outs[0].shape[1]
    tm = _row_tile(n, 512)
    return pl.pallas_call(
        _merge_body,
        grid=(n // tm,),
        in_specs=[pl.BlockSpec((tm, d), lambda i: (i, 0))]
                 + [pl.BlockSpec((tm, bw), lambda i: (i, 0))] * 4
                 + [_resident((1, d)), _resident(wmg.shape), _resident(wb.shape), _resident(wo.shape)],
        out_specs=pl.BlockSpec((tm, d), lambda i: (i, 0)),
        out_shape=jax.ShapeDtypeStruct((n, d), F32),
        compiler_params=_cparams("parallel"),
        name="merge",
    )(x, *outs, g, wmg, wb, wo)


def _ple_body(x_ref, pe_ref, g_ref, wg_ref, wp_ref, o_ref):
    x = x_ref[...]
    h = _rms(x, g_ref[...]).astype(BF16)
    gate = _sigmoid(jnp.dot(h, wg_ref[...], preferred_element_type=F32))
    o_ref[...] = x + gate * jnp.dot(pe_ref[...].astype(BF16), wp_ref[...], preferred_element_type=F32)


def _ple(x, pe, g, wg, wp):
    n, d = x.shape
    pd = pe.shape[1]
    tm = _row_tile(n, 512)
    return pl.pallas_call(
        _ple_body,
        grid=(n // tm,),
        in_specs=[pl.BlockSpec((tm, d), lambda i: (i, 0)), pl.BlockSpec((tm, pd), lambda i: (i, 0)),
                  _resident((1, d)), _resident((d, d)), _resident((pd, d))],
        out_specs=pl.BlockSpec((tm, d), lambda i: (i, 0)),
        out_shape=jax.ShapeDtypeStruct((n, d), F32),
        compiler_params=_cparams("parallel"),
        name="ple",
    )(x, pe, g, wg, wp)


def _t5_bucket_np(rel):
    half = N_BUCKETS // 2
    max_exact = half // 2
    ret = np.where(rel > 0, half, 0)
    n = np.abs(rel)
    nf = np.maximum(n, 1).astype(np.float32)
    large = max_exact + (np.log(nf / np.float32(max_exact)) / np.float32(math.log(MAX_DIST / max_exact))
                         * np.float32(half - max_exact)).astype(np.int32)
    large = np.minimum(large, half - 1)
    return (ret + np.where(n < max_exact, n, large)).astype(np.int32)


def _bias_body(idx_ref, table_ref, o_ref):
    idx = idx_ref[...]
    for h in range(A_HEADS):
        acc = jnp.zeros(idx.shape, F32)
        for b in range(N_BUCKETS):
            acc = jnp.where(idx == b, table_ref[b, h], acc)
        o_ref[h] = acc


def _rel_bias(table, n_q, n_k):
    rel = np.arange(n_k)[None, :] - WINDOW - np.arange(n_q)[:, None]
    idx = jnp.asarray(_t5_bucket_np(rel))
    return pl.pallas_call(
        _bias_body,
        in_specs=[pl.BlockSpec(memory_space=pltpu.VMEM), pl.BlockSpec(memory_space=pltpu.SMEM)],
        out_specs=pl.BlockSpec(memory_space=pltpu.VMEM),
        out_shape=jax.ShapeDtypeStruct((A_HEADS, n_q, n_k), F32),
        name="rel_bias",
    )(idx, table)


def _head_rms(x, e, g):
    ms = _segsum(x * x, e) * (1.0 / A_HD)
    return x * lax.rsqrt(ms + EPS) * g


def _attend(q, kb, vb, bias_ref, sink_ref, valid):
    outs = []
    for h in range(A_HEADS):
        g = h // A_GROUP
        s = _dot_nt(q[:, h * A_HD:(h + 1) * A_HD], kb[:, g * A_HD:(g + 1) * A_HD]) * (A_HD ** -0.5)
        s = s + bias_ref[h]
        if valid is not None:
            s = jnp.where(valid, s, NEG_INF)
        sk = sink_ref[h]
        m = jnp.maximum(jnp.max(s, axis=-1, keepdims=True), sk)
        e = jnp.exp(s - m)
        den = jnp.sum(e, axis=-1, keepdims=True) + jnp.exp(sk - m)
        p = e / den
        outs.append(_dot(p, vb[:, g * A_HD:(g + 1) * A_HD]))
    return jnp.concatenate(outs, axis=1)


def _attn_prompt_body(main_ref, prev_ref, bias_ref, sink_ref, qn_ref, kn_ref, eq_ref, ek_ref,
                      o_ref, kout_ref, vout_ref, kf_ref, vf_ref, *, tb):
    i = pl.program_id(1)
    qw = A_HEADS * A_HD
    kw = A_KV_HEADS * A_HD
    band = WINDOW + CHUNK
    ek = ek_ref[...]
    q = _head_rms(main_ref[:, 0:qw], eq_ref[...], qn_ref[...])
    kf_ref[WINDOW:, :] = _head_rms(main_ref[:, qw:qw + kw], ek, kn_ref[...])
    vf_ref[WINDOW:, :] = main_ref[:, qw + kw:qw + 2 * kw]
    kf_ref[0:WINDOW, :] = _head_rms(prev_ref[:, qw:qw + kw], ek, kn_ref[...])
    vf_ref[0:WINDOW, :] = prev_ref[:, qw + kw:qw + 2 * kw]
    kidx = lax.broadcasted_iota(jnp.int32, (CHUNK, band), 1)
    for c in range(tb // CHUNK):
        kb = kf_ref[c * CHUNK:c * CHUNK + band, :]
        vb = vf_ref[c * CHUNK:c * CHUNK + band, :]
        valid = (kidx + (i * tb + c * CHUNK - WINDOW)) >= 0 if c * CHUNK < WINDOW else None
        o_ref[c * CHUNK:(c + 1) * CHUNK, :] = _attend(q[c * CHUNK:(c + 1) * CHUNK], kb, vb,
                                                      bias_ref, sink_ref, valid)

    @pl.when(i == pl.num_programs(1) - 1)
    def _():
        kout_ref[0] = kf_ref[tb:tb + WINDOW, :]
        vout_ref[0] = vf_ref[tb:tb + WINDOW, :]


def _attn_prompt(pa, bias, sink, qn, kn, nb, t):
    tb = _row_tile(t, 512)
    assert tb % WINDOW == 0
    nq = t // tb
    wpb = tb // WINDOW
    qw, kw = A_HEADS * A_HD, A_KV_HEADS * A_HD
    width = pa.shape[1]
    return pl.pallas_call(
        functools.partial(_attn_prompt_body, tb=tb),
        grid=(nb, nq),
        in_specs=[pl.BlockSpec((tb, width), lambda b, i: (b * nq + i, 0)),
                  pl.BlockSpec((WINDOW, width),
                               lambda b, i: (jnp.maximum((b * nq + i) * wpb - 1, 0), 0)),
                  _resident(bias.shape),
                  pl.BlockSpec(memory_space=pltpu.SMEM),
                  _resident((1, qw)), _resident((1, kw)), _resident((qw, qw)), _resident((kw, kw))],
        out_specs=[pl.BlockSpec((tb, qw), lambda b, i: (b * nq + i, 0)),
                   pl.BlockSpec((1, WINDOW, kw), lambda b, i: (b, 0, 0)),
                   pl.BlockSpec((1, WINDOW, kw), lambda b, i: (b, 0, 0))],
        out_shape=[jax.ShapeDtypeStruct((nb * t, qw), F32),
                   jax.ShapeDtypeStruct((nb, WINDOW, kw), F32),
                   jax.ShapeDtypeStruct((nb, WINDOW, kw), F32)],
        scratch_shapes=[pltpu.VMEM((tb + WINDOW, kw), F32), pltpu.VMEM((tb + WINDOW, kw), F32)],
        compiler_params=_cparams("parallel", "arbitrary"),
        name="attn_prompt",
    )(pa, pa, bias, sink, qn, kn, _seg_matrix(qw, A_HD), _seg_matrix(kw, A_HD))


def _attn_sample_body(pa_ref, ck_ref, cv_ref, bias_ref, sink_ref, qn_ref, kn_ref, eq_ref, ek_ref,
                      o_ref, kout_ref, vout_ref, kf_ref, vf_ref, *, s):
    qw = A_HEADS * A_HD
    kw = A_KV_HEADS * A_HD
    q = _head_rms(pa_ref[:, 0:qw], eq_ref[...], qn_ref[...])
    kf_ref[0:WINDOW, :] = ck_ref[0]
    vf_ref[0:WINDOW, :] = cv_ref[0]
    kf_ref[WINDOW:, :] = _head_rms(pa_ref[:, qw:qw + kw], ek_ref[...], kn_ref[...])
    vf_ref[WINDOW:, :] = pa_ref[:, qw + kw:qw + 2 * kw]
    o_ref[...] = _attend(q, kf_ref[...], vf_ref[...], bias_ref, sink_ref, None)
    kout_ref[0] = kf_ref[s:s + WINDOW, :]
    vout_ref[0] = vf_ref[s:s + WINDOW, :]


def _attn_sample(pa, ck, cv, bias, sink, qn, kn, nb, s):
    qw, kw = A_HEADS * A_HD, A_KV_HEADS * A_HD
    width = pa.shape[1]
    return pl.pallas_call(
        functools.partial(_attn_sample_body, s=s),
        grid=(nb,),
        in_specs=[pl.BlockSpec((s, width), lambda b: (b, 0)),
                  pl.BlockSpec((1, WINDOW, kw), lambda b: (b, 0, 0)),
                  pl.BlockSpec((1, WINDOW, kw), lambda b: (b, 0, 0)),
                  _resident(bias.shape),
                  pl.BlockSpec(memory_space=pltpu.SMEM),
                  _resident((1, qw)), _resident((1, kw)), _resident((qw, qw)), _resident((kw, kw))],
        out_specs=[pl.BlockSpec((s, qw), lambda b: (b, 0)),
                   pl.BlockSpec((1, WINDOW, kw), lambda b: (b, 0, 0)),
                   pl.BlockSpec((1, WINDOW, kw), lambda b: (b, 0, 0))],
        out_shape=[jax.ShapeDtypeStruct((nb * s, qw), F32),
                   jax.ShapeDtypeStruct((nb, WINDOW, kw), F32),
                   jax.ShapeDtypeStruct((nb, WINDOW, kw), F32)],
        scratch_shapes=[pltpu.VMEM((WINDOW + s, kw), F32), pltpu.VMEM((WINDOW + s, kw), F32)],
        compiler_params=_cparams("parallel"),
        name="attn_sample",
    )(pa, ck, cv, bias, sink, qn, kn, _seg_matrix(qw, A_HD), _seg_matrix(kw, A_HD))


def _pad_rows(x, rows):
    if x.shape[0] == rows:
        return x
    return jnp.concatenate([x, jnp.zeros((rows - x.shape[0], x.shape[1]), x.dtype)], axis=0)


def _rwkv_chunk(r, lw, k, v, kk, a, s_bd):
    L = r.shape[0]
    c = _split_dot(_tril_ones(L), lw, 3)
    ec = jnp.exp(c)
    enc = jnp.exp(-c)
    c_last = c[L - 1:L, :]
    e_last = jnp.exp(c_last - c)
    beta = kk * a
    at = -kk * jnp.exp(c - lw)
    ar = jnp.concatenate([at, r * ec], axis=0)
    mb = _dot_nt(ar, _bd_rows(beta * enc, B_HD, HG))
    mk = _dot_nt(ar, _bd_rows(k * enc, B_HD, HG))
    row = lax.broadcasted_iota(jnp.int32, (L, HG * L), 0)
    col = lax.broadcasted_iota(jnp.int32, (L, HG * L), 1) % L
    strict = col < row
    incl = col <= row
    m_b = jnp.where(strict, mb[:L], 0.0)
    m_k = jnp.where(strict, mk[:L], 0.0)
    n_b = jnp.where(incl, mb[L:], 0.0)
    n_k = jnp.where(incl, mk[L:], 0.0)
    ars = _dot_nt(ar, s_bd)
    v_bd = _bd_rows(v, B_HD, HG)
    rhs = ars[:L] + _dot(m_k, v_bd)
    t_inv = jnp.where(col == row, 1.0, 0.0) + m_b
    p = m_b
    levels = int(math.log2(L))
    for lvl in range(1, levels):
        p_bd = _bd_rows(p, L, HG)
        if lvl == 1:
            p = _dot(p, p_bd)
        else:
            tp = _dot(jnp.concatenate([t_inv, p], axis=0), p_bd)
            t_inv = t_inv + tp[:L]
            p = tp[L:]
    t_inv = t_inv + _dot(t_inv, _bd_rows(p, L, HG))
    u = _dot(t_inv, _bd_rows(rhs, B_HD, HG))
    y = ars[L:] + _dot(jnp.concatenate([n_b, n_k], axis=1),
                       jnp.concatenate([_bd_rows(u, B_HD, HG), v_bd], axis=0))
    upd = _dot_tn(jnp.concatenate([u, v], axis=0),
                  jnp.concatenate([beta * e_last, k * e_last], axis=0))
    ri = lax.broadcasted_iota(jnp.int32, (HGW, HGW), 0) // B_HD
    ci = lax.broadcasted_iota(jnp.int32, (HGW, HGW), 1) // B_HD
    s_new = s_bd * jnp.exp(c_last) + jnp.where(ri == ci, upd, 0.0)
    return y, s_new


def _rwkv_body(pb_ref, shift_ref, s0_ref, mu_ref, w0_ref, w2_ref, a0_ref, a2_ref, g2_ref, kk_ref,
               ka_ref, rk_ref, lng_ref, lnb_ref, e_ref,
               o_ref, sout_ref,
               ext_ref, s_ref, r_s, lw_s, k_s, v_s, kk_s, a_s, y_s, *, tb, lc):
    t = pl.program_id(1)
    W = BRANCH_W
    n_groups = W // HGW

    @pl.when(t == 0)
    def _():
        ext_ref[0:SUBLANES, :] = jnp.broadcast_to(shift_ref[0], (SUBLANES, ext_ref.shape[1]))
        s_ref[...] = s0_ref[0]

    p = pb_ref[...]
    ext_ref[SUBLANES:, :] = p
    prev = ext_ref[SUBLANES - 1:SUBLANES - 1 + tb, :]
    xs = p + (prev - p) * mu_ref[...]
    ext_ref[0:SUBLANES, :] = p[tb - SUBLANES:, :]

    e = e_ref[...]
    r = xs[:, 0:W]
    k = xs[:, W:2 * W]
    v = xs[:, 2 * W:3 * W]
    wl = xs[:, 3 * W:3 * W + 128]
    al = xs[:, 3 * W + 128:3 * W + 256]
    gl = xs[:, 3 * W + 256:3 * W + 384]
    w_log = -_softplus(-(w0_ref[...] + _dot(jnp.tanh(wl), w2_ref[...]))) - 0.5
    a = _sigmoid(a0_ref[...] + _dot(al, a2_ref[...]))
    g = _dot(_sigmoid(gl), g2_ref[...])
    kkr = k * kk_ref[...]
    kk = kkr / jnp.maximum(jnp.sqrt(_segsum(kkr * kkr, e)), 1e-12)
    k2 = k * (1.0 + (a - 1.0) * ka_ref[...])
    r_s[0:tb, :] = r
    lw_s[0:tb, :] = -jnp.exp(w_log)
    k_s[0:tb, :] = k2
    v_s[0:tb, :] = v
    kk_s[0:tb, :] = kk
    a_s[0:tb, :] = a
    if tb < lc:
        z = jnp.zeros((lc - tb, W), F32)
        for ref in (r_s, lw_s, k_s, v_s, kk_s, a_s):
            ref[tb:lc, :] = z

    def chunk(ci, carry):
        r0 = pl.multiple_of(ci * lc, lc)
        rows = pl.ds(r0, lc)
        for gi in range(n_groups):
            lanes = slice(gi * HGW, (gi + 1) * HGW)
            y, s_new = _rwkv_chunk(r_s[rows, lanes], lw_s[rows, lanes], k_s[rows, lanes],
                                   v_s[rows, lanes], kk_s[rows, lanes], a_s[rows, lanes], s_ref[gi])
            y_s[rows, lanes] = y
            s_ref[gi] = s_new
        return carry

    lax.fori_loop(0, max(tb // lc, 1), chunk, 0)

    y = y_s[0:tb, :]
    mean = _segsum(y, e) * (1.0 / B_HD)
    yc = y - mean
    var = _segsum(yc * yc, e) * (1.0 / B_HD)
    yn = yc * lax.rsqrt(var + RWKV_GN_EPS) * lng_ref[...] + lnb_ref[...]
    bonus = _segsum(r * k2 * rk_ref[...], e) * v
    o_ref[...] = (yn + bonus) * g

    @pl.when(t == pl.num_programs(1) - 1)
    def _():
        sout_ref[0] = s_ref[...]


def _rwkv(pb, shift0, s0_bd, wts, nb, t):
    tb = _row_tile(t, 256)
    lc = CHUNK
    assert tb % lc == 0 or tb < lc
    nt = t // tb
    width = pb.shape[1]
    rows = max(tb, lc)
    W = BRANCH_W
    n_groups = W // HGW
    row = lambda n: _resident((1, n))
    return pl.pallas_call(
        functools.partial(_rwkv_body, tb=tb, lc=lc),
        grid=(nb, nt),
        in_specs=[pl.BlockSpec((tb, width), lambda b, i: (b * nt + i, 0)),
                  pl.BlockSpec((1, 1, width), lambda b, i: (b, 0, 0)),
                  pl.BlockSpec((1, n_groups, HGW, HGW), lambda b, i: (b, 0, 0, 0)),
                  row(width), row(W), _resident((128, W)), row(W), _resident((128, W)),
                  _resident((128, W)), row(W), row(W), row(W), row(W), row(W), _resident((W, W))],
        out_specs=[pl.BlockSpec((tb, W), lambda b, i: (b * nt + i, 0)),
                   pl.BlockSpec((1, n_groups, HGW, HGW), lambda b, i: (b, 0, 0, 0))],
        out_shape=[jax.ShapeDtypeStruct((nb * t, W), F32),
                   jax.ShapeDtypeStruct((nb, n_groups, HGW, HGW), F32)],
        scratch_shapes=[pltpu.VMEM((tb + SUBLANES, width), F32),
                        pltpu.VMEM((n_groups, HGW, HGW), F32)]
                       + [pltpu.VMEM((rows, W), F32)] * 7,
        compiler_params=_cparams("parallel", "arbitrary"),
        name="rwkv7",
    )(pb, shift0, s0_bd, *wts, _seg_matrix(W, B_HD))


def _gla_chunk(q, k, v, gk, st_bd):
    L = q.shape[0]
    kw = C_HEADS * C_DK
    vw = C_HEADS * C_DV
    b = _split_dot(_tril_ones(L), gk, 3)
    qe = q * jnp.exp(b)
    ke = k * jnp.exp(-b)
    a_all = _dot_nt(qe, _bd_rows(ke, C_DK, C_HEADS))
    row = lax.broadcasted_iota(jnp.int32, a_all.shape, 0)
    col = lax.broadcasted_iota(jnp.int32, a_all.shape, 1) % L
    a_all = jnp.where(col <= row, a_all, 0.0)
    o = _dot_nt(qe, st_bd) + _dot(a_all, _bd_rows(v, C_DV, C_HEADS))
    b_last = b[L - 1:L, :]
    kd = k * jnp.exp(b_last - b)
    upd = _dot_tn(v, kd)
    ri = lax.broadcasted_iota(jnp.int32, (vw, kw), 0) // C_DV
    ci = lax.broadcasted_iota(jnp.int32, (vw, kw), 1) // C_DK
    st_new = st_bd * jnp.exp(b_last) + jnp.where(ri == ci, upd, 0.0)
    return o, st_new


def _gla_body(pc_ref, s0_ref, g2_ref, gb_ref, norm_ref, o_ref, sout_ref,
              s_ref, q_s, k_s, v_s, gk_s, y_s, *, tb, lc):
    t = pl.program_id(1)
    kw = C_HEADS * C_DK
    vw = C_HEADS * C_DV

    @pl.when(t == 0)
    def _():
        s_ref[...] = s0_ref[0]

    pc = pc_ref[...]
    og = pc[:, 2 * kw + vw:2 * kw + 2 * vw]
    gl = pc[:, 2 * kw + 2 * vw:2 * kw + 2 * vw + 128]
    z = _dot(gl, g2_ref[...]) + gb_ref[...]
    q_s[0:tb, :] = pc[:, 0:kw] * (C_DK ** -0.5)
    k_s[0:tb, :] = pc[:, kw:2 * kw]
    v_s[0:tb, :] = pc[:, 2 * kw:2 * kw + vw]
    gk_s[0:tb, :] = -_softplus(-z) * (1.0 / C_TAU)
    if tb < lc:
        for ref in (q_s, k_s, v_s, gk_s):
            ref[tb:lc, :] = jnp.zeros((lc - tb, ref.shape[1]), F32)

    def chunk(ci, carry):
        rows = pl.ds(pl.multiple_of(ci * lc, lc), lc)
        o, st_new = _gla_chunk(q_s[rows, :], k_s[rows, :], v_s[rows, :], gk_s[rows, :], s_ref[...])
        y_s[rows, :] = o
        s_ref[...] = st_new
        return carry

    lax.fori_loop(0, max(tb // lc, 1), chunk, 0)

    y = y_s[0:tb, :]
    outs = []
    for h in range(C_HEADS):
        yh = y[:, h * C_DV:(h + 1) * C_DV]
        outs.append(yh * lax.rsqrt(jnp.mean(yh * yh, axis=-1, keepdims=True) + EPS) * norm_ref[...])
    yn = jnp.concatenate(outs, axis=1)
    o_ref[...] = yn * (og * _sigmoid(og))

    @pl.when(t == pl.num_programs(1) - 1)
    def _():
        sout_ref[0] = s_ref[...]


def _gla(pc, s0_bd, g2, gb, norm, nb, t):
    tb = _row_tile(t, 256)
    lc = CHUNK
    assert tb % lc == 0 or tb < lc
    nt = t // tb
    width = pc.shape[1]
    rows = max(tb, lc)
    kw, vw = C_HEADS * C_DK, C_HEADS * C_DV
    return pl.pallas_call(
        functools.partial(_gla_body, tb=tb, lc=lc),
        grid=(nb, nt),
        in_specs=[pl.BlockSpec((tb, width), lambda b, i: (b * nt + i, 0)),
                  pl.BlockSpec((1, vw, kw), lambda b, i: (b, 0, 0)),
                  _resident((128, kw)), _resident((1, kw)), _resident((1, C_DV))],
        out_specs=[pl.BlockSpec((tb, vw), lambda b, i: (b * nt + i, 0)),
                   pl.BlockSpec((1, vw, kw), lambda b, i: (b, 0, 0))],
        out_shape=[jax.ShapeDtypeStruct((nb * t, vw), F32),
                   jax.ShapeDtypeStruct((nb, vw, kw), F32)],
        scratch_shapes=[pltpu.VMEM((vw, kw), F32),
                        pltpu.VMEM((rows, kw), F32), pltpu.VMEM((rows, kw), F32),
                        pltpu.VMEM((rows, vw), F32), pltpu.VMEM((rows, kw), F32),
                        pltpu.VMEM((rows, vw), F32)],
        compiler_params=_cparams("parallel", "arbitrary"),
        name="gla",
    )(pc, s0_bd, g2, gb, norm)


def _shift_rows(x, d, fill):
    n = x.shape[0]
    if d % SUBLANES == 0:
        head = jnp.full((d, x.shape[1]), fill, x.dtype)
        return jnp.concatenate([head, x[:n - d]], axis=0)
    rolled = pltpu.roll(x, d, 0)
    row = lax.broadcasted_iota(jnp.int32, x.shape, 0)
    return jnp.where(row < d, fill, rolled)


def _lru_body(pd_ref, conv0_ref, h0_ref, cw_ref, cb_ref, wa_ref, ba_ref, wx_ref, bx_ref, lam_ref,
              o_ref, convout_ref, hout_ref, ext_ref, h_ref, *, tb):
    t = pl.program_id(1)
    W = BRANCH_W

    @pl.when(t == 0)
    def _():
        ext_ref[0:SUBLANES, :] = conv0_ref[0]
        h_ref[...] = h0_ref[0]

    gate = pd_ref[:, 0:W]
    xr = pd_ref[:, W:2 * W]
    ext_ref[SUBLANES:, :] = xr
    xc = cb_ref[...] + xr * cw_ref[CONV_W - 1:CONV_W, :]
    for j in range(CONV_W - 1):
        off = SUBLANES - (CONV_W - 1) + j
        xc = xc + ext_ref[off:off + tb, :] * cw_ref[j:j + 1, :]
    tail = ext_ref[tb:tb + SUBLANES, :]
    ext_ref[0:SUBLANES, :] = tail

    r = _sigmoid(_dot(xc, wa_ref[...]) + ba_ref[...])
    gi = _sigmoid(_dot(xc, wx_ref[...]) + bx_ref[...])
    log_a = (-LRU_C) * r * _softplus(-lam_ref[...])
    a = jnp.exp(log_a)
    u = jnp.sqrt(1.0 - jnp.exp(2.0 * log_a)) * (gi * xc)
    d = 1
    while d < tb:
        u = u + a * _shift_rows(u, d, 0.0)
        a = a * _shift_rows(a, d, 1.0)
        d *= 2
    h = u + a * h_ref[...]
    h_ref[...] = h[tb - 1:tb, :]
    o_ref[...] = h * (0.5 * gate * (1.0 + jnp.tanh(math.sqrt(2.0 / math.pi)
                                                   * (gate + 0.044715 * gate * gate * gate))))

    @pl.when(t == pl.num_programs(1) - 1)
    def _():
        convout_ref[0] = tail
        hout_ref[0] = h[tb - 1:tb, :]


def _lru(pd, conv0, h0, wts, nb, t):
    tb = _row_tile(t, 256)
    nt = t // tb
    W = BRANCH_W
    row = lambda: _resident((1, W))
    return pl.pallas_call(
        functools.partial(_lru_body, tb=tb),
        grid=(nb, nt),
        in_specs=[pl.BlockSpec((tb, 2 * W), lambda b, i: (b * nt + i, 0)),
                  pl.BlockSpec((1, SUBLANES, W), lambda b, i: (b, 0, 0)),
                  pl.BlockSpec((1, 1, W), lambda b, i: (b, 0, 0)),
                  _resident((CONV_W, W)), row(), _resident((W, W)), row(), _resident((W, W)), row(),
                  row()],
        out_specs=[pl.BlockSpec((tb, W), lambda b, i: (b * nt + i, 0)),
                   pl.BlockSpec((1, SUBLANES, W), lambda b, i: (b, 0, 0)),
                   pl.BlockSpec((1, 1, W), lambda b, i: (b, 0, 0))],
        out_shape=[jax.ShapeDtypeStruct((nb * t, W), F32),
                   jax.ShapeDtypeStruct((nb, SUBLANES, W), F32),
                   jax.ShapeDtypeStruct((nb, 1, W), F32)],
        scratch_shapes=[pltpu.VMEM((tb + SUBLANES, W), F32), pltpu.VMEM((1, W), F32)],
        compiler_params=_cparams("parallel", "arbitrary"),
        name="rglru",
    )(pd, conv0, h0, *wts)


def _block_diag(w):
    n, a, b = w.shape
    eye = jnp.eye(n, dtype=w.dtype)
    return (eye[:, None, :, None] * w[:, :, None, :]).reshape(n * a, n * b)


def _pad_rows_to(w, rows):
    return jnp.pad(w, ((0, rows - w.shape[0]), (0, 0)))


def _layer_weights(i, W):
    bf = lambda a: a.astype(BF16)
    row = lambda a: a.reshape(1, -1).astype(F32)
    bw = BRANCH_W
    a_cols = A_HEADS * A_HD + 2 * A_KV_HEADS * A_HD
    b_cols = 3 * bw + B_W_RANK + B_A_RANK + B_G_RANK
    c_cols = 2 * C_HEADS * C_DK + C_HEADS * C_DV + C_G_RANK + bw
    w_in = W['w_in'][i]
    d = w_in.shape[0]
    wa = w_in[:, :a_cols]
    wb = w_in[:, a_cols:a_cols + b_cols]
    wc = w_in[:, a_cols + b_cols:a_cols + b_cols + c_cols]
    wd = w_in[:, a_cols + b_cols + c_cols:]
    zpad = lambda n: jnp.zeros((d, n), w_in.dtype)
    wb2 = jnp.concatenate([wb[:, :3 * bw], wb[:, 3 * bw:3 * bw + B_W_RANK], zpad(128 - B_W_RANK),
                           wb[:, 3 * bw + B_W_RANK:3 * bw + B_W_RANK + B_A_RANK], zpad(128 - B_A_RANK),
                           wb[:, 3 * bw + B_W_RANK + B_A_RANK:]], axis=1)
    mu = W['rwkv_mu'][i]
    z1 = lambda n: jnp.zeros((n,), F32)
    mu2 = jnp.concatenate([mu[:3 * bw], mu[3 * bw:3 * bw + B_W_RANK], z1(128 - B_W_RANK),
                           mu[3 * bw + B_W_RANK:3 * bw + B_W_RANK + B_A_RANK], z1(128 - B_A_RANK),
                           mu[3 * bw + B_W_RANK + B_A_RANK:]])
    qkv = 2 * C_HEADS * C_DK + C_HEADS * C_DV
    wc2 = jnp.concatenate([wc[:, :qkv], wc[:, qkv + C_G_RANK:], wc[:, qkv:qkv + C_G_RANK],
                           zpad(128 - C_G_RANK)], axis=1)
    return dict(
        ffn1=(row(W['g_ffn1'][i]), bf(W['w_ffn1_gate'][i]), bf(W['w_ffn1_up'][i]), bf(W['w_ffn1_down'][i])),
        ffn2=(row(W['g_ffn2'][i]), bf(W['w_ffn2_gate'][i]), bf(W['w_ffn2_up'][i]), bf(W['w_ffn2_down'][i])),
        g_mix=row(W['g_mix'][i]),
        w_in=(bf(wa), bf(wb2), bf(wc2), bf(wd)),
        q_norm=row(jnp.tile(W['q_norm'][i], A_HEADS)),
        k_norm=row(jnp.tile(W['k_norm'][i], A_KV_HEADS)),
        sink=W['attn_sink'][i].astype(F32),
        rwkv=(row(mu2), row(W['rwkv_w0'][i]), bf(_pad_rows_to(W['rwkv_w2'][i], 128)),
              row(W['rwkv_a0'][i]), bf(_pad_rows_to(W['rwkv_a2'][i], 128)), bf(W['rwkv_g2'][i]),
              row(W['rwkv_k_k'][i]), row(W['rwkv_k_a'][i]), row(W['rwkv_r_k'][i]),
              row(W['rwkv_ln_g'][i]), row(W['rwkv_ln_b'][i])),
        gla=(bf(_pad_rows_to(W['gla_g2'][i], 128)), row(W['gla_gb'][i]), row(W['gla_norm'][i])),
        lru=(W['lru_conv_w'][i].astype(F32), row(W['lru_conv_b'][i]),
             bf(_block_diag(W['lru_wa'][i])), row(W['lru_ba'][i]),
             bf(_block_diag(W['lru_wx'][i])), row(W['lru_bx'][i]), row(W['lru_lambda'][i])),
        merge=(bf(W['w_merge_gate'][i]), bf(W['w_branch'][i]), bf(W['w_out'][i])),
        ple=(row(W['g_ple'][i]), bf(W['w_ple_gate'][i]), bf(W['w_ple_proj'][i])),
    )


def _b_cols_to_kernel(x):
    bw = BRANCH_W
    z = jnp.zeros(x.shape[:-1] + (128 - B_W_RANK,), x.dtype)
    return jnp.concatenate([x[..., :3 * bw], x[..., 3 * bw:3 * bw + B_W_RANK], z,
                            x[..., 3 * bw + B_W_RANK:3 * bw + B_W_RANK + B_A_RANK], z,
                            x[..., 3 * bw + B_W_RANK + B_A_RANK:]], axis=-1)


def _b_cols_from_kernel(x):
    bw = BRANCH_W
    return jnp.concatenate([x[..., :3 * bw + B_W_RANK], x[..., 3 * bw + 128:3 * bw + 128 + B_A_RANK],
                            x[..., 3 * bw + 256:]], axis=-1)


def _heads_to_bd(s, hg):
    nb, h, a, b = s.shape
    s = s.reshape(nb, h // hg, hg, a, b)
    eye = jnp.eye(hg, dtype=s.dtype)
    return (eye[None, None, :, None, :, None] * s[:, :, :, :, None, :]).reshape(nb, h // hg, hg * a, hg * b)


def _bd_to_heads(s, hg):
    nb, ng, ra, cb = s.shape
    a, b = ra // hg, cb // hg
    s = s.reshape(nb, ng, hg, a, hg, b)
    idx = jnp.arange(hg)
    return s[:, :, idx, :, idx, :].transpose(1, 2, 0, 3, 4).reshape(nb, ng * hg, a, b)


def _trunk_layer(x, pe, st, lw, bias, first_chunk):
    nb, t, d = x.shape
    n = nb * t
    ck, cv, shift0, s_rwkv0, s_gla0, conv0, lru0 = st
    x2 = x.reshape(n, d)
    x2 = _ffn(x2, *lw['ffn1'])
    pa, pb, pc, pd = _inproj(x2, lw['g_mix'], lw['w_in'])
    kw = A_KV_HEADS * A_HD
    if first_chunk:
        o_a, k_win, v_win = _attn_prompt(pa, bias, lw['sink'], lw['q_norm'], lw['k_norm'], nb, t)
    else:
        o_a, k_win, v_win = _attn_sample(pa, ck.reshape(nb, WINDOW, kw), cv.reshape(nb, WINDOW, kw),
                                         bias, lw['sink'], lw['q_norm'], lw['k_norm'], nb, t)
    k_win = k_win.reshape(nb, WINDOW, A_KV_HEADS, A_HD)
    v_win = v_win.reshape(nb, WINDOW, A_KV_HEADS, A_HD)

    o_b, s_bd = _rwkv(pb, _b_cols_to_kernel(shift0)[:, None, :], _heads_to_bd(s_rwkv0, HG),
                      lw['rwkv'], nb, t)
    shift1 = _b_cols_from_kernel(pb.reshape(nb, t, -1)[:, -1])
    s_rwkv1 = _bd_to_heads(s_bd, HG)

    st_bd0 = _heads_to_bd(jnp.swapaxes(s_gla0, -1, -2), C_HEADS)[:, 0]
    o_c, st_bd = _gla(pc, st_bd0, *lw['gla'], nb, t)
    s_gla1 = jnp.swapaxes(_bd_to_heads(st_bd[:, None], C_HEADS), -1, -2)

    conv_pad = jnp.pad(conv0, ((0, 0), (SUBLANES - (CONV_W - 1), 0), (0, 0)))
    o_d, conv_out, h_out = _lru(pd, conv_pad, lru0[:, None, :], lw['lru'], nb, t)
    conv1 = conv_out[:, SUBLANES - (CONV_W - 1):]
    lru1 = h_out[:, 0]

    x2 = _merge(x2, (o_a, o_b, o_c, o_d), lw['g_mix'], *lw['merge'])
    x2 = _ffn(x2, *lw['ffn2'])
    x2 = _ple(x2, pe.reshape(n, -1), *lw['ple'])
    return x2.reshape(nb, t, d), (k_win, v_win, shift1, s_rwkv1, s_gla1, conv1, lru1)


def kernel(x_prompt, x_sample, cache_attn_k, cache_attn_v, state_rwkv_shift, state_rwkv, state_gla, state_lru_conv, state_lru, p_prompt, p_sample, rel_bias_table, g_ffn1, w_ffn1_gate, w_ffn1_up, w_ffn1_down, g_mix, w_in, q_norm, k_norm, attn_sink, rwkv_mu, rwkv_w0, rwkv_w2, rwkv_a0, rwkv_a2, rwkv_g2, rwkv_k_k, rwkv_k_a, rwkv_r_k, rwkv_ln_g, rwkv_ln_b, gla_g2, gla_gb, gla_norm, lru_conv_w, lru_conv_b, lru_wa, lru_ba, lru_wx, lru_bx, lru_lambda, w_merge_gate, w_branch, w_out, g_ffn2, w_ffn2_gate, w_ffn2_up, w_ffn2_down, g_ple, w_ple_gate, w_ple_proj):
    W = dict(g_ffn1=g_ffn1, w_ffn1_gate=w_ffn1_gate, w_ffn1_up=w_ffn1_up, w_ffn1_down=w_ffn1_down,
             g_mix=g_mix, w_in=w_in, q_norm=q_norm, k_norm=k_norm, attn_sink=attn_sink,
             rwkv_mu=rwkv_mu, rwkv_w0=rwkv_w0, rwkv_w2=rwkv_w2, rwkv_a0=rwkv_a0, rwkv_a2=rwkv_a2,
             rwkv_g2=rwkv_g2, rwkv_k_k=rwkv_k_k, rwkv_k_a=rwkv_k_a, rwkv_r_k=rwkv_r_k,
             rwkv_ln_g=rwkv_ln_g, rwkv_ln_b=rwkv_ln_b, gla_g2=gla_g2, gla_gb=gla_gb,
             gla_norm=gla_norm, lru_conv_w=lru_conv_w, lru_conv_b=lru_conv_b, lru_wa=lru_wa,
             lru_ba=lru_ba, lru_wx=lru_wx, lru_bx=lru_bx, lru_lambda=lru_lambda,
             w_merge_gate=w_merge_gate, w_branch=w_branch, w_out=w_out, g_ffn2=g_ffn2,
             w_ffn2_gate=w_ffn2_gate, w_ffn2_up=w_ffn2_up, w_ffn2_down=w_ffn2_down,
             g_ple=g_ple, w_ple_gate=w_ple_gate, w_ple_proj=w_ple_proj)
    depth = w_in.shape[0]
    dt = x_prompt.dtype
    bp, tp = x_prompt.shape[:2]
    ts = x_sample.shape[1]
    b_cols = state_rwkv_shift.shape[-1]
    bias_p = _rel_bias(rel_bias_table, CHUNK, WINDOW + CHUNK)
    bias_s = _rel_bias(rel_bias_table, ts, WINDOW + ts)
    yp, ys = x_prompt, x_sample
    st_p, st_s = [], []
    for i in range(depth):
        lw = _layer_weights(i, W)
        zero_st = (None, None,
                   jnp.zeros((bp, b_cols), dt),
                   jnp.zeros((bp, B_HEADS, B_HD, B_HD), dt),
                   jnp.zeros((bp, C_HEADS, C_DK, C_DV), dt),
                   jnp.zeros((bp, CONV_W - 1, BRANCH_W), dt),
                   jnp.zeros((bp, BRANCH_W), dt))
        yp, sp = _trunk_layer(yp, p_prompt[i], zero_st, lw, bias_p, True)
        cache_st = (cache_attn_k[i], cache_attn_v[i], state_rwkv_shift[i], state_rwkv[i],
                    state_gla[i], state_lru_conv[i], state_lru[i])
        ys, ss = _trunk_layer(ys, p_sample[i], cache_st, lw, bias_s, False)
        st_p.append(sp)
        st_s.append(ss)
    stack = lambda states, j: jnp.stack([s[j] for s in states])
    return (yp, ys) + tuple(stack(st_p, j) for j in range(7)) + tuple(stack(st_s, j) for j in range(7))
```

```python
import functools
import math

import numpy as np
import jax
import jax.numpy as jnp
from jax import lax
from jax.experimental import pallas as pl
from jax.experimental.pallas import tpu as pltpu

F32 = jnp.float32
BF16 = jnp.bfloat16

V7X_VMEM_BYTES = 64 * 1024 * 1024
VMEM_LIMIT = V7X_VMEM_BYTES - 8 * 1024 * 1024
SUBLANES = 8

EPS = 1e-6
NEG_INF = -1e30
CHUNK = 64
WINDOW = 128
N_BUCKETS = 32
MAX_DIST = 128
A_HEADS, A_KV_HEADS, A_HD = 8, 2, 64
A_GROUP = A_HEADS // A_KV_HEADS
B_HEADS, B_HD = 8, 64
B_W_RANK, B_A_RANK, B_G_RANK = 64, 64, 128
RWKV_GN_EPS = 64e-5
C_HEADS, C_DK, C_DV = 4, 64, 128
C_G_RANK = 16
C_TAU = 16.0
D_BLOCKS = 8
CONV_W = 4
LRU_C = 8.0
BRANCH_W = 512
HG = 4
HGW = HG * B_HD


def _cparams(*sem):
    return pltpu.CompilerParams(dimension_semantics=sem, vmem_limit_bytes=VMEM_LIMIT)


def _resident(shape):
    nd = len(shape)
    return pl.BlockSpec(shape, lambda *_: (0,) * nd, pipeline_mode=pl.Buffered(1))


def _row_tile(n, cap):
    t = min(n, cap)
    while n % t:
        t //= 2
    return t


def _dot(a, b):
    return jnp.dot(a.astype(BF16), b.astype(BF16), preferred_element_type=F32)


def _dot_nt(a, b):
    return lax.dot_general(a.astype(BF16), b.astype(BF16), (((1,), (1,)), ((), ())),
                           preferred_element_type=F32)


def _dot_tn(a, b):
    return lax.dot_general(a.astype(BF16), b.astype(BF16), (((0,), (0,)), ((), ())),
                           preferred_element_type=F32)


def _rms(x, g):
    return x * lax.rsqrt(jnp.mean(x * x, axis=-1, keepdims=True) + EPS) * g


def _sigmoid(x):
    return 1.0 / (1.0 + jnp.exp(-x))


def _softplus(x):
    return jnp.maximum(x, 0.0) + jnp.log1p(jnp.exp(-jnp.abs(x)))


def _softplus_big(x):
    return jnp.maximum(x, 0.0) + jnp.log(1.0 + jnp.exp(-jnp.abs(x)))


def _split_dot(e_lhs, x, terms):
    acc = None
    rem = x
    for n in range(terms):
        piece = rem.astype(BF16)
        d = jnp.dot(e_lhs, piece, preferred_element_type=F32)
        acc = d if acc is None else acc + d
        if n + 1 < terms:
            rem = rem - piece.astype(F32)
    return acc


def _segsum(x, e):
    hi = x.astype(BF16)
    lo = (x - hi.astype(F32)).astype(BF16)
    return (jnp.dot(hi, e, preferred_element_type=F32)
            + jnp.dot(lo, e, preferred_element_type=F32))


def _tril_ones(n):
    r = lax.broadcasted_iota(jnp.int32, (n, n), 0)
    c = lax.broadcasted_iota(jnp.int32, (n, n), 1)
    return jnp.where(r >= c, 1.0, 0.0).astype(BF16)


def _bd_rows(x, blk, nblk):
    lane_blk = lax.broadcasted_iota(jnp.int32, x.shape, 1) // blk
    return jnp.concatenate([jnp.where(lane_blk == h, x, 0.0) for h in range(nblk)], axis=0)


def _seg_matrix(width, seg):
    i = np.arange(width)
    return jnp.asarray((i[:, None] // seg) == (i[None, :] // seg), dtype=BF16)


def _ffn_body(x_ref, g_ref, wg_ref, wu_ref, wd_ref, o_ref, *, fchunk):
    x = x_ref[...]
    h = _rms(x, g_ref[...]).astype(BF16)
    acc = None
    for c in range(wg_ref.shape[1] // fchunk):
        sl = pl.ds(c * fchunk, fchunk)
        gt = jnp.dot(h, wg_ref[:, sl], preferred_element_type=F32)
        up = jnp.dot(h, wu_ref[:, sl], preferred_element_type=F32)
        act = (gt * _sigmoid(gt) * up).astype(BF16)
        d = jnp.dot(act, wd_ref[sl, :], preferred_element_type=F32)
        acc = d if acc is None else acc + d
    o_ref[...] = x + 0.5 * acc


def _ffn(x, g, wg, wu, wd):
    n, d = x.shape
    f = wg.shape[1]
    tm = _row_tile(n, 512)
    fchunk = f // 2 if (f // 2) % 128 == 0 else f
    return pl.pallas_call(
        functools.partial(_ffn_body, fchunk=fchunk),
        grid=(n // tm,),
        in_specs=[pl.BlockSpec((tm, d), lambda i: (i, 0)),
                  _resident((1, d)), _resident((d, f)), _resident((d, f)), _resident((f, d))],
        out_specs=pl.BlockSpec((tm, d), lambda i: (i, 0)),
        out_shape=jax.ShapeDtypeStruct((n, d), F32),
        compiler_params=_cparams("parallel"),
        name="ffn",
    )(x, g, wg, wu, wd)


def _inproj_body(x_ref, g_ref, wa_ref, wb_ref, wc_ref, wd_ref, pa_ref, pb_ref, pc_ref, pd_ref):
    h = _rms(x_ref[...], g_ref[...]).astype(BF16)
    for w_ref, p_ref in ((wa_ref, pa_ref), (wb_ref, pb_ref), (wc_ref, pc_ref), (wd_ref, pd_ref)):
        p_ref[...] = jnp.dot(h, w_ref[...], preferred_element_type=F32)


def _inproj(x, g, ws):
    n, d = x.shape
    tm = _row_tile(n, 512)
    widths = [w.shape[1] for w in ws]
    return pl.pallas_call(
        _inproj_body,
        grid=(n // tm,),
        in_specs=[pl.BlockSpec((tm, d), lambda i: (i, 0)), _resident((1, d))]
                 + [_resident((d, w)) for w in widths],
        out_specs=[pl.BlockSpec((tm, w), lambda i: (i, 0)) for w in widths],
        out_shape=[jax.ShapeDtypeStruct((n, w), F32) for w in widths],
        compiler_params=_cparams("parallel"),
        name="inproj",
    )(x, g, *ws)


def _merge_body(x_ref, oa_ref, ob_ref, oc_ref, od_ref, g_ref, wmg_ref, wb_ref, wo_ref, o_ref):
    x = x_ref[...]
    h = _rms(x, g_ref[...]).astype(BF16)
    y = None
    for n, b_ref in enumerate((oa_ref, ob_ref, oc_ref, od_ref)):
        gate = _sigmoid(jnp.dot(h, wmg_ref[n], preferred_element_type=F32))
        t = gate * jnp.dot(b_ref[...].astype(BF16), wb_ref[n], preferred_element_type=F32)
        y = t if y is None else y + t
    o_ref[...] = x + jnp.dot(y.astype(BF16), wo_ref[...], preferred_element_type=F32)


def _merge(x, outs, g, wmg, wb, wo):
    n, d = x.shape
    bw = outs[0].shape[1]
    tm = _row_tile(n, 512)
    return pl.pallas_call(
        _merge_body,
        grid=(n // tm,),
        in_specs=[pl.BlockSpec((tm, d), lambda i: (i, 0))]
                 + [pl.BlockSpec((tm, bw), lambda i: (i, 0))] * 4
                 + [_resident((1, d)), _resident(wmg.shape), _resident(wb.shape), _resident(wo.shape)],
        out_specs=pl.BlockSpec((tm, d), lambda i: (i, 0)),
        out_shape=jax.ShapeDtypeStruct((n, d), F32),
        compiler_params=_cparams("parallel"),
        name="merge",
    )(x, *outs, g, wmg, wb, wo)


def _ple_body(x_ref, pe_ref, g_ref, wg_ref, wp_ref, o_ref):
    x = x_ref[...]
    h = _rms(x, g_ref[...]).astype(BF16)
    gate = _sigmoid(jnp.dot(h, wg_ref[...], preferred_element_type=F32))
    o_ref[...] = x + gate * jnp.dot(pe_ref[...].astype(BF16), wp_ref[...], preferred_element_type=F32)


def _ple(x, pe, g, wg, wp):
    n, d = x.shape
    pd = pe.shape[1]
    tm = _row_tile(n, 512)
    return pl.pallas_call(
        _ple_body,
        grid=(n // tm,),
        in_specs=[pl.BlockSpec((tm, d), lambda i: (i, 0)), pl.BlockSpec((tm, pd), lambda i: (i, 0)),
                  _resident((1, d)), _resident((d, d)), _resident((pd, d))],
        out_specs=pl.BlockSpec((tm, d), lambda i: (i, 0)),
        out_shape=jax.ShapeDtypeStruct((n, d), F32),
        compiler_params=_cparams("parallel"),
        name="ple",
    )(x, pe, g, wg, wp)


def _t5_bucket_np(rel):
    half = N_BUCKETS // 2
    max_exact = half // 2
    ret = np.where(rel > 0, half, 0)
    n = np.abs(rel)
    nf = np.maximum(n, 1).astype(np.float32)
    large = max_exact + (np.log(nf / np.float32(max_exact)) / np.float32(math.log(MAX_DIST / max_exact))
                         * np.float32(half - max_exact)).astype(np.int32)
    large = np.minimum(large, half - 1)
    return (ret + np.where(n < max_exact, n, large)).astype(np.int32)


def _bias_body(idx_ref, table_ref, o_ref):
    idx = idx_ref[...]
    for h in range(A_HEADS):
        acc = jnp.zeros(idx.shape, F32)
        for b in range(N_BUCKETS):
            acc = jnp.where(idx == b, table_ref[b, h], acc)
        o_ref[h] = acc


def _rel_bias(table, n_q, n_k):
    rel = np.arange(n_k)[None, :] - WINDOW - np.arange(n_q)[:, None]
    idx = jnp.asarray(_t5_bucket_np(rel))
    return pl.pallas_call(
        _bias_body,
        in_specs=[pl.BlockSpec(memory_space=pltpu.VMEM), pl.BlockSpec(memory_space=pltpu.SMEM)],
        out_specs=pl.BlockSpec(memory_space=pltpu.VMEM),
        out_shape=jax.ShapeDtypeStruct((A_HEADS, n_q, n_k), F32),
        name="rel_bias",
    )(idx, table)


def _head_rms(x, e, g):
    ms = _segsum(x * x, e) * (1.0 / A_HD)
    return x * lax.rsqrt(ms + EPS) * g


def _kv_replication():
    lane = np.arange(A_HEADS * A_HD)
    src = (lane // (A_GROUP * A_HD)) * A_HD + lane % A_HD
    return jnp.asarray(np.arange(A_KV_HEADS * A_HD)[:, None] == src[None, :], dtype=BF16)


def _attend_blocks(q_list, k8_list, v8_list, bias, sink, valid_list):
    n = q_list[0].shape[0]
    lane_head = lax.broadcasted_iota(jnp.int32, (n, A_HEADS * A_HD), 1) // A_HD
    s_list = []
    for q, k8, valid in zip(q_list, k8_list, valid_list):
        q_bd = jnp.concatenate([jnp.where(lane_head == h, q, 0.0) for h in range(A_HEADS)], axis=0)
        s = _dot_nt(q_bd, k8) + bias
        if valid is not None:
            s = jnp.where(valid, s, NEG_INF)
        s_list.append(s)
    s = jnp.stack(s_list)
    m = jnp.maximum(jnp.max(s, axis=-1, keepdims=True), sink)
    e = jnp.exp(s - m)
    den = jnp.sum(e, axis=-1, keepdims=True) + jnp.exp(sink - m)
    p = e * (1.0 / den)
    outs = []
    for c, v8 in enumerate(v8_list):
        o8 = _dot(p[c], v8)
        o = None
        for h in range(A_HEADS):
            part = jnp.where(lane_head == h, o8[h * n:(h + 1) * n], 0.0)
            o = part if o is None else o + part
        outs.append(o)
    return outs


def _attn_prompt_body(main_ref, prev_ref, bias_ref, sink_ref, qn_ref, kn_ref, eq_ref, ek_ref, rep_ref,
                      o_ref, kout_ref, vout_ref, *, tb):
    i = pl.program_id(1)
    qw = A_HEADS * A_HD
    kw = A_KV_HEADS * A_HD
    band = WINDOW + CHUNK
    ek = ek_ref[...]
    rep = rep_ref[...]
    q = _head_rms(main_ref[:, 0:qw], eq_ref[...], qn_ref[...]) * (A_HD ** -0.5)
    kf = jnp.concatenate([_head_rms(prev_ref[:, qw:qw + kw], ek, kn_ref[...]),
                          _head_rms(main_ref[:, qw:qw + kw], ek, kn_ref[...])], axis=0)
    vf = jnp.concatenate([prev_ref[:, qw + kw:qw + 2 * kw], main_ref[:, qw + kw:qw + 2 * kw]], axis=0)
    k8 = jnp.dot(kf.astype(BF16), rep, preferred_element_type=F32).astype(BF16)
    v8 = jnp.dot(vf.astype(BF16), rep, preferred_element_type=F32).astype(BF16)
    kidx = lax.broadcasted_iota(jnp.int32, (A_HEADS * CHUNK, band), 1)
    n_chunks = tb // CHUNK
    valid = [(kidx + (i * tb + c * CHUNK - WINDOW)) >= 0 if c * CHUNK < WINDOW else None
             for c in range(n_chunks)]
    outs = _attend_blocks([q[c * CHUNK:(c + 1) * CHUNK] for c in range(n_chunks)],
                          [k8[c * CHUNK:c * CHUNK + band] for c in range(n_chunks)],
                          [v8[c * CHUNK:c * CHUNK + band] for c in range(n_chunks)],
                          bias_ref[...], sink_ref[...], valid)
    for c in range(n_chunks):
        o_ref[c * CHUNK:(c + 1) * CHUNK, :] = outs[c]

    @pl.when(i == pl.num_programs(1) - 1)
    def _():
        kout_ref[0] = kf[tb:tb + WINDOW, :]
        vout_ref[0] = vf[tb:tb + WINDOW, :]


def _attn_prompt(pa, bias, sink, qn, kn, nb, t):
    tb = _row_tile(t, 512)
    assert tb % WINDOW == 0
    nq = t // tb
    wpb = tb // WINDOW
    qw, kw = A_HEADS * A_HD, A_KV_HEADS * A_HD
    width = pa.shape[1]
    return pl.pallas_call(
        functools.partial(_attn_prompt_body, tb=tb),
        grid=(nb, nq),
        in_specs=[pl.BlockSpec((tb, width), lambda b, i: (b * nq + i, 0)),
                  pl.BlockSpec((WINDOW, width),
                               lambda b, i: (jnp.maximum((b * nq + i) * wpb - 1, 0), 0)),
                  _resident(bias.shape), _resident(sink.shape),
                  _resident((1, qw)), _resident((1, kw)), _resident((qw, qw)), _resident((kw, kw)),
                  _resident((kw, qw))],
        out_specs=[pl.BlockSpec((tb, qw), lambda b, i: (b * nq + i, 0)),
                   pl.BlockSpec((1, WINDOW, kw), lambda b, i: (b, 0, 0)),
                   pl.BlockSpec((1, WINDOW, kw), lambda b, i: (b, 0, 0))],
        out_shape=[jax.ShapeDtypeStruct((nb * t, qw), F32),
                   jax.ShapeDtypeStruct((nb, WINDOW, kw), F32),
                   jax.ShapeDtypeStruct((nb, WINDOW, kw), F32)],
        compiler_params=_cparams("parallel", "arbitrary"),
        name="attn_prompt",
    )(pa, pa, bias, sink, qn, kn, _seg_matrix(qw, A_HD), _seg_matrix(kw, A_HD), _kv_replication())


def _attn_sample_body(pa_ref, ck_ref, cv_ref, bias_ref, sink_ref, qn_ref, kn_ref, eq_ref, ek_ref, rep_ref,
                      o_ref, kout_ref, vout_ref, *, s, gb):
    qw = A_HEADS * A_HD
    kw = A_KV_HEADS * A_HD
    rep = rep_ref[...]
    q = _head_rms(pa_ref[:, 0:qw], eq_ref[...], qn_ref[...]) * (A_HD ** -0.5)
    kn = _head_rms(pa_ref[:, qw:qw + kw], ek_ref[...], kn_ref[...])
    vn = pa_ref[:, qw + kw:qw + 2 * kw]
    q_list, k8_list, v8_list = [], [], []
    for b in range(gb):
        kf = jnp.concatenate([ck_ref[b], kn[b * s:(b + 1) * s]], axis=0)
        vf = jnp.concatenate([cv_ref[b], vn[b * s:(b + 1) * s]], axis=0)
        kout_ref[b] = kf[s:s + WINDOW, :]
        vout_ref[b] = vf[s:s + WINDOW, :]
        q_list.append(q[b * s:(b + 1) * s])
        k8_list.append(jnp.dot(kf.astype(BF16), rep, preferred_element_type=F32).astype(BF16))
        v8_list.append(jnp.dot(vf.astype(BF16), rep, preferred_element_type=F32).astype(BF16))
    outs = _attend_blocks(q_list, k8_list, v8_list, bias_ref[...], sink_ref[...], [None] * gb)
    for b in range(gb):
        o_ref[b * s:(b + 1) * s, :] = outs[b]


def _attn_sample(pa, ck, cv, bias, sink, qn, kn, nb, s):
    qw, kw = A_HEADS * A_HD, A_KV_HEADS * A_HD
    width = pa.shape[1]
    gb = _row_tile(nb, 8)
    return pl.pallas_call(
        functools.partial(_attn_sample_body, s=s, gb=gb),
        grid=(nb // gb,),
        in_specs=[pl.BlockSpec((gb * s, width), lambda b: (b, 0)),
                  pl.BlockSpec((gb, WINDOW, kw), lambda b: (b, 0, 0)),
                  pl.BlockSpec((gb, WINDOW, kw), lambda b: (b, 0, 0)),
                  _resident(bias.shape), _resident(sink.shape),
                  _resident((1, qw)), _resident((1, kw)), _resident((qw, qw)), _resident((kw, kw)),
                  _resident((kw, qw))],
        out_specs=[pl.BlockSpec((gb * s, qw), lambda b: (b, 0)),
                   pl.BlockSpec((gb, WINDOW, kw), lambda b: (b, 0, 0)),
                   pl.BlockSpec((gb, WINDOW, kw), lambda b: (b, 0, 0))],
        out_shape=[jax.ShapeDtypeStruct((nb * s, qw), F32),
                   jax.ShapeDtypeStruct((nb, WINDOW, kw), F32),
                   jax.ShapeDtypeStruct((nb, WINDOW, kw), F32)],
        compiler_params=_cparams("parallel"),
        name="attn_sample",
    )(pa, ck, cv, bias, sink, qn, kn, _seg_matrix(qw, A_HD), _seg_matrix(kw, A_HD), _kv_replication())


def _rwkv_chunk_terms(r, lw, k, v, kk, a):
    each = lambda f, *seqs: [f(*xs) for xs in zip(*seqs)]
    cat0 = lambda *xs: jnp.concatenate(xs, axis=0)
    cat1 = lambda *xs: jnp.concatenate(xs, axis=1)
    bd = lambda x: _bd_rows(x, B_HD, HG)
    L = r[0].shape[0]
    tril = _tril_ones(L)
    c = each(lambda x: _split_dot(tril, x, 3), lw)
    c_last = each(lambda x: x[L - 1:L, :], c)
    e_last = each(lambda x, xl: jnp.exp(xl - x), c, c_last)
    beta = each(lambda x, y: x * y, kk, a)
    at = each(lambda x, cc, l: -x * jnp.exp(cc - l), kk, c, lw)
    rt = each(lambda x, cc: x * jnp.exp(cc), r, c)
    enc = each(lambda cc: jnp.exp(-cc), c)
    ar = each(cat0, at, rt)
    mb = each(lambda x, b, e: _dot_nt(x, bd(b * e)), ar, beta, enc)
    mk = each(lambda x, b, e: _dot_nt(x, bd(b * e)), ar, k, enc)
    row = lax.broadcasted_iota(jnp.int32, (L, HG * L), 0)
    col = lax.broadcasted_iota(jnp.int32, (L, HG * L), 1) % L
    strict = col < row
    incl = col <= row
    m_b = each(lambda x: jnp.where(strict, x[:L], 0.0), mb)
    m_k = each(lambda x: jnp.where(strict, x[:L], 0.0), mk)
    n_b = each(lambda x: jnp.where(incl, x[L:], 0.0), mb)
    n_k = each(lambda x: jnp.where(incl, x[L:], 0.0), mk)
    eye = jnp.where(col == row, 1.0, 0.0)
    t_inv = each(lambda x: eye + x, m_b)
    p = m_b
    for lvl in range(1, int(math.log2(L))):
        p_bd = each(lambda x: _bd_rows(x, L, HG), p)
        if lvl == 1:
            p = each(_dot, p, p_bd)
        else:
            tp = each(lambda t, x, xb: _dot(cat0(t, x), xb), t_inv, p, p_bd)
            t_inv = each(lambda t, x: t + x[:L], t_inv, tp)
            p = each(lambda x: x[L:], tp)
    t_inv = each(lambda t, x: t + _dot(t, _bd_rows(x, L, HG)), t_inv, p)
    v_bd = each(bd, v)
    mkv = each(_dot, m_k, v_bd)
    wu = each(lambda t, x, y: _dot(t, cat1(bd(x), bd(y))), t_inv, at, mkv)
    w1 = each(lambda x: x[:, :HGW], wu)
    u0 = each(lambda x: x[:, HGW:], wu)
    nbo = each(lambda n, x, y: _dot(n, cat1(bd(x), bd(y))), n_b, w1, u0)
    q1 = each(lambda x, y: x + y[:, :HGW], rt, nbo)
    y0 = each(lambda x, n, vb: x[:, HGW:] + _dot(n, vb), nbo, n_k, v_bd)
    bh = each(lambda x, y: x * y, beta, e_last)
    kh = each(lambda x, y: x * y, k, e_last)
    ri = lax.broadcasted_iota(jnp.int32, (HGW, HGW), 0) // B_HD
    ci = lax.broadcasted_iota(jnp.int32, (HGW, HGW), 1) // B_HD
    diag = ri == ci
    g_mat = each(lambda x, y: jnp.where(diag, _dot_tn(x, y), 0.0), w1, bh)
    h_mat = each(lambda x, y, z, w: jnp.where(diag, _dot_tn(cat0(x, y), cat0(z, w)), 0.0), u0, v, bh, kh)
    gam = each(jnp.exp, c_last)
    return q1, y0, g_mat, h_mat, gam


def _rwkv_body(pb_ref, shift_ref, s0_ref, mu_ref, w0_ref, w2_ref, a0_ref, a2_ref, g2_ref, kk_ref,
               ka_ref, rk_ref, lng_ref, lnb_ref, e_ref,
               o_ref, sout_ref,
               ext_ref, s_ref, *, tb, lc):
    t = pl.program_id(1)
    W = BRANCH_W
    n_groups = W // HGW

    @pl.when(t == 0)
    def _():
        ext_ref[0:SUBLANES, :] = jnp.broadcast_to(shift_ref[0], (SUBLANES, ext_ref.shape[1]))
        s_ref[...] = s0_ref[0]

    p = pb_ref[...]
    ext_ref[SUBLANES:, :] = p
    prev = ext_ref[SUBLANES - 1:SUBLANES - 1 + tb, :]
    xs = p + (prev - p) * mu_ref[...]
    ext_ref[0:SUBLANES, :] = p[tb - SUBLANES:, :]

    e = e_ref[...]
    r = xs[:, 0:W]
    k = xs[:, W:2 * W]
    v = xs[:, 2 * W:3 * W]
    wl = xs[:, 3 * W:3 * W + 128]
    al = xs[:, 3 * W + 128:3 * W + 256]
    gl = xs[:, 3 * W + 256:3 * W + 384]
    w_log = -_softplus_big(-(w0_ref[...] + _dot(jnp.tanh(wl), w2_ref[...]))) - 0.5
    a = _sigmoid(a0_ref[...] + _dot(al, a2_ref[...]))
    g = _dot(_sigmoid(gl), g2_ref[...])
    kkr = k * kk_ref[...]
    kk = kkr / jnp.maximum(jnp.sqrt(_segsum(kkr * kkr, e)), 1e-12)
    k2 = k * (1.0 + (a - 1.0) * ka_ref[...])
    lw = -jnp.exp(w_log)
    n_chunks = max(tb // lc, 1)

    def piece(x, ci, gi):
        blk = x[ci * lc:min((ci + 1) * lc, tb), gi * HGW:(gi + 1) * HGW]
        if tb < lc:
            blk = jnp.concatenate([blk, jnp.zeros((lc - tb, HGW), F32)], axis=0)
        return blk

    inst = [(ci, gi) for ci in range(n_chunks) for gi in range(n_groups)]
    terms = _rwkv_chunk_terms(*[[piece(x, ci, gi) for ci, gi in inst] for x in (r, lw, k2, v, kk, a)])
    y_rows = []
    for ci in range(n_chunks):
        y_cols = []
        for gi in range(n_groups):
            q1, y0, g_mat, h_mat, gam = [tm[ci * n_groups + gi] for tm in terms]
            s = s_ref[gi]
            y_cols.append(_dot_nt(q1, s) + y0)
            s_ref[gi] = s * gam + _dot(s, g_mat) + h_mat
        y_rows.append(jnp.concatenate(y_cols, axis=1))
    y = jnp.concatenate(y_rows, axis=0)[0:tb]
    mean = _segsum(y, e) * (1.0 / B_HD)
    yc = y - mean
    var = _segsum(yc * yc, e) * (1.0 / B_HD)
    yn = yc * lax.rsqrt(var + RWKV_GN_EPS) * lng_ref[...] + lnb_ref[...]
    bonus = _segsum(r * k2 * rk_ref[...], e) * v
    o_ref[...] = (yn + bonus) * g

    @pl.when(t == pl.num_programs(1) - 1)
    def _():
        sout_ref[0] = s_ref[...]


def _rwkv(pb, shift0, s0_bd, wts, nb, t):
    tb = _row_tile(t, 256)
    lc = CHUNK
    assert tb % lc == 0 or tb < lc
    nt = t // tb
    width = pb.shape[1]
    W = BRANCH_W
    n_groups = W // HGW
    row = lambda n: _resident((1, n))
    return pl.pallas_call(
        functools.partial(_rwkv_body, tb=tb, lc=lc),
        grid=(nb, nt),
        in_specs=[pl.BlockSpec((tb, width), lambda b, i: (b * nt + i, 0)),
                  pl.BlockSpec((1, 1, width), lambda b, i: (b, 0, 0)),
                  pl.BlockSpec((1, n_groups, HGW, HGW), lambda b, i: (b, 0, 0, 0)),
                  row(width), row(W), _resident((128, W)), row(W), _resident((128, W)),
                  _resident((128, W)), row(W), row(W), row(W), row(W), row(W), _resident((W, W))],
        out_specs=[pl.BlockSpec((tb, W), lambda b, i: (b * nt + i, 0)),
                   pl.BlockSpec((1, n_groups, HGW, HGW), lambda b, i: (b, 0, 0, 0))],
        out_shape=[jax.ShapeDtypeStruct((nb * t, W), F32),
                   jax.ShapeDtypeStruct((nb, n_groups, HGW, HGW), F32)],
        scratch_shapes=[pltpu.VMEM((tb + SUBLANES, width), F32),
                        pltpu.VMEM((n_groups, HGW, HGW), F32)],
        compiler_params=_cparams("parallel", "arbitrary"),
        name="rwkv7",
    )(pb, shift0, s0_bd, *wts, _seg_matrix(W, B_HD))


def _gla_chunk_terms(q, k, v, gk):
    each = lambda f, *seqs: [f(*xs) for xs in zip(*seqs)]
    L = q[0].shape[0]
    kw = C_HEADS * C_DK
    vw = C_HEADS * C_DV
    tril = _tril_ones(L)
    b = each(lambda x: _split_dot(tril, x, 3), gk)
    qe = each(lambda x, y: x * jnp.exp(y), q, b)
    ke = each(lambda x, y: x * jnp.exp(-y), k, b)
    a_all = each(lambda x, y: _dot_nt(x, _bd_rows(y, C_DK, C_HEADS)), qe, ke)
    row = lax.broadcasted_iota(jnp.int32, (L, C_HEADS * L), 0)
    col = lax.broadcasted_iota(jnp.int32, (L, C_HEADS * L), 1) % L
    causal = col <= row
    o_intra = each(lambda x, y: _dot(jnp.where(causal, x, 0.0), _bd_rows(y, C_DV, C_HEADS)), a_all, v)
    b_last = each(lambda x: x[L - 1:L, :], b)
    kd = each(lambda x, y, z: x * jnp.exp(z - y), k, b, b_last)
    ri = lax.broadcasted_iota(jnp.int32, (vw, kw), 0) // C_DV
    ci = lax.broadcasted_iota(jnp.int32, (vw, kw), 1) // C_DK
    diag = ri == ci
    upd = each(lambda x, y: jnp.where(diag, _dot_tn(x, y), 0.0), v, kd)
    decay = each(jnp.exp, b_last)
    return qe, o_intra, upd, decay


def _gla_body(pc_ref, s0_ref, g2_ref, gb_ref, norm_ref, o_ref, sout_ref,
              s_ref, *, tb, lc):
    t = pl.program_id(1)
    kw = C_HEADS * C_DK
    vw = C_HEADS * C_DV

    @pl.when(t == 0)
    def _():
        s_ref[...] = s0_ref[0]

    pc = pc_ref[...]
    og = pc[:, 2 * kw + vw:2 * kw + 2 * vw]
    gl = pc[:, 2 * kw + 2 * vw:2 * kw + 2 * vw + 128]
    z = _dot(gl, g2_ref[...]) + gb_ref[...]
    q = pc[:, 0:kw] * (C_DK ** -0.5)
    k = pc[:, kw:2 * kw]
    v = pc[:, 2 * kw:2 * kw + vw]
    gk = -_softplus_big(-z) * (1.0 / C_TAU)
    n_chunks = max(tb // lc, 1)

    def piece(x, ci):
        blk = x[ci * lc:min((ci + 1) * lc, tb), :]
        if tb < lc:
            blk = jnp.concatenate([blk, jnp.zeros((lc - tb, x.shape[1]), F32)], axis=0)
        return blk

    qe, o_intra, upd, decay = _gla_chunk_terms(
        *[[piece(x, ci) for ci in range(n_chunks)] for x in (q, k, v, gk)])
    y_rows = []
    for ci in range(n_chunks):
        st = s_ref[...]
        y_rows.append(_dot_nt(qe[ci], st) + o_intra[ci])
        s_ref[...] = st * decay[ci] + upd[ci]
    y = jnp.concatenate(y_rows, axis=0)[0:tb]
    outs = []
    for h in range(C_HEADS):
        yh = y[:, h * C_DV:(h + 1) * C_DV]
        outs.append(yh * lax.rsqrt(jnp.mean(yh * yh, axis=-1, keepdims=True) + EPS) * norm_ref[...])
    yn = jnp.concatenate(outs, axis=1)
    o_ref[...] = yn * (og * _sigmoid(og))

    @pl.when(t == pl.num_programs(1) - 1)
    def _():
        sout_ref[0] = s_ref[...]


def _gla(pc, s0_bd, g2, gb, norm, nb, t):
    tb = _row_tile(t, 256)
    lc = CHUNK
    assert tb % lc == 0 or tb < lc
    nt = t // tb
    width = pc.shape[1]
    kw, vw = C_HEADS * C_DK, C_HEADS * C_DV
    return pl.pallas_call(
        functools.partial(_gla_body, tb=tb, lc=lc),
        grid=(nb, nt),
        in_specs=[pl.BlockSpec((tb, width), lambda b, i: (b * nt + i, 0)),
                  pl.BlockSpec((1, vw, kw), lambda b, i: (b, 0, 0)),
                  _resident((128, kw)), _resident((1, kw)), _resident((1, C_DV))],
        out_specs=[pl.BlockSpec((tb, vw), lambda b, i: (b * nt + i, 0)),
                   pl.BlockSpec((1, vw, kw), lambda b, i: (b, 0, 0))],
        out_shape=[jax.ShapeDtypeStruct((nb * t, vw), F32),
                   jax.ShapeDtypeStruct((nb, vw, kw), F32)],
        scratch_shapes=[pltpu.VMEM((vw, kw), F32)],
        compiler_params=_cparams("parallel", "arbitrary"),
        name="gla",
    )(pc, s0_bd, g2, gb, norm)


def _shift_rows(x, d, fill):
    n = x.shape[0]
    if d % SUBLANES == 0:
        head = jnp.full((d, x.shape[1]), fill, x.dtype)
        return jnp.concatenate([head, x[:n - d]], axis=0)
    rolled = pltpu.roll(x, d, 0)
    row = lax.broadcasted_iota(jnp.int32, x.shape, 0)
    return jnp.where(row < d, fill, rolled)


def _lru_body(pd_ref, conv0_ref, h0_ref, cw_ref, cb_ref, wa_ref, ba_ref, wx_ref, bx_ref, lam_ref,
              o_ref, convout_ref, hout_ref, ext_ref, h_ref, *, tb):
    t = pl.program_id(1)
    W = BRANCH_W

    @pl.when(t == 0)
    def _():
        ext_ref[0:SUBLANES, :] = conv0_ref[0]
        h_ref[...] = h0_ref[0]

    gate = pd_ref[:, 0:W]
    xr = pd_ref[:, W:2 * W]
    ext_ref[SUBLANES:, :] = xr
    xc = cb_ref[...] + xr * cw_ref[CONV_W - 1:CONV_W, :]
    for j in range(CONV_W - 1):
        off = SUBLANES - (CONV_W - 1) + j
        xc = xc + ext_ref[off:off + tb, :] * cw_ref[j:j + 1, :]
    tail = ext_ref[tb:tb + SUBLANES, :]
    ext_ref[0:SUBLANES, :] = tail

    r = _sigmoid(_dot(xc, wa_ref[...]) + ba_ref[...])
    gi = _sigmoid(_dot(xc, wx_ref[...]) + bx_ref[...])
    log_a = (-LRU_C) * r * _softplus(-lam_ref[...])
    a = jnp.exp(log_a)
    u = jnp.sqrt(1.0 - jnp.exp(2.0 * log_a)) * (gi * xc)
    d = 1
    while d < tb:
        u = u + a * _shift_rows(u, d, 0.0)
        a = a * _shift_rows(a, d, 1.0)
        d *= 2
    h = u + a * h_ref[...]
    h_ref[...] = h[tb - 1:tb, :]
    o_ref[...] = h * (0.5 * gate * (1.0 + jnp.tanh(math.sqrt(2.0 / math.pi)
                                                   * (gate + 0.044715 * gate * gate * gate))))

    @pl.when(t == pl.num_programs(1) - 1)
    def _():
        convout_ref[0] = tail
        hout_ref[0] = h[tb - 1:tb, :]


def _lru(pd, conv0, h0, wts, nb, t):
    tb = _row_tile(t, 256)
    nt = t // tb
    W = BRANCH_W
    row = lambda: _resident((1, W))
    return pl.pallas_call(
        functools.partial(_lru_body, tb=tb),
        grid=(nb, nt),
        in_specs=[pl.BlockSpec((tb, 2 * W), lambda b, i: (b * nt + i, 0)),
                  pl.BlockSpec((1, SUBLANES, W), lambda b, i: (b, 0, 0)),
                  pl.BlockSpec((1, 1, W), lambda b, i: (b, 0, 0)),
                  _resident((CONV_W, W)), row(), _resident((W, W)), row(), _resident((W, W)), row(),
                  row()],
        out_specs=[pl.BlockSpec((tb, W), lambda b, i: (b * nt + i, 0)),
                   pl.BlockSpec((1, SUBLANES, W), lambda b, i: (b, 0, 0)),
                   pl.BlockSpec((1, 1, W), lambda b, i: (b, 0, 0))],
        out_shape=[jax.ShapeDtypeStruct((nb * t, W), F32),
                   jax.ShapeDtypeStruct((nb, SUBLANES, W), F32),
                   jax.ShapeDtypeStruct((nb, 1, W), F32)],
        scratch_shapes=[pltpu.VMEM((tb + SUBLANES, W), F32), pltpu.VMEM((1, W), F32)],
        compiler_params=_cparams("parallel", "arbitrary"),
        name="rglru",
    )(pd, conv0, h0, *wts)


def _block_diag(w):
    n, a, b = w.shape
    eye = jnp.eye(n, dtype=w.dtype)
    return (eye[:, None, :, None] * w[:, :, None, :]).reshape(n * a, n * b)


def _pad_rows_to(w, rows):
    return jnp.pad(w, ((0, rows - w.shape[0]), (0, 0)))


def _layer_weights(i, W):
    bf = lambda a: a.astype(BF16)
    row = lambda a: a.reshape(1, -1).astype(F32)
    bw = BRANCH_W
    a_cols = A_HEADS * A_HD + 2 * A_KV_HEADS * A_HD
    b_cols = 3 * bw + B_W_RANK + B_A_RANK + B_G_RANK
    c_cols = 2 * C_HEADS * C_DK + C_HEADS * C_DV + C_G_RANK + bw
    w_in = W['w_in'][i]
    d = w_in.shape[0]
    wa = w_in[:, :a_cols]
    wb = w_in[:, a_cols:a_cols + b_cols]
    wc = w_in[:, a_cols + b_cols:a_cols + b_cols + c_cols]
    wd = w_in[:, a_cols + b_cols + c_cols:]
    zpad = lambda n: jnp.zeros((d, n), w_in.dtype)
    wb2 = jnp.concatenate([wb[:, :3 * bw], wb[:, 3 * bw:3 * bw + B_W_RANK], zpad(128 - B_W_RANK),
                           wb[:, 3 * bw + B_W_RANK:3 * bw + B_W_RANK + B_A_RANK], zpad(128 - B_A_RANK),
                           wb[:, 3 * bw + B_W_RANK + B_A_RANK:]], axis=1)
    mu = W['rwkv_mu'][i]
    z1 = lambda n: jnp.zeros((n,), F32)
    mu2 = jnp.concatenate([mu[:3 * bw], mu[3 * bw:3 * bw + B_W_RANK], z1(128 - B_W_RANK),
                           mu[3 * bw + B_W_RANK:3 * bw + B_W_RANK + B_A_RANK], z1(128 - B_A_RANK),
                           mu[3 * bw + B_W_RANK + B_A_RANK:]])
    qkv = 2 * C_HEADS * C_DK + C_HEADS * C_DV
    wc2 = jnp.concatenate([wc[:, :qkv], wc[:, qkv + C_G_RANK:], wc[:, qkv:qkv + C_G_RANK],
                           zpad(128 - C_G_RANK)], axis=1)
    return dict(
        ffn1=(row(W['g_ffn1'][i]), bf(W['w_ffn1_gate'][i]), bf(W['w_ffn1_up'][i]), bf(W['w_ffn1_down'][i])),
        ffn2=(row(W['g_ffn2'][i]), bf(W['w_ffn2_gate'][i]), bf(W['w_ffn2_up'][i]), bf(W['w_ffn2_down'][i])),
        g_mix=row(W['g_mix'][i]),
        w_in=(bf(wa), bf(wb2), bf(wc2), bf(wd)),
        q_norm=row(jnp.tile(W['q_norm'][i], A_HEADS)),
        k_norm=row(jnp.tile(W['k_norm'][i], A_KV_HEADS)),
        sink=W['attn_sink'][i].astype(F32),
        rwkv=(row(mu2), row(W['rwkv_w0'][i]), bf(_pad_rows_to(W['rwkv_w2'][i], 128)),
              row(W['rwkv_a0'][i]), bf(_pad_rows_to(W['rwkv_a2'][i], 128)), bf(W['rwkv_g2'][i]),
              row(W['rwkv_k_k'][i]), row(W['rwkv_k_a'][i]), row(W['rwkv_r_k'][i]),
              row(W['rwkv_ln_g'][i]), row(W['rwkv_ln_b'][i])),
        gla=(bf(_pad_rows_to(W['gla_g2'][i], 128)), row(W['gla_gb'][i]), row(W['gla_norm'][i])),
        lru=(W['lru_conv_w'][i].astype(F32), row(W['lru_conv_b'][i]),
             bf(_block_diag(W['lru_wa'][i])), row(W['lru_ba'][i]),
             bf(_block_diag(W['lru_wx'][i])), row(W['lru_bx'][i]), row(W['lru_lambda'][i])),
        merge=(bf(W['w_merge_gate'][i]), bf(W['w_branch'][i]), bf(W['w_out'][i])),
        ple=(row(W['g_ple'][i]), bf(W['w_ple_gate'][i]), bf(W['w_ple_proj'][i])),
    )


def _b_cols_to_kernel(x):
    bw = BRANCH_W
    z = jnp.zeros(x.shape[:-1] + (128 - B_W_RANK,), x.dtype)
    return jnp.concatenate([x[..., :3 * bw], x[..., 3 * bw:3 * bw + B_W_RANK], z,
                            x[..., 3 * bw + B_W_RANK:3 * bw + B_W_RANK + B_A_RANK], z,
                            x[..., 3 * bw + B_W_RANK + B_A_RANK:]], axis=-1)


def _b_cols_from_kernel(x):
    bw = BRANCH_W
    return jnp.concatenate([x[..., :3 * bw + B_W_RANK], x[..., 3 * bw + 128:3 * bw + 128 + B_A_RANK],
                            x[..., 3 * bw + 256:]], axis=-1)


def _heads_to_bd(s, hg):
    nb, h, a, b = s.shape
    s = s.reshape(nb, h // hg, hg, a, b)
    eye = jnp.eye(hg, dtype=s.dtype)
    return (eye[None, None, :, None, :, None] * s[:, :, :, :, None, :]).reshape(nb, h // hg, hg * a, hg * b)


def _bd_to_heads(s, hg):
    nb, ng, ra, cb = s.shape
    a, b = ra // hg, cb // hg
    s = s.reshape(nb, ng, hg, a, hg, b)
    idx = jnp.arange(hg)
    return s[:, :, idx, :, idx, :].transpose(1, 2, 0, 3, 4).reshape(nb, ng * hg, a, b)


def _trunk_layer(x, pe, st, lw, bias, first_chunk):
    nb, t, d = x.shape
    n = nb * t
    ck, cv, shift0, s_rwkv0, s_gla0, conv0, lru0 = st
    x2 = x.reshape(n, d)
    x2 = _ffn(x2, *lw['ffn1'])
    pa, pb, pc, pd = _inproj(x2, lw['g_mix'], lw['w_in'])
    kw = A_KV_HEADS * A_HD
    n_q = bias.shape[1]
    bias2 = bias.reshape(A_HEADS * n_q, bias.shape[2])
    sink2 = jnp.repeat(lw['sink'], n_q).reshape(A_HEADS * n_q, 1)
    if first_chunk:
        o_a, k_win, v_win = _attn_prompt(pa, bias2, sink2, lw['q_norm'], lw['k_norm'], nb, t)
    else:
        o_a, k_win, v_win = _attn_sample(pa, ck.reshape(nb, WINDOW, kw), cv.reshape(nb, WINDOW, kw),
                                         bias2, sink2, lw['q_norm'], lw['k_norm'], nb, t)
    k_win = k_win.reshape(nb, WINDOW, A_KV_HEADS, A_HD)
    v_win = v_win.reshape(nb, WINDOW, A_KV_HEADS, A_HD)

    o_b, s_bd = _rwkv(pb, _b_cols_to_kernel(shift0)[:, None, :], _heads_to_bd(s_rwkv0, HG),
                      lw['rwkv'], nb, t)
    shift1 = _b_cols_from_kernel(pb.reshape(nb, t, -1)[:, -1])
    s_rwkv1 = _bd_to_heads(s_bd, HG)

    st_bd0 = _heads_to_bd(jnp.swapaxes(s_gla0, -1, -2), C_HEADS)[:, 0]
    o_c, st_bd = _gla(pc, st_bd0, *lw['gla'], nb, t)
    s_gla1 = jnp.swapaxes(_bd_to_heads(st_bd[:, None], C_HEADS), -1, -2)

    conv_pad = jnp.pad(conv0, ((0, 0), (SUBLANES - (CONV_W - 1), 0), (0, 0)))
    o_d, conv_out, h_out = _lru(pd, conv_pad, lru0[:, None, :], lw['lru'], nb, t)
    conv1 = conv_out[:, SUBLANES - (CONV_W - 1):]
    lru1 = h_out[:, 0]

    x2 = _merge(x2, (o_a, o_b, o_c, o_d), lw['g_mix'], *lw['merge'])
    x2 = _ffn(x2, *lw['ffn2'])
    x2 = _ple(x2, pe.reshape(n, -1), *lw['ple'])
    return x2.reshape(nb, t, d), (k_win, v_win, shift1, s_rwkv1, s_gla1, conv1, lru1)


def kernel(x_prompt, x_sample, cache_attn_k, cache_attn_v, state_rwkv_shift, state_rwkv, state_gla, state_lru_conv, state_lru, p_prompt, p_sample, rel_bias_table, g_ffn1, w_ffn1_gate, w_ffn1_up, w_ffn1_down, g_mix, w_in, q_norm, k_norm, attn_sink, rwkv_mu, rwkv_w0, rwkv_w2, rwkv_a0, rwkv_a2, rwkv_g2, rwkv_k_k, rwkv_k_a, rwkv_r_k, rwkv_ln_g, rwkv_ln_b, gla_g2, gla_gb, gla_norm, lru_conv_w, lru_conv_b, lru_wa, lru_ba, lru_wx, lru_bx, lru_lambda, w_merge_gate, w_branch, w_out, g_ffn2, w_ffn2_gate, w_ffn2_up, w_ffn2_down, g_ple, w_ple_gate, w_ple_proj):
    W = dict(g_ffn1=g_ffn1, w_ffn1_gate=w_ffn1_gate, w_ffn1_up=w_ffn1_up, w_ffn1_down=w_ffn1_down,
             g_mix=g_mix, w_in=w_in, q_norm=q_norm, k_norm=k_norm, attn_sink=attn_sink,
             rwkv_mu=rwkv_mu, rwkv_w0=rwkv_w0, rwkv_w2=rwkv_w2, rwkv_a0=rwkv_a0, rwkv_a2=rwkv_a2,
             rwkv_g2=rwkv_g2, rwkv_k_k=rwkv_k_k, rwkv_k_a=rwkv_k_a, rwkv_r_k=rwkv_r_k,
             rwkv_ln_g=rwkv_ln_g, rwkv_ln_b=rwkv_ln_b, gla_g2=gla_g2, gla_gb=gla_gb,
             gla_norm=gla_norm, lru_conv_w=lru_conv_w, lru_conv_b=lru_conv_b, lru_wa=lru_wa,
             lru_ba=lru_ba, lru_wx=lru_wx, lru_bx=lru_bx, lru_lambda=lru_lambda,
             w_merge_gate=w_merge_gate, w_branch=w_branch, w_out=w_out, g_ffn2=g_ffn2,
             w_ffn2_gate=w_ffn2_gate, w_ffn2_up=w_ffn2_up, w_ffn2_down=w_ffn2_down,
             g_ple=g_ple, w_ple_gate=w_ple_gate, w_ple_proj=w_ple_proj)
    depth = w_in.shape[0]
    dt = x_prompt.dtype
    bp, tp = x_prompt.shape[:2]
    ts = x_sample.shape[1]
    b_cols = state_rwkv_shift.shape[-1]
    bias_p = _rel_bias(rel_bias_table, CHUNK, WINDOW + CHUNK)
    bias_s = _rel_bias(rel_bias_table, ts, WINDOW + ts)
    yp, ys = x_prompt, x_sample
    st_p, st_s = [], []
    for i in range(depth):
        lw = _layer_weights(i, W)
        zero_st = (None, None,
                   jnp.zeros((bp, b_cols), dt),
                   jnp.zeros((bp, B_HEADS, B_HD, B_HD), dt),
                   jnp.zeros((bp, C_HEADS, C_DK, C_DV), dt),
                   jnp.zeros((bp, CONV_W - 1, BRANCH_W), dt),
                   jnp.zeros((bp, BRANCH_W), dt))
        yp, sp = _trunk_layer(yp, p_prompt[i], zero_st, lw, bias_p, True)
        cache_st = (cache_attn_k[i], cache_attn_v[i], state_rwkv_shift[i], state_rwkv[i],
                    state_gla[i], state_lru_conv[i], state_lru[i])
        ys, ss = _trunk_layer(ys, p_sample[i], cache_st, lw, bias_s, False)
        st_p.append(sp)
        st_s.append(ss)
    stack = lambda states, j: jnp.stack([s[j] for s in states])
    return (yp, ys) + tuple(stack(st_p, j) for j in range(7)) + tuple(stack(st_s, j) for j in range(7))
```

```python
import functools
import math

import numpy as np
import jax
import jax.numpy as jnp
from jax import lax
from jax.experimental import pallas as pl
from jax.experimental.pallas import tpu as pltpu

F32 = jnp.float32
BF16 = jnp.bfloat16

V7X_VMEM_BYTES = 64 * 1024 * 1024
VMEM_LIMIT = V7X_VMEM_BYTES - 8 * 1024 * 1024
SUBLANES = 8
LANES = 128

EPS = 1e-6
NEG_INF = -1e30
CHUNK = 64
WINDOW = 128
N_BUCKETS = 32
MAX_DIST = 128
A_HEADS, A_KV_HEADS, A_HD = 8, 2, 64
A_GROUP = A_HEADS // A_KV_HEADS
B_HEADS, B_HD = 8, 64
B_W_RANK, B_A_RANK, B_G_RANK = 64, 64, 128
RWKV_GN_EPS = 64e-5
C_HEADS, C_DK, C_DV = 4, 64, 128
C_G_RANK = 16
C_TAU = 16.0
D_BLOCKS = 8
CONV_W = 4
LRU_C = 8.0
BRANCH_W = 512
HG = 4
HGW = HG * B_HD


def _cparams(*sem):
    return pltpu.CompilerParams(dimension_semantics=sem, vmem_limit_bytes=VMEM_LIMIT)


def _resident(shape):
    nd = len(shape)
    return pl.BlockSpec(shape, lambda *_: (0,) * nd, pipeline_mode=pl.Buffered(1))


def _row_tile(n, cap):
    t = min(n, cap)
    while n % t:
        t //= 2
    return t


def _dot(a, b):
    return jnp.dot(a.astype(BF16), b.astype(BF16), preferred_element_type=F32)


def _dot_nt(a, b):
    return lax.dot_general(a.astype(BF16), b.astype(BF16), (((1,), (1,)), ((), ())),
                           preferred_element_type=F32)


def _dot_tn(a, b):
    return lax.dot_general(a.astype(BF16), b.astype(BF16), (((0,), (0,)), ((), ())),
                           preferred_element_type=F32)


def _rms(x, g):
    return x * lax.rsqrt(jnp.mean(x * x, axis=-1, keepdims=True) + EPS) * g


def _sigmoid(x):
    return 1.0 / (1.0 + jnp.exp(-x))


def _softplus(x):
    return jnp.maximum(x, 0.0) + jnp.log1p(jnp.exp(-jnp.abs(x)))


def _softplus_big(x):
    return jnp.maximum(x, 0.0) + jnp.log(1.0 + jnp.exp(-jnp.abs(x)))


def _split_dot(e_lhs, x, terms):
    acc = None
    rem = x
    for n in range(terms):
        piece = rem.astype(BF16)
        d = jnp.dot(e_lhs, piece, preferred_element_type=F32)
        acc = d if acc is None else acc + d
        if n + 1 < terms:
            rem = rem - piece.astype(F32)
    return acc


def _segsum(x, e):
    n = x.shape[0]
    blk = e.shape[0]
    hi = x.astype(BF16)
    lo = (x - hi.astype(F32)).astype(BF16)
    parts = jnp.concatenate([hi, lo], axis=0)
    cols = []
    for j in range(0, x.shape[1], blk):
        d = jnp.dot(parts[:, j:j + blk], e, preferred_element_type=F32)
        cols.append(d[:n] + d[n:])
    return cols[0] if len(cols) == 1 else jnp.concatenate(cols, axis=1)


def _tril_ones(n):
    r = lax.broadcasted_iota(jnp.int32, (n, n), 0)
    c = lax.broadcasted_iota(jnp.int32, (n, n), 1)
    return jnp.where(r >= c, 1.0, 0.0).astype(BF16)


def _bd_rows(x, blk, nblk):
    lane_blk = lax.broadcasted_iota(jnp.int32, x.shape, 1) // blk
    return jnp.concatenate([jnp.where(lane_blk == h, x, 0.0) for h in range(nblk)], axis=0)


def _seg_matrix(width, seg):
    i = np.arange(width)
    return jnp.asarray((i[:, None] // seg) == (i[None, :] // seg), dtype=BF16)


def _swiglu_residual(x, g_ref, wg_ref, wu_ref, wd_ref, fchunk):
    h = _rms(x, g_ref[...]).astype(BF16)
    acc = None
    for c in range(wg_ref.shape[1] // fchunk):
        sl = pl.ds(c * fchunk, fchunk)
        gt = jnp.dot(h, wg_ref[:, sl], preferred_element_type=F32)
        up = jnp.dot(h, wu_ref[:, sl], preferred_element_type=F32)
        act = (gt * _sigmoid(gt) * up).astype(BF16)
        d = jnp.dot(act, wd_ref[sl, :], preferred_element_type=F32)
        acc = d if acc is None else acc + d
    return x + 0.5 * acc


def _ffn_body(x_ref, g_ref, wg_ref, wu_ref, wd_ref, o_ref, *, fchunk):
    o_ref[...] = _swiglu_residual(x_ref[...], g_ref, wg_ref, wu_ref, wd_ref, fchunk)


def _ffn_ple_body(x_ref, pe_ref, g_ref, wg_ref, wu_ref, wd_ref, gp_ref, wpg_ref, wpp_ref, o_ref, *,
                  fchunk):
    x = _swiglu_residual(x_ref[...], g_ref, wg_ref, wu_ref, wd_ref, fchunk)
    h = _rms(x, gp_ref[...]).astype(BF16)
    gate = _sigmoid(jnp.dot(h, wpg_ref[...], preferred_element_type=F32))
    o_ref[...] = x + gate * jnp.dot(pe_ref[...].astype(BF16), wpp_ref[...], preferred_element_type=F32)


def _ffn(x, g, wg, wu, wd):
    n, d = x.shape
    f = wg.shape[1]
    tm = _row_tile(n, 512)
    fchunk = f // 2 if (f // 2) % 128 == 0 else f
    return pl.pallas_call(
        functools.partial(_ffn_body, fchunk=fchunk),
        grid=(n // tm,),
        in_specs=[pl.BlockSpec((tm, d), lambda i: (i, 0)),
                  _resident((1, d)), _resident((d, f)), _resident((d, f)), _resident((f, d))],
        out_specs=pl.BlockSpec((tm, d), lambda i: (i, 0)),
        out_shape=jax.ShapeDtypeStruct((n, d), F32),
        compiler_params=_cparams("parallel"),
        name="ffn",
    )(x, g, wg, wu, wd)


def _inproj_body(x_ref, g_ref, wa_ref, wb_ref, wc_ref, wd_ref, pa_ref, pb_ref, pc_ref, pd_ref):
    h = _rms(x_ref[...], g_ref[...]).astype(BF16)
    for w_ref, p_ref in ((wa_ref, pa_ref), (wb_ref, pb_ref), (wc_ref, pc_ref), (wd_ref, pd_ref)):
        p_ref[...] = jnp.dot(h, w_ref[...], preferred_element_type=F32)


def _inproj(x, g, ws):
    n, d = x.shape
    tm = _row_tile(n, 512)
    widths = [w.shape[1] for w in ws]
    return pl.pallas_call(
        _inproj_body,
        grid=(n // tm,),
        in_specs=[pl.BlockSpec((tm, d), lambda i: (i, 0)), _resident((1, d))]
                 + [_resident((d, w)) for w in widths],
        out_specs=[pl.BlockSpec((tm, w), lambda i: (i, 0)) for w in widths],
        out_shape=[jax.ShapeDtypeStruct((n, w), F32) for w in widths],
        compiler_params=_cparams("parallel"),
        name="inproj",
    )(x, g, *ws)


def _merge_body(x_ref, oa_ref, ob_ref, oc_ref, od_ref, g_ref, wmg_ref, wb_ref, wo_ref, o_ref):
    x = x_ref[...]
    h = _rms(x, g_ref[...]).astype(BF16)
    y = None
    for n, b_ref in enumerate((oa_ref, ob_ref, oc_ref, od_ref)):
        gate = _sigmoid(jnp.dot(h, wmg_ref[n], preferred_element_type=F32))
        t = gate * jnp.dot(b_ref[...].astype(BF16), wb_ref[n], preferred_element_type=F32)
        y = t if y is None else y + t
    o_ref[...] = x + jnp.dot(y.astype(BF16), wo_ref[...], preferred_element_type=F32)


def _merge(x, outs, g, wmg, wb, wo):
    n, d = x.shape
    bw = outs[0].shape[1]
    tm = _row_tile(n, 512)
    return pl.pallas_call(
        _merge_body,
        grid=(n // tm,),
        in_specs=[pl.BlockSpec((tm, d), lambda i: (i, 0))]
                 + [pl.BlockSpec((tm, bw), lambda i: (i, 0))] * 4
                 + [_resident((1, d)), _resident(wmg.shape), _resident(wb.shape), _resident(wo.shape)],
        out_specs=pl.BlockSpec((tm, d), lambda i: (i, 0)),
        out_shape=jax.ShapeDtypeStruct((n, d), F32),
        compiler_params=_cparams("parallel"),
        name="merge",
    )(x, *outs, g, wmg, wb, wo)


def _ffn_ple(x, pe, g, wg, wu, wd, gp, wpg, wpp):
    n, d = x.shape
    f = wg.shape[1]
    pd = pe.shape[1]
    tm = _row_tile(n, 512)
    fchunk = f // 2 if (f // 2) % 128 == 0 else f
    return pl.pallas_call(
        functools.partial(_ffn_ple_body, fchunk=fchunk),
        grid=(n // tm,),
        in_specs=[pl.BlockSpec((tm, d), lambda i: (i, 0)), pl.BlockSpec((tm, pd), lambda i: (i, 0)),
                  _resident((1, d)), _resident((d, f)), _resident((d, f)), _resident((f, d)),
                  _resident((1, d)), _resident((d, d)), _resident((pd, d))],
        out_specs=pl.BlockSpec((tm, d), lambda i: (i, 0)),
        out_shape=jax.ShapeDtypeStruct((n, d), F32),
        compiler_params=_cparams("parallel"),
        name="ffn_ple",
    )(x, pe, g, wg, wu, wd, gp, wpg, wpp)


def _t5_bucket_np(rel):
    half = N_BUCKETS // 2
    max_exact = half // 2
    ret = np.where(rel > 0, half, 0)
    n = np.abs(rel)
    nf = np.maximum(n, 1).astype(np.float32)
    large = max_exact + (np.log(nf / np.float32(max_exact)) / np.float32(math.log(MAX_DIST / max_exact))
                         * np.float32(half - max_exact)).astype(np.int32)
    large = np.minimum(large, half - 1)
    return (ret + np.where(n < max_exact, n, large)).astype(np.int32)


def _bias_body(idx_ref, table_ref, o_ref):
    idx = idx_ref[...]
    for h in range(A_HEADS):
        acc = jnp.zeros(idx.shape, F32)
        for b in range(N_BUCKETS):
            acc = jnp.where(idx == b, table_ref[b, h], acc)
        o_ref[h] = acc


def _rel_bias(table, n_q, n_k):
    rel = np.arange(n_k)[:, None] - WINDOW - np.arange(n_q)[None, :]
    idx = jnp.asarray(_t5_bucket_np(rel))
    out = pl.pallas_call(
        _bias_body,
        in_specs=[pl.BlockSpec(memory_space=pltpu.VMEM), pl.BlockSpec(memory_space=pltpu.SMEM)],
        out_specs=pl.BlockSpec(memory_space=pltpu.VMEM),
        out_shape=jax.ShapeDtypeStruct((A_HEADS, n_k, n_q), F32),
        name="rel_bias",
    )(idx, table)
    return jnp.transpose(out, (1, 0, 2)).reshape(n_k, A_HEADS * n_q)


def _head_rms(x, e, g):
    ms = _segsum(x * x, e) * (1.0 / A_HD)
    return x * lax.rsqrt(ms + EPS) * g


def _kv_replication():
    lane = np.arange(A_HEADS * A_HD)
    src = (lane // (A_GROUP * A_HD)) * A_HD + lane % A_HD
    return jnp.asarray(np.arange(A_KV_HEADS * A_HD)[:, None] == src[None, :], dtype=BF16)


def _attend_blocks(q_list, k8_list, v8_list, bias, sink, valid_list):
    n = q_list[0].shape[0]
    lane_head = lax.broadcasted_iota(jnp.int32, (n, A_HEADS * A_HD), 1) // A_HD
    q_bd = [jnp.concatenate([jnp.where(lane_head == h, q, 0.0) for h in range(A_HEADS)], axis=0)
            for q in q_list]
    s = [_dot_nt(k8, qb) + bias for k8, qb in zip(k8_list, q_bd)]
    s = [x if valid is None else jnp.where(valid, x, NEG_INF) for x, valid in zip(s, valid_list)]
    m = [jnp.maximum(jnp.max(x, axis=0, keepdims=True), sink) for x in s]
    e = [jnp.exp(x - mm) for x, mm in zip(s, m)]
    rinv = [1.0 / (jnp.sum(x, axis=0, keepdims=True) + jnp.exp(sink - mm)) for x, mm in zip(e, m)]
    p = [x * r for x, r in zip(e, rinv)]
    o8 = [_dot_tn(pp, v8) for pp, v8 in zip(p, v8_list)]
    outs = []
    for x in o8:
        o = None
        for h in range(A_HEADS):
            part = jnp.where(lane_head == h, x[h * n:(h + 1) * n], 0.0)
            o = part if o is None else o + part
        outs.append(o)
    return outs


def _attn_prompt_body(main_ref, prev_ref, bias_ref, sink_ref, qn_ref, kn_ref, eq_ref, ek_ref, rep_ref,
                      o_ref, kout_ref, vout_ref, *, tb):
    i = pl.program_id(1)
    qw = A_HEADS * A_HD
    kw = A_KV_HEADS * A_HD
    band = WINDOW + CHUNK
    ek = ek_ref[...]
    rep = rep_ref[...]
    q = _head_rms(main_ref[:, 0:qw], eq_ref[...], qn_ref[...]) * (A_HD ** -0.5)
    kf = jnp.concatenate([_head_rms(prev_ref[:, qw:qw + kw], ek, kn_ref[...]),
                          _head_rms(main_ref[:, qw:qw + kw], ek, kn_ref[...])], axis=0)
    vf = jnp.concatenate([prev_ref[:, qw + kw:qw + 2 * kw], main_ref[:, qw + kw:qw + 2 * kw]], axis=0)
    k8 = jnp.dot(kf.astype(BF16), rep, preferred_element_type=F32).astype(BF16)
    v8 = jnp.dot(vf.astype(BF16), rep, preferred_element_type=F32).astype(BF16)
    kidx = lax.broadcasted_iota(jnp.int32, (band, A_HEADS * CHUNK), 0)
    n_chunks = tb // CHUNK
    valid = [(kidx + (i * tb + c * CHUNK - WINDOW)) >= 0 if c * CHUNK < WINDOW else None
             for c in range(n_chunks)]
    outs = _attend_blocks([q[c * CHUNK:(c + 1) * CHUNK] for c in range(n_chunks)],
                          [k8[c * CHUNK:c * CHUNK + band] for c in range(n_chunks)],
                          [v8[c * CHUNK:c * CHUNK + band] for c in range(n_chunks)],
                          bias_ref[...], sink_ref[...], valid)
    for c in range(n_chunks):
        o_ref[c * CHUNK:(c + 1) * CHUNK, :] = outs[c]

    @pl.when(i == pl.num_programs(1) - 1)
    def _():
        kout_ref[0] = kf[tb:tb + WINDOW, :]
        vout_ref[0] = vf[tb:tb + WINDOW, :]


def _attn_prompt(pa, bias, sink, qn, kn, nb, t):
    tb = _row_tile(t, 512)
    assert tb % WINDOW == 0
    nq = t // tb
    wpb = tb // WINDOW
    qw, kw = A_HEADS * A_HD, A_KV_HEADS * A_HD
    width = pa.shape[1]
    return pl.pallas_call(
        functools.partial(_attn_prompt_body, tb=tb),
        grid=(nb, nq),
        in_specs=[pl.BlockSpec((tb, width), lambda b, i: (b * nq + i, 0)),
                  pl.BlockSpec((WINDOW, width),
                               lambda b, i: (jnp.maximum((b * nq + i) * wpb - 1, 0), 0)),
                  _resident(bias.shape), _resident(sink.shape),
                  _resident((1, qw)), _resident((1, kw)), _resident((HGW, HGW)), _resident((kw, kw)),
                  _resident((kw, qw))],
        out_specs=[pl.BlockSpec((tb, qw), lambda b, i: (b * nq + i, 0)),
                   pl.BlockSpec((1, WINDOW, kw), lambda b, i: (b, 0, 0)),
                   pl.BlockSpec((1, WINDOW, kw), lambda b, i: (b, 0, 0))],
        out_shape=[jax.ShapeDtypeStruct((nb * t, qw), F32),
                   jax.ShapeDtypeStruct((nb, WINDOW, kw), F32),
                   jax.ShapeDtypeStruct((nb, WINDOW, kw), F32)],
        compiler_params=_cparams("parallel", "arbitrary"),
        name="attn_prompt",
    )(pa, pa, bias, sink, qn, kn, _seg_matrix(HGW, A_HD), _seg_matrix(kw, A_HD), _kv_replication())


def _attn_sample_body(pa_ref, ck_ref, cv_ref, bias_ref, sink_ref, qn_ref, kn_ref, eq_ref, ek_ref, rep_ref,
                      o_ref, kout_ref, vout_ref, *, s, gb):
    qw = A_HEADS * A_HD
    kw = A_KV_HEADS * A_HD
    rep = rep_ref[...]
    q = _head_rms(pa_ref[:, 0:qw], eq_ref[...], qn_ref[...]) * (A_HD ** -0.5)
    kn = _head_rms(pa_ref[:, qw:qw + kw], ek_ref[...], kn_ref[...])
    vn = pa_ref[:, qw + kw:qw + 2 * kw]
    q_list, k8_list, v8_list = [], [], []
    for b in range(gb):
        kf = jnp.concatenate([ck_ref[b], kn[b * s:(b + 1) * s]], axis=0)
        vf = jnp.concatenate([cv_ref[b], vn[b * s:(b + 1) * s]], axis=0)
        kout_ref[b] = kf[s:s + WINDOW, :]
        vout_ref[b] = vf[s:s + WINDOW, :]
        q_list.append(q[b * s:(b + 1) * s])
        k8_list.append(jnp.dot(kf.astype(BF16), rep, preferred_element_type=F32).astype(BF16))
        v8_list.append(jnp.dot(vf.astype(BF16), rep, preferred_element_type=F32).astype(BF16))
    outs = _attend_blocks(q_list, k8_list, v8_list, bias_ref[...], sink_ref[...], [None] * gb)
    for b in range(gb):
        o_ref[b * s:(b + 1) * s, :] = outs[b]


def _attn_sample(pa, ck, cv, bias, sink, qn, kn, nb, s):
    qw, kw = A_HEADS * A_HD, A_KV_HEADS * A_HD
    width = pa.shape[1]
    gb = _row_tile(nb, 8)
    return pl.pallas_call(
        functools.partial(_attn_sample_body, s=s, gb=gb),
        grid=(nb // gb,),
        in_specs=[pl.BlockSpec((gb * s, width), lambda b: (b, 0)),
                  pl.BlockSpec((gb, WINDOW, kw), lambda b: (b, 0, 0)),
                  pl.BlockSpec((gb, WINDOW, kw), lambda b: (b, 0, 0)),
                  _resident(bias.shape), _resident(sink.shape),
                  _resident((1, qw)), _resident((1, kw)), _resident((HGW, HGW)), _resident((kw, kw)),
                  _resident((kw, qw))],
        out_specs=[pl.BlockSpec((gb * s, qw), lambda b: (b, 0)),
                   pl.BlockSpec((gb, WINDOW, kw), lambda b: (b, 0, 0)),
                   pl.BlockSpec((gb, WINDOW, kw), lambda b: (b, 0, 0))],
        out_shape=[jax.ShapeDtypeStruct((nb * s, qw), F32),
                   jax.ShapeDtypeStruct((nb, WINDOW, kw), F32),
                   jax.ShapeDtypeStruct((nb, WINDOW, kw), F32)],
        compiler_params=_cparams("parallel"),
        name="attn_sample",
    )(pa, ck, cv, bias, sink, qn, kn, _seg_matrix(HGW, A_HD), _seg_matrix(kw, A_HD), _kv_replication())


def _rwkv_chunk_terms(r, lw, k, v, kk, a):
    each = lambda f, *seqs: [f(*xs) for xs in zip(*seqs)]
    cat0 = lambda *xs: jnp.concatenate(xs, axis=0)
    cat1 = lambda *xs: jnp.concatenate(xs, axis=1)
    bd = lambda x: _bd_rows(x, B_HD, HG)
    L = r[0].shape[0]
    tril = _tril_ones(L)
    c = each(lambda x: _split_dot(tril, x, 3), lw)
    c_last = each(lambda x: x[L - 1:L, :], c)
    e_last = each(lambda x, xl: jnp.exp(xl - x), c, c_last)
    beta = each(lambda x, y: x * y, kk, a)
    at = each(lambda x, cc, l: -x * jnp.exp(cc - l), kk, c, lw)
    rt = each(lambda x, cc: x * jnp.exp(cc), r, c)
    enc = each(lambda cc: jnp.exp(-cc), c)
    ar = each(cat0, at, rt)
    mb = each(lambda x, b, e: _dot_nt(x, bd(b * e)), ar, beta, enc)
    mk = each(lambda x, b, e: _dot_nt(x, bd(b * e)), ar, k, enc)
    row = lax.broadcasted_iota(jnp.int32, (L, HG * L), 0)
    col = lax.broadcasted_iota(jnp.int32, (L, HG * L), 1) % L
    strict = col < row
    incl = col <= row
    m_b = each(lambda x: jnp.where(strict, x[:L], 0.0), mb)
    m_k = each(lambda x: jnp.where(strict, x[:L], 0.0), mk)
    n_b = each(lambda x: jnp.where(incl, x[L:], 0.0), mb)
    n_k = each(lambda x: jnp.where(incl, x[L:], 0.0), mk)
    eye = jnp.where(col == row, 1.0, 0.0)
    t_inv = each(lambda x: eye + x, m_b)
    p = m_b
    for lvl in range(1, int(math.log2(L))):
        p_bd = each(lambda x: _bd_rows(x, L, HG), p)
        if lvl == 1:
            p = each(_dot, p, p_bd)
        else:
            tp = each(lambda t, x, xb: _dot(cat0(t, x), xb), t_inv, p, p_bd)
            t_inv = each(lambda t, x: t + x[:L], t_inv, tp)
            p = each(lambda x: x[L:], tp)
    t_inv = each(lambda t, x: t + _dot(t, _bd_rows(x, L, HG)), t_inv, p)
    v_bd = each(bd, v)
    mkv = each(_dot, m_k, v_bd)
    wu = each(lambda t, x, y: _dot(t, cat1(bd(x), bd(y))), t_inv, at, mkv)
    w1 = each(lambda x: x[:, :HGW], wu)
    u0 = each(lambda x: x[:, HGW:], wu)
    nbo = each(lambda n, x, y: _dot(n, cat1(bd(x), bd(y))), n_b, w1, u0)
    q1 = each(lambda x, y: x + y[:, :HGW], rt, nbo)
    y0 = each(lambda x, n, vb: x[:, HGW:] + _dot(n, vb), nbo, n_k, v_bd)
    bh = each(lambda x, y: x * y, beta, e_last)
    kh = each(lambda x, y: x * y, k, e_last)
    ri = lax.broadcasted_iota(jnp.int32, (HGW, HGW), 0) // B_HD
    ci = lax.broadcasted_iota(jnp.int32, (HGW, HGW), 1) // B_HD
    diag = ri == ci
    g_mat = each(lambda x, y: jnp.where(diag, _dot_tn(x, y), 0.0), w1, bh)
    h_mat = each(lambda x, y, z, w: jnp.where(diag, _dot_tn(cat0(x, y), cat0(z, w)), 0.0), u0, v, bh, kh)
    gam = each(jnp.exp, c_last)
    return q1, y0, g_mat, h_mat, gam


def _rwkv_body(pb_ref, shift_ref, s0_ref, mu_ref, w0_ref, w2_ref, a0_ref, a2_ref, g2_ref, kk_ref,
               ka_ref, rk_ref, lng_ref, lnb_ref, e_ref,
               o_ref, sout_ref,
               ext_ref, s_ref, *, tb, lc):
    t = pl.program_id(1)
    W = BRANCH_W
    n_groups = W // HGW

    @pl.when(t == 0)
    def _():
        ext_ref[0:SUBLANES, :] = jnp.broadcast_to(shift_ref[0], (SUBLANES, ext_ref.shape[1]))
        s_ref[...] = s0_ref[0]

    p = pb_ref[...]
    ext_ref[SUBLANES:, :] = p
    prev = ext_ref[SUBLANES - 1:SUBLANES - 1 + tb, :]
    xs = p + (prev - p) * mu_ref[...]
    ext_ref[0:SUBLANES, :] = p[tb - SUBLANES:, :]

    e = e_ref[...]
    r = xs[:, 0:W]
    k = xs[:, W:2 * W]
    v = xs[:, 2 * W:3 * W]
    wl = xs[:, 3 * W:3 * W + 128]
    al = xs[:, 3 * W + 128:3 * W + 256]
    gl = xs[:, 3 * W + 256:3 * W + 384]
    w_log = -_softplus_big(-(w0_ref[...] + _dot(jnp.tanh(wl), w2_ref[...]))) - 0.5
    a = _sigmoid(a0_ref[...] + _dot(al, a2_ref[...]))
    g = _dot(_sigmoid(gl), g2_ref[...])
    kkr = k * kk_ref[...]
    kk = kkr / jnp.maximum(jnp.sqrt(_segsum(kkr * kkr, e)), 1e-12)
    k2 = k * (1.0 + (a - 1.0) * ka_ref[...])
    lw = -jnp.exp(w_log)
    n_chunks = max(tb // lc, 1)

    def piece(x, ci, gi):
        blk = x[ci * lc:min((ci + 1) * lc, tb), gi * HGW:(gi + 1) * HGW]
        if tb < lc:
            blk = jnp.concatenate([blk, jnp.zeros((lc - tb, HGW), F32)], axis=0)
        return blk

    inst = [(ci, gi) for ci in range(n_chunks) for gi in range(n_groups)]
    terms = _rwkv_chunk_terms(*[[piece(x, ci, gi) for ci, gi in inst] for x in (r, lw, k2, v, kk, a)])
    y_rows = []
    for ci in range(n_chunks):
        y_cols = []
        for gi in range(n_groups):
            q1, y0, g_mat, h_mat, gam = [tm[ci * n_groups + gi] for tm in terms]
            s = s_ref[gi]
            y_cols.append(_dot_nt(q1, s) + y0)
            s_ref[gi] = s * gam + _dot(s, g_mat) + h_mat
        y_rows.append(jnp.concatenate(y_cols, axis=1))
    y = jnp.concatenate(y_rows, axis=0)[0:tb]
    mean = _segsum(y, e) * (1.0 / B_HD)
    yc = y - mean
    var = _segsum(yc * yc, e) * (1.0 / B_HD)
    yn = yc * lax.rsqrt(var + RWKV_GN_EPS) * lng_ref[...] + lnb_ref[...]
    bonus = _segsum(r * k2 * rk_ref[...], e) * v
    o_ref[...] = (yn + bonus) * g

    @pl.when(t == pl.num_programs(1) - 1)
    def _():
        sout_ref[0] = s_ref[...]


def _rwkv(pb, shift0, s0_bd, wts, nb, t):
    tb = _row_tile(t, 256)
    lc = CHUNK
    assert tb % lc == 0 or tb < lc
    nt = t // tb
    width = pb.shape[1]
    W = BRANCH_W
    n_groups = W // HGW
    row = lambda n: _resident((1, n))
    return pl.pallas_call(
        functools.partial(_rwkv_body, tb=tb, lc=lc),
        grid=(nb, nt),
        in_specs=[pl.BlockSpec((tb, width), lambda b, i: (b * nt + i, 0)),
                  pl.BlockSpec((1, 1, width), lambda b, i: (b, 0, 0)),
                  pl.BlockSpec((1, n_groups, HGW, HGW), lambda b, i: (b, 0, 0, 0)),
                  row(width), row(W), _resident((128, W)), row(W), _resident((128, W)),
                  _resident((128, W)), row(W), row(W), row(W), row(W), row(W), _resident((HGW, HGW))],
        out_specs=[pl.BlockSpec((tb, W), lambda b, i: (b * nt + i, 0)),
                   pl.BlockSpec((1, n_groups, HGW, HGW), lambda b, i: (b, 0, 0, 0))],
        out_shape=[jax.ShapeDtypeStruct((nb * t, W), F32),
                   jax.ShapeDtypeStruct((nb, n_groups, HGW, HGW), F32)],
        scratch_shapes=[pltpu.VMEM((tb + SUBLANES, width), F32),
                        pltpu.VMEM((n_groups, HGW, HGW), F32)],
        compiler_params=_cparams("parallel", "arbitrary"),
        name="rwkv7",
    )(pb, shift0, s0_bd, *wts, _seg_matrix(HGW, B_HD))


def _gla_chunk_terms(q, k, v, gk):
    each = lambda f, *seqs: [f(*xs) for xs in zip(*seqs)]
    L = q[0].shape[0]
    kw = C_HEADS * C_DK
    vw = C_HEADS * C_DV
    tril = _tril_ones(L)
    b = each(lambda x: _split_dot(tril, x, 3), gk)
    qe = each(lambda x, y: x * jnp.exp(y), q, b)
    ke = each(lambda x, y: x * jnp.exp(-y), k, b)
    a_all = each(lambda x, y: _dot_nt(x, _bd_rows(y, C_DK, C_HEADS)), qe, ke)
    row = lax.broadcasted_iota(jnp.int32, (L, C_HEADS * L), 0)
    col = lax.broadcasted_iota(jnp.int32, (L, C_HEADS * L), 1) % L
    causal = col <= row
    o_intra = each(lambda x, y: _dot(jnp.where(causal, x, 0.0), _bd_rows(y, C_DV, C_HEADS)), a_all, v)
    b_last = each(lambda x: x[L - 1:L, :], b)
    kd = each(lambda x, y, z: x * jnp.exp(z - y), k, b, b_last)
    ri = lax.broadcasted_iota(jnp.int32, (vw, kw), 0) // C_DV
    ci = lax.broadcasted_iota(jnp.int32, (vw, kw), 1) // C_DK
    diag = ri == ci
    upd = each(lambda x, y: jnp.where(diag, _dot_tn(x, y), 0.0), v, kd)
    decay = each(jnp.exp, b_last)
    return qe, o_intra, upd, decay


def _gla_body(pc_ref, s0_ref, g2_ref, gb_ref, norm_ref, o_ref, sout_ref,
              s_ref, *, tb, lc):
    t = pl.program_id(1)
    kw = C_HEADS * C_DK
    vw = C_HEADS * C_DV

    @pl.when(t == 0)
    def _():
        s_ref[...] = s0_ref[0]

    pc = pc_ref[...]
    og = pc[:, 2 * kw + vw:2 * kw + 2 * vw]
    gl = pc[:, 2 * kw + 2 * vw:2 * kw + 2 * vw + 128]
    z = _dot(gl, g2_ref[...]) + gb_ref[...]
    q = pc[:, 0:kw] * (C_DK ** -0.5)
    k = pc[:, kw:2 * kw]
    v = pc[:, 2 * kw:2 * kw + vw]
    gk = -_softplus_big(-z) * (1.0 / C_TAU)
    n_chunks = max(tb // lc, 1)

    def piece(x, ci):
        blk = x[ci * lc:min((ci + 1) * lc, tb), :]
        if tb < lc:
            blk = jnp.concatenate([blk, jnp.zeros((lc - tb, x.shape[1]), F32)], axis=0)
        return blk

    qe, o_intra, upd, decay = _gla_chunk_terms(
        *[[piece(x, ci) for ci in range(n_chunks)] for x in (q, k, v, gk)])
    y_rows = []
    for ci in range(n_chunks):
        st = s_ref[...]
        y_rows.append(_dot_nt(qe[ci], st) + o_intra[ci])
        s_ref[...] = st * decay[ci] + upd[ci]
    y = jnp.concatenate(y_rows, axis=0)[0:tb]
    outs = []
    for h in range(C_HEADS):
        yh = y[:, h * C_DV:(h + 1) * C_DV]
        outs.append(yh * lax.rsqrt(jnp.mean(yh * yh, axis=-1, keepdims=True) + EPS) * norm_ref[...])
    yn = jnp.concatenate(outs, axis=1)
    o_ref[...] = yn * (og * _sigmoid(og))

    @pl.when(t == pl.num_programs(1) - 1)
    def _():
        sout_ref[0] = s_ref[...]


def _gla(pc, s0_bd, g2, gb, norm, nb, t):
    tb = _row_tile(t, 256)
    lc = CHUNK
    assert tb % lc == 0 or tb < lc
    nt = t // tb
    width = pc.shape[1]
    kw, vw = C_HEADS * C_DK, C_HEADS * C_DV
    return pl.pallas_call(
        functools.partial(_gla_body, tb=tb, lc=lc),
        grid=(nb, nt),
        in_specs=[pl.BlockSpec((tb, width), lambda b, i: (b * nt + i, 0)),
                  pl.BlockSpec((1, vw, kw), lambda b, i: (b, 0, 0)),
                  _resident((128, kw)), _resident((1, kw)), _resident((1, C_DV))],
        out_specs=[pl.BlockSpec((tb, vw), lambda b, i: (b * nt + i, 0)),
                   pl.BlockSpec((1, vw, kw), lambda b, i: (b, 0, 0))],
        out_shape=[jax.ShapeDtypeStruct((nb * t, vw), F32),
                   jax.ShapeDtypeStruct((nb, vw, kw), F32)],
        scratch_shapes=[pltpu.VMEM((vw, kw), F32)],
        compiler_params=_cparams("parallel", "arbitrary"),
        name="gla",
    )(pc, s0_bd, g2, gb, norm)


def _shift_rows(x, d, fill):
    n = x.shape[0]
    if d % SUBLANES == 0:
        head = jnp.full((d, x.shape[1]), fill, x.dtype)
        return jnp.concatenate([head, x[:n - d]], axis=0)
    rolled = pltpu.roll(x, d, 0)
    row = lax.broadcasted_iota(jnp.int32, x.shape, 0)
    return jnp.where(row < d, fill, rolled)


def _lru_body(pd_ref, conv0_ref, h0_ref, cw_ref, cb_ref, wa_ref, ba_ref, wx_ref, bx_ref, lam_ref,
              o_ref, convout_ref, hout_ref, ext_ref, h_ref, au_ref, *, tb):
    t = pl.program_id(1)
    W = BRANCH_W

    @pl.when(t == 0)
    def _():
        ext_ref[0:SUBLANES, :] = conv0_ref[0]
        h_ref[...] = h0_ref[0]

    gate = pd_ref[:, 0:W]
    xr = pd_ref[:, W:2 * W]
    ext_ref[SUBLANES:, :] = xr
    xc = cb_ref[...] + xr * cw_ref[CONV_W - 1:CONV_W, :]
    for j in range(CONV_W - 1):
        off = SUBLANES - (CONV_W - 1) + j
        xc = xc + ext_ref[off:off + tb, :] * cw_ref[j:j + 1, :]
    tail = ext_ref[tb:tb + SUBLANES, :]
    ext_ref[0:SUBLANES, :] = tail

    r = _sigmoid(_dot(xc, wa_ref[...]) + ba_ref[...])
    gi = _sigmoid(_dot(xc, wx_ref[...]) + bx_ref[...])
    log_a = (-LRU_C) * r * _softplus(-lam_ref[...])
    a = jnp.exp(log_a)
    u = jnp.sqrt(1.0 - jnp.exp(2.0 * log_a)) * (gi * xc)
    ng = tb // SUBLANES
    row_in_group = lax.broadcasted_iota(jnp.int32, (ng, SUBLANES, W), 1)
    a = a.reshape(ng, SUBLANES, W)
    u = u.reshape(ng, SUBLANES, W)
    d = 1
    while d < SUBLANES:
        inside = row_in_group >= d
        u = u + a * jnp.where(inside, pltpu.roll(u, d, 1), 0.0)
        a = a * jnp.where(inside, pltpu.roll(a, d, 1), 1.0)
        d *= 2
    a = a.reshape(tb, W)
    u = u.reshape(tb, W)
    h_in = h_ref[...]
    if ng % SUBLANES == 0:
        ends = pl.ds(SUBLANES - 1, ng, stride=SUBLANES)
        for j in range(W // LANES):
            au_ref[0, j] = a[:, j * LANES:(j + 1) * LANES]
            au_ref[1, j] = u[:, j * LANES:(j + 1) * LANES]
        ae = jnp.concatenate([au_ref[0, j, ends, :] for j in range(W // LANES)], axis=1)
        ue = jnp.concatenate([au_ref[1, j, ends, :] for j in range(W // LANES)], axis=1)
        d = 1
        while d < ng:
            ue = ue + ae * _shift_rows(ue, d, 0.0)
            ae = ae * _shift_rows(ae, d, 1.0)
            d *= 2
        carry = _shift_rows(ue + ae * h_in, 1, h_in)
        carry_rows = jnp.broadcast_to(carry[:, None, :], (ng, SUBLANES, W)).reshape(tb, W)
    else:
        rows = []
        for g in range(ng):
            rows.append(jnp.broadcast_to(h_in, (SUBLANES, W)))
            end = (g + 1) * SUBLANES - 1
            h_in = u[end:end + 1, :] + a[end:end + 1, :] * h_in
        carry_rows = jnp.concatenate(rows, axis=0)
    h = u + a * carry_rows
    h_ref[...] = h[tb - 1:tb, :]
    o_ref[...] = h * (0.5 * gate * (1.0 + jnp.tanh(math.sqrt(2.0 / math.pi)
                                                   * (gate + 0.044715 * gate * gate * gate))))

    @pl.when(t == pl.num_programs(1) - 1)
    def _():
        convout_ref[0] = tail
        hout_ref[0] = h[tb - 1:tb, :]


def _lru(pd, conv0, h0, wts, nb, t):
    tb = _row_tile(t, 256)
    nt = t // tb
    W = BRANCH_W
    row = lambda: _resident((1, W))
    return pl.pallas_call(
        functools.partial(_lru_body, tb=tb),
        grid=(nb, nt),
        in_specs=[pl.BlockSpec((tb, 2 * W), lambda b, i: (b * nt + i, 0)),
                  pl.BlockSpec((1, SUBLANES, W), lambda b, i: (b, 0, 0)),
                  pl.BlockSpec((1, 1, W), lambda b, i: (b, 0, 0)),
                  _resident((CONV_W, W)), row(), _resident((W, W)), row(), _resident((W, W)), row(),
                  row()],
        out_specs=[pl.BlockSpec((tb, W), lambda b, i: (b * nt + i, 0)),
                   pl.BlockSpec((1, SUBLANES, W), lambda b, i: (b, 0, 0)),
                   pl.BlockSpec((1, 1, W), lambda b, i: (b, 0, 0))],
        out_shape=[jax.ShapeDtypeStruct((nb * t, W), F32),
                   jax.ShapeDtypeStruct((nb, SUBLANES, W), F32),
                   jax.ShapeDtypeStruct((nb, 1, W), F32)],
        scratch_shapes=[pltpu.VMEM((tb + SUBLANES, W), F32), pltpu.VMEM((1, W), F32),
                        pltpu.VMEM((2, W // LANES, tb, LANES), F32)],
        compiler_params=_cparams("parallel", "arbitrary"),
        name="rglru",
    )(pd, conv0, h0, *wts)


def _block_diag(w):
    n, a, b = w.shape
    eye = jnp.eye(n, dtype=w.dtype)
    return (eye[:, None, :, None] * w[:, :, None, :]).reshape(n * a, n * b)


def _pad_rows_to(w, rows):
    return jnp.pad(w, ((0, rows - w.shape[0]), (0, 0)))


def _layer_weights(i, W):
    bf = lambda a: a.astype(BF16)
    row = lambda a: a.reshape(1, -1).astype(F32)
    bw = BRANCH_W
    a_cols = A_HEADS * A_HD + 2 * A_KV_HEADS * A_HD
    b_cols = 3 * bw + B_W_RANK + B_A_RANK + B_G_RANK
    c_cols = 2 * C_HEADS * C_DK + C_HEADS * C_DV + C_G_RANK + bw
    w_in = W['w_in'][i]
    d = w_in.shape[0]
    wa = w_in[:, :a_cols]
    wb = w_in[:, a_cols:a_cols + b_cols]
    wc = w_in[:, a_cols + b_cols:a_cols + b_cols + c_cols]
    wd = w_in[:, a_cols + b_cols + c_cols:]
    zpad = lambda n: jnp.zeros((d, n), w_in.dtype)
    wb2 = jnp.concatenate([wb[:, :3 * bw], wb[:, 3 * bw:3 * bw + B_W_RANK], zpad(128 - B_W_RANK),
                           wb[:, 3 * bw + B_W_RANK:3 * bw + B_W_RANK + B_A_RANK], zpad(128 - B_A_RANK),
                           wb[:, 3 * bw + B_W_RANK + B_A_RANK:]], axis=1)
    mu = W['rwkv_mu'][i]
    z1 = lambda n: jnp.zeros((n,), F32)
    mu2 = jnp.concatenate([mu[:3 * bw], mu[3 * bw:3 * bw + B_W_RANK], z1(128 - B_W_RANK),
                           mu[3 * bw + B_W_RANK:3 * bw + B_W_RANK + B_A_RANK], z1(128 - B_A_RANK),
                           mu[3 * bw + B_W_RANK + B_A_RANK:]])
    qkv = 2 * C_HEADS * C_DK + C_HEADS * C_DV
    wc2 = jnp.concatenate([wc[:, :qkv], wc[:, qkv + C_G_RANK:], wc[:, qkv:qkv + C_G_RANK],
                           zpad(128 - C_G_RANK)], axis=1)
    return dict(
        ffn1=(row(W['g_ffn1'][i]), bf(W['w_ffn1_gate'][i]), bf(W['w_ffn1_up'][i]), bf(W['w_ffn1_down'][i])),
        ffn2=(row(W['g_ffn2'][i]), bf(W['w_ffn2_gate'][i]), bf(W['w_ffn2_up'][i]), bf(W['w_ffn2_down'][i])),
        g_mix=row(W['g_mix'][i]),
        w_in=(bf(wa), bf(wb2), bf(wc2), bf(wd)),
        q_norm=row(jnp.tile(W['q_norm'][i], A_HEADS)),
        k_norm=row(jnp.tile(W['k_norm'][i], A_KV_HEADS)),
        sink=W['attn_sink'][i].astype(F32),
        rwkv=(row(mu2), row(W['rwkv_w0'][i]), bf(_pad_rows_to(W['rwkv_w2'][i], 128)),
              row(W['rwkv_a0'][i]), bf(_pad_rows_to(W['rwkv_a2'][i], 128)), bf(W['rwkv_g2'][i]),
              row(W['rwkv_k_k'][i]), row(W['rwkv_k_a'][i]), row(W['rwkv_r_k'][i]),
              row(W['rwkv_ln_g'][i]), row(W['rwkv_ln_b'][i])),
        gla=(bf(_pad_rows_to(W['gla_g2'][i], 128)), row(W['gla_gb'][i]), row(W['gla_norm'][i])),
        lru=(W['lru_conv_w'][i].astype(F32), row(W['lru_conv_b'][i]),
             bf(_block_diag(W['lru_wa'][i])), row(W['lru_ba'][i]),
             bf(_block_diag(W['lru_wx'][i])), row(W['lru_bx'][i]), row(W['lru_lambda'][i])),
        merge=(bf(W['w_merge_gate'][i]), bf(W['w_branch'][i]), bf(W['w_out'][i])),
        ple=(row(W['g_ple'][i]), bf(W['w_ple_gate'][i]), bf(W['w_ple_proj'][i])),
    )


def _b_cols_to_kernel(x):
    bw = BRANCH_W
    z = jnp.zeros(x.shape[:-1] + (128 - B_W_RANK,), x.dtype)
    return jnp.concatenate([x[..., :3 * bw], x[..., 3 * bw:3 * bw + B_W_RANK], z,
                            x[..., 3 * bw + B_W_RANK:3 * bw + B_W_RANK + B_A_RANK], z,
                            x[..., 3 * bw + B_W_RANK + B_A_RANK:]], axis=-1)


def _b_cols_from_kernel(x):
    bw = BRANCH_W
    return jnp.concatenate([x[..., :3 * bw + B_W_RANK], x[..., 3 * bw + 128:3 * bw + 128 + B_A_RANK],
                            x[..., 3 * bw + 256:]], axis=-1)


def _heads_to_bd(s, hg):
    nb, h, a, b = s.shape
    s = s.reshape(nb, h // hg, hg, a, b)
    eye = jnp.eye(hg, dtype=s.dtype)
    return (eye[None, None, :, None, :, None] * s[:, :, :, :, None, :]).reshape(nb, h // hg, hg * a, hg * b)


def _bd_to_heads(s, hg):
    nb, ng, ra, cb = s.shape
    a, b = ra // hg, cb // hg
    s = s.reshape(nb, ng, hg, a, hg, b)
    idx = jnp.arange(hg)
    return s[:, :, idx, :, idx, :].transpose(1, 2, 0, 3, 4).reshape(nb, ng * hg, a, b)


def _trunk_layer(x, pe, st, lw, bias, first_chunk):
    nb, t, d = x.shape
    n = nb * t
    ck, cv, shift0, s_rwkv0, s_gla0, conv0, lru0 = st
    x2 = x.reshape(n, d)
    x2 = _ffn(x2, *lw['ffn1'])
    pa, pb, pc, pd = _inproj(x2, lw['g_mix'], lw['w_in'])
    kw = A_KV_HEADS * A_HD
    n_q = bias.shape[1] // A_HEADS
    sink2 = jnp.repeat(lw['sink'], n_q).reshape(1, A_HEADS * n_q)
    if first_chunk:
        o_a, k_win, v_win = _attn_prompt(pa, bias, sink2, lw['q_norm'], lw['k_norm'], nb, t)
    else:
        o_a, k_win, v_win = _attn_sample(pa, ck.reshape(nb, WINDOW, kw), cv.reshape(nb, WINDOW, kw),
                                         bias, sink2, lw['q_norm'], lw['k_norm'], nb, t)
    k_win = k_win.reshape(nb, WINDOW, A_KV_HEADS, A_HD)
    v_win = v_win.reshape(nb, WINDOW, A_KV_HEADS, A_HD)

    o_b, s_bd = _rwkv(pb, _b_cols_to_kernel(shift0)[:, None, :], _heads_to_bd(s_rwkv0, HG),
                      lw['rwkv'], nb, t)
    shift1 = _b_cols_from_kernel(pb.reshape(nb, t, -1)[:, -1])
    s_rwkv1 = _bd_to_heads(s_bd, HG)

    st_bd0 = _heads_to_bd(jnp.swapaxes(s_gla0, -1, -2), C_HEADS)[:, 0]
    o_c, st_bd = _gla(pc, st_bd0, *lw['gla'], nb, t)
    s_gla1 = jnp.swapaxes(_bd_to_heads(st_bd[:, None], C_HEADS), -1, -2)

    conv_pad = jnp.pad(conv0, ((0, 0), (SUBLANES - (CONV_W - 1), 0), (0, 0)))
    o_d, conv_out, h_out = _lru(pd, conv_pad, lru0[:, None, :], lw['lru'], nb, t)
    conv1 = conv_out[:, SUBLANES - (CONV_W - 1):]
    lru1 = h_out[:, 0]

    x2 = _merge(x2, (o_a, o_b, o_c, o_d), lw['g_mix'], *lw['merge'])
    x2 = _ffn_ple(x2, pe.reshape(n, -1), *lw['ffn2'], *lw['ple'])
    return x2.reshape(nb, t, d), (k_win, v_win, shift1, s_rwkv1, s_gla1, conv1, lru1)


def kernel(x_prompt, x_sample, cache_attn_k, cache_attn_v, state_rwkv_shift, state_rwkv, state_gla, state_lru_conv, state_lru, p_prompt, p_sample, rel_bias_table, g_ffn1, w_ffn1_gate, w_ffn1_up, w_ffn1_down, g_mix, w_in, q_norm, k_norm, attn_sink, rwkv_mu, rwkv_w0, rwkv_w2, rwkv_a0, rwkv_a2, rwkv_g2, rwkv_k_k, rwkv_k_a, rwkv_r_k, rwkv_ln_g, rwkv_ln_b, gla_g2, gla_gb, gla_norm, lru_conv_w, lru_conv_b, lru_wa, lru_ba, lru_wx, lru_bx, lru_lambda, w_merge_gate, w_branch, w_out, g_ffn2, w_ffn2_gate, w_ffn2_up, w_ffn2_down, g_ple, w_ple_gate, w_ple_proj):
    W = dict(g_ffn1=g_ffn1, w_ffn1_gate=w_ffn1_gate, w_ffn1_up=w_ffn1_up, w_ffn1_down=w_ffn1_down,
             g_mix=g_mix, w_in=w_in, q_norm=q_norm, k_norm=k_norm, attn_sink=attn_sink,
             rwkv_mu=rwkv_mu, rwkv_w0=rwkv_w0, rwkv_w2=rwkv_w2, rwkv_a0=rwkv_a0, rwkv_a2=rwkv_a2,
             rwkv_g2=rwkv_g2, rwkv_k_k=rwkv_k_k, rwkv_k_a=rwkv_k_a, rwkv_r_k=rwkv_r_k,
             rwkv_ln_g=rwkv_ln_g, rwkv_ln_b=rwkv_ln_b, gla_g2=gla_g2, gla_gb=gla_gb,
             gla_norm=gla_norm, lru_conv_w=lru_conv_w, lru_conv_b=lru_conv_b, lru_wa=lru_wa,
             lru_ba=lru_ba, lru_wx=lru_wx, lru_bx=lru_bx, lru_lambda=lru_lambda,
             w_merge_gate=w_merge_gate, w_branch=w_branch, w_out=w_out, g_ffn2=g_ffn2,
             w_ffn2_gate=w_ffn2_gate, w_ffn2_up=w_ffn2_up, w_ffn2_down=w_ffn2_down,
             g_ple=g_ple, w_ple_gate=w_ple_gate, w_ple_proj=w_ple_proj)
    depth = w_in.shape[0]
    dt = x_prompt.dtype
    bp, tp = x_prompt.shape[:2]
    ts = x_sample.shape[1]
    b_cols = state_rwkv_shift.shape[-1]
    bias_p = _rel_bias(rel_bias_table, CHUNK, WINDOW + CHUNK)
    bias_s = _rel_bias(rel_bias_table, ts, WINDOW + ts)
    yp, ys = x_prompt, x_sample
    st_p, st_s = [], []
    for i in range(depth):
        lw = _layer_weights(i, W)
        zero_st = (None, None,
                   jnp.zeros((bp, b_cols), dt),
                   jnp.zeros((bp, B_HEADS, B_HD, B_HD), dt),
                   jnp.zeros((bp, C_HEADS, C_DK, C_DV), dt),
                   jnp.zeros((bp, CONV_W - 1, BRANCH_W), dt),
                   jnp.zeros((bp, BRANCH_W), dt))
        yp, sp = _trunk_layer(yp, p_prompt[i], zero_st, lw, bias_p, True)
        cache_st = (cache_attn_k[i], cache_attn_v[i], state_rwkv_shift[i], state_rwkv[i],
                    state_gla[i], state_lru_conv[i], state_lru[i])
        ys, ss = _trunk_layer(ys, p_sample[i], cache_st, lw, bias_s, False)
        st_p.append(sp)
        st_s.append(ss)
    stack = lambda states, j: jnp.stack([s[j] for s in states])
    return (yp, ys) + tuple(stack(st_p, j) for j in range(7)) + tuple(stack(st_s, j) for j in range(7))
```

```python
import functools
import math

import numpy as np
import jax
import jax.numpy as jnp
from jax import lax
from jax.experimental import pallas as pl
from jax.experimental.pallas import tpu as pltpu

F32 = jnp.float32
BF16 = jnp.bfloat16

V7X_VMEM_BYTES = 64 * 1024 * 1024
VMEM_LIMIT = V7X_VMEM_BYTES - 8 * 1024 * 1024
SUBLANES = 8
LANES = 128
MXU_COLS = 256

EPS = 1e-6
NEG_INF = -1e30
CHUNK = 64
WINDOW = 128
N_BUCKETS = 32
MAX_DIST = 128
A_HEADS, A_KV_HEADS, A_HD = 8, 2, 64
A_GROUP = A_HEADS // A_KV_HEADS
B_HEADS, B_HD = 8, 64
B_W_RANK, B_A_RANK, B_G_RANK = 64, 64, 128
RWKV_GN_EPS = 64e-5
C_HEADS, C_DK, C_DV = 4, 64, 128
C_G_RANK = 16
C_TAU = 16.0
D_BLOCKS = 8
CONV_W = 4
LRU_C = 8.0
BRANCH_W = 512
HG = 4
HGW = HG * B_HD


def _cparams(*sem):
    return pltpu.CompilerParams(dimension_semantics=sem, vmem_limit_bytes=VMEM_LIMIT)


def _resident(shape):
    nd = len(shape)
    return pl.BlockSpec(shape, lambda *_: (0,) * nd, pipeline_mode=pl.Buffered(1))


def _row_tile(n, cap):
    t = min(n, cap)
    while n % t:
        t //= 2
    return t


def _dot(a, b):
    return jnp.dot(a.astype(BF16), b.astype(BF16), preferred_element_type=F32)


def _dot_nt(a, b):
    return lax.dot_general(a.astype(BF16), b.astype(BF16), (((1,), (1,)), ((), ())),
                           preferred_element_type=F32)


def _dot_tn(a, b):
    return lax.dot_general(a.astype(BF16), b.astype(BF16), (((0,), (0,)), ((), ())),
                           preferred_element_type=F32)


def _rms(x, g):
    return x * lax.rsqrt(jnp.mean(x * x, axis=-1, keepdims=True) + EPS) * g


def _sigmoid(x):
    return 1.0 / (1.0 + jnp.exp(-x))


def _softplus(x):
    return jnp.maximum(x, 0.0) + jnp.log1p(jnp.exp(-jnp.abs(x)))


def _softplus_big(x):
    return jnp.maximum(x, 0.0) + jnp.log(1.0 + jnp.exp(-jnp.abs(x)))


def _split_dot(e_lhs, x, terms):
    acc = None
    rem = x
    for n in range(terms):
        piece = rem.astype(BF16)
        d = jnp.dot(e_lhs, piece, preferred_element_type=F32)
        acc = d if acc is None else acc + d
        if n + 1 < terms:
            rem = rem - piece.astype(F32)
    return acc


def _segsum(x, e):
    n = x.shape[0]
    blk = e.shape[0]
    hi = x.astype(BF16)
    lo = (x - hi.astype(F32)).astype(BF16)
    parts = jnp.concatenate([hi, lo], axis=0)
    cols = []
    for j in range(0, x.shape[1], blk):
        d = jnp.dot(parts[:, j:j + blk], e, preferred_element_type=F32)
        cols.append(d[:n] + d[n:])
    return cols[0] if len(cols) == 1 else jnp.concatenate(cols, axis=1)


def _tril_ones(n):
    r = lax.broadcasted_iota(jnp.int32, (n, n), 0)
    c = lax.broadcasted_iota(jnp.int32, (n, n), 1)
    return jnp.where(r >= c, 1.0, 0.0).astype(BF16)


def _bd_rows(x, blk, nblk):
    lane_blk = lax.broadcasted_iota(jnp.int32, x.shape, 1) // blk
    return jnp.concatenate([jnp.where(lane_blk == h, x, 0.0) for h in range(nblk)], axis=0)


def _seg_matrix(width, seg):
    i = np.arange(width)
    return jnp.asarray((i[:, None] // seg) == (i[None, :] // seg), dtype=BF16)


def _swiglu_residual(x, g_ref, wg_ref, wu_ref, wd_ref, fchunk):
    h = _rms(x, g_ref[...]).astype(BF16)
    acc = None
    for c in range(wg_ref.shape[1] // fchunk):
        sl = pl.ds(c * fchunk, fchunk)
        gt = jnp.dot(h, wg_ref[:, sl], preferred_element_type=F32)
        up = jnp.dot(h, wu_ref[:, sl], preferred_element_type=F32)
        act = (gt * _sigmoid(gt) * up).astype(BF16)
        d = jnp.dot(act, wd_ref[sl, :], preferred_element_type=F32)
        acc = d if acc is None else acc + d
    return x + 0.5 * acc


def _ffn_body(x_ref, g_ref, wg_ref, wu_ref, wd_ref, o_ref, *, fchunk):
    o_ref[...] = _swiglu_residual(x_ref[...], g_ref, wg_ref, wu_ref, wd_ref, fchunk)


def _ffn_ple_body(x_ref, pe_ref, g_ref, wg_ref, wu_ref, wd_ref, gp_ref, wpg_ref, wpp_ref, o_ref, *,
                  fchunk):
    x = _swiglu_residual(x_ref[...], g_ref, wg_ref, wu_ref, wd_ref, fchunk)
    h = _rms(x, gp_ref[...]).astype(BF16)
    gate = _sigmoid(jnp.dot(h, wpg_ref[...], preferred_element_type=F32))
    o_ref[...] = x + gate * jnp.dot(pe_ref[...].astype(BF16), wpp_ref[...], preferred_element_type=F32)


def _ffn(x, g, wg, wu, wd):
    n, d = x.shape
    f = wg.shape[1]
    tm = _row_tile(n, 512)
    fchunk = f // 2 if (f // 2) % 128 == 0 else f
    return pl.pallas_call(
        functools.partial(_ffn_body, fchunk=fchunk),
        grid=(n // tm,),
        in_specs=[pl.BlockSpec((tm, d), lambda i: (i, 0)),
                  _resident((1, d)), _resident((d, f)), _resident((d, f)), _resident((f, d))],
        out_specs=pl.BlockSpec((tm, d), lambda i: (i, 0)),
        out_shape=jax.ShapeDtypeStruct((n, d), F32),
        compiler_params=_cparams("parallel"),
        name="ffn",
    )(x, g, wg, wu, wd)


def _inproj_body(x_ref, g_ref, wa_ref, wb_ref, wc_ref, wd_ref, pa_ref, pb_ref, pc_ref, pd_ref):
    h = _rms(x_ref[...], g_ref[...]).astype(BF16)
    for w_ref, p_ref in ((wa_ref, pa_ref), (wb_ref, pb_ref), (wc_ref, pc_ref), (wd_ref, pd_ref)):
        p_ref[...] = jnp.dot(h, w_ref[...], preferred_element_type=F32)


def _inproj(x, g, ws):
    n, d = x.shape
    tm = _row_tile(n, 512)
    widths = [w.shape[1] for w in ws]
    return pl.pallas_call(
        _inproj_body,
        grid=(n // tm,),
        in_specs=[pl.BlockSpec((tm, d), lambda i: (i, 0)), _resident((1, d))]
                 + [_resident((d, w)) for w in widths],
        out_specs=[pl.BlockSpec((tm, w), lambda i: (i, 0)) for w in widths],
        out_shape=[jax.ShapeDtypeStruct((n, w), F32) for w in widths],
        compiler_params=_cparams("parallel"),
        name="inproj",
    )(x, g, *ws)


def _merge_body(x_ref, oa_ref, ob_ref, oc_ref, od_ref, g_ref, wmg_ref, wb_ref, wo_ref, o_ref):
    x = x_ref[...]
    h = _rms(x, g_ref[...]).astype(BF16)
    y = None
    for n, b_ref in enumerate((oa_ref, ob_ref, oc_ref, od_ref)):
        gate = _sigmoid(jnp.dot(h, wmg_ref[n], preferred_element_type=F32))
        t = gate * jnp.dot(b_ref[...].astype(BF16), wb_ref[n], preferred_element_type=F32)
        y = t if y is None else y + t
    o_ref[...] = x + jnp.dot(y.astype(BF16), wo_ref[...], preferred_element_type=F32)


def _merge(x, outs, g, wmg, wb, wo):
    n, d = x.shape
    bw = outs[0].shape[1]
    tm = _row_tile(n, 512)
    return pl.pallas_call(
        _merge_body,
        grid=(n // tm,),
        in_specs=[pl.BlockSpec((tm, d), lambda i: (i, 0))]
                 + [pl.BlockSpec((tm, bw), lambda i: (i, 0))] * 4
                 + [_resident((1, d)), _resident(wmg.shape), _resident(wb.shape), _resident(wo.shape)],
        out_specs=pl.BlockSpec((tm, d), lambda i: (i, 0)),
        out_shape=jax.ShapeDtypeStruct((n, d), F32),
        compiler_params=_cparams("parallel"),
        name="merge",
    )(x, *outs, g, wmg, wb, wo)


def _ffn_ple(x, pe, g, wg, wu, wd, gp, wpg, wpp):
    n, d = x.shape
    f = wg.shape[1]
    pd = pe.shape[1]
    tm = _row_tile(n, 512)
    fchunk = f // 2 if (f // 2) % 128 == 0 else f
    return pl.pallas_call(
        functools.partial(_ffn_ple_body, fchunk=fchunk),
        grid=(n // tm,),
        in_specs=[pl.BlockSpec((tm, d), lambda i: (i, 0)), pl.BlockSpec((tm, pd), lambda i: (i, 0)),
                  _resident((1, d)), _resident((d, f)), _resident((d, f)), _resident((f, d)),
                  _resident((1, d)), _resident((d, d)), _resident((pd, d))],
        out_specs=pl.BlockSpec((tm, d), lambda i: (i, 0)),
        out_shape=jax.ShapeDtypeStruct((n, d), F32),
        compiler_params=_cparams("parallel"),
        name="ffn_ple",
    )(x, pe, g, wg, wu, wd, gp, wpg, wpp)


def _t5_bucket_np(rel):
    half = N_BUCKETS // 2
    max_exact = half // 2
    ret = np.where(rel > 0, half, 0)
    n = np.abs(rel)
    nf = np.maximum(n, 1).astype(np.float32)
    large = max_exact + (np.log(nf / np.float32(max_exact)) / np.float32(math.log(MAX_DIST / max_exact))
                         * np.float32(half - max_exact)).astype(np.int32)
    large = np.minimum(large, half - 1)
    return (ret + np.where(n < max_exact, n, large)).astype(np.int32)


def _bias_body(idx_ref, table_ref, o_ref):
    idx = idx_ref[...]
    for h in range(A_HEADS):
        acc = jnp.zeros(idx.shape, F32)
        for b in range(N_BUCKETS):
            acc = jnp.where(idx == b, table_ref[b, h], acc)
        o_ref[h] = acc


def _rel_bias(table, n_q, n_k):
    rel = np.arange(n_k)[:, None] - WINDOW - np.arange(n_q)[None, :]
    idx = jnp.asarray(_t5_bucket_np(rel))
    out = pl.pallas_call(
        _bias_body,
        in_specs=[pl.BlockSpec(memory_space=pltpu.VMEM), pl.BlockSpec(memory_space=pltpu.SMEM)],
        out_specs=pl.BlockSpec(memory_space=pltpu.VMEM),
        out_shape=jax.ShapeDtypeStruct((A_HEADS, n_k, n_q), F32),
        name="rel_bias",
    )(idx, table)
    return jnp.transpose(out, (1, 0, 2)).reshape(n_k, A_HEADS * n_q)


def _head_rms(x, e, g):
    ms = _segsum(x * x, e) * (1.0 / A_HD)
    return x * lax.rsqrt(ms + EPS) * g


def _kv_replication():
    lane = np.arange(A_HEADS * A_HD)
    src = (lane // (A_GROUP * A_HD)) * A_HD + lane % A_HD
    return jnp.asarray(np.arange(A_KV_HEADS * A_HD)[:, None] == src[None, :], dtype=BF16)


def _attend_blocks(q_list, k8_list, v8_list, bias, sink, valid_list, fill=lambda: None):
    def staged(f, *seqs):
        out = []
        for j, xs in enumerate(zip(*seqs)):
            out.append(f(*xs))
            if j % 4 == 3:
                fill()
        return out

    n = q_list[0].shape[0]
    lane_head = lax.broadcasted_iota(jnp.int32, (n, A_HEADS * A_HD), 1) // A_HD
    q_bd = staged(lambda q: jnp.concatenate([jnp.where(lane_head == h, q, 0.0)
                                             for h in range(A_HEADS)], axis=0), q_list)
    s = staged(lambda k8, qb, valid: (_dot_nt(k8, qb) + bias if valid is None else
                                      jnp.where(valid, _dot_nt(k8, qb) + bias, NEG_INF)),
               k8_list, q_bd, valid_list)
    m = staged(lambda x: jnp.maximum(jnp.max(x, axis=0, keepdims=True), sink), s)
    e = staged(lambda x, mm: jnp.exp(x - mm), s, m)
    rinv = [1.0 / (jnp.sum(x, axis=0, keepdims=True) + jnp.exp(sink - mm)) for x, mm in zip(e, m)]
    p = staged(lambda x, r: x * r, e, rinv)
    o8 = staged(_dot_tn, p, v8_list)

    def pick_heads(x):
        o = None
        for h in range(A_HEADS):
            part = jnp.where(lane_head == h, x[h * n:(h + 1) * n], 0.0)
            o = part if o is None else o + part
        return o

    return staged(pick_heads, o8)


def _attn_prompt_block(pa, kv_tail_ref, t_idx, bias_ref, sink_ref, qn_ref, kn_ref, eq_ref, ek_ref,
                       rep_ref, tb, fill):
    qw = A_HEADS * A_HD
    kw = A_KV_HEADS * A_HD
    band = WINDOW + CHUNK
    rep = rep_ref[...]
    q = _head_rms(pa[:, 0:qw], eq_ref[...], qn_ref[...]) * (A_HD ** -0.5)
    kf = jnp.concatenate([kv_tail_ref[0], _head_rms(pa[:, qw:qw + kw], ek_ref[...], kn_ref[...])], axis=0)
    vf = jnp.concatenate([kv_tail_ref[1], pa[:, qw + kw:qw + 2 * kw]], axis=0)
    kv_tail_ref[0] = kf[tb:tb + WINDOW, :]
    kv_tail_ref[1] = vf[tb:tb + WINDOW, :]
    k8 = jnp.dot(kf.astype(BF16), rep, preferred_element_type=F32).astype(BF16)
    v8 = jnp.dot(vf.astype(BF16), rep, preferred_element_type=F32).astype(BF16)
    kidx = lax.broadcasted_iota(jnp.int32, (band, A_HEADS * CHUNK), 0)
    n_chunks = tb // CHUNK
    valid = [(kidx + (t_idx * tb + c * CHUNK - WINDOW)) >= 0 if c * CHUNK < WINDOW else None
             for c in range(n_chunks)]
    return _attend_blocks([q[c * CHUNK:(c + 1) * CHUNK] for c in range(n_chunks)],
                          [k8[c * CHUNK:c * CHUNK + band] for c in range(n_chunks)],
                          [v8[c * CHUNK:c * CHUNK + band] for c in range(n_chunks)],
                          bias_ref[...], sink_ref[...], valid, fill)


def _attn_sample_body(pa_ref, ck_ref, cv_ref, bias_ref, sink_ref, qn_ref, kn_ref, eq_ref, ek_ref, rep_ref,
                      o_ref, kout_ref, vout_ref, *, s, gb):
    qw = A_HEADS * A_HD
    kw = A_KV_HEADS * A_HD
    rep = rep_ref[...]
    q = _head_rms(pa_ref[:, 0:qw], eq_ref[...], qn_ref[...]) * (A_HD ** -0.5)
    kn = _head_rms(pa_ref[:, qw:qw + kw], ek_ref[...], kn_ref[...])
    vn = pa_ref[:, qw + kw:qw + 2 * kw]
    q_list, k8_list, v8_list = [], [], []
    for b in range(gb):
        kf = jnp.concatenate([ck_ref[b], kn[b * s:(b + 1) * s]], axis=0)
        vf = jnp.concatenate([cv_ref[b], vn[b * s:(b + 1) * s]], axis=0)
        kout_ref[b] = kf[s:s + WINDOW, :]
        vout_ref[b] = vf[s:s + WINDOW, :]
        q_list.append(q[b * s:(b + 1) * s])
        k8_list.append(jnp.dot(kf.astype(BF16), rep, preferred_element_type=F32).astype(BF16))
        v8_list.append(jnp.dot(vf.astype(BF16), rep, preferred_element_type=F32).astype(BF16))
    outs = _attend_blocks(q_list, k8_list, v8_list, bias_ref[...], sink_ref[...], [None] * gb)
    for b in range(gb):
        o_ref[b * s:(b + 1) * s, :] = outs[b]


def _attn_sample(pa, ck, cv, bias, sink, qn, kn, nb, s):
    qw, kw = A_HEADS * A_HD, A_KV_HEADS * A_HD
    width = pa.shape[1]
    gb = _row_tile(nb, 8)
    return pl.pallas_call(
        functools.partial(_attn_sample_body, s=s, gb=gb),
        grid=(nb // gb,),
        in_specs=[pl.BlockSpec((gb * s, width), lambda b: (b, 0)),
                  pl.BlockSpec((gb, WINDOW, kw), lambda b: (b, 0, 0)),
                  pl.BlockSpec((gb, WINDOW, kw), lambda b: (b, 0, 0)),
                  _resident(bias.shape), _resident(sink.shape),
                  _resident((1, qw)), _resident((1, kw)), _resident((HGW, HGW)), _resident((kw, kw)),
                  _resident((kw, qw))],
        out_specs=[pl.BlockSpec((gb * s, qw), lambda b: (b, 0)),
                   pl.BlockSpec((gb, WINDOW, kw), lambda b: (b, 0, 0)),
                   pl.BlockSpec((gb, WINDOW, kw), lambda b: (b, 0, 0))],
        out_shape=[jax.ShapeDtypeStruct((nb * s, qw), F32),
                   jax.ShapeDtypeStruct((nb, WINDOW, kw), F32),
                   jax.ShapeDtypeStruct((nb, WINDOW, kw), F32)],
        compiler_params=_cparams("parallel"),
        name="attn_sample",
    )(pa, ck, cv, bias, sink, qn, kn, _seg_matrix(HGW, A_HD), _seg_matrix(kw, A_HD), _kv_replication())


def _rwkv_chunk_terms(r, lw, k, v, kk, a):
    each = lambda f, *seqs: [f(*xs) for xs in zip(*seqs)]
    cat0 = lambda *xs: jnp.concatenate(xs, axis=0)
    cat1 = lambda *xs: jnp.concatenate(xs, axis=1)
    bd = lambda x: _bd_rows(x, B_HD, HG)
    L = r[0].shape[0]
    tril = _tril_ones(L)
    c = each(lambda x: _split_dot(tril, x, 3), lw)
    c_last = each(lambda x: x[L - 1:L, :], c)
    e_last = each(lambda x, xl: jnp.exp(xl - x), c, c_last)
    beta = each(lambda x, y: x * y, kk, a)
    at = each(lambda x, cc, l: -x * jnp.exp(cc - l), kk, c, lw)
    rt = each(lambda x, cc: x * jnp.exp(cc), r, c)
    enc = each(lambda cc: jnp.exp(-cc), c)
    ar = each(cat0, at, rt)
    mb = each(lambda x, b, e: _dot_nt(x, bd(b * e)), ar, beta, enc)
    mk = each(lambda x, b, e: _dot_nt(x, bd(b * e)), ar, k, enc)
    row = lax.broadcasted_iota(jnp.int32, (L, HG * L), 0)
    col = lax.broadcasted_iota(jnp.int32, (L, HG * L), 1) % L
    strict = col < row
    incl = col <= row
    m_b = each(lambda x: jnp.where(strict, x[:L], 0.0), mb)
    m_k = each(lambda x: jnp.where(strict, x[:L], 0.0), mk)
    n_b = each(lambda x: jnp.where(incl, x[L:], 0.0), mb)
    n_k = each(lambda x: jnp.where(incl, x[L:], 0.0), mk)
    eye = jnp.where(col == row, 1.0, 0.0)
    t_inv = each(lambda x: eye + x, m_b)
    p = m_b
    for lvl in range(1, int(math.log2(L))):
        p_bd = each(lambda x: _bd_rows(x, L, HG), p)
        if lvl == 1:
            p = each(_dot, p, p_bd)
        else:
            tp = each(lambda t, x, xb: _dot(cat0(t, x), xb), t_inv, p, p_bd)
            t_inv = each(lambda t, x: t + x[:L], t_inv, tp)
            p = each(lambda x: x[L:], tp)
    t_inv = each(lambda t, x: t + _dot(t, _bd_rows(x, L, HG)), t_inv, p)
    v_bd = each(bd, v)
    mkv = each(_dot, m_k, v_bd)
    wu = each(lambda t, x, y: _dot(t, cat1(bd(x), bd(y))), t_inv, at, mkv)
    w1 = each(lambda x: x[:, :HGW], wu)
    u0 = each(lambda x: x[:, HGW:], wu)
    nbo = each(lambda n, x, y: _dot(n, cat1(bd(x), bd(y))), n_b, w1, u0)
    q1 = each(lambda x, y: x + y[:, :HGW], rt, nbo)
    y0 = each(lambda x, n, vb: x[:, HGW:] + _dot(n, vb), nbo, n_k, v_bd)
    bh = each(lambda x, y: x * y, beta, e_last)
    kh = each(lambda x, y: x * y, k, e_last)
    ri = lax.broadcasted_iota(jnp.int32, (HGW, HGW), 0) // B_HD
    ci = lax.broadcasted_iota(jnp.int32, (HGW, HGW), 1) // B_HD
    diag = ri == ci
    g_mat = each(lambda x, y: jnp.where(diag, _dot_tn(x, y), 0.0), w1, bh)
    h_mat = each(lambda x, y, z, w: jnp.where(diag, _dot_tn(cat0(x, y), cat0(z, w)), 0.0), u0, v, bh, kh)
    gam = each(jnp.exp, c_last)
    return q1, y0, g_mat, h_mat, gam


def _rwkv_body(pb_ref, shift_ref, s0_ref, mu_ref, w0_ref, w2_ref, a0_ref, a2_ref, g2_ref, kk_ref,
               ka_ref, rk_ref, lng_ref, lnb_ref, e_ref,
               o_ref, sout_ref,
               ext_ref, s_ref, *, tb, lc):
    t = pl.program_id(1)
    W = BRANCH_W
    n_groups = W // HGW

    @pl.when(t == 0)
    def _():
        ext_ref[0:SUBLANES, :] = jnp.broadcast_to(shift_ref[0], (SUBLANES, ext_ref.shape[1]))
        s_ref[...] = s0_ref[0]

    p = pb_ref[...]
    ext_ref[SUBLANES:, :] = p
    prev = ext_ref[SUBLANES - 1:SUBLANES - 1 + tb, :]
    xs = p + (prev - p) * mu_ref[...]
    ext_ref[0:SUBLANES, :] = p[tb - SUBLANES:, :]

    e = e_ref[...]
    r = xs[:, 0:W]
    k = xs[:, W:2 * W]
    v = xs[:, 2 * W:3 * W]
    wl = xs[:, 3 * W:3 * W + 128]
    al = xs[:, 3 * W + 128:3 * W + 256]
    gl = xs[:, 3 * W + 256:3 * W + 384]
    w_log = -_softplus_big(-(w0_ref[...] + _dot(jnp.tanh(wl), w2_ref[...]))) - 0.5
    a = _sigmoid(a0_ref[...] + _dot(al, a2_ref[...]))
    g = _dot(_sigmoid(gl), g2_ref[...])
    kkr = k * kk_ref[...]
    kk = kkr / jnp.maximum(jnp.sqrt(_segsum(kkr * kkr, e)), 1e-12)
    k2 = k * (1.0 + (a - 1.0) * ka_ref[...])
    lw = -jnp.exp(w_log)
    n_chunks = max(tb // lc, 1)

    def piece(x, ci, gi):
        blk = x[ci * lc:min((ci + 1) * lc, tb), gi * HGW:(gi + 1) * HGW]
        if tb < lc:
            blk = jnp.concatenate([blk, jnp.zeros((lc - tb, HGW), F32)], axis=0)
        return blk

    inst = [(ci, gi) for ci in range(n_chunks) for gi in range(n_groups)]
    terms = _rwkv_chunk_terms(*[[piece(x, ci, gi) for ci, gi in inst] for x in (r, lw, k2, v, kk, a)])
    y_rows = []
    for ci in range(n_chunks):
        y_cols = []
        for gi in range(n_groups):
            q1, y0, g_mat, h_mat, gam = [tm[ci * n_groups + gi] for tm in terms]
            s = s_ref[gi]
            y_cols.append(_dot_nt(q1, s) + y0)
            s_ref[gi] = s * gam + _dot(s, g_mat) + h_mat
        y_rows.append(jnp.concatenate(y_cols, axis=1))
    y = jnp.concatenate(y_rows, axis=0)[0:tb]
    mean = _segsum(y, e) * (1.0 / B_HD)
    yc = y - mean
    var = _segsum(yc * yc, e) * (1.0 / B_HD)
    yn = yc * lax.rsqrt(var + RWKV_GN_EPS) * lng_ref[...] + lnb_ref[...]
    bonus = _segsum(r * k2 * rk_ref[...], e) * v
    o_ref[...] = (yn + bonus) * g

    @pl.when(t == pl.num_programs(1) - 1)
    def _():
        sout_ref[0] = s_ref[...]


def _rwkv(pb, shift0, s0_bd, wts, nb, t):
    tb = _row_tile(t, 256)
    lc = CHUNK
    assert tb % lc == 0 or tb < lc
    nt = t // tb
    width = pb.shape[1]
    W = BRANCH_W
    n_groups = W // HGW
    row = lambda n: _resident((1, n))
    return pl.pallas_call(
        functools.partial(_rwkv_body, tb=tb, lc=lc),
        grid=(nb, nt),
        in_specs=[pl.BlockSpec((tb, width), lambda b, i: (b * nt + i, 0)),
                  pl.BlockSpec((1, 1, width), lambda b, i: (b, 0, 0)),
                  pl.BlockSpec((1, n_groups, HGW, HGW), lambda b, i: (b, 0, 0, 0)),
                  row(width), row(W), _resident((128, W)), row(W), _resident((128, W)),
                  _resident((128, W)), row(W), row(W), row(W), row(W), row(W), _resident((HGW, HGW))],
        out_specs=[pl.BlockSpec((tb, W), lambda b, i: (b * nt + i, 0)),
                   pl.BlockSpec((1, n_groups, HGW, HGW), lambda b, i: (b, 0, 0, 0))],
        out_shape=[jax.ShapeDtypeStruct((nb * t, W), F32),
                   jax.ShapeDtypeStruct((nb, n_groups, HGW, HGW), F32)],
        scratch_shapes=[pltpu.VMEM((tb + SUBLANES, width), F32),
                        pltpu.VMEM((n_groups, HGW, HGW), F32)],
        compiler_params=_cparams("parallel", "arbitrary"),
        name="rwkv7",
    )(pb, shift0, s0_bd, *wts, _seg_matrix(HGW, B_HD))


def _gla_chunk_terms(q, k, v, gk):
    each = lambda f, *seqs: [f(*xs) for xs in zip(*seqs)]
    L = q[0].shape[0]
    kw = C_HEADS * C_DK
    vw = C_HEADS * C_DV
    tril = _tril_ones(L)
    b = each(lambda x: _split_dot(tril, x, 3), gk)
    qe = each(lambda x, y: x * jnp.exp(y), q, b)
    ke = each(lambda x, y: x * jnp.exp(-y), k, b)
    a_all = each(lambda x, y: _dot_nt(x, _bd_rows(y, C_DK, C_HEADS)), qe, ke)
    row = lax.broadcasted_iota(jnp.int32, (L, C_HEADS * L), 0)
    col = lax.broadcasted_iota(jnp.int32, (L, C_HEADS * L), 1) % L
    causal = col <= row
    o_intra = each(lambda x, y: _dot(jnp.where(causal, x, 0.0), _bd_rows(y, C_DV, C_HEADS)), a_all, v)
    b_last = each(lambda x: x[L - 1:L, :], b)
    kd = each(lambda x, y, z: x * jnp.exp(z - y), k, b, b_last)
    ri = lax.broadcasted_iota(jnp.int32, (vw, kw), 0) // C_DV
    ci = lax.broadcasted_iota(jnp.int32, (vw, kw), 1) // C_DK
    diag = ri == ci
    upd = each(lambda x, y: jnp.where(diag, _dot_tn(x, y), 0.0), v, kd)
    decay = each(jnp.exp, b_last)
    return qe, o_intra, upd, decay


def _gla_body(pc_ref, s0_ref, g2_ref, gb_ref, norm_ref, o_ref, sout_ref,
              s_ref, *, tb, lc):
    t = pl.program_id(1)
    kw = C_HEADS * C_DK
    vw = C_HEADS * C_DV

    @pl.when(t == 0)
    def _():
        s_ref[...] = s0_ref[0]

    pc = pc_ref[...]
    og = pc[:, 2 * kw + vw:2 * kw + 2 * vw]
    gl = pc[:, 2 * kw + 2 * vw:2 * kw + 2 * vw + 128]
    z = _dot(gl, g2_ref[...]) + gb_ref[...]
    q = pc[:, 0:kw] * (C_DK ** -0.5)
    k = pc[:, kw:2 * kw]
    v = pc[:, 2 * kw:2 * kw + vw]
    gk = -_softplus_big(-z) * (1.0 / C_TAU)
    n_chunks = max(tb // lc, 1)

    def piece(x, ci):
        blk = x[ci * lc:min((ci + 1) * lc, tb), :]
        if tb < lc:
            blk = jnp.concatenate([blk, jnp.zeros((lc - tb, x.shape[1]), F32)], axis=0)
        return blk

    qe, o_intra, upd, decay = _gla_chunk_terms(
        *[[piece(x, ci) for ci in range(n_chunks)] for x in (q, k, v, gk)])
    y_rows = []
    for ci in range(n_chunks):
        st = s_ref[...]
        y_rows.append(_dot_nt(qe[ci], st) + o_intra[ci])
        s_ref[...] = st * decay[ci] + upd[ci]
    y = jnp.concatenate(y_rows, axis=0)[0:tb]
    outs = []
    for h in range(C_HEADS):
        yh = y[:, h * C_DV:(h + 1) * C_DV]
        outs.append(yh * lax.rsqrt(jnp.mean(yh * yh, axis=-1, keepdims=True) + EPS) * norm_ref[...])
    yn = jnp.concatenate(outs, axis=1)
    o_ref[...] = yn * (og * _sigmoid(og))

    @pl.when(t == pl.num_programs(1) - 1)
    def _():
        sout_ref[0] = s_ref[...]


def _gla(pc, s0_bd, g2, gb, norm, nb, t):
    tb = _row_tile(t, 256)
    lc = CHUNK
    assert tb % lc == 0 or tb < lc
    nt = t // tb
    width = pc.shape[1]
    kw, vw = C_HEADS * C_DK, C_HEADS * C_DV
    return pl.pallas_call(
        functools.partial(_gla_body, tb=tb, lc=lc),
        grid=(nb, nt),
        in_specs=[pl.BlockSpec((tb, width), lambda b, i: (b * nt + i, 0)),
                  pl.BlockSpec((1, vw, kw), lambda b, i: (b, 0, 0)),
                  _resident((128, kw)), _resident((1, kw)), _resident((1, C_DV))],
        out_specs=[pl.BlockSpec((tb, vw), lambda b, i: (b * nt + i, 0)),
                   pl.BlockSpec((1, vw, kw), lambda b, i: (b, 0, 0))],
        out_shape=[jax.ShapeDtypeStruct((nb * t, vw), F32),
                   jax.ShapeDtypeStruct((nb, vw, kw), F32)],
        scratch_shapes=[pltpu.VMEM((vw, kw), F32)],
        compiler_params=_cparams("parallel", "arbitrary"),
        name="gla",
    )(pc, s0_bd, g2, gb, norm)


def _shift_rows(x, d, fill):
    n = x.shape[0]
    if d % SUBLANES == 0:
        head = jnp.full((d, x.shape[1]), fill, x.dtype)
        return jnp.concatenate([head, x[:n - d]], axis=0)
    rolled = pltpu.roll(x, d, 0)
    row = lax.broadcasted_iota(jnp.int32, x.shape, 0)
    return jnp.where(row < d, fill, rolled)


def _lru_block(gate, xr, ext_ref, h_ref, au_ref, cw_ref, cb_ref, wa_ref, ba_ref, wx_ref, bx_ref,
               lam_ref, tb, fill=lambda: None):
    W = BRANCH_W
    ext_ref[SUBLANES:, :] = xr
    xc = cb_ref[...] + xr * cw_ref[CONV_W - 1:CONV_W, :]
    for j in range(CONV_W - 1):
        off = SUBLANES - (CONV_W - 1) + j
        xc = xc + ext_ref[off:off + tb, :] * cw_ref[j:j + 1, :]
    tail = ext_ref[tb:tb + SUBLANES, :]
    ext_ref[0:SUBLANES, :] = tail
    fill()

    r = _sigmoid(_dot(xc, wa_ref[...]) + ba_ref[...])
    fill()
    gi = _sigmoid(_dot(xc, wx_ref[...]) + bx_ref[...])
    fill()
    log_a = (-LRU_C) * r * _softplus(-lam_ref[...])
    a = jnp.exp(log_a)
    u = jnp.sqrt(1.0 - jnp.exp(2.0 * log_a)) * (gi * xc)
    fill()
    ng = tb // SUBLANES
    row_in_group = lax.broadcasted_iota(jnp.int32, (ng, SUBLANES, W), 1)
    a = a.reshape(ng, SUBLANES, W)
    u = u.reshape(ng, SUBLANES, W)
    d = 1
    while d < SUBLANES:
        inside = row_in_group >= d
        u = u + a * jnp.where(inside, pltpu.roll(u, d, 1), 0.0)
        a = a * jnp.where(inside, pltpu.roll(a, d, 1), 1.0)
        d *= 2
        fill()
    a = a.reshape(tb, W)
    u = u.reshape(tb, W)
    h_in = h_ref[...]
    if ng % SUBLANES == 0:
        ends = pl.ds(SUBLANES - 1, ng, stride=SUBLANES)
        for j in range(W // LANES):
            au_ref[0, j] = a[:, j * LANES:(j + 1) * LANES]
            au_ref[1, j] = u[:, j * LANES:(j + 1) * LANES]
        ae = jnp.concatenate([au_ref[0, j, ends, :] for j in range(W // LANES)], axis=1)
        ue = jnp.concatenate([au_ref[1, j, ends, :] for j in range(W // LANES)], axis=1)
        d = 1
        while d < ng:
            ue = ue + ae * _shift_rows(ue, d, 0.0)
            ae = ae * _shift_rows(ae, d, 1.0)
            d *= 2
        carry = _shift_rows(ue + ae * h_in, 1, h_in)
        carry_rows = jnp.broadcast_to(carry[:, None, :], (ng, SUBLANES, W)).reshape(tb, W)
    else:
        rows = []
        for g in range(ng):
            rows.append(jnp.broadcast_to(h_in, (SUBLANES, W)))
            end = (g + 1) * SUBLANES - 1
            h_in = u[end:end + 1, :] + a[end:end + 1, :] * h_in
        carry_rows = jnp.concatenate(rows, axis=0)
    h = u + a * carry_rows
    h_last = h[tb - 1:tb, :]
    h_ref[...] = h_last
    fill()
    out = h * (0.5 * gate * (1.0 + jnp.tanh(math.sqrt(2.0 / math.pi)
                                            * (gate + 0.044715 * gate * gate * gate))))
    return out, tail, h_last


def _lru_body(pd_ref, conv0_ref, h0_ref, cw_ref, cb_ref, wa_ref, ba_ref, wx_ref, bx_ref, lam_ref,
              o_ref, convout_ref, hout_ref, ext_ref, h_ref, au_ref, *, tb):
    t = pl.program_id(1)
    W = BRANCH_W

    @pl.when(t == 0)
    def _():
        ext_ref[0:SUBLANES, :] = conv0_ref[0]
        h_ref[...] = h0_ref[0]

    out, tail, h_last = _lru_block(pd_ref[:, 0:W], pd_ref[:, W:2 * W], ext_ref, h_ref, au_ref, cw_ref,
                                   cb_ref, wa_ref, ba_ref, wx_ref, bx_ref, lam_ref, tb)
    o_ref[...] = out

    @pl.when(t == pl.num_programs(1) - 1)
    def _():
        convout_ref[0] = tail
        hout_ref[0] = h_last


def _mix_in_body(x_ref, g_ref, wa_ref, wb_ref, wc_ref, wd_ref,
                 bias_ref, sink_ref, qn_ref, kn_ref, eq_ref, ek_ref, rep_ref,
                 conv0_ref, h0_ref, cw_ref, cb_ref, lwa_ref, lba_ref, lwx_ref, lbx_ref, lam_ref,
                 pb_ref, pc_ref, oa_ref, od_ref, kout_ref, vout_ref, convout_ref, hout_ref,
                 kv_tail_ref, ext_ref, h_ref, au_ref, *, tb):
    t = pl.program_id(1)
    W = BRANCH_W

    @pl.when(t == 0)
    def _():
        kv_tail_ref[...] = jnp.zeros(kv_tail_ref.shape, F32)
        ext_ref[0:SUBLANES, :] = conv0_ref[0]
        h_ref[...] = h0_ref[0]

    h = _rms(x_ref[...], g_ref[...]).astype(BF16)
    pd = jnp.dot(h, wd_ref[...], preferred_element_type=F32)
    pa = jnp.dot(h, wa_ref[...], preferred_element_type=F32)

    pending = [(w_ref, p_ref, c0, min(MXU_COLS, w_ref.shape[1] - c0))
               for w_ref, p_ref in ((wb_ref, pb_ref), (wc_ref, pc_ref))
               for c0 in range(0, w_ref.shape[1], MXU_COLS)]

    def fill():
        if pending:
            w_ref, p_ref, c0, width = pending.pop(0)
            p_ref[:, c0:c0 + width] = jnp.dot(h, w_ref[:, c0:c0 + width], preferred_element_type=F32)

    od, tail, h_last = _lru_block(pd[:, 0:W], pd[:, W:2 * W], ext_ref, h_ref, au_ref, cw_ref, cb_ref,
                                  lwa_ref, lba_ref, lwx_ref, lbx_ref, lam_ref, tb, fill)
    od_ref[...] = od
    outs = _attn_prompt_block(pa, kv_tail_ref, t, bias_ref, sink_ref, qn_ref, kn_ref, eq_ref, ek_ref,
                              rep_ref, tb, fill)
    for c, o in enumerate(outs):
        oa_ref[c * CHUNK:(c + 1) * CHUNK, :] = o
    while pending:
        fill()

    @pl.when(t == pl.num_programs(1) - 1)
    def _():
        kout_ref[0] = kv_tail_ref[0]
        vout_ref[0] = kv_tail_ref[1]
        convout_ref[0] = tail
        hout_ref[0] = h_last


def _mix_in(x, g, ws, bias, sink, qn, kn, conv0, h0, lru_wts, nb, t):
    n, d = x.shape
    tb = _row_tile(t, 512)
    assert tb % WINDOW == 0
    nt = t // tb
    W = BRANCH_W
    qw, kw = A_HEADS * A_HD, A_KV_HEADS * A_HD
    wa, wb, wc, wd = ws
    rows = lambda width: pl.BlockSpec((tb, width), lambda b, i: (b * nt + i, 0))
    per_seq = lambda *shape: pl.BlockSpec((1,) + shape, lambda b, i: (b,) + (0,) * len(shape))
    row = lambda width: _resident((1, width))
    return pl.pallas_call(
        functools.partial(_mix_in_body, tb=tb),
        grid=(nb, nt),
        in_specs=[rows(d), row(d)] + [_resident(w.shape) for w in ws]
                 + [_resident(bias.shape), _resident(sink.shape), row(qw), row(kw),
                    _resident((HGW, HGW)), _resident((kw, kw)), _resident((kw, qw)),
                    per_seq(SUBLANES, W), per_seq(1, W),
                    _resident((CONV_W, W)), row(W), _resident((W, W)), row(W), _resident((W, W)), row(W),
                    row(W)],
        out_specs=[rows(wb.shape[1]), rows(wc.shape[1]), rows(qw), rows(W),
                   per_seq(WINDOW, kw), per_seq(WINDOW, kw), per_seq(SUBLANES, W), per_seq(1, W)],
        out_shape=[jax.ShapeDtypeStruct((n, wb.shape[1]), F32), jax.ShapeDtypeStruct((n, wc.shape[1]), F32),
                   jax.ShapeDtypeStruct((n, qw), F32), jax.ShapeDtypeStruct((n, W), F32),
                   jax.ShapeDtypeStruct((nb, WINDOW, kw), F32), jax.ShapeDtypeStruct((nb, WINDOW, kw), F32),
                   jax.ShapeDtypeStruct((nb, SUBLANES, W), F32), jax.ShapeDtypeStruct((nb, 1, W), F32)],
        scratch_shapes=[pltpu.VMEM((2, WINDOW, kw), F32),
                        pltpu.VMEM((tb + SUBLANES, W), F32), pltpu.VMEM((1, W), F32),
                        pltpu.VMEM((2, W // LANES, tb, LANES), F32)],
        compiler_params=_cparams("parallel", "arbitrary"),
        name="mix_in",
    )(x, g, wa, wb, wc, wd, bias, sink, qn, kn, _seg_matrix(HGW, A_HD), _seg_matrix(kw, A_HD),
      _kv_replication(), conv0, h0, *lru_wts)


def _lru(pd, conv0, h0, wts, nb, t):
    tb = _row_tile(t, 256)
    nt = t // tb
    W = BRANCH_W
    row = lambda: _resident((1, W))
    return pl.pallas_call(
        functools.partial(_lru_body, tb=tb),
        grid=(nb, nt),
        in_specs=[pl.BlockSpec((tb, 2 * W), lambda b, i: (b * nt + i, 0)),
                  pl.BlockSpec((1, SUBLANES, W), lambda b, i: (b, 0, 0)),
                  pl.BlockSpec((1, 1, W), lambda b, i: (b, 0, 0)),
                  _resident((CONV_W, W)), row(), _resident((W, W)), row(), _resident((W, W)), row(),
                  row()],
        out_specs=[pl.BlockSpec((tb, W), lambda b, i: (b * nt + i, 0)),
                   pl.BlockSpec((1, SUBLANES, W), lambda b, i: (b, 0, 0)),
                   pl.BlockSpec((1, 1, W), lambda b, i: (b, 0, 0))],
        out_shape=[jax.ShapeDtypeStruct((nb * t, W), F32),
                   jax.ShapeDtypeStruct((nb, SUBLANES, W), F32),
                   jax.ShapeDtypeStruct((nb, 1, W), F32)],
        scratch_shapes=[pltpu.VMEM((tb + SUBLANES, W), F32), pltpu.VMEM((1, W), F32),
                        pltpu.VMEM((2, W // LANES, tb, LANES), F32)],
        compiler_params=_cparams("parallel", "arbitrary"),
        name="rglru",
    )(pd, conv0, h0, *wts)


def _block_diag(w):
    n, a, b = w.shape
    eye = jnp.eye(n, dtype=w.dtype)
    return (eye[:, None, :, None] * w[:, :, None, :]).reshape(n * a, n * b)


def _pad_rows_to(w, rows):
    return jnp.pad(w, ((0, rows - w.shape[0]), (0, 0)))


def _layer_weights(i, W):
    bf = lambda a: a.astype(BF16)
    row = lambda a: a.reshape(1, -1).astype(F32)
    bw = BRANCH_W
    a_cols = A_HEADS * A_HD + 2 * A_KV_HEADS * A_HD
    b_cols = 3 * bw + B_W_RANK + B_A_RANK + B_G_RANK
    c_cols = 2 * C_HEADS * C_DK + C_HEADS * C_DV + C_G_RANK + bw
    w_in = W['w_in'][i]
    d = w_in.shape[0]
    wa = w_in[:, :a_cols]
    wb = w_in[:, a_cols:a_cols + b_cols]
    wc = w_in[:, a_cols + b_cols:a_cols + b_cols + c_cols]
    wd = w_in[:, a_cols + b_cols + c_cols:]
    zpad = lambda n: jnp.zeros((d, n), w_in.dtype)
    wb2 = jnp.concatenate([wb[:, :3 * bw], wb[:, 3 * bw:3 * bw + B_W_RANK], zpad(128 - B_W_RANK),
                           wb[:, 3 * bw + B_W_RANK:3 * bw + B_W_RANK + B_A_RANK], zpad(128 - B_A_RANK),
                           wb[:, 3 * bw + B_W_RANK + B_A_RANK:]], axis=1)
    mu = W['rwkv_mu'][i]
    z1 = lambda n: jnp.zeros((n,), F32)
    mu2 = jnp.concatenate([mu[:3 * bw], mu[3 * bw:3 * bw + B_W_RANK], z1(128 - B_W_RANK),
                           mu[3 * bw + B_W_RANK:3 * bw + B_W_RANK + B_A_RANK], z1(128 - B_A_RANK),
                           mu[3 * bw + B_W_RANK + B_A_RANK:]])
    qkv = 2 * C_HEADS * C_DK + C_HEADS * C_DV
    wc2 = jnp.concatenate([wc[:, :qkv], wc[:, qkv + C_G_RANK:], wc[:, qkv:qkv + C_G_RANK],
                           zpad(128 - C_G_RANK)], axis=1)
    return dict(
        ffn1=(row(W['g_ffn1'][i]), bf(W['w_ffn1_gate'][i]), bf(W['w_ffn1_up'][i]), bf(W['w_ffn1_down'][i])),
        ffn2=(row(W['g_ffn2'][i]), bf(W['w_ffn2_gate'][i]), bf(W['w_ffn2_up'][i]), bf(W['w_ffn2_down'][i])),
        g_mix=row(W['g_mix'][i]),
        w_in=(bf(wa), bf(wb2), bf(wc2), bf(wd)),
        q_norm=row(jnp.tile(W['q_norm'][i], A_HEADS)),
        k_norm=row(jnp.tile(W['k_norm'][i], A_KV_HEADS)),
        sink=W['attn_sink'][i].astype(F32),
        rwkv=(row(mu2), row(W['rwkv_w0'][i]), bf(_pad_rows_to(W['rwkv_w2'][i], 128)),
              row(W['rwkv_a0'][i]), bf(_pad_rows_to(W['rwkv_a2'][i], 128)), bf(W['rwkv_g2'][i]),
              row(W['rwkv_k_k'][i]), row(W['rwkv_k_a'][i]), row(W['rwkv_r_k'][i]),
              row(W['rwkv_ln_g'][i]), row(W['rwkv_ln_b'][i])),
        gla=(bf(_pad_rows_to(W['gla_g2'][i], 128)), row(W['gla_gb'][i]), row(W['gla_norm'][i])),
        lru=(W['lru_conv_w'][i].astype(F32), row(W['lru_conv_b'][i]),
             bf(_block_diag(W['lru_wa'][i])), row(W['lru_ba'][i]),
             bf(_block_diag(W['lru_wx'][i])), row(W['lru_bx'][i]), row(W['lru_lambda'][i])),
        merge=(bf(W['w_merge_gate'][i]), bf(W['w_branch'][i]), bf(W['w_out'][i])),
        ple=(row(W['g_ple'][i]), bf(W['w_ple_gate'][i]), bf(W['w_ple_proj'][i])),
    )


def _b_cols_to_kernel(x):
    bw = BRANCH_W
    z = jnp.zeros(x.shape[:-1] + (128 - B_W_RANK,), x.dtype)
    return jnp.concatenate([x[..., :3 * bw], x[..., 3 * bw:3 * bw + B_W_RANK], z,
                            x[..., 3 * bw + B_W_RANK:3 * bw + B_W_RANK + B_A_RANK], z,
                            x[..., 3 * bw + B_W_RANK + B_A_RANK:]], axis=-1)


def _b_cols_from_kernel(x):
    bw = BRANCH_W
    return jnp.concatenate([x[..., :3 * bw + B_W_RANK], x[..., 3 * bw + 128:3 * bw + 128 + B_A_RANK],
                            x[..., 3 * bw + 256:]], axis=-1)


def _heads_to_bd(s, hg):
    nb, h, a, b = s.shape
    s = s.reshape(nb, h // hg, hg, a, b)
    eye = jnp.eye(hg, dtype=s.dtype)
    return (eye[None, None, :, None, :, None] * s[:, :, :, :, None, :]).reshape(nb, h // hg, hg * a, hg * b)


def _bd_to_heads(s, hg):
    nb, ng, ra, cb = s.shape
    a, b = ra // hg, cb // hg
    s = s.reshape(nb, ng, hg, a, hg, b)
    idx = jnp.arange(hg)
    return s[:, :, idx, :, idx, :].transpose(1, 2, 0, 3, 4).reshape(nb, ng * hg, a, b)


def _trunk_layer(x, pe, st, lw, bias, first_chunk):
    nb, t, d = x.shape
    n = nb * t
    ck, cv, shift0, s_rwkv0, s_gla0, conv0, lru0 = st
    x2 = x.reshape(n, d)
    x2 = _ffn(x2, *lw['ffn1'])
    kw = A_KV_HEADS * A_HD
    n_q = bias.shape[1] // A_HEADS
    sink2 = jnp.repeat(lw['sink'], n_q).reshape(1, A_HEADS * n_q)
    conv_pad = jnp.pad(conv0, ((0, 0), (SUBLANES - (CONV_W - 1), 0), (0, 0)))
    if first_chunk:
        pb, pc, o_a, o_d, k_win, v_win, conv_out, h_out = _mix_in(
            x2, lw['g_mix'], lw['w_in'], bias, sink2, lw['q_norm'], lw['k_norm'],
            conv_pad, lru0[:, None, :], lw['lru'], nb, t)
    else:
        pa, pb, pc, pd = _inproj(x2, lw['g_mix'], lw['w_in'])
        o_a, k_win, v_win = _attn_sample(pa, ck.reshape(nb, WINDOW, kw), cv.reshape(nb, WINDOW, kw),
                                         bias, sink2, lw['q_norm'], lw['k_norm'], nb, t)
        o_d, conv_out, h_out = _lru(pd, conv_pad, lru0[:, None, :], lw['lru'], nb, t)
    k_win = k_win.reshape(nb, WINDOW, A_KV_HEADS, A_HD)
    v_win = v_win.reshape(nb, WINDOW, A_KV_HEADS, A_HD)

    o_b, s_bd = _rwkv(pb, _b_cols_to_kernel(shift0)[:, None, :], _heads_to_bd(s_rwkv0, HG),
                      lw['rwkv'], nb, t)
    shift1 = _b_cols_from_kernel(pb.reshape(nb, t, -1)[:, -1])
    s_rwkv1 = _bd_to_heads(s_bd, HG)

    st_bd0 = _heads_to_bd(jnp.swapaxes(s_gla0, -1, -2), C_HEADS)[:, 0]
    o_c, st_bd = _gla(pc, st_bd0, *lw['gla'], nb, t)
    s_gla1 = jnp.swapaxes(_bd_to_heads(st_bd[:, None], C_HEADS), -1, -2)

    conv1 = conv_out[:, SUBLANES - (CONV_W - 1):]
    lru1 = h_out[:, 0]

    x2 = _merge(x2, (o_a, o_b, o_c, o_d), lw['g_mix'], *lw['merge'])
    x2 = _ffn_ple(x2, pe.reshape(n, -1), *lw['ffn2'], *lw['ple'])
    return x2.reshape(nb, t, d), (k_win, v_win, shift1, s_rwkv1, s_gla1, conv1, lru1)


def kernel(x_prompt, x_sample, cache_attn_k, cache_attn_v, state_rwkv_shift, state_rwkv, state_gla, state_lru_conv, state_lru, p_prompt, p_sample, rel_bias_table, g_ffn1, w_ffn1_gate, w_ffn1_up, w_ffn1_down, g_mix, w_in, q_norm, k_norm, attn_sink, rwkv_mu, rwkv_w0, rwkv_w2, rwkv_a0, rwkv_a2, rwkv_g2, rwkv_k_k, rwkv_k_a, rwkv_r_k, rwkv_ln_g, rwkv_ln_b, gla_g2, gla_gb, gla_norm, lru_conv_w, lru_conv_b, lru_wa, lru_ba, lru_wx, lru_bx, lru_lambda, w_merge_gate, w_branch, w_out, g_ffn2, w_ffn2_gate, w_ffn2_up, w_ffn2_down, g_ple, w_ple_gate, w_ple_proj):
    W = dict(g_ffn1=g_ffn1, w_ffn1_gate=w_ffn1_gate, w_ffn1_up=w_ffn1_up, w_ffn1_down=w_ffn1_down,
             g_mix=g_mix, w_in=w_in, q_norm=q_norm, k_norm=k_norm, attn_sink=attn_sink,
             rwkv_mu=rwkv_mu, rwkv_w0=rwkv_w0, rwkv_w2=rwkv_w2, rwkv_a0=rwkv_a0, rwkv_a2=rwkv_a2,
             rwkv_g2=rwkv_g2, rwkv_k_k=rwkv_k_k, rwkv_k_a=rwkv_k_a, rwkv_r_k=rwkv_r_k,
             rwkv_ln_g=rwkv_ln_g, rwkv_ln_b=rwkv_ln_b, gla_g2=gla_g2, gla_gb=gla_gb,
             gla_norm=gla_norm, lru_conv_w=lru_conv_w, lru_conv_b=lru_conv_b, lru_wa=lru_wa,
             lru_ba=lru_ba, lru_wx=lru_wx, lru_bx=lru_bx, lru_lambda=lru_lambda,
             w_merge_gate=w_merge_gate, w_branch=w_branch, w_out=w_out, g_ffn2=g_ffn2,
             w_ffn2_gate=w_ffn2_gate, w_ffn2_up=w_ffn2_up, w_ffn2_down=w_ffn2_down,
             g_ple=g_ple, w_ple_gate=w_ple_gate, w_ple_proj=w_ple_proj)
    depth = w_in.shape[0]
    dt = x_prompt.dtype
    bp, tp = x_prompt.shape[:2]
    ts = x_sample.shape[1]
    b_cols = state_rwkv_shift.shape[-1]
    bias_p = _rel_bias(rel_bias_table, CHUNK, WINDOW + CHUNK)
    bias_s = _rel_bias(rel_bias_table, ts, WINDOW + ts)
    yp, ys = x_prompt, x_sample
    st_p, st_s = [], []
    for i in range(depth):
        lw = _layer_weights(i, W)
        zero_st = (None, None,
                   jnp.zeros((bp, b_cols), dt),
                   jnp.zeros((bp, B_HEADS, B_HD, B_HD), dt),
                   jnp.zeros((bp, C_HEADS, C_DK, C_DV), dt),
                   jnp.zeros((bp, CONV_W - 1, BRANCH_W), dt),
                   jnp.zeros((bp, BRANCH_W), dt))
        yp, sp = _trunk_layer(yp, p_prompt[i], zero_st, lw, bias_p, True)
        cache_st = (cache_attn_k[i], cache_attn_v[i], state_rwkv_shift[i], state_rwkv[i],
                    state_gla[i], state_lru_conv[i], state_lru[i])
        ys, ss = _trunk_layer(ys, p_sample[i], cache_st, lw, bias_s, False)
        st_p.append(sp)
        st_s.append(ss)
    stack = lambda states, j: jnp.stack([s[j] for s in states])
    return (yp, ys) + tuple(stack(st_p, j) for j in range(7)) + tuple(stack(st_s, j) for j in range(7))
```

```python
import functools
import math

import numpy as np
import jax
import jax.numpy as jnp
from jax import lax
from jax.experimental import pallas as pl
from jax.experimental.pallas import tpu as pltpu

F32 = jnp.float32
BF16 = jnp.bfloat16

V7X_VMEM_BYTES = 64 * 1024 * 1024
VMEM_LIMIT = V7X_VMEM_BYTES - 8 * 1024 * 1024
SUBLANES = 8
LANES = 128
MXU_COLS = 256

EPS = 1e-6
NEG_INF = -1e30
CHUNK = 64
WINDOW = 128
N_BUCKETS = 32
MAX_DIST = 128
A_HEADS, A_KV_HEADS, A_HD = 8, 2, 64
A_GROUP = A_HEADS // A_KV_HEADS
B_HEADS, B_HD = 8, 64
B_W_RANK, B_A_RANK, B_G_RANK = 64, 64, 128
RWKV_GN_EPS = 64e-5
C_HEADS, C_DK, C_DV = 4, 64, 128
C_G_RANK = 16
C_TAU = 16.0
D_BLOCKS = 8
CONV_W = 4
LRU_C = 8.0
BRANCH_W = 512
HG = 4
HGW = HG * B_HD


def _cparams(*sem):
    return pltpu.CompilerParams(dimension_semantics=sem, vmem_limit_bytes=VMEM_LIMIT)


def _resident(shape):
    nd = len(shape)
    return pl.BlockSpec(shape, lambda *_: (0,) * nd, pipeline_mode=pl.Buffered(1))


def _row_tile(n, cap):
    t = min(n, cap)
    while n % t:
        t //= 2
    return t


def _dot(a, b):
    return jnp.dot(a.astype(BF16), b.astype(BF16), preferred_element_type=F32)


def _dot_nt(a, b):
    return lax.dot_general(a.astype(BF16), b.astype(BF16), (((1,), (1,)), ((), ())),
                           preferred_element_type=F32)


def _dot_tn(a, b):
    return lax.dot_general(a.astype(BF16), b.astype(BF16), (((0,), (0,)), ((), ())),
                           preferred_element_type=F32)


def _rms(x, g):
    return x * lax.rsqrt(jnp.mean(x * x, axis=-1, keepdims=True) + EPS) * g


def _sigmoid(x):
    return 1.0 / (1.0 + jnp.exp(-x))


def _softplus(x):
    return jnp.maximum(x, 0.0) + jnp.log1p(jnp.exp(-jnp.abs(x)))


def _softplus_big(x):
    return jnp.maximum(x, 0.0) + jnp.log(1.0 + jnp.exp(-jnp.abs(x)))


def _split_dot(e_lhs, x, terms):
    acc = None
    rem = x
    for n in range(terms):
        piece = rem.astype(BF16)
        d = jnp.dot(e_lhs, piece, preferred_element_type=F32)
        acc = d if acc is None else acc + d
        if n + 1 < terms:
            rem = rem - piece.astype(F32)
    return acc


def _segsum(x, e):
    blk = e.shape[0]
    xb = x.astype(BF16)
    cols = [jnp.dot(xb[:, j:j + blk], e, preferred_element_type=F32)
            for j in range(0, x.shape[1], blk)]
    return cols[0] if len(cols) == 1 else jnp.concatenate(cols, axis=1)


def _tril_ones(n):
    r = lax.broadcasted_iota(jnp.int32, (n, n), 0)
    c = lax.broadcasted_iota(jnp.int32, (n, n), 1)
    return jnp.where(r >= c, 1.0, 0.0).astype(BF16)


def _bd_rows(x, blk, nblk):
    lane_blk = lax.broadcasted_iota(jnp.int32, x.shape, 1) // blk
    return jnp.concatenate([jnp.where(lane_blk == h, x, 0.0) for h in range(nblk)], axis=0)


def _seg_matrix(width, seg):
    i = np.arange(width)
    return jnp.asarray((i[:, None] // seg) == (i[None, :] // seg), dtype=BF16)


def _ffn_chunks(f):
    tiles = -(-f // MXU_COLS)
    cut = min(f, (tiles + 1) // 2 * MXU_COLS)
    return [(0, cut)] + ([(cut, f - cut)] if cut < f else [])


def _swiglu_residual(x, g_ref, wg_ref, wu_ref, wd_ref):
    h = _rms(x, g_ref[...]).astype(BF16)
    acc = None
    for start, size in _ffn_chunks(wg_ref.shape[1]):
        sl = pl.ds(start, size)
        gt = jnp.dot(h, wg_ref[:, sl], preferred_element_type=F32)
        up = jnp.dot(h, wu_ref[:, sl], preferred_element_type=F32)
        act = (gt * _sigmoid(gt) * up).astype(BF16)
        d = jnp.dot(act, wd_ref[sl, :], preferred_element_type=F32)
        acc = d if acc is None else acc + d
    return x + 0.5 * acc


def _ffn_body(x_ref, g_ref, wg_ref, wu_ref, wd_ref, o_ref):
    o_ref[...] = _swiglu_residual(x_ref[...], g_ref, wg_ref, wu_ref, wd_ref)


def _ffn_ple_body(x_ref, pe_ref, g_ref, wg_ref, wu_ref, wd_ref, gp_ref, wpg_ref, wpp_ref, o_ref):
    x = _swiglu_residual(x_ref[...], g_ref, wg_ref, wu_ref, wd_ref)
    h = _rms(x, gp_ref[...]).astype(BF16)
    gate = _sigmoid(jnp.dot(h, wpg_ref[...], preferred_element_type=F32))
    o_ref[...] = x + gate * jnp.dot(pe_ref[...].astype(BF16), wpp_ref[...], preferred_element_type=F32)


def _ffn(x, g, wg, wu, wd):
    n, d = x.shape
    f = wg.shape[1]
    tm = _row_tile(n, 512)
    return pl.pallas_call(
        _ffn_body,
        grid=(n // tm,),
        in_specs=[pl.BlockSpec((tm, d), lambda i: (i, 0)),
                  _resident((1, d)), _resident((d, f)), _resident((d, f)), _resident((f, d))],
        out_specs=pl.BlockSpec((tm, d), lambda i: (i, 0)),
        out_shape=jax.ShapeDtypeStruct((n, d), F32),
        compiler_params=_cparams("parallel"),
        name="ffn",
    )(x, g, wg, wu, wd)


def _inproj_body(x_ref, g_ref, wa_ref, wb_ref, wc_ref, wd_ref, pa_ref, pb_ref, pc_ref, pd_ref):
    h = _rms(x_ref[...], g_ref[...]).astype(BF16)
    for w_ref, p_ref in ((wa_ref, pa_ref), (wb_ref, pb_ref), (wc_ref, pc_ref), (wd_ref, pd_ref)):
        p_ref[...] = jnp.dot(h, w_ref[...], preferred_element_type=F32)


def _inproj(x, g, ws):
    n, d = x.shape
    tm = _row_tile(n, 512)
    widths = [w.shape[1] for w in ws]
    return pl.pallas_call(
        _inproj_body,
        grid=(n // tm,),
        in_specs=[pl.BlockSpec((tm, d), lambda i: (i, 0)), _resident((1, d))]
                 + [_resident((d, w)) for w in widths],
        out_specs=[pl.BlockSpec((tm, w), lambda i: (i, 0)) for w in widths],
        out_shape=[jax.ShapeDtypeStruct((n, w), F32) for w in widths],
        compiler_params=_cparams("parallel"),
        name="inproj",
    )(x, g, *ws)


def _merge_body(x_ref, oa_ref, ob_ref, oc_ref, od_ref, g_ref, wmg_ref, wb_ref, wo_ref, o_ref):
    x = x_ref[...]
    h = _rms(x, g_ref[...]).astype(BF16)
    y = None
    for n, b_ref in enumerate((oa_ref, ob_ref, oc_ref, od_ref)):
        gate = _sigmoid(jnp.dot(h, wmg_ref[n], preferred_element_type=F32))
        t = gate * jnp.dot(b_ref[...].astype(BF16), wb_ref[n], preferred_element_type=F32)
        y = t if y is None else y + t
    o_ref[...] = x + jnp.dot(y.astype(BF16), wo_ref[...], preferred_element_type=F32)


def _merge(x, outs, g, wmg, wb, wo):
    n, d = x.shape
    bw = outs[0].shape[1]
    tm = _row_tile(n, 512)
    return pl.pallas_call(
        _merge_body,
        grid=(n // tm,),
        in_specs=[pl.BlockSpec((tm, d), lambda i: (i, 0))]
                 + [pl.BlockSpec((tm, bw), lambda i: (i, 0))] * 4
                 + [_resident((1, d)), _resident(wmg.shape), _resident(wb.shape), _resident(wo.shape)],
        out_specs=pl.BlockSpec((tm, d), lambda i: (i, 0)),
        out_shape=jax.ShapeDtypeStruct((n, d), F32),
        compiler_params=_cparams("parallel"),
        name="merge",
    )(x, *outs, g, wmg, wb, wo)


def _ffn_ple(x, pe, g, wg, wu, wd, gp, wpg, wpp):
    n, d = x.shape
    f = wg.shape[1]
    pd = pe.shape[1]
    tm = _row_tile(n, 512)
    return pl.pallas_call(
        _ffn_ple_body,
        grid=(n // tm,),
        in_specs=[pl.BlockSpec((tm, d), lambda i: (i, 0)), pl.BlockSpec((tm, pd), lambda i: (i, 0)),
                  _resident((1, d)), _resident((d, f)), _resident((d, f)), _resident((f, d)),
                  _resident((1, d)), _resident((d, d)), _resident((pd, d))],
        out_specs=pl.BlockSpec((tm, d), lambda i: (i, 0)),
        out_shape=jax.ShapeDtypeStruct((n, d), F32),
        compiler_params=_cparams("parallel"),
        name="ffn_ple",
    )(x, pe, g, wg, wu, wd, gp, wpg, wpp)


def _t5_bucket_np(rel):
    half = N_BUCKETS // 2
    max_exact = half // 2
    ret = np.where(rel > 0, half, 0)
    n = np.abs(rel)
    nf = np.maximum(n, 1).astype(np.float32)
    large = max_exact + (np.log(nf / np.float32(max_exact)) / np.float32(math.log(MAX_DIST / max_exact))
                         * np.float32(half - max_exact)).astype(np.int32)
    large = np.minimum(large, half - 1)
    return (ret + np.where(n < max_exact, n, large)).astype(np.int32)


def _bias_body(idx_ref, table_ref, o_ref):
    idx = idx_ref[...]
    for h in range(A_HEADS):
        acc = jnp.zeros(idx.shape, F32)
        for b in range(N_BUCKETS):
            acc = jnp.where(idx == b, table_ref[b, h], acc)
        o_ref[h] = acc


def _rel_bias(table, n_q, n_k):
    rel = np.arange(n_k)[:, None] - WINDOW - np.arange(n_q)[None, :]
    idx = jnp.asarray(_t5_bucket_np(rel))
    out = pl.pallas_call(
        _bias_body,
        in_specs=[pl.BlockSpec(memory_space=pltpu.VMEM), pl.BlockSpec(memory_space=pltpu.SMEM)],
        out_specs=pl.BlockSpec(memory_space=pltpu.VMEM),
        out_shape=jax.ShapeDtypeStruct((A_HEADS, n_k, n_q), F32),
        name="rel_bias",
    )(idx, table)
    return jnp.transpose(out, (1, 0, 2)).reshape(n_k, A_HEADS * n_q)


def _head_rms(x, e, g):
    ms = _segsum(x * x, e) * (1.0 / A_HD)
    return x * lax.rsqrt(ms + EPS) * g


def _kv_replication():
    lane = np.arange(A_HEADS * A_HD)
    src = (lane // (A_GROUP * A_HD)) * A_HD + lane % A_HD
    return jnp.asarray(np.arange(A_KV_HEADS * A_HD)[:, None] == src[None, :], dtype=BF16)


def _attend_blocks(q_list, k8_list, v8_list, bias, sink, valid_list, fill=lambda: None):
    def staged(f, *seqs):
        out = []
        for j, xs in enumerate(zip(*seqs)):
            out.append(f(*xs))
            if j % 4 == 3:
                fill()
        return out

    n = q_list[0].shape[0]
    lane_head = lax.broadcasted_iota(jnp.int32, (n, A_HEADS * A_HD), 1) // A_HD
    q_bd = staged(lambda q: jnp.concatenate([jnp.where(lane_head == h, q, 0.0)
                                             for h in range(A_HEADS)], axis=0), q_list)
    s = staged(lambda k8, qb, valid: (_dot_nt(k8, qb) + bias if valid is None else
                                      jnp.where(valid, _dot_nt(k8, qb) + bias, NEG_INF)),
               k8_list, q_bd, valid_list)
    m = staged(lambda x: jnp.maximum(jnp.max(x, axis=0, keepdims=True), sink), s)
    e = staged(lambda x, mm: jnp.exp(x - mm), s, m)
    rinv = [1.0 / (jnp.sum(x, axis=0, keepdims=True) + jnp.exp(sink - mm)) for x, mm in zip(e, m)]
    p = staged(lambda x, r: x * r, e, rinv)
    o8 = staged(_dot_tn, p, v8_list)

    def pick_heads(x):
        o = None
        for h in range(A_HEADS):
            part = jnp.where(lane_head == h, x[h * n:(h + 1) * n], 0.0)
            o = part if o is None else o + part
        return o

    return staged(pick_heads, o8)


def _attn_prompt_block(pa, kv_tail_ref, t_idx, bias_ref, sink_ref, qn_ref, kn_ref, eq_ref, ek_ref,
                       rep_ref, tb, fill):
    qw = A_HEADS * A_HD
    kw = A_KV_HEADS * A_HD
    band = WINDOW + CHUNK
    rep = rep_ref[...]
    q = _head_rms(pa[:, 0:qw], eq_ref[...], qn_ref[...]) * (A_HD ** -0.5)
    kf = jnp.concatenate([kv_tail_ref[0], _head_rms(pa[:, qw:qw + kw], ek_ref[...], kn_ref[...])], axis=0)
    vf = jnp.concatenate([kv_tail_ref[1], pa[:, qw + kw:qw + 2 * kw]], axis=0)
    kv_tail_ref[0] = kf[tb:tb + WINDOW, :]
    kv_tail_ref[1] = vf[tb:tb + WINDOW, :]
    k8 = jnp.dot(kf.astype(BF16), rep, preferred_element_type=F32).astype(BF16)
    v8 = jnp.dot(vf.astype(BF16), rep, preferred_element_type=F32).astype(BF16)
    kidx = lax.broadcasted_iota(jnp.int32, (band, A_HEADS * CHUNK), 0)
    n_chunks = tb // CHUNK
    valid = [(kidx + (t_idx * tb + c * CHUNK - WINDOW)) >= 0 if c * CHUNK < WINDOW else None
             for c in range(n_chunks)]
    return _attend_blocks([q[c * CHUNK:(c + 1) * CHUNK] for c in range(n_chunks)],
                          [k8[c * CHUNK:c * CHUNK + band] for c in range(n_chunks)],
                          [v8[c * CHUNK:c * CHUNK + band] for c in range(n_chunks)],
                          bias_ref[...], sink_ref[...], valid, fill)


def _attn_sample_body(pa_ref, ck_ref, cv_ref, bias_ref, sink_ref, qn_ref, kn_ref, eq_ref, ek_ref, rep_ref,
                      o_ref, kout_ref, vout_ref, *, s, gb):
    qw = A_HEADS * A_HD
    kw = A_KV_HEADS * A_HD
    rep = rep_ref[...]
    q = _head_rms(pa_ref[:, 0:qw], eq_ref[...], qn_ref[...]) * (A_HD ** -0.5)
    kn = _head_rms(pa_ref[:, qw:qw + kw], ek_ref[...], kn_ref[...])
    vn = pa_ref[:, qw + kw:qw + 2 * kw]
    q_list, k8_list, v8_list = [], [], []
    for b in range(gb):
        kf = jnp.concatenate([ck_ref[b], kn[b * s:(b + 1) * s]], axis=0)
        vf = jnp.concatenate([cv_ref[b], vn[b * s:(b + 1) * s]], axis=0)
        kout_ref[b] = kf[s:s + WINDOW, :]
        vout_ref[b] = vf[s:s + WINDOW, :]
        q_list.append(q[b * s:(b + 1) * s])
        k8_list.append(jnp.dot(kf.astype(BF16), rep, preferred_element_type=F32).astype(BF16))
        v8_list.append(jnp.dot(vf.astype(BF16), rep, preferred_element_type=F32).astype(BF16))
    outs = _attend_blocks(q_list, k8_list, v8_list, bias_ref[...], sink_ref[...], [None] * gb)
    for b in range(gb):
        o_ref[b * s:(b + 1) * s, :] = outs[b]


def _attn_sample(pa, ck, cv, bias, sink, qn, kn, nb, s):
    qw, kw = A_HEADS * A_HD, A_KV_HEADS * A_HD
    width = pa.shape[1]
    gb = _row_tile(nb, 8)
    return pl.pallas_call(
        functools.partial(_attn_sample_body, s=s, gb=gb),
        grid=(nb // gb,),
        in_specs=[pl.BlockSpec((gb * s, width), lambda b: (b, 0)),
                  pl.BlockSpec((gb, WINDOW, kw), lambda b: (b, 0, 0)),
                  pl.BlockSpec((gb, WINDOW, kw), lambda b: (b, 0, 0)),
                  _resident(bias.shape), _resident(sink.shape),
                  _resident((1, qw)), _resident((1, kw)), _resident((HGW, HGW)), _resident((kw, kw)),
                  _resident((kw, qw))],
        out_specs=[pl.BlockSpec((gb * s, qw), lambda b: (b, 0)),
                   pl.BlockSpec((gb, WINDOW, kw), lambda b: (b, 0, 0)),
                   pl.BlockSpec((gb, WINDOW, kw), lambda b: (b, 0, 0))],
        out_shape=[jax.ShapeDtypeStruct((nb * s, qw), F32),
                   jax.ShapeDtypeStruct((nb, WINDOW, kw), F32),
                   jax.ShapeDtypeStruct((nb, WINDOW, kw), F32)],
        compiler_params=_cparams("parallel"),
        name="attn_sample",
    )(pa, ck, cv, bias, sink, qn, kn, _seg_matrix(HGW, A_HD), _seg_matrix(kw, A_HD), _kv_replication())


def _rwkv_chunk_terms(r, lw, k, v, kk, a):
    each = lambda f, *seqs: [f(*xs) for xs in zip(*seqs)]
    cat0 = lambda *xs: jnp.concatenate(xs, axis=0)
    cat1 = lambda *xs: jnp.concatenate(xs, axis=1)
    bd = lambda x: _bd_rows(x, B_HD, HG)
    L = r[0].shape[0]
    tril = _tril_ones(L)
    c = each(lambda x: _split_dot(tril, x, 3), lw)
    c_last = each(lambda x: x[L - 1:L, :], c)
    e_last = each(lambda x, xl: jnp.exp(xl - x), c, c_last)
    beta = each(lambda x, y: x * y, kk, a)
    at = each(lambda x, cc, l: -x * jnp.exp(cc - l), kk, c, lw)
    rt = each(lambda x, cc: x * jnp.exp(cc), r, c)
    enc = each(lambda cc: jnp.exp(-cc), c)
    ar = each(cat0, at, rt)
    mb = each(lambda x, b, e: _dot_nt(x, bd(b * e)), ar, beta, enc)
    mk = each(lambda x, b, e: _dot_nt(x, bd(b * e)), ar, k, enc)
    row = lax.broadcasted_iota(jnp.int32, (L, HG * L), 0)
    col = lax.broadcasted_iota(jnp.int32, (L, HG * L), 1) % L
    strict = col < row
    incl = col <= row
    m_b = each(lambda x: jnp.where(strict, x[:L], 0.0), mb)
    m_k = each(lambda x: jnp.where(strict, x[:L], 0.0), mk)
    n_b = each(lambda x: jnp.where(incl, x[L:], 0.0), mb)
    n_k = each(lambda x: jnp.where(incl, x[L:], 0.0), mk)
    eye = jnp.where(col == row, 1.0, 0.0)
    t_inv = each(lambda x: eye + x, m_b)
    p = m_b
    for lvl in range(1, int(math.log2(L))):
        p_bd = each(lambda x: _bd_rows(x, L, HG), p)
        if lvl == 1:
            p = each(_dot, p, p_bd)
        else:
            tp = each(lambda t, x, xb: _dot(cat0(t, x), xb), t_inv, p, p_bd)
            t_inv = each(lambda t, x: t + x[:L], t_inv, tp)
            p = each(lambda x: x[L:], tp)
    t_inv = each(lambda t, x: t + _dot(t, _bd_rows(x, L, HG)), t_inv, p)
    v_bd = each(bd, v)
    mkv = each(_dot, m_k, v_bd)
    wu = each(lambda t, x, y: _dot(t, cat1(bd(x), bd(y))), t_inv, at, mkv)
    w1 = each(lambda x: x[:, :HGW], wu)
    u0 = each(lambda x: x[:, HGW:], wu)
    nbo = each(lambda n, x, y: _dot(n, cat1(bd(x), bd(y))), n_b, w1, u0)
    q1 = each(lambda x, y: x + y[:, :HGW], rt, nbo)
    y0 = each(lambda x, n, vb: x[:, HGW:] + _dot(n, vb), nbo, n_k, v_bd)
    bh = each(lambda x, y: x * y, beta, e_last)
    kh = each(lambda x, y: x * y, k, e_last)
    ri = lax.broadcasted_iota(jnp.int32, (HGW, HGW), 0) // B_HD
    ci = lax.broadcasted_iota(jnp.int32, (HGW, HGW), 1) // B_HD
    diag = ri == ci
    g_mat = each(lambda x, y: jnp.where(diag, _dot_tn(x, y), 0.0), w1, bh)
    h_mat = each(lambda x, y, z, w: jnp.where(diag, _dot_tn(cat0(x, y), cat0(z, w)), 0.0), u0, v, bh, kh)
    gam = each(jnp.exp, c_last)
    return q1, y0, g_mat, h_mat, gam


def _rwkv_body(pb_ref, shift_ref, s0_ref, mu_ref, w0_ref, w2_ref, a0_ref, a2_ref, g2_ref, kk_ref,
               ka_ref, rk_ref, lng_ref, lnb_ref, e_ref,
               o_ref, sout_ref,
               ext_ref, s_ref, *, tb, lc):
    t = pl.program_id(1)
    W = BRANCH_W
    n_groups = W // HGW

    @pl.when(t == 0)
    def _():
        ext_ref[0:SUBLANES, :] = jnp.broadcast_to(shift_ref[0], (SUBLANES, ext_ref.shape[1]))
        s_ref[...] = s0_ref[0]

    p = pb_ref[...]
    ext_ref[SUBLANES:, :] = p
    prev = ext_ref[SUBLANES - 1:SUBLANES - 1 + tb, :]
    xs = p + (prev - p) * mu_ref[...]
    ext_ref[0:SUBLANES, :] = p[tb - SUBLANES:, :]

    e = e_ref[...]
    r = xs[:, 0:W]
    k = xs[:, W:2 * W]
    v = xs[:, 2 * W:3 * W]
    wal = xs[:, 3 * W:3 * W + LANES]
    gl = xs[:, 3 * W + LANES:3 * W + 2 * LANES]
    w_log = -_softplus_big(-(w0_ref[...] + _dot(jnp.tanh(wal), w2_ref[...]))) - 0.5
    a = _sigmoid(a0_ref[...] + _dot(wal, a2_ref[...]))
    g = _dot(_sigmoid(gl), g2_ref[...])
    kkr = k * kk_ref[...]
    kk = kkr / jnp.maximum(jnp.sqrt(_segsum(kkr * kkr, e)), 1e-12)
    k2 = k * (1.0 + (a - 1.0) * ka_ref[...])
    lw = -jnp.exp(w_log)
    n_chunks = max(tb // lc, 1)

    def piece(x, ci, gi):
        blk = x[ci * lc:min((ci + 1) * lc, tb), gi * HGW:(gi + 1) * HGW]
        if tb < lc:
            blk = jnp.concatenate([blk, jnp.zeros((lc - tb, HGW), F32)], axis=0)
        return blk

    inst = [(ci, gi) for ci in range(n_chunks) for gi in range(n_groups)]
    terms = _rwkv_chunk_terms(*[[piece(x, ci, gi) for ci, gi in inst] for x in (r, lw, k2, v, kk, a)])
    y_rows = []
    for ci in range(n_chunks):
        y_cols = []
        for gi in range(n_groups):
            q1, y0, g_mat, h_mat, gam = [tm[ci * n_groups + gi] for tm in terms]
            s = s_ref[gi]
            y_cols.append(_dot_nt(q1, s) + y0)
            s_ref[gi] = s * gam + _dot(s, g_mat) + h_mat
        y_rows.append(jnp.concatenate(y_cols, axis=1))
    y = jnp.concatenate(y_rows, axis=0)[0:tb]
    mean = _segsum(y, e) * (1.0 / B_HD)
    yc = y - mean
    var = _segsum(yc * yc, e) * (1.0 / B_HD)
    yn = yc * lax.rsqrt(var + RWKV_GN_EPS) * lng_ref[...] + lnb_ref[...]
    bonus = _segsum(r * k2 * rk_ref[...], e) * v
    o_ref[...] = (yn + bonus) * g

    @pl.when(t == pl.num_programs(1) - 1)
    def _():
        sout_ref[0] = s_ref[...]


def _rwkv(pb, shift0, s0_bd, wts, nb, t):
    tb = _row_tile(t, 256)
    lc = CHUNK
    assert tb % lc == 0 or tb < lc
    nt = t // tb
    width = pb.shape[1]
    W = BRANCH_W
    n_groups = W // HGW
    row = lambda n: _resident((1, n))
    return pl.pallas_call(
        functools.partial(_rwkv_body, tb=tb, lc=lc),
        grid=(nb, nt),
        in_specs=[pl.BlockSpec((tb, width), lambda b, i: (b * nt + i, 0)),
                  pl.BlockSpec((1, 1, width), lambda b, i: (b, 0, 0)),
                  pl.BlockSpec((1, n_groups, HGW, HGW), lambda b, i: (b, 0, 0, 0)),
                  row(width), row(W), _resident((128, W)), row(W), _resident((128, W)),
                  _resident((128, W)), row(W), row(W), row(W), row(W), row(W), _resident((HGW, HGW))],
        out_specs=[pl.BlockSpec((tb, W), lambda b, i: (b * nt + i, 0)),
                   pl.BlockSpec((1, n_groups, HGW, HGW), lambda b, i: (b, 0, 0, 0))],
        out_shape=[jax.ShapeDtypeStruct((nb * t, W), F32),
                   jax.ShapeDtypeStruct((nb, n_groups, HGW, HGW), F32)],
        scratch_shapes=[pltpu.VMEM((tb + SUBLANES, width), F32),
                        pltpu.VMEM((n_groups, HGW, HGW), F32)],
        compiler_params=_cparams("parallel", "arbitrary"),
        name="rwkv7",
    )(pb, shift0, s0_bd, *wts, _seg_matrix(HGW, B_HD))


def _gla_chunk_terms(q, k, v, gk):
    each = lambda f, *seqs: [f(*xs) for xs in zip(*seqs)]
    L = q[0].shape[0]
    kw = C_HEADS * C_DK
    vw = C_HEADS * C_DV
    tril = _tril_ones(L)
    b = each(lambda x: _split_dot(tril, x, 3), gk)
    qe = each(lambda x, y: x * jnp.exp(y), q, b)
    ke = each(lambda x, y: x * jnp.exp(-y), k, b)
    a_all = each(lambda x, y: _dot_nt(x, _bd_rows(y, C_DK, C_HEADS)), qe, ke)
    row = lax.broadcasted_iota(jnp.int32, (L, C_HEADS * L), 0)
    col = lax.broadcasted_iota(jnp.int32, (L, C_HEADS * L), 1) % L
    causal = col <= row
    o_intra = each(lambda x, y: _dot(jnp.where(causal, x, 0.0), _bd_rows(y, C_DV, C_HEADS)), a_all, v)
    b_last = each(lambda x: x[L - 1:L, :], b)
    kd = each(lambda x, y, z: x * jnp.exp(z - y), k, b, b_last)
    ri = lax.broadcasted_iota(jnp.int32, (vw, kw), 0) // C_DV
    ci = lax.broadcasted_iota(jnp.int32, (vw, kw), 1) // C_DK
    diag = ri == ci
    upd = each(lambda x, y: jnp.where(diag, _dot_tn(x, y), 0.0), v, kd)
    decay = each(jnp.exp, b_last)
    return qe, o_intra, upd, decay


def _gla_body(pc_ref, s0_ref, g2_ref, gb_ref, norm_ref, o_ref, sout_ref,
              s_ref, *, tb, lc):
    t = pl.program_id(1)
    kw = C_HEADS * C_DK
    vw = C_HEADS * C_DV

    @pl.when(t == 0)
    def _():
        s_ref[...] = s0_ref[0]

    pc = pc_ref[...]
    og = pc[:, 2 * kw + vw:2 * kw + 2 * vw]
    gl = pc[:, 2 * kw + 2 * vw:2 * kw + 2 * vw + 128]
    z = _dot(gl, g2_ref[...]) + gb_ref[...]
    q = pc[:, 0:kw] * (C_DK ** -0.5)
    k = pc[:, kw:2 * kw]
    v = pc[:, 2 * kw:2 * kw + vw]
    gk = -_softplus_big(-z) * (1.0 / C_TAU)
    n_chunks = max(tb // lc, 1)

    def piece(x, ci):
        blk = x[ci * lc:min((ci + 1) * lc, tb), :]
        if tb < lc:
            blk = jnp.concatenate([blk, jnp.zeros((lc - tb, x.shape[1]), F32)], axis=0)
        return blk

    qe, o_intra, upd, decay = _gla_chunk_terms(
        *[[piece(x, ci) for ci in range(n_chunks)] for x in (q, k, v, gk)])
    y_rows = []
    for ci in range(n_chunks):
        st = s_ref[...]
        y_rows.append(_dot_nt(qe[ci], st) + o_intra[ci])
        s_ref[...] = st * decay[ci] + upd[ci]
    y = jnp.concatenate(y_rows, axis=0)[0:tb]
    outs = []
    for h in range(C_HEADS):
        yh = y[:, h * C_DV:(h + 1) * C_DV]
        outs.append(yh * lax.rsqrt(jnp.mean(yh * yh, axis=-1, keepdims=True) + EPS) * norm_ref[...])
    yn = jnp.concatenate(outs, axis=1)
    o_ref[...] = yn * (og * _sigmoid(og))

    @pl.when(t == pl.num_programs(1) - 1)
    def _():
        sout_ref[0] = s_ref[...]


def _gla(pc, s0_bd, g2, gb, norm, nb, t):
    tb = _row_tile(t, 256)
    lc = CHUNK
    assert tb % lc == 0 or tb < lc
    nt = t // tb
    width = pc.shape[1]
    kw, vw = C_HEADS * C_DK, C_HEADS * C_DV
    return pl.pallas_call(
        functools.partial(_gla_body, tb=tb, lc=lc),
        grid=(nb, nt),
        in_specs=[pl.BlockSpec((tb, width), lambda b, i: (b * nt + i, 0)),
                  pl.BlockSpec((1, vw, kw), lambda b, i: (b, 0, 0)),
                  _resident((128, kw)), _resident((1, kw)), _resident((1, C_DV))],
        out_specs=[pl.BlockSpec((tb, vw), lambda b, i: (b * nt + i, 0)),
                   pl.BlockSpec((1, vw, kw), lambda b, i: (b, 0, 0))],
        out_shape=[jax.ShapeDtypeStruct((nb * t, vw), F32),
                   jax.ShapeDtypeStruct((nb, vw, kw), F32)],
        scratch_shapes=[pltpu.VMEM((vw, kw), F32)],
        compiler_params=_cparams("parallel", "arbitrary"),
        name="gla",
    )(pc, s0_bd, g2, gb, norm)


def _shift_rows(x, d, fill):
    n = x.shape[0]
    if d % SUBLANES == 0:
        head = jnp.full((d, x.shape[1]), fill, x.dtype)
        return jnp.concatenate([head, x[:n - d]], axis=0)
    rolled = pltpu.roll(x, d, 0)
    row = lax.broadcasted_iota(jnp.int32, x.shape, 0)
    return jnp.where(row < d, fill, rolled)


def _lru_block(gate, xr, ext_ref, h_ref, au_ref, cw_ref, cb_ref, wa_ref, ba_ref, wx_ref, bx_ref,
               lam_ref, tb, fill=lambda: None):
    W = BRANCH_W
    ext_ref[SUBLANES:, :] = xr
    xc = cb_ref[...] + xr * cw_ref[CONV_W - 1:CONV_W, :]
    for j in range(CONV_W - 1):
        off = SUBLANES - (CONV_W - 1) + j
        xc = xc + ext_ref[off:off + tb, :] * cw_ref[j:j + 1, :]
    tail = ext_ref[tb:tb + SUBLANES, :]
    ext_ref[0:SUBLANES, :] = tail
    fill()

    r = _sigmoid(_dot(xc, wa_ref[...]) + ba_ref[...])
    fill()
    gi = _sigmoid(_dot(xc, wx_ref[...]) + bx_ref[...])
    fill()
    log_a = (-LRU_C) * r * _softplus(-lam_ref[...])
    a = jnp.exp(log_a)
    u = jnp.sqrt(1.0 - jnp.exp(2.0 * log_a)) * (gi * xc)
    fill()
    ng = tb // SUBLANES
    row_in_group = lax.broadcasted_iota(jnp.int32, (ng, SUBLANES, W), 1)
    a = a.reshape(ng, SUBLANES, W)
    u = u.reshape(ng, SUBLANES, W)
    d = 1
    while d < SUBLANES:
        inside = row_in_group >= d
        u = u + a * jnp.where(inside, pltpu.roll(u, d, 1), 0.0)
        a = a * jnp.where(inside, pltpu.roll(a, d, 1), 1.0)
        d *= 2
        fill()
    a = a.reshape(tb, W)
    u = u.reshape(tb, W)
    h_in = h_ref[...]
    if ng % SUBLANES == 0:
        ends = pl.ds(SUBLANES - 1, ng, stride=SUBLANES)
        for j in range(W // LANES):
            au_ref[0, j] = a[:, j * LANES:(j + 1) * LANES]
            au_ref[1, j] = u[:, j * LANES:(j + 1) * LANES]
        ae = jnp.concatenate([au_ref[0, j, ends, :] for j in range(W // LANES)], axis=1)
        ue = jnp.concatenate([au_ref[1, j, ends, :] for j in range(W // LANES)], axis=1)
        d = 1
        while d < ng:
            ue = ue + ae * _shift_rows(ue, d, 0.0)
            ae = ae * _shift_rows(ae, d, 1.0)
            d *= 2
        carry = _shift_rows(ue + ae * h_in, 1, h_in)
        carry_rows = jnp.broadcast_to(carry[:, None, :], (ng, SUBLANES, W)).reshape(tb, W)
    else:
        rows = []
        for g in range(ng):
            rows.append(jnp.broadcast_to(h_in, (SUBLANES, W)))
            end = (g + 1) * SUBLANES - 1
            h_in = u[end:end + 1, :] + a[end:end + 1, :] * h_in
        carry_rows = jnp.concatenate(rows, axis=0)
    h = u + a * carry_rows
    h_last = h[tb - 1:tb, :]
    h_ref[...] = h_last
    fill()
    out = h * (0.5 * gate * (1.0 + jnp.tanh(math.sqrt(2.0 / math.pi)
                                            * (gate + 0.044715 * gate * gate * gate))))
    return out, tail, h_last


def _lru_body(pd_ref, conv0_ref, h0_ref, cw_ref, cb_ref, wa_ref, ba_ref, wx_ref, bx_ref, lam_ref,
              o_ref, convout_ref, hout_ref, ext_ref, h_ref, au_ref, *, tb):
    t = pl.program_id(1)
    W = BRANCH_W

    @pl.when(t == 0)
    def _():
        ext_ref[0:SUBLANES, :] = conv0_ref[0]
        h_ref[...] = h0_ref[0]

    out, tail, h_last = _lru_block(pd_ref[:, 0:W], pd_ref[:, W:2 * W], ext_ref, h_ref, au_ref, cw_ref,
                                   cb_ref, wa_ref, ba_ref, wx_ref, bx_ref, lam_ref, tb)
    o_ref[...] = out

    @pl.when(t == pl.num_programs(1) - 1)
    def _():
        convout_ref[0] = tail
        hout_ref[0] = h_last


def _mix_in_body(x_ref, g_ref, wa_ref, wb_ref, wc_ref, wd_ref,
                 bias_ref, sink_ref, qn_ref, kn_ref, eq_ref, ek_ref, rep_ref,
                 conv0_ref, h0_ref, cw_ref, cb_ref, lwa_ref, lba_ref, lwx_ref, lbx_ref, lam_ref,
                 pb_ref, pc_ref, oa_ref, od_ref, kout_ref, vout_ref, convout_ref, hout_ref,
                 kv_tail_ref, ext_ref, h_ref, au_ref, *, tb):
    t = pl.program_id(1)
    W = BRANCH_W

    @pl.when(t == 0)
    def _():
        kv_tail_ref[...] = jnp.zeros(kv_tail_ref.shape, F32)
        ext_ref[0:SUBLANES, :] = conv0_ref[0]
        h_ref[...] = h0_ref[0]

    h = _rms(x_ref[...], g_ref[...]).astype(BF16)
    pd = jnp.dot(h, wd_ref[...], preferred_element_type=F32)
    pa = jnp.dot(h, wa_ref[...], preferred_element_type=F32)

    pending = [(w_ref, p_ref, c0, min(MXU_COLS, w_ref.shape[1] - c0))
               for w_ref, p_ref in ((wb_ref, pb_ref), (wc_ref, pc_ref))
               for c0 in range(0, w_ref.shape[1], MXU_COLS)]

    def fill():
        if pending:
            w_ref, p_ref, c0, width = pending.pop(0)
            p_ref[:, c0:c0 + width] = jnp.dot(h, w_ref[:, c0:c0 + width], preferred_element_type=F32)

    od, tail, h_last = _lru_block(pd[:, 0:W], pd[:, W:2 * W], ext_ref, h_ref, au_ref, cw_ref, cb_ref,
                                  lwa_ref, lba_ref, lwx_ref, lbx_ref, lam_ref, tb, fill)
    od_ref[...] = od
    outs = _attn_prompt_block(pa, kv_tail_ref, t, bias_ref, sink_ref, qn_ref, kn_ref, eq_ref, ek_ref,
                              rep_ref, tb, fill)
    for c, o in enumerate(outs):
        oa_ref[c * CHUNK:(c + 1) * CHUNK, :] = o
    while pending:
        fill()

    @pl.when(t == pl.num_programs(1) - 1)
    def _():
        kout_ref[0] = kv_tail_ref[0]
        vout_ref[0] = kv_tail_ref[1]
        convout_ref[0] = tail
        hout_ref[0] = h_last


def _mix_in(x, g, ws, bias, sink, qn, kn, conv0, h0, lru_wts, nb, t):
    n, d = x.shape
    tb = _row_tile(t, 512)
    assert tb % WINDOW == 0
    nt = t // tb
    W = BRANCH_W
    qw, kw = A_HEADS * A_HD, A_KV_HEADS * A_HD
    wa, wb, wc, wd = ws
    rows = lambda width: pl.BlockSpec((tb, width), lambda b, i: (b * nt + i, 0))
    per_seq = lambda *shape: pl.BlockSpec((1,) + shape, lambda b, i: (b,) + (0,) * len(shape))
    row = lambda width: _resident((1, width))
    return pl.pallas_call(
        functools.partial(_mix_in_body, tb=tb),
        grid=(nb, nt),
        in_specs=[rows(d), row(d)] + [_resident(w.shape) for w in ws]
                 + [_resident(bias.shape), _resident(sink.shape), row(qw), row(kw),
                    _resident((HGW, HGW)), _resident((kw, kw)), _resident((kw, qw)),
                    per_seq(SUBLANES, W), per_seq(1, W),
                    _resident((CONV_W, W)), row(W), _resident((W, W)), row(W), _resident((W, W)), row(W),
                    row(W)],
        out_specs=[rows(wb.shape[1]), rows(wc.shape[1]), rows(qw), rows(W),
                   per_seq(WINDOW, kw), per_seq(WINDOW, kw), per_seq(SUBLANES, W), per_seq(1, W)],
        out_shape=[jax.ShapeDtypeStruct((n, wb.shape[1]), F32), jax.ShapeDtypeStruct((n, wc.shape[1]), F32),
                   jax.ShapeDtypeStruct((n, qw), F32), jax.ShapeDtypeStruct((n, W), F32),
                   jax.ShapeDtypeStruct((nb, WINDOW, kw), F32), jax.ShapeDtypeStruct((nb, WINDOW, kw), F32),
                   jax.ShapeDtypeStruct((nb, SUBLANES, W), F32), jax.ShapeDtypeStruct((nb, 1, W), F32)],
        scratch_shapes=[pltpu.VMEM((2, WINDOW, kw), F32),
                        pltpu.VMEM((tb + SUBLANES, W), F32), pltpu.VMEM((1, W), F32),
                        pltpu.VMEM((2, W // LANES, tb, LANES), F32)],
        compiler_params=_cparams("parallel", "arbitrary"),
        name="mix_in",
    )(x, g, wa, wb, wc, wd, bias, sink, qn, kn, _seg_matrix(HGW, A_HD), _seg_matrix(kw, A_HD),
      _kv_replication(), conv0, h0, *lru_wts)


def _lru(pd, conv0, h0, wts, nb, t):
    tb = _row_tile(t, 256)
    nt = t // tb
    W = BRANCH_W
    row = lambda: _resident((1, W))
    return pl.pallas_call(
        functools.partial(_lru_body, tb=tb),
        grid=(nb, nt),
        in_specs=[pl.BlockSpec((tb, 2 * W), lambda b, i: (b * nt + i, 0)),
                  pl.BlockSpec((1, SUBLANES, W), lambda b, i: (b, 0, 0)),
                  pl.BlockSpec((1, 1, W), lambda b, i: (b, 0, 0)),
                  _resident((CONV_W, W)), row(), _resident((W, W)), row(), _resident((W, W)), row(),
                  row()],
        out_specs=[pl.BlockSpec((tb, W), lambda b, i: (b * nt + i, 0)),
                   pl.BlockSpec((1, SUBLANES, W), lambda b, i: (b, 0, 0)),
                   pl.BlockSpec((1, 1, W), lambda b, i: (b, 0, 0))],
        out_shape=[jax.ShapeDtypeStruct((nb * t, W), F32),
                   jax.ShapeDtypeStruct((nb, SUBLANES, W), F32),
                   jax.ShapeDtypeStruct((nb, 1, W), F32)],
        scratch_shapes=[pltpu.VMEM((tb + SUBLANES, W), F32), pltpu.VMEM((1, W), F32),
                        pltpu.VMEM((2, W // LANES, tb, LANES), F32)],
        compiler_params=_cparams("parallel", "arbitrary"),
        name="rglru",
    )(pd, conv0, h0, *wts)


def _block_diag(w):
    n, a, b = w.shape
    eye = jnp.eye(n, dtype=w.dtype)
    return (eye[:, None, :, None] * w[:, :, None, :]).reshape(n * a, n * b)


def _pad_rows_to(w, rows):
    return jnp.pad(w, ((0, rows - w.shape[0]), (0, 0)))


def _layer_weights(i, W):
    bf = lambda a: a.astype(BF16)
    row = lambda a: a.reshape(1, -1).astype(F32)
    bw = BRANCH_W
    a_cols = A_HEADS * A_HD + 2 * A_KV_HEADS * A_HD
    b_cols = 3 * bw + B_W_RANK + B_A_RANK + B_G_RANK
    c_cols = 2 * C_HEADS * C_DK + C_HEADS * C_DV + C_G_RANK + bw
    w_in = W['w_in'][i]
    d = w_in.shape[0]
    wa = w_in[:, :a_cols]
    wb = w_in[:, a_cols:a_cols + b_cols]
    wc = w_in[:, a_cols + b_cols:a_cols + b_cols + c_cols]
    wd = w_in[:, a_cols + b_cols + c_cols:]
    zpad = lambda n: jnp.zeros((d, n), w_in.dtype)
    assert B_W_RANK + B_A_RANK == LANES
    w2 = jnp.pad(W['rwkv_w2'][i], ((0, B_A_RANK), (0, 0)))
    a2 = jnp.pad(W['rwkv_a2'][i], ((B_W_RANK, 0), (0, 0)))
    qkv = 2 * C_HEADS * C_DK + C_HEADS * C_DV
    wc2 = jnp.concatenate([wc[:, :qkv], wc[:, qkv + C_G_RANK:], wc[:, qkv:qkv + C_G_RANK],
                           zpad(128 - C_G_RANK)], axis=1)
    return dict(
        ffn1=(row(W['g_ffn1'][i]), bf(W['w_ffn1_gate'][i]), bf(W['w_ffn1_up'][i]), bf(W['w_ffn1_down'][i])),
        ffn2=(row(W['g_ffn2'][i]), bf(W['w_ffn2_gate'][i]), bf(W['w_ffn2_up'][i]), bf(W['w_ffn2_down'][i])),
        g_mix=row(W['g_mix'][i]),
        w_in=(bf(wa), bf(wb), bf(wc2), bf(wd)),
        q_norm=row(jnp.tile(W['q_norm'][i], A_HEADS)),
        k_norm=row(jnp.tile(W['k_norm'][i], A_KV_HEADS)),
        sink=W['attn_sink'][i].astype(F32),
        rwkv=(row(W['rwkv_mu'][i]), row(W['rwkv_w0'][i]), bf(w2),
              row(W['rwkv_a0'][i]), bf(a2), bf(W['rwkv_g2'][i]),
              row(W['rwkv_k_k'][i]), row(W['rwkv_k_a'][i]), row(W['rwkv_r_k'][i]),
              row(W['rwkv_ln_g'][i]), row(W['rwkv_ln_b'][i])),
        gla=(bf(_pad_rows_to(W['gla_g2'][i], 128)), row(W['gla_gb'][i]), row(W['gla_norm'][i])),
        lru=(W['lru_conv_w'][i].astype(F32), row(W['lru_conv_b'][i]),
             bf(_block_diag(W['lru_wa'][i])), row(W['lru_ba'][i]),
             bf(_block_diag(W['lru_wx'][i])), row(W['lru_bx'][i]), row(W['lru_lambda'][i])),
        merge=(bf(W['w_merge_gate'][i]), bf(W['w_branch'][i]), bf(W['w_out'][i])),
        ple=(row(W['g_ple'][i]), bf(W['w_ple_gate'][i]), bf(W['w_ple_proj'][i])),
    )


def _heads_to_bd(s, hg):
    nb, h, a, b = s.shape
    s = s.reshape(nb, h // hg, hg, a, b)
    eye = jnp.eye(hg, dtype=s.dtype)
    return (eye[None, None, :, None, :, None] * s[:, :, :, :, None, :]).reshape(nb, h // hg, hg * a, hg * b)


def _bd_to_heads(s, hg):
    nb, ng, ra, cb = s.shape
    a, b = ra // hg, cb // hg
    s = s.reshape(nb, ng, hg, a, hg, b)
    idx = jnp.arange(hg)
    return s[:, :, idx, :, idx, :].transpose(1, 2, 0, 3, 4).reshape(nb, ng * hg, a, b)


def _trunk_layer(x, pe, st, lw, bias, first_chunk):
    nb, t, d = x.shape
    n = nb * t
    ck, cv, shift0, s_rwkv0, s_gla0, conv0, lru0 = st
    x2 = x.reshape(n, d)
    x2 = _ffn(x2, *lw['ffn1'])
    kw = A_KV_HEADS * A_HD
    n_q = bias.shape[1] // A_HEADS
    sink2 = jnp.repeat(lw['sink'], n_q).reshape(1, A_HEADS * n_q)
    conv_pad = jnp.pad(conv0, ((0, 0), (SUBLANES - (CONV_W - 1), 0), (0, 0)))
    if first_chunk:
        pb, pc, o_a, o_d, k_win, v_win, conv_out, h_out = _mix_in(
            x2, lw['g_mix'], lw['w_in'], bias, sink2, lw['q_norm'], lw['k_norm'],
            conv_pad, lru0[:, None, :], lw['lru'], nb, t)
    else:
        pa, pb, pc, pd = _inproj(x2, lw['g_mix'], lw['w_in'])
        o_a, k_win, v_win = _attn_sample(pa, ck.reshape(nb, WINDOW, kw), cv.reshape(nb, WINDOW, kw),
                                         bias, sink2, lw['q_norm'], lw['k_norm'], nb, t)
        o_d, conv_out, h_out = _lru(pd, conv_pad, lru0[:, None, :], lw['lru'], nb, t)
    k_win = k_win.reshape(nb, WINDOW, A_KV_HEADS, A_HD)
    v_win = v_win.reshape(nb, WINDOW, A_KV_HEADS, A_HD)

    o_b, s_bd = _rwkv(pb, shift0[:, None, :], _heads_to_bd(s_rwkv0, HG), lw['rwkv'], nb, t)
    shift1 = pb.reshape(nb, t, -1)[:, -1]
    s_rwkv1 = _bd_to_heads(s_bd, HG)

    st_bd0 = _heads_to_bd(jnp.swapaxes(s_gla0, -1, -2), C_HEADS)[:, 0]
    o_c, st_bd = _gla(pc, st_bd0, *lw['gla'], nb, t)
    s_gla1 = jnp.swapaxes(_bd_to_heads(st_bd[:, None], C_HEADS), -1, -2)

    conv1 = conv_out[:, SUBLANES - (CONV_W - 1):]
    lru1 = h_out[:, 0]

    x2 = _merge(x2, (o_a, o_b, o_c, o_d), lw['g_mix'], *lw['merge'])
    x2 = _ffn_ple(x2, pe.reshape(n, -1), *lw['ffn2'], *lw['ple'])
    return x2.reshape(nb, t, d), (k_win, v_win, shift1, s_rwkv1, s_gla1, conv1, lru1)


def kernel(x_prompt, x_sample, cache_attn_k, cache_attn_v, state_rwkv_shift, state_rwkv, state_gla, state_lru_conv, state_lru, p_prompt, p_sample, rel_bias_table, g_ffn1, w_ffn1_gate, w_ffn1_up, w_ffn1_down, g_mix, w_in, q_norm, k_norm, attn_sink, rwkv_mu, rwkv_w0, rwkv_w2, rwkv_a0, rwkv_a2, rwkv_g2, rwkv_k_k, rwkv_k_a, rwkv_r_k, rwkv_ln_g, rwkv_ln_b, gla_g2, gla_gb, gla_norm, lru_conv_w, lru_conv_b, lru_wa, lru_ba, lru_wx, lru_bx, lru_lambda, w_merge_gate, w_branch, w_out, g_ffn2, w_ffn2_gate, w_ffn2_up, w_ffn2_down, g_ple, w_ple_gate, w_ple_proj):
    W = dict(g_ffn1=g_ffn1, w_ffn1_gate=w_ffn1_gate, w_ffn1_up=w_ffn1_up, w_ffn1_down=w_ffn1_down,
             g_mix=g_mix, w_in=w_in, q_norm=q_norm, k_norm=k_norm, attn_sink=attn_sink,
             rwkv_mu=rwkv_mu, rwkv_w0=rwkv_w0, rwkv_w2=rwkv_w2, rwkv_a0=rwkv_a0, rwkv_a2=rwkv_a2,
             rwkv_g2=rwkv_g2, rwkv_k_k=rwkv_k_k, rwkv_k_a=rwkv_k_a, rwkv_r_k=rwkv_r_k,
             rwkv_ln_g=rwkv_ln_g, rwkv_ln_b=rwkv_ln_b, gla_g2=gla_g2, gla_gb=gla_gb,
             gla_norm=gla_norm, lru_conv_w=lru_conv_w, lru_conv_b=lru_conv_b, lru_wa=lru_wa,
             lru_ba=lru_ba, lru_wx=lru_wx, lru_bx=lru_bx, lru_lambda=lru_lambda,
             w_merge_gate=w_merge_gate, w_branch=w_branch, w_out=w_out, g_ffn2=g_ffn2,
             w_ffn2_gate=w_ffn2_gate, w_ffn2_up=w_ffn2_up, w_ffn2_down=w_ffn2_down,
             g_ple=g_ple, w_ple_gate=w_ple_gate, w_ple_proj=w_ple_proj)
    depth = w_in.shape[0]
    dt = x_prompt.dtype
    bp, tp = x_prompt.shape[:2]
    ts = x_sample.shape[1]
    b_cols = state_rwkv_shift.shape[-1]
    bias_p = _rel_bias(rel_bias_table, CHUNK, WINDOW + CHUNK)
    bias_s = _rel_bias(rel_bias_table, ts, WINDOW + ts)
    yp, ys = x_prompt, x_sample
    st_p, st_s = [], []
    for i in range(depth):
        lw = _layer_weights(i, W)
        zero_st = (None, None,
                   jnp.zeros((bp, b_cols), dt),
                   jnp.zeros((bp, B_HEADS, B_HD, B_HD), dt),
                   jnp.zeros((bp, C_HEADS, C_DK, C_DV), dt),
                   jnp.zeros((bp, CONV_W - 1, BRANCH_W), dt),
                   jnp.zeros((bp, BRANCH_W), dt))
        yp, sp = _trunk_layer(yp, p_prompt[i], zero_st, lw, bias_p, True)
        cache_st = (cache_attn_k[i], cache_attn_v[i], state_rwkv_shift[i], state_rwkv[i],
                    state_gla[i], state_lru_conv[i], state_lru[i])
        ys, ss = _trunk_layer(ys, p_sample[i], cache_st, lw, bias_s, False)
        st_p.append(sp)
        st_s.append(ss)
    stack = lambda states, j: jnp.stack([s[j] for s in states])
    return (yp, ys) + tuple(stack(st_p, j) for j in range(7)) + tuple(stack(st_s, j) for j in range(7))
```

```python
import functools
import math

import numpy as np
import jax
import jax.numpy as jnp
from jax import lax
from jax.experimental import pallas as pl
from jax.experimental.pallas import tpu as pltpu

F32 = jnp.float32
BF16 = jnp.bfloat16

V7X_VMEM_BYTES = 64 * 1024 * 1024
VMEM_LIMIT = V7X_VMEM_BYTES - 8 * 1024 * 1024
SUBLANES = 8
LANES = 128
MXU_COLS = 256

EPS = 1e-6
NEG_INF = -1e30
CHUNK = 64
WINDOW = 128
N_BUCKETS = 32
MAX_DIST = 128
A_HEADS, A_KV_HEADS, A_HD = 8, 2, 64
A_GROUP = A_HEADS // A_KV_HEADS
B_HEADS, B_HD = 8, 64
B_W_RANK, B_A_RANK, B_G_RANK = 64, 64, 128
RWKV_GN_EPS = 64e-5
C_HEADS, C_DK, C_DV = 4, 64, 128
C_G_RANK = 16
C_TAU = 16.0
D_BLOCKS = 8
CONV_W = 4
LRU_C = 8.0
BRANCH_W = 512
HG = 4
HGW = HG * B_HD


def _cparams(*sem):
    return pltpu.CompilerParams(dimension_semantics=sem, vmem_limit_bytes=VMEM_LIMIT)


def _resident(shape):
    nd = len(shape)
    return pl.BlockSpec(shape, lambda *_: (0,) * nd, pipeline_mode=pl.Buffered(1))


def _row_tile(n, cap):
    t = min(n, cap)
    while n % t:
        t //= 2
    return t


def _dot(a, b):
    return jnp.dot(a.astype(BF16), b.astype(BF16), preferred_element_type=F32)


def _dot_nt(a, b):
    return lax.dot_general(a.astype(BF16), b.astype(BF16), (((1,), (1,)), ((), ())),
                           preferred_element_type=F32)


def _dot_tn(a, b):
    return lax.dot_general(a.astype(BF16), b.astype(BF16), (((0,), (0,)), ((), ())),
                           preferred_element_type=F32)


def _rms(x, g):
    return x * lax.rsqrt(jnp.mean(x * x, axis=-1, keepdims=True) + EPS) * g


def _sigmoid(x):
    return 1.0 / (1.0 + jnp.exp(-x))


def _softplus(x):
    return jnp.maximum(x, 0.0) + jnp.log1p(jnp.exp(-jnp.abs(x)))


def _softplus_big(x):
    return jnp.maximum(x, 0.0) + jnp.log(1.0 + jnp.exp(-jnp.abs(x)))


def _split_dot(e_lhs, x, terms):
    acc = None
    rem = x
    for n in range(terms):
        piece = rem.astype(BF16)
        d = jnp.dot(e_lhs, piece, preferred_element_type=F32)
        acc = d if acc is None else acc + d
        if n + 1 < terms:
            rem = rem - piece.astype(F32)
    return acc


def _segsum(x, e):
    blk = e.shape[0]
    xb = x.astype(BF16)
    cols = [jnp.dot(xb[:, j:j + blk], e, preferred_element_type=F32)
            for j in range(0, x.shape[1], blk)]
    return cols[0] if len(cols) == 1 else jnp.concatenate(cols, axis=1)


def _tril_ones(n):
    r = lax.broadcasted_iota(jnp.int32, (n, n), 0)
    c = lax.broadcasted_iota(jnp.int32, (n, n), 1)
    return jnp.where(r >= c, 1.0, 0.0).astype(BF16)


def _bd_rows(x, blk, nblk):
    lane_blk = lax.broadcasted_iota(jnp.int32, x.shape, 1) // blk
    return jnp.concatenate([jnp.where(lane_blk == h, x, 0.0) for h in range(nblk)], axis=0)


def _block_diag_value(blocks):
    n = len(blocks)
    a, b = blocks[0].shape
    rows = []
    for h, blk in enumerate(blocks):
        parts = ([jnp.zeros((a, h * b), F32)] if h else []) + [blk] \
            + ([jnp.zeros((a, (n - 1 - h) * b), F32)] if h < n - 1 else [])
        rows.append(jnp.concatenate(parts, axis=1))
    return jnp.concatenate(rows, axis=0)


def _transpose_exact(x):
    n = x.shape[0]
    r = lax.broadcasted_iota(jnp.int32, (n, n), 0)
    c = lax.broadcasted_iota(jnp.int32, (n, n), 1)
    eye = jnp.where(r == c, 1.0, 0.0).astype(BF16)
    acc = None
    rem = x
    for step in range(3):
        piece = rem.astype(BF16)
        d = lax.dot_general(piece, eye, (((0,), (0,)), ((), ())), preferred_element_type=F32)
        acc = d if acc is None else acc + d
        if step < 2:
            rem = rem - piece.astype(F32)
    return acc


def _seg_matrix(width, seg):
    i = np.arange(width)
    return jnp.asarray((i[:, None] // seg) == (i[None, :] // seg), dtype=BF16)


def _ffn_chunks(f):
    tiles = -(-f // MXU_COLS)
    cut = min(f, (tiles + 1) // 2 * MXU_COLS)
    return [(0, cut)] + ([(cut, f - cut)] if cut < f else [])


def _swiglu_residual(x, g_ref, wg_ref, wu_ref, wd_ref):
    h = _rms(x, g_ref[...]).astype(BF16)
    acc = None
    for start, size in _ffn_chunks(wg_ref.shape[1]):
        sl = pl.ds(start, size)
        gt = jnp.dot(h, wg_ref[:, sl], preferred_element_type=F32)
        up = jnp.dot(h, wu_ref[:, sl], preferred_element_type=F32)
        act = (gt * _sigmoid(gt) * up).astype(BF16)
        d = jnp.dot(act, wd_ref[sl, :], preferred_element_type=F32)
        acc = d if acc is None else acc + d
    return x + 0.5 * acc


def _ffn_body(x_ref, g_ref, wg_ref, wu_ref, wd_ref, o_ref):
    o_ref[...] = _swiglu_residual(x_ref[...], g_ref, wg_ref, wu_ref, wd_ref)


def _ffn_ple_body(x_ref, pe_ref, g_ref, wg_ref, wu_ref, wd_ref, gp_ref, wpg_ref, wpp_ref, o_ref):
    x = _swiglu_residual(x_ref[...], g_ref, wg_ref, wu_ref, wd_ref)
    h = _rms(x, gp_ref[...]).astype(BF16)
    gate = _sigmoid(jnp.dot(h, wpg_ref[...], preferred_element_type=F32))
    o_ref[...] = x + gate * jnp.dot(pe_ref[...].astype(BF16), wpp_ref[...], preferred_element_type=F32)


def _ffn(x, g, wg, wu, wd):
    n, d = x.shape
    f = wg.shape[1]
    tm = _row_tile(n, 512)
    return pl.pallas_call(
        _ffn_body,
        grid=(n // tm,),
        in_specs=[pl.BlockSpec((tm, d), lambda i: (i, 0)),
                  _resident((1, d)), _resident((d, f)), _resident((d, f)), _resident((f, d))],
        out_specs=pl.BlockSpec((tm, d), lambda i: (i, 0)),
        out_shape=jax.ShapeDtypeStruct((n, d), F32),
        compiler_params=_cparams("parallel"),
        name="ffn",
    )(x, g, wg, wu, wd)


def _inproj_body(x_ref, g_ref, wa_ref, wb_ref, wc_ref, wd_ref, pa_ref, pb_ref, pc_ref, pd_ref):
    h = _rms(x_ref[...], g_ref[...]).astype(BF16)
    for w_ref, p_ref in ((wa_ref, pa_ref), (wb_ref, pb_ref), (wc_ref, pc_ref), (wd_ref, pd_ref)):
        p_ref[...] = jnp.dot(h, w_ref[...], preferred_element_type=F32)


def _inproj(x, g, ws):
    n, d = x.shape
    tm = _row_tile(n, 512)
    widths = [w.shape[1] for w in ws]
    return pl.pallas_call(
        _inproj_body,
        grid=(n // tm,),
        in_specs=[pl.BlockSpec((tm, d), lambda i: (i, 0)), _resident((1, d))]
                 + [_resident((d, w)) for w in widths],
        out_specs=[pl.BlockSpec((tm, w), lambda i: (i, 0)) for w in widths],
        out_shape=[jax.ShapeDtypeStruct((n, w), F32) for w in widths],
        compiler_params=_cparams("parallel"),
        name="inproj",
    )(x, g, *ws)


def _merge_body(x_ref, oa_ref, ob_ref, oc_ref, od_ref, g_ref, wmg_ref, wb_ref, wo_ref, o_ref):
    x = x_ref[...]
    h = _rms(x, g_ref[...]).astype(BF16)
    y = None
    for n, b_ref in enumerate((oa_ref, ob_ref, oc_ref, od_ref)):
        gate = _sigmoid(jnp.dot(h, wmg_ref[n], preferred_element_type=F32))
        t = gate * jnp.dot(b_ref[...].astype(BF16), wb_ref[n], preferred_element_type=F32)
        y = t if y is None else y + t
    o_ref[...] = x + jnp.dot(y.astype(BF16), wo_ref[...], preferred_element_type=F32)


def _merge(x, outs, g, wmg, wb, wo):
    n, d = x.shape
    bw = outs[0].shape[1]
    tm = _row_tile(n, 512)
    return pl.pallas_call(
        _merge_body,
        grid=(n // tm,),
        in_specs=[pl.BlockSpec((tm, d), lambda i: (i, 0))]
                 + [pl.BlockSpec((tm, bw), lambda i: (i, 0))] * 4
                 + [_resident((1, d)), _resident(wmg.shape), _resident(wb.shape), _resident(wo.shape)],
        out_specs=pl.BlockSpec((tm, d), lambda i: (i, 0)),
        out_shape=jax.ShapeDtypeStruct((n, d), F32),
        compiler_params=_cparams("parallel"),
        name="merge",
    )(x, *outs, g, wmg, wb, wo)


def _ffn_ple(x, pe_all, layer, g, wg, wu, wd, gp, wpg, wpp):
    n, d = x.shape
    f = wg.shape[1]
    pd = pe_all.shape[2]
    tm = _row_tile(n, 512)
    return pl.pallas_call(
        _ffn_ple_body,
        grid=(n // tm,),
        in_specs=[pl.BlockSpec((tm, d), lambda i: (i, 0)),
                  pl.BlockSpec((None, tm, pd), lambda i: (layer, i, 0)),
                  _resident((1, d)), _resident((d, f)), _resident((d, f)), _resident((f, d)),
                  _resident((1, d)), _resident((d, d)), _resident((pd, d))],
        out_specs=pl.BlockSpec((tm, d), lambda i: (i, 0)),
        out_shape=jax.ShapeDtypeStruct((n, d), F32),
        compiler_params=_cparams("parallel"),
        name="ffn_ple",
    )(x, pe_all, g, wg, wu, wd, gp, wpg, wpp)


def _t5_bucket_np(rel):
    half = N_BUCKETS // 2
    max_exact = half // 2
    ret = np.where(rel > 0, half, 0)
    n = np.abs(rel)
    nf = np.maximum(n, 1).astype(np.float32)
    large = max_exact + (np.log(nf / np.float32(max_exact)) / np.float32(math.log(MAX_DIST / max_exact))
                         * np.float32(half - max_exact)).astype(np.int32)
    large = np.minimum(large, half - 1)
    return (ret + np.where(n < max_exact, n, large)).astype(np.int32)


def _bias_body(idx_ref, table_ref, o_ref):
    idx = idx_ref[...]
    for h in range(A_HEADS):
        acc = jnp.zeros(idx.shape, F32)
        for b in range(N_BUCKETS):
            acc = jnp.where(idx == b, table_ref[b, h], acc)
        o_ref[h] = acc


def _rel_bias(table, n_q, n_k):
    rel = np.arange(n_k)[:, None] - WINDOW - np.arange(n_q)[None, :]
    idx = jnp.asarray(_t5_bucket_np(rel))
    out = pl.pallas_call(
        _bias_body,
        in_specs=[pl.BlockSpec(memory_space=pltpu.VMEM), pl.BlockSpec(memory_space=pltpu.SMEM)],
        out_specs=pl.BlockSpec(memory_space=pltpu.VMEM),
        out_shape=jax.ShapeDtypeStruct((A_HEADS, n_k, n_q), F32),
        name="rel_bias",
    )(idx, table)
    return jnp.transpose(out, (1, 0, 2)).reshape(n_k, A_HEADS * n_q)


def _head_rms(x, e, g):
    ms = _segsum(x * x, e) * (1.0 / A_HD)
    return x * lax.rsqrt(ms + EPS) * g


def _kv_replication():
    lane = np.arange(A_HEADS * A_HD)
    src = (lane // (A_GROUP * A_HD)) * A_HD + lane % A_HD
    return jnp.asarray(np.arange(A_KV_HEADS * A_HD)[:, None] == src[None, :], dtype=BF16)


def _attend_blocks(q_list, k8_list, v8_list, bias, sink, valid_list, fill=lambda: None):
    def staged(f, *seqs):
        out = []
        for j, xs in enumerate(zip(*seqs)):
            out.append(f(*xs))
            if j % 4 == 3:
                fill()
        return out

    n = q_list[0].shape[0]
    lane_head = lax.broadcasted_iota(jnp.int32, (n, A_HEADS * A_HD), 1) // A_HD
    q_bd = staged(lambda q: jnp.concatenate([jnp.where(lane_head == h, q, 0.0)
                                             for h in range(A_HEADS)], axis=0), q_list)
    s = staged(lambda k8, qb, valid: (_dot_nt(k8, qb) + bias if valid is None else
                                      jnp.where(valid, _dot_nt(k8, qb) + bias, NEG_INF)),
               k8_list, q_bd, valid_list)
    m = staged(lambda x: jnp.maximum(jnp.max(x, axis=0, keepdims=True), sink), s)
    e = staged(lambda x, mm: jnp.exp(x - mm), s, m)
    rinv = [1.0 / (jnp.sum(x, axis=0, keepdims=True) + jnp.exp(sink - mm)) for x, mm in zip(e, m)]
    p = staged(lambda x, r: x * r, e, rinv)
    o8 = staged(_dot_tn, p, v8_list)

    def pick_heads(x):
        o = None
        for h in range(A_HEADS):
            part = jnp.where(lane_head == h, x[h * n:(h + 1) * n], 0.0)
            o = part if o is None else o + part
        return o

    return staged(pick_heads, o8)


def _attn_prompt_block(pa, kv_tail_ref, t_idx, bias_ref, sink_ref, qn_ref, kn_ref, eq_ref, ek_ref,
                       rep_ref, tb, fill):
    qw = A_HEADS * A_HD
    kw = A_KV_HEADS * A_HD
    band = WINDOW + CHUNK
    rep = rep_ref[...]
    q = _head_rms(pa[:, 0:qw], eq_ref[...], qn_ref[...]) * (A_HD ** -0.5)
    kf = jnp.concatenate([kv_tail_ref[0], _head_rms(pa[:, qw:qw + kw], ek_ref[...], kn_ref[...])], axis=0)
    vf = jnp.concatenate([kv_tail_ref[1], pa[:, qw + kw:qw + 2 * kw]], axis=0)
    kv_tail_ref[0] = kf[tb:tb + WINDOW, :]
    kv_tail_ref[1] = vf[tb:tb + WINDOW, :]
    k8 = jnp.dot(kf.astype(BF16), rep, preferred_element_type=F32).astype(BF16)
    v8 = jnp.dot(vf.astype(BF16), rep, preferred_element_type=F32).astype(BF16)
    kidx = lax.broadcasted_iota(jnp.int32, (band, A_HEADS * CHUNK), 0)
    n_chunks = tb // CHUNK
    valid = [(kidx + (t_idx * tb + c * CHUNK - WINDOW)) >= 0 if c * CHUNK < WINDOW else None
             for c in range(n_chunks)]
    return _attend_blocks([q[c * CHUNK:(c + 1) * CHUNK] for c in range(n_chunks)],
                          [k8[c * CHUNK:c * CHUNK + band] for c in range(n_chunks)],
                          [v8[c * CHUNK:c * CHUNK + band] for c in range(n_chunks)],
                          bias_ref[...], sink_ref[...], valid, fill)


def _attn_sample_body(pa_ref, ck_ref, cv_ref, bias_ref, sink_ref, qn_ref, kn_ref, eq_ref, ek_ref, rep_ref,
                      o_ref, kout_ref, vout_ref, *, s, gb):
    qw = A_HEADS * A_HD
    kw = A_KV_HEADS * A_HD
    rep = rep_ref[...]
    q = _head_rms(pa_ref[:, 0:qw], eq_ref[...], qn_ref[...]) * (A_HD ** -0.5)
    kn = _head_rms(pa_ref[:, qw:qw + kw], ek_ref[...], kn_ref[...])
    vn = pa_ref[:, qw + kw:qw + 2 * kw]
    q_list, k8_list, v8_list = [], [], []
    for b in range(gb):
        kf = jnp.concatenate([ck_ref[b], kn[b * s:(b + 1) * s]], axis=0)
        vf = jnp.concatenate([cv_ref[b], vn[b * s:(b + 1) * s]], axis=0)
        kout_ref[b] = kf[s:s + WINDOW, :]
        vout_ref[b] = vf[s:s + WINDOW, :]
        q_list.append(q[b * s:(b + 1) * s])
        k8_list.append(jnp.dot(kf.astype(BF16), rep, preferred_element_type=F32).astype(BF16))
        v8_list.append(jnp.dot(vf.astype(BF16), rep, preferred_element_type=F32).astype(BF16))
    outs = _attend_blocks(q_list, k8_list, v8_list, bias_ref[...], sink_ref[...], [None] * gb)
    for b in range(gb):
        o_ref[b * s:(b + 1) * s, :] = outs[b]


def _attn_sample(pa, ck, cv, bias, sink, qn, kn, nb, s):
    qw, kw = A_HEADS * A_HD, A_KV_HEADS * A_HD
    width = pa.shape[1]
    gb = _row_tile(nb, 8)
    return pl.pallas_call(
        functools.partial(_attn_sample_body, s=s, gb=gb),
        grid=(nb // gb,),
        in_specs=[pl.BlockSpec((gb * s, width), lambda b: (b, 0)),
                  pl.BlockSpec((gb, WINDOW, kw), lambda b: (b, 0, 0)),
                  pl.BlockSpec((gb, WINDOW, kw), lambda b: (b, 0, 0)),
                  _resident(bias.shape), _resident(sink.shape),
                  _resident((1, qw)), _resident((1, kw)), _resident((HGW, HGW)), _resident((kw, kw)),
                  _resident((kw, qw))],
        out_specs=[pl.BlockSpec((gb * s, qw), lambda b: (b, 0)),
                   pl.BlockSpec((gb, WINDOW, kw), lambda b: (b, 0, 0)),
                   pl.BlockSpec((gb, WINDOW, kw), lambda b: (b, 0, 0))],
        out_shape=[jax.ShapeDtypeStruct((nb * s, qw), F32),
                   jax.ShapeDtypeStruct((nb, WINDOW, kw), F32),
                   jax.ShapeDtypeStruct((nb, WINDOW, kw), F32)],
        compiler_params=_cparams("parallel"),
        name="attn_sample",
    )(pa, ck, cv, bias, sink, qn, kn, _seg_matrix(HGW, A_HD), _seg_matrix(kw, A_HD), _kv_replication())


def _rwkv_chunk_terms(r, lw, k, v, kk, a, fill=lambda: None):
    stage_count = [0]

    def each(f, *seqs):
        out = [f(*xs) for xs in zip(*seqs)]
        stage_count[0] += 1
        if stage_count[0] % 3 == 0:
            fill()
        return out

    cat0 = lambda *xs: jnp.concatenate(xs, axis=0)
    cat1 = lambda *xs: jnp.concatenate(xs, axis=1)
    bd = lambda x: _bd_rows(x, B_HD, HG)
    L = r[0].shape[0]
    tril = _tril_ones(L)
    c = each(lambda x: _split_dot(tril, x, 3), lw)
    c_last = each(lambda x: x[L - 1:L, :], c)
    e_last = each(lambda x, xl: jnp.exp(xl - x), c, c_last)
    beta = each(lambda x, y: x * y, kk, a)
    at = each(lambda x, cc, l: -x * jnp.exp(cc - l), kk, c, lw)
    rt = each(lambda x, cc: x * jnp.exp(cc), r, c)
    enc = each(lambda cc: jnp.exp(-cc), c)
    ar = each(cat0, at, rt)
    mb = each(lambda x, b, e: _dot_nt(x, bd(b * e)), ar, beta, enc)
    mk = each(lambda x, b, e: _dot_nt(x, bd(b * e)), ar, k, enc)
    row = lax.broadcasted_iota(jnp.int32, (L, HG * L), 0)
    col = lax.broadcasted_iota(jnp.int32, (L, HG * L), 1) % L
    strict = col < row
    incl = col <= row
    m_b = each(lambda x: jnp.where(strict, x[:L], 0.0), mb)
    m_k = each(lambda x: jnp.where(strict, x[:L], 0.0), mk)
    n_b = each(lambda x: jnp.where(incl, x[L:], 0.0), mb)
    n_k = each(lambda x: jnp.where(incl, x[L:], 0.0), mk)
    eye = jnp.where(col == row, 1.0, 0.0)
    t_inv = each(lambda x: eye + x, m_b)
    p = m_b
    for lvl in range(1, int(math.log2(L))):
        p_bd = each(lambda x: _bd_rows(x, L, HG), p)
        if lvl == 1:
            p = each(_dot, p, p_bd)
        else:
            tp = each(lambda t, x, xb: _dot(cat0(t, x), xb), t_inv, p, p_bd)
            t_inv = each(lambda t, x: t + x[:L], t_inv, tp)
            p = each(lambda x: x[L:], tp)
    t_inv = each(lambda t, x: t + _dot(t, _bd_rows(x, L, HG)), t_inv, p)
    v_bd = each(bd, v)
    mkv = each(_dot, m_k, v_bd)
    wu = each(lambda t, x, y: _dot(t, cat1(bd(x), bd(y))), t_inv, at, mkv)
    w1 = each(lambda x: x[:, :HGW], wu)
    u0 = each(lambda x: x[:, HGW:], wu)
    nbo = each(lambda n, x, y: _dot(n, cat1(bd(x), bd(y))), n_b, w1, u0)
    q1 = each(lambda x, y: x + y[:, :HGW], rt, nbo)
    y0 = each(lambda x, n, vb: x[:, HGW:] + _dot(n, vb), nbo, n_k, v_bd)
    bh = each(lambda x, y: x * y, beta, e_last)
    kh = each(lambda x, y: x * y, k, e_last)
    ri = lax.broadcasted_iota(jnp.int32, (HGW, HGW), 0) // B_HD
    ci = lax.broadcasted_iota(jnp.int32, (HGW, HGW), 1) // B_HD
    diag = ri == ci
    g_mat = each(lambda x, y: jnp.where(diag, _dot_tn(x, y), 0.0), w1, bh)
    h_mat = each(lambda x, y, z, w: jnp.where(diag, _dot_tn(cat0(x, y), cat0(z, w)), 0.0), u0, v, bh, kh)
    gam = each(jnp.exp, c_last)
    return q1, y0, g_mat, h_mat, gam


def _rwkv_body(pb_ref, shift_ref, s0_ref, mu_ref, w0_ref, w2_ref, a0_ref, a2_ref, g2_ref, kk_ref,
               ka_ref, rk_ref, lng_ref, lnb_ref, e_ref,
               o_ref, sout_ref,
               ext_ref, s_ref, *, tb, sub, lc):
    t = pl.program_id(1)
    W = BRANCH_W
    n_groups = W // HGW

    @pl.when(t == 0)
    def _():
        ext_ref[0:SUBLANES, :] = jnp.broadcast_to(shift_ref[0], (SUBLANES, ext_ref.shape[1]))
        for gi in range(n_groups):
            s_ref[gi] = _block_diag_value([s0_ref[0, gi * HG + h] for h in range(HG)])

    p = pb_ref[...]
    ext_ref[SUBLANES:, :] = p
    prev = ext_ref[SUBLANES - 1:SUBLANES - 1 + tb, :]
    xs = p + (prev - p) * mu_ref[...]
    ext_ref[0:SUBLANES, :] = p[tb - SUBLANES:, :]

    e = e_ref[...]
    n_sub = tb // sub

    rows_c = min(sub, lc)
    n_chunks = sub // rows_c

    def prep_stages(lo):
        f = {}
        x = xs[lo:lo + sub]
        f['r'], f['v'] = x[:, 0:W], x[:, 2 * W:3 * W]
        k = x[:, W:2 * W]
        wal = x[:, 3 * W:3 * W + LANES]
        gl = x[:, 3 * W + LANES:3 * W + 2 * LANES]

        def decay():
            w_log = -_softplus_big(-(w0_ref[...] + _dot(jnp.tanh(wal), w2_ref[...]))) - 0.5
            f['lw'] = -jnp.exp(w_log)

        def gates():
            f['a'] = _sigmoid(a0_ref[...] + _dot(wal, a2_ref[...]))
            f['g'] = _dot(_sigmoid(gl), g2_ref[...])

        def keys():
            kkr = k * kk_ref[...]
            f['kk'] = kkr / jnp.maximum(jnp.sqrt(_segsum(kkr * kkr, e)), 1e-12)
            f['k2'] = k * (1.0 + (f['a'] - 1.0) * ka_ref[...])

        return f, [decay, gates, keys]

    def post_stages(lo, f, y):
        t = {}

        def center():
            t['yc'] = y - _segsum(y, e) * (1.0 / B_HD)

        def norm():
            var = _segsum(t['yc'] * t['yc'], e) * (1.0 / B_HD)
            t['yn'] = t['yc'] * lax.rsqrt(var + RWKV_GN_EPS) * lng_ref[...] + lnb_ref[...]

        def out():
            bonus = _segsum(f['r'] * f['k2'] * rk_ref[...], e) * f['v']
            o_ref[lo:lo + sub, :] = (t['yn'] + bonus) * f['g']

        return [center, norm, out]

    def piece(x, ci, gi):
        blk = x[ci * rows_c:(ci + 1) * rows_c, gi * HGW:(gi + 1) * HGW]
        if rows_c < lc:
            blk = jnp.concatenate([blk, jnp.zeros((lc - rows_c, HGW), F32)], axis=0)
        return blk

    feats = [prep_stages(j * sub) for j in range(n_sub)]
    for stage in feats[0][1]:
        stage()
    pending = []

    def fill():
        if pending:
            pending.pop(0)()

    inst = [(ci, gi) for ci in range(n_chunks) for gi in range(n_groups)]
    for j in range(n_sub):
        f = feats[j][0]
        if j + 1 < n_sub:
            pending.extend(feats[j + 1][1])
        terms = _rwkv_chunk_terms(*[[piece(f[name], ci, gi) for ci, gi in inst]
                                    for name in ('r', 'lw', 'k2', 'v', 'kk', 'a')], fill)
        while pending:
            fill()
        y_rows = []
        for ci in range(n_chunks):
            y_cols = []
            for gi in range(n_groups):
                q1, y0, g_mat, h_mat, gam = [tm[ci * n_groups + gi] for tm in terms]
                s = s_ref[gi]
                y_cols.append(_dot_nt(q1, s) + y0)
                s_ref[gi] = s * gam + _dot(s, g_mat) + h_mat
            y_rows.append(jnp.concatenate(y_cols, axis=1)[0:rows_c])
        pending.extend(post_stages(j * sub, f, jnp.concatenate(y_rows, axis=0)))
    while pending:
        fill()

    @pl.when(t == pl.num_programs(1) - 1)
    def _():
        for gi in range(n_groups):
            s = s_ref[gi]
            for h in range(HG):
                sout_ref[0, gi * HG + h] = s[h * B_HD:(h + 1) * B_HD, h * B_HD:(h + 1) * B_HD]


def _rwkv(pb, shift0, s0, wts, nb, t):
    tb = _row_tile(t, 512)
    sub = min(tb, 256)
    lc = CHUNK
    assert sub % lc == 0 or sub < lc
    nt = t // tb
    width = pb.shape[1]
    W = BRANCH_W
    n_groups = W // HGW
    row = lambda n: _resident((1, n))
    return pl.pallas_call(
        functools.partial(_rwkv_body, tb=tb, sub=sub, lc=lc),
        grid=(nb, nt),
        in_specs=[pl.BlockSpec((tb, width), lambda b, i: (b * nt + i, 0)),
                  pl.BlockSpec((1, 1, width), lambda b, i: (b, 0, 0)),
                  pl.BlockSpec((1, B_HEADS, B_HD, B_HD), lambda b, i: (b, 0, 0, 0)),
                  row(width), row(W), _resident((128, W)), row(W), _resident((128, W)),
                  _resident((128, W)), row(W), row(W), row(W), row(W), row(W), _resident((HGW, HGW))],
        out_specs=[pl.BlockSpec((tb, W), lambda b, i: (b * nt + i, 0)),
                   pl.BlockSpec((1, B_HEADS, B_HD, B_HD), lambda b, i: (b, 0, 0, 0))],
        out_shape=[jax.ShapeDtypeStruct((nb * t, W), F32),
                   jax.ShapeDtypeStruct((nb, B_HEADS, B_HD, B_HD), F32)],
        scratch_shapes=[pltpu.VMEM((tb + SUBLANES, width), F32),
                        pltpu.VMEM((n_groups, HGW, HGW), F32)],
        compiler_params=_cparams("parallel", "arbitrary"),
        name="rwkv7",
    )(pb, shift0, s0, *wts, _seg_matrix(HGW, B_HD))


def _gla_chunk_terms(q, k, v, gk):
    each = lambda f, *seqs: [f(*xs) for xs in zip(*seqs)]
    L = q[0].shape[0]
    kw = C_HEADS * C_DK
    vw = C_HEADS * C_DV
    tril = _tril_ones(L)
    b = each(lambda x: _split_dot(tril, x, 3), gk)
    qe = each(lambda x, y: x * jnp.exp(y), q, b)
    ke = each(lambda x, y: x * jnp.exp(-y), k, b)
    a_all = each(lambda x, y: _dot_nt(x, _bd_rows(y, C_DK, C_HEADS)), qe, ke)
    row = lax.broadcasted_iota(jnp.int32, (L, C_HEADS * L), 0)
    col = lax.broadcasted_iota(jnp.int32, (L, C_HEADS * L), 1) % L
    causal = col <= row
    o_intra = each(lambda x, y: _dot(jnp.where(causal, x, 0.0), _bd_rows(y, C_DV, C_HEADS)), a_all, v)
    b_last = each(lambda x: x[L - 1:L, :], b)
    kd = each(lambda x, y, z: x * jnp.exp(z - y), k, b, b_last)
    ri = lax.broadcasted_iota(jnp.int32, (vw, kw), 0) // C_DV
    ci = lax.broadcasted_iota(jnp.int32, (vw, kw), 1) // C_DK
    diag = ri == ci
    upd = each(lambda x, y: jnp.where(diag, _dot_tn(x, y), 0.0), v, kd)
    decay = each(jnp.exp, b_last)
    return qe, o_intra, upd, decay


def _gla_body(pc_ref, s0_ref, g2_ref, gb_ref, norm_ref, o_ref, sout_ref,
              s_ref, *, tb, lc):
    t = pl.program_id(1)
    kw = C_HEADS * C_DK
    vw = C_HEADS * C_DV

    @pl.when(t == 0)
    def _():
        s_ref[...] = _block_diag_value([_transpose_exact(s0_ref[0, h]) for h in range(C_HEADS)])

    pc = pc_ref[...]
    og = pc[:, 2 * kw + vw:2 * kw + 2 * vw]
    gl = pc[:, 2 * kw + 2 * vw:2 * kw + 2 * vw + 128]
    z = _dot(gl, g2_ref[...]) + gb_ref[...]
    q = pc[:, 0:kw] * (C_DK ** -0.5)
    k = pc[:, kw:2 * kw]
    v = pc[:, 2 * kw:2 * kw + vw]
    gk = -_softplus_big(-z) * (1.0 / C_TAU)
    n_chunks = max(tb // lc, 1)

    def piece(x, ci):
        blk = x[ci * lc:min((ci + 1) * lc, tb), :]
        if tb < lc:
            blk = jnp.concatenate([blk, jnp.zeros((lc - tb, x.shape[1]), F32)], axis=0)
        return blk

    qe, o_intra, upd, decay = _gla_chunk_terms(
        *[[piece(x, ci) for ci in range(n_chunks)] for x in (q, k, v, gk)])
    y_rows = []
    for ci in range(n_chunks):
        st = s_ref[...]
        y_rows.append(_dot_nt(qe[ci], st) + o_intra[ci])
        s_ref[...] = st * decay[ci] + upd[ci]
    y = jnp.concatenate(y_rows, axis=0)[0:tb]
    outs = []
    for h in range(C_HEADS):
        yh = y[:, h * C_DV:(h + 1) * C_DV]
        outs.append(yh * lax.rsqrt(jnp.mean(yh * yh, axis=-1, keepdims=True) + EPS) * norm_ref[...])
    yn = jnp.concatenate(outs, axis=1)
    o_ref[...] = yn * (og * _sigmoid(og))

    @pl.when(t == pl.num_programs(1) - 1)
    def _():
        st = s_ref[...]
        for h in range(C_HEADS):
            sout_ref[0, h] = _transpose_exact(st[h * C_DV:(h + 1) * C_DV, h * C_DK:(h + 1) * C_DK])


def _gla(pc, s0, g2, gb, norm, nb, t):
    tb = _row_tile(t, 256)
    lc = CHUNK
    assert tb % lc == 0 or tb < lc
    nt = t // tb
    width = pc.shape[1]
    kw, vw = C_HEADS * C_DK, C_HEADS * C_DV
    return pl.pallas_call(
        functools.partial(_gla_body, tb=tb, lc=lc),
        grid=(nb, nt),
        in_specs=[pl.BlockSpec((tb, width), lambda b, i: (b * nt + i, 0)),
                  pl.BlockSpec((1, C_HEADS, C_DK, C_DV), lambda b, i: (b, 0, 0, 0)),
                  _resident((128, kw)), _resident((1, kw)), _resident((1, C_DV))],
        out_specs=[pl.BlockSpec((tb, vw), lambda b, i: (b * nt + i, 0)),
                   pl.BlockSpec((1, C_HEADS, C_DK, C_DV), lambda b, i: (b, 0, 0, 0))],
        out_shape=[jax.ShapeDtypeStruct((nb * t, vw), F32),
                   jax.ShapeDtypeStruct((nb, C_HEADS, C_DK, C_DV), F32)],
        scratch_shapes=[pltpu.VMEM((vw, kw), F32)],
        compiler_params=_cparams("parallel", "arbitrary"),
        name="gla",
    )(pc, s0, g2, gb, norm)


def _shift_rows(x, d, fill):
    n = x.shape[0]
    if d % SUBLANES == 0:
        head = jnp.full((d, x.shape[1]), fill, x.dtype)
        return jnp.concatenate([head, x[:n - d]], axis=0)
    rolled = pltpu.roll(x, d, 0)
    row = lax.broadcasted_iota(jnp.int32, x.shape, 0)
    return jnp.where(row < d, fill, rolled)


def _lru_block(gate, xr, ext_ref, h_ref, au_ref, cw_ref, cb_ref, wa_ref, ba_ref, wx_ref, bx_ref,
               lam_ref, tb, fill=lambda: None):
    W = BRANCH_W
    ext_ref[SUBLANES:, :] = xr
    xc = cb_ref[...] + xr * cw_ref[CONV_W - 1:CONV_W, :]
    for j in range(CONV_W - 1):
        off = SUBLANES - (CONV_W - 1) + j
        xc = xc + ext_ref[off:off + tb, :] * cw_ref[j:j + 1, :]
    tail = ext_ref[tb:tb + SUBLANES, :]
    ext_ref[0:SUBLANES, :] = tail
    fill()

    r = _sigmoid(_dot(xc, wa_ref[...]) + ba_ref[...])
    fill()
    gi = _sigmoid(_dot(xc, wx_ref[...]) + bx_ref[...])
    fill()
    log_a = (-LRU_C) * r * _softplus(-lam_ref[...])
    a = jnp.exp(log_a)
    u = jnp.sqrt(1.0 - jnp.exp(2.0 * log_a)) * (gi * xc)
    fill()
    ng = tb // SUBLANES
    row_in_group = lax.broadcasted_iota(jnp.int32, (ng, SUBLANES, W), 1)
    a = a.reshape(ng, SUBLANES, W)
    u = u.reshape(ng, SUBLANES, W)
    d = 1
    while d < SUBLANES:
        inside = row_in_group >= d
        u = u + a * jnp.where(inside, pltpu.roll(u, d, 1), 0.0)
        a = a * jnp.where(inside, pltpu.roll(a, d, 1), 1.0)
        d *= 2
        fill()
    a = a.reshape(tb, W)
    u = u.reshape(tb, W)
    h_in = h_ref[...]
    if ng % SUBLANES == 0:
        ends = pl.ds(SUBLANES - 1, ng, stride=SUBLANES)
        for j in range(W // LANES):
            au_ref[0, j] = a[:, j * LANES:(j + 1) * LANES]
            au_ref[1, j] = u[:, j * LANES:(j + 1) * LANES]
        ae = jnp.concatenate([au_ref[0, j, ends, :] for j in range(W // LANES)], axis=1)
        ue = jnp.concatenate([au_ref[1, j, ends, :] for j in range(W // LANES)], axis=1)
        d = 1
        while d < ng:
            ue = ue + ae * _shift_rows(ue, d, 0.0)
            ae = ae * _shift_rows(ae, d, 1.0)
            d *= 2
        carry = _shift_rows(ue + ae * h_in, 1, h_in)
        carry_rows = jnp.broadcast_to(carry[:, None, :], (ng, SUBLANES, W)).reshape(tb, W)
    else:
        rows = []
        for g in range(ng):
            rows.append(jnp.broadcast_to(h_in, (SUBLANES, W)))
            end = (g + 1) * SUBLANES - 1
            h_in = u[end:end + 1, :] + a[end:end + 1, :] * h_in
        carry_rows = jnp.concatenate(rows, axis=0)
    h = u + a * carry_rows
    h_last = h[tb - 1:tb, :]
    h_ref[...] = h_last
    fill()
    out = h * (0.5 * gate * (1.0 + jnp.tanh(math.sqrt(2.0 / math.pi)
                                            * (gate + 0.044715 * gate * gate * gate))))
    return out, tail, h_last


def _lru_body(pd_ref, conv0_ref, h0_ref, cw_ref, cb_ref, wa_ref, ba_ref, wx_ref, bx_ref, lam_ref,
              o_ref, convout_ref, hout_ref, ext_ref, h_ref, au_ref, *, tb):
    t = pl.program_id(1)
    W = BRANCH_W

    @pl.when(t == 0)
    def _():
        ext_ref[0:SUBLANES, :] = conv0_ref[0]
        h_ref[...] = h0_ref[0]

    out, tail, h_last = _lru_block(pd_ref[:, 0:W], pd_ref[:, W:2 * W], ext_ref, h_ref, au_ref, cw_ref,
                                   cb_ref, wa_ref, ba_ref, wx_ref, bx_ref, lam_ref, tb)
    o_ref[...] = out

    @pl.when(t == pl.num_programs(1) - 1)
    def _():
        convout_ref[0] = tail
        hout_ref[0] = h_last


def _mix_in_body(x_ref, g_ref, wa_ref, wb_ref, wc_ref, wd_ref,
                 bias_ref, sink_ref, qn_ref, kn_ref, eq_ref, ek_ref, rep_ref,
                 conv0_ref, h0_ref, cw_ref, cb_ref, lwa_ref, lba_ref, lwx_ref, lbx_ref, lam_ref,
                 pb_ref, pc_ref, oa_ref, od_ref, kout_ref, vout_ref, convout_ref, hout_ref,
                 kv_tail_ref, ext_ref, h_ref, au_ref, *, tb):
    t = pl.program_id(1)
    W = BRANCH_W

    @pl.when(t == 0)
    def _():
        kv_tail_ref[...] = jnp.zeros(kv_tail_ref.shape, F32)
        ext_ref[0:SUBLANES, :] = conv0_ref[0]
        h_ref[...] = h0_ref[0]

    h = _rms(x_ref[...], g_ref[...]).astype(BF16)
    pd = jnp.dot(h, wd_ref[...], preferred_element_type=F32)
    pa = jnp.dot(h, wa_ref[...], preferred_element_type=F32)

    pending = [(w_ref, p_ref, c0, min(MXU_COLS, w_ref.shape[1] - c0))
               for w_ref, p_ref in ((wb_ref, pb_ref), (wc_ref, pc_ref))
               for c0 in range(0, w_ref.shape[1], MXU_COLS)]

    def fill():
        if pending:
            w_ref, p_ref, c0, width = pending.pop(0)
            p_ref[:, c0:c0 + width] = jnp.dot(h, w_ref[:, c0:c0 + width], preferred_element_type=F32)

    od, tail, h_last = _lru_block(pd[:, 0:W], pd[:, W:2 * W], ext_ref, h_ref, au_ref, cw_ref, cb_ref,
                                  lwa_ref, lba_ref, lwx_ref, lbx_ref, lam_ref, tb, fill)
    od_ref[...] = od
    outs = _attn_prompt_block(pa, kv_tail_ref, t, bias_ref, sink_ref, qn_ref, kn_ref, eq_ref, ek_ref,
                              rep_ref, tb, fill)
    for c, o in enumerate(outs):
        oa_ref[c * CHUNK:(c + 1) * CHUNK, :] = o
    while pending:
        fill()

    @pl.when(t == pl.num_programs(1) - 1)
    def _():
        kout_ref[0] = kv_tail_ref[0]
        vout_ref[0] = kv_tail_ref[1]
        convout_ref[0] = tail
        hout_ref[0] = h_last


def _mix_in(x, g, ws, bias, sink, qn, kn, conv0, h0, lru_wts, nb, t):
    n, d = x.shape
    tb = _row_tile(t, 512)
    assert tb % WINDOW == 0
    nt = t // tb
    W = BRANCH_W
    qw, kw = A_HEADS * A_HD, A_KV_HEADS * A_HD
    wa, wb, wc, wd = ws
    rows = lambda width: pl.BlockSpec((tb, width), lambda b, i: (b * nt + i, 0))
    per_seq = lambda *shape: pl.BlockSpec((1,) + shape, lambda b, i: (b,) + (0,) * len(shape))
    row = lambda width: _resident((1, width))
    return pl.pallas_call(
        functools.partial(_mix_in_body, tb=tb),
        grid=(nb, nt),
        in_specs=[rows(d), row(d)] + [_resident(w.shape) for w in ws]
                 + [_resident(bias.shape), _resident(sink.shape), row(qw), row(kw),
                    _resident((HGW, HGW)), _resident((kw, kw)), _resident((kw, qw)),
                    per_seq(SUBLANES, W), per_seq(1, W),
                    _resident((CONV_W, W)), row(W), _resident((W, W)), row(W), _resident((W, W)), row(W),
                    row(W)],
        out_specs=[rows(wb.shape[1]), rows(wc.shape[1]), rows(qw), rows(W),
                   per_seq(WINDOW, kw), per_seq(WINDOW, kw), per_seq(SUBLANES, W), per_seq(1, W)],
        out_shape=[jax.ShapeDtypeStruct((n, wb.shape[1]), F32), jax.ShapeDtypeStruct((n, wc.shape[1]), F32),
                   jax.ShapeDtypeStruct((n, qw), F32), jax.ShapeDtypeStruct((n, W), F32),
                   jax.ShapeDtypeStruct((nb, WINDOW, kw), F32), jax.ShapeDtypeStruct((nb, WINDOW, kw), F32),
                   jax.ShapeDtypeStruct((nb, SUBLANES, W), F32), jax.ShapeDtypeStruct((nb, 1, W), F32)],
        scratch_shapes=[pltpu.VMEM((2, WINDOW, kw), F32),
                        pltpu.VMEM((tb + SUBLANES, W), F32), pltpu.VMEM((1, W), F32),
                        pltpu.VMEM((2, W // LANES, tb, LANES), F32)],
        compiler_params=_cparams("parallel", "arbitrary"),
        name="mix_in",
    )(x, g, wa, wb, wc, wd, bias, sink, qn, kn, _seg_matrix(HGW, A_HD), _seg_matrix(kw, A_HD),
      _kv_replication(), conv0, h0, *lru_wts)


def _lru(pd, conv0, h0, wts, nb, t):
    tb = _row_tile(t, 256)
    nt = t // tb
    W = BRANCH_W
    row = lambda: _resident((1, W))
    return pl.pallas_call(
        functools.partial(_lru_body, tb=tb),
        grid=(nb, nt),
        in_specs=[pl.BlockSpec((tb, 2 * W), lambda b, i: (b * nt + i, 0)),
                  pl.BlockSpec((1, SUBLANES, W), lambda b, i: (b, 0, 0)),
                  pl.BlockSpec((1, 1, W), lambda b, i: (b, 0, 0)),
                  _resident((CONV_W, W)), row(), _resident((W, W)), row(), _resident((W, W)), row(),
                  row()],
        out_specs=[pl.BlockSpec((tb, W), lambda b, i: (b * nt + i, 0)),
                   pl.BlockSpec((1, SUBLANES, W), lambda b, i: (b, 0, 0)),
                   pl.BlockSpec((1, 1, W), lambda b, i: (b, 0, 0))],
        out_shape=[jax.ShapeDtypeStruct((nb * t, W), F32),
                   jax.ShapeDtypeStruct((nb, SUBLANES, W), F32),
                   jax.ShapeDtypeStruct((nb, 1, W), F32)],
        scratch_shapes=[pltpu.VMEM((tb + SUBLANES, W), F32), pltpu.VMEM((1, W), F32),
                        pltpu.VMEM((2, W // LANES, tb, LANES), F32)],
        compiler_params=_cparams("parallel", "arbitrary"),
        name="rglru",
    )(pd, conv0, h0, *wts)


def _block_diag(w):
    n, a, b = w.shape
    eye = jnp.eye(n, dtype=w.dtype)
    return (eye[:, None, :, None] * w[:, :, None, :]).reshape(n * a, n * b)


def _pad_rows_to(w, rows):
    return jnp.pad(w, ((0, rows - w.shape[0]), (0, 0)))


def _layer_weights(i, W):
    bf = lambda a: a.astype(BF16)
    row = lambda a: a.reshape(1, -1).astype(F32)
    bw = BRANCH_W
    a_cols = A_HEADS * A_HD + 2 * A_KV_HEADS * A_HD
    b_cols = 3 * bw + B_W_RANK + B_A_RANK + B_G_RANK
    c_cols = 2 * C_HEADS * C_DK + C_HEADS * C_DV + C_G_RANK + bw
    w_in = W['w_in'][i]
    d = w_in.shape[0]
    wa = w_in[:, :a_cols]
    wb = w_in[:, a_cols:a_cols + b_cols]
    wc = w_in[:, a_cols + b_cols:a_cols + b_cols + c_cols]
    wd = w_in[:, a_cols + b_cols + c_cols:]
    zpad = lambda n: jnp.zeros((d, n), w_in.dtype)
    assert B_W_RANK + B_A_RANK == LANES
    w2 = jnp.pad(W['rwkv_w2'][i], ((0, B_A_RANK), (0, 0)))
    a2 = jnp.pad(W['rwkv_a2'][i], ((B_W_RANK, 0), (0, 0)))
    qkv = 2 * C_HEADS * C_DK + C_HEADS * C_DV
    wc2 = jnp.concatenate([wc[:, :qkv], wc[:, qkv + C_G_RANK:], wc[:, qkv:qkv + C_G_RANK],
                           zpad(128 - C_G_RANK)], axis=1)
    return dict(
        ffn1=(row(W['g_ffn1'][i]), bf(W['w_ffn1_gate'][i]), bf(W['w_ffn1_up'][i]), bf(W['w_ffn1_down'][i])),
        ffn2=(row(W['g_ffn2'][i]), bf(W['w_ffn2_gate'][i]), bf(W['w_ffn2_up'][i]), bf(W['w_ffn2_down'][i])),
        g_mix=row(W['g_mix'][i]),
        w_in=(bf(wa), bf(wb), bf(wc2), bf(wd)),
        q_norm=row(jnp.tile(W['q_norm'][i], A_HEADS)),
        k_norm=row(jnp.tile(W['k_norm'][i], A_KV_HEADS)),
        sink=W['attn_sink'][i].astype(F32),
        rwkv=(row(W['rwkv_mu'][i]), row(W['rwkv_w0'][i]), bf(w2),
              row(W['rwkv_a0'][i]), bf(a2), bf(W['rwkv_g2'][i]),
              row(W['rwkv_k_k'][i]), row(W['rwkv_k_a'][i]), row(W['rwkv_r_k'][i]),
              row(W['rwkv_ln_g'][i]), row(W['rwkv_ln_b'][i])),
        gla=(bf(_pad_rows_to(W['gla_g2'][i], 128)), row(W['gla_gb'][i]), row(W['gla_norm'][i])),
        lru=(W['lru_conv_w'][i].astype(F32), row(W['lru_conv_b'][i]),
             bf(_block_diag(W['lru_wa'][i])), row(W['lru_ba'][i]),
             bf(_block_diag(W['lru_wx'][i])), row(W['lru_bx'][i]), row(W['lru_lambda'][i])),
        merge=(bf(W['w_merge_gate'][i]), bf(W['w_branch'][i]), bf(W['w_out'][i])),
        ple=(row(W['g_ple'][i]), bf(W['w_ple_gate'][i]), bf(W['w_ple_proj'][i])),
    )


def _trunk_layer(x, pe_all, layer, st, lw, bias, first_chunk):
    nb, t, d = x.shape
    n = nb * t
    ck, cv, shift0, s_rwkv0, s_gla0, conv0, lru0 = st
    x2 = x.reshape(n, d)
    x2 = _ffn(x2, *lw['ffn1'])
    kw = A_KV_HEADS * A_HD
    n_q = bias.shape[1] // A_HEADS
    sink2 = jnp.repeat(lw['sink'], n_q).reshape(1, A_HEADS * n_q)
    conv_pad = jnp.pad(conv0, ((0, 0), (SUBLANES - (CONV_W - 1), 0), (0, 0)))
    if first_chunk:
        pb, pc, o_a, o_d, k_win, v_win, conv_out, h_out = _mix_in(
            x2, lw['g_mix'], lw['w_in'], bias, sink2, lw['q_norm'], lw['k_norm'],
            conv_pad, lru0[:, None, :], lw['lru'], nb, t)
    else:
        pa, pb, pc, pd = _inproj(x2, lw['g_mix'], lw['w_in'])
        o_a, k_win, v_win = _attn_sample(pa, ck.reshape(nb, WINDOW, kw), cv.reshape(nb, WINDOW, kw),
                                         bias, sink2, lw['q_norm'], lw['k_norm'], nb, t)
        o_d, conv_out, h_out = _lru(pd, conv_pad, lru0[:, None, :], lw['lru'], nb, t)
    k_win = k_win.reshape(nb, WINDOW, A_KV_HEADS, A_HD)
    v_win = v_win.reshape(nb, WINDOW, A_KV_HEADS, A_HD)

    o_b, s_rwkv1 = _rwkv(pb, shift0[:, None, :], s_rwkv0, lw['rwkv'], nb, t)
    shift1 = pb.reshape(nb, t, -1)[:, -1]
    o_c, s_gla1 = _gla(pc, s_gla0, *lw['gla'], nb, t)
    conv1 = conv_out[:, SUBLANES - (CONV_W - 1):]
    lru1 = h_out[:, 0]

    x2 = _merge(x2, (o_a, o_b, o_c, o_d), lw['g_mix'], *lw['merge'])
    x2 = _ffn_ple(x2, pe_all.reshape(pe_all.shape[0], n, -1), layer, *lw['ffn2'], *lw['ple'])
    return x2.reshape(nb, t, d), (k_win, v_win, shift1, s_rwkv1, s_gla1, conv1, lru1)


def kernel(x_prompt, x_sample, cache_attn_k, cache_attn_v, state_rwkv_shift, state_rwkv, state_gla, state_lru_conv, state_lru, p_prompt, p_sample, rel_bias_table, g_ffn1, w_ffn1_gate, w_ffn1_up, w_ffn1_down, g_mix, w_in, q_norm, k_norm, attn_sink, rwkv_mu, rwkv_w0, rwkv_w2, rwkv_a0, rwkv_a2, rwkv_g2, rwkv_k_k, rwkv_k_a, rwkv_r_k, rwkv_ln_g, rwkv_ln_b, gla_g2, gla_gb, gla_norm, lru_conv_w, lru_conv_b, lru_wa, lru_ba, lru_wx, lru_bx, lru_lambda, w_merge_gate, w_branch, w_out, g_ffn2, w_ffn2_gate, w_ffn2_up, w_ffn2_down, g_ple, w_ple_gate, w_ple_proj):
    W = dict(g_ffn1=g_ffn1, w_ffn1_gate=w_ffn1_gate, w_ffn1_up=w_ffn1_up, w_ffn1_down=w_ffn1_down,
             g_mix=g_mix, w_in=w_in, q_norm=q_norm, k_norm=k_norm, attn_sink=attn_sink,
             rwkv_mu=rwkv_mu, rwkv_w0=rwkv_w0, rwkv_w2=rwkv_w2, rwkv_a0=rwkv_a0, rwkv_a2=rwkv_a2,
             rwkv_g2=rwkv_g2, rwkv_k_k=rwkv_k_k, rwkv_k_a=rwkv_k_a, rwkv_r_k=rwkv_r_k,
             rwkv_ln_g=rwkv_ln_g, rwkv_ln_b=rwkv_ln_b, gla_g2=gla_g2, gla_gb=gla_gb,
             gla_norm=gla_norm, lru_conv_w=lru_conv_w, lru_conv_b=lru_conv_b, lru_wa=lru_wa,
             lru_ba=lru_ba, lru_wx=lru_wx, lru_bx=lru_bx, lru_lambda=lru_lambda,
             w_merge_gate=w_merge_gate, w_branch=w_branch, w_out=w_out, g_ffn2=g_ffn2,
             w_ffn2_gate=w_ffn2_gate, w_ffn2_up=w_ffn2_up, w_ffn2_down=w_ffn2_down,
             g_ple=g_ple, w_ple_gate=w_ple_gate, w_ple_proj=w_ple_proj)
    depth = w_in.shape[0]
    dt = x_prompt.dtype
    bp, tp = x_prompt.shape[:2]
    ts = x_sample.shape[1]
    b_cols = state_rwkv_shift.shape[-1]
    bias_p = _rel_bias(rel_bias_table, CHUNK, WINDOW + CHUNK)
    bias_s = _rel_bias(rel_bias_table, ts, WINDOW + ts)
    yp, ys = x_prompt, x_sample
    st_p, st_s = [], []
    for i in range(depth):
        lw = _layer_weights(i, W)
        zero_st = (None, None,
                   jnp.zeros((bp, b_cols), dt),
                   jnp.zeros((bp, B_HEADS, B_HD, B_HD), dt),
                   jnp.zeros((bp, C_HEADS, C_DK, C_DV), dt),
                   jnp.zeros((bp, CONV_W - 1, BRANCH_W), dt),
                   jnp.zeros((bp, BRANCH_W), dt))
        yp, sp = _trunk_layer(yp, p_prompt, i, zero_st, lw, bias_p, True)
        cache_st = (cache_attn_k[i], cache_attn_v[i], state_rwkv_shift[i], state_rwkv[i],
                    state_gla[i], state_lru_conv[i], state_lru[i])
        ys, ss = _trunk_layer(ys, p_sample, i, cache_st, lw, bias_s, False)
        st_p.append(sp)
        st_s.append(ss)
    stack = lambda states, j: jnp.stack([s[j] for s in states])
    return (yp, ys) + tuple(stack(st_p, j) for j in range(7)) + tuple(stack(st_s, j) for j in range(7))
```

```python
import functools
import math

import numpy as np
import jax
import jax.numpy as jnp
from jax import lax
from jax.experimental import pallas as pl
from jax.experimental.pallas import tpu as pltpu

F32 = jnp.float32
BF16 = jnp.bfloat16

V7X_VMEM_BYTES = 64 * 1024 * 1024
VMEM_LIMIT = V7X_VMEM_BYTES - 8 * 1024 * 1024
SUBLANES = 8
LANES = 128
MXU_COLS = 256

EPS = 1e-6
NEG_INF = -1e30
CHUNK = 64
WINDOW = 128
N_BUCKETS = 32
MAX_DIST = 128
A_HEADS, A_KV_HEADS, A_HD = 8, 2, 64
A_GROUP = A_HEADS // A_KV_HEADS
B_HEADS, B_HD = 8, 64
B_W_RANK, B_A_RANK, B_G_RANK = 64, 64, 128
RWKV_GN_EPS = 64e-5
C_HEADS, C_DK, C_DV = 4, 64, 128
C_G_RANK = 16
C_TAU = 16.0
D_BLOCKS = 8
CONV_W = 4
LRU_C = 8.0
BRANCH_W = 512
HG = 4
HGW = HG * B_HD


def _cparams(*sem):
    return pltpu.CompilerParams(dimension_semantics=sem, vmem_limit_bytes=VMEM_LIMIT)


def _resident(shape):
    nd = len(shape)
    return pl.BlockSpec(shape, lambda *_: (0,) * nd, pipeline_mode=pl.Buffered(1))


def _row_tile(n, cap):
    t = min(n, cap)
    while n % t:
        t //= 2
    return t


def _dot(a, b):
    return jnp.dot(a.astype(BF16), b.astype(BF16), preferred_element_type=F32)


def _dot_nt(a, b):
    return lax.dot_general(a.astype(BF16), b.astype(BF16), (((1,), (1,)), ((), ())),
                           preferred_element_type=F32)


def _dot_tn(a, b):
    return lax.dot_general(a.astype(BF16), b.astype(BF16), (((0,), (0,)), ((), ())),
                           preferred_element_type=F32)


def _rms(x, g):
    return x * lax.rsqrt(jnp.mean(x * x, axis=-1, keepdims=True) + EPS) * g


def _sigmoid(x):
    return 1.0 / (1.0 + jnp.exp(-x))


def _softplus(x):
    return jnp.maximum(x, 0.0) + jnp.log1p(jnp.exp(-jnp.abs(x)))


def _softplus_big(x):
    return jnp.maximum(x, 0.0) + jnp.log(1.0 + jnp.exp(-jnp.abs(x)))


def _split_dot(e_lhs, x, terms):
    acc = None
    rem = x
    for n in range(terms):
        piece = rem.astype(BF16)
        d = jnp.dot(e_lhs, piece, preferred_element_type=F32)
        acc = d if acc is None else acc + d
        if n + 1 < terms:
            rem = rem - piece.astype(F32)
    return acc


def _segsum(x, e):
    blk = e.shape[0]
    xb = x.astype(BF16)
    cols = [jnp.dot(xb[:, j:j + blk], e, preferred_element_type=F32)
            for j in range(0, x.shape[1], blk)]
    return cols[0] if len(cols) == 1 else jnp.concatenate(cols, axis=1)


def _tril_ones(n):
    r = lax.broadcasted_iota(jnp.int32, (n, n), 0)
    c = lax.broadcasted_iota(jnp.int32, (n, n), 1)
    return jnp.where(r >= c, 1.0, 0.0).astype(BF16)


def _bd_rows(x, blk, nblk):
    lane_blk = lax.broadcasted_iota(jnp.int32, x.shape, 1) // blk
    return jnp.concatenate([jnp.where(lane_blk == h, x, 0.0) for h in range(nblk)], axis=0)


def _block_diag_value(blocks):
    n = len(blocks)
    a, b = blocks[0].shape
    rows = []
    for h, blk in enumerate(blocks):
        parts = ([jnp.zeros((a, h * b), F32)] if h else []) + [blk] \
            + ([jnp.zeros((a, (n - 1 - h) * b), F32)] if h < n - 1 else [])
        rows.append(jnp.concatenate(parts, axis=1))
    return jnp.concatenate(rows, axis=0)


def _transpose_exact(x):
    n = x.shape[0]
    r = lax.broadcasted_iota(jnp.int32, (n, n), 0)
    c = lax.broadcasted_iota(jnp.int32, (n, n), 1)
    eye = jnp.where(r == c, 1.0, 0.0).astype(BF16)
    acc = None
    rem = x
    for step in range(3):
        piece = rem.astype(BF16)
        d = lax.dot_general(piece, eye, (((0,), (0,)), ((), ())), preferred_element_type=F32)
        acc = d if acc is None else acc + d
        if step < 2:
            rem = rem - piece.astype(F32)
    return acc


def _seg_matrix(width, seg):
    i = np.arange(width)
    return jnp.asarray((i[:, None] // seg) == (i[None, :] // seg), dtype=BF16)


def _ffn_chunks(f):
    tiles = -(-f // MXU_COLS)
    cut = min(f, (tiles + 1) // 2 * MXU_COLS)
    return [(0, cut)] + ([(cut, f - cut)] if cut < f else [])


def _swiglu_residual(x, g_ref, wg_ref, wu_ref, wd_ref):
    h = _rms(x, g_ref[...]).astype(BF16)
    acc = None
    for start, size in _ffn_chunks(wg_ref.shape[1]):
        sl = pl.ds(start, size)
        gt = jnp.dot(h, wg_ref[:, sl], preferred_element_type=F32)
        up = jnp.dot(h, wu_ref[:, sl], preferred_element_type=F32)
        act = (gt * _sigmoid(gt) * up).astype(BF16)
        d = jnp.dot(act, wd_ref[sl, :], preferred_element_type=F32)
        acc = d if acc is None else acc + d
    return x + 0.5 * acc


def _ffn_body(x_ref, g_ref, wg_ref, wu_ref, wd_ref, o_ref):
    o_ref[...] = _swiglu_residual(x_ref[...], g_ref, wg_ref, wu_ref, wd_ref)


def _ffn_ple_body(x_ref, pe_ref, g_ref, wg_ref, wu_ref, wd_ref, gp_ref, wpg_ref, wpp_ref, o_ref):
    x = _swiglu_residual(x_ref[...], g_ref, wg_ref, wu_ref, wd_ref)
    h = _rms(x, gp_ref[...]).astype(BF16)
    gate = _sigmoid(jnp.dot(h, wpg_ref[...], preferred_element_type=F32))
    o_ref[...] = x + gate * jnp.dot(pe_ref[...].astype(BF16), wpp_ref[...], preferred_element_type=F32)


def _ffn(x, g, wg, wu, wd):
    n, d = x.shape
    f = wg.shape[1]
    tm = _row_tile(n, 512)
    return pl.pallas_call(
        _ffn_body,
        grid=(n // tm,),
        in_specs=[pl.BlockSpec((tm, d), lambda i: (i, 0)),
                  _resident((1, d)), _resident((d, f)), _resident((d, f)), _resident((f, d))],
        out_specs=pl.BlockSpec((tm, d), lambda i: (i, 0)),
        out_shape=jax.ShapeDtypeStruct((n, d), F32),
        compiler_params=_cparams("parallel"),
        name="ffn",
    )(x, g, wg, wu, wd)


def _inproj_body(x_ref, g_ref, wa_ref, wb_ref, wc_ref, wd_ref, pa_ref, pb_ref, pc_ref, pd_ref):
    h = _rms(x_ref[...], g_ref[...]).astype(BF16)
    for w_ref, p_ref in ((wa_ref, pa_ref), (wb_ref, pb_ref), (wc_ref, pc_ref), (wd_ref, pd_ref)):
        p_ref[...] = jnp.dot(h, w_ref[...], preferred_element_type=F32)


def _inproj(x, g, ws):
    n, d = x.shape
    tm = _row_tile(n, 512)
    widths = [w.shape[1] for w in ws]
    return pl.pallas_call(
        _inproj_body,
        grid=(n // tm,),
        in_specs=[pl.BlockSpec((tm, d), lambda i: (i, 0)), _resident((1, d))]
                 + [_resident((d, w)) for w in widths],
        out_specs=[pl.BlockSpec((tm, w), lambda i: (i, 0)) for w in widths],
        out_shape=[jax.ShapeDtypeStruct((n, w), F32) for w in widths],
        compiler_params=_cparams("parallel"),
        name="inproj",
    )(x, g, *ws)


def _merge_body(x_ref, oa_ref, ob_ref, oc_ref, od_ref, g_ref, wmg_ref, wb_ref, wo_ref, o_ref):
    x = x_ref[...]
    h = _rms(x, g_ref[...]).astype(BF16)
    y = None
    for n, b_ref in enumerate((oa_ref, ob_ref, oc_ref, od_ref)):
        gate = _sigmoid(jnp.dot(h, wmg_ref[n], preferred_element_type=F32))
        t = gate * jnp.dot(b_ref[...].astype(BF16), wb_ref[n], preferred_element_type=F32)
        y = t if y is None else y + t
    o_ref[...] = x + jnp.dot(y.astype(BF16), wo_ref[...], preferred_element_type=F32)


def _merge(x, outs, g, wmg, wb, wo):
    n, d = x.shape
    bw = outs[0].shape[1]
    tm = _row_tile(n, 512)
    return pl.pallas_call(
        _merge_body,
        grid=(n // tm,),
        in_specs=[pl.BlockSpec((tm, d), lambda i: (i, 0))]
                 + [pl.BlockSpec((tm, bw), lambda i: (i, 0))] * 4
                 + [_resident((1, d)), _resident(wmg.shape), _resident(wb.shape), _resident(wo.shape)],
        out_specs=pl.BlockSpec((tm, d), lambda i: (i, 0)),
        out_shape=jax.ShapeDtypeStruct((n, d), F32),
        compiler_params=_cparams("parallel"),
        name="merge",
    )(x, *outs, g, wmg, wb, wo)


def _ffn_ple(x, pe_all, layer, g, wg, wu, wd, gp, wpg, wpp):
    n, d = x.shape
    f = wg.shape[1]
    pd = pe_all.shape[2]
    tm = _row_tile(n, 512)
    return pl.pallas_call(
        _ffn_ple_body,
        grid=(n // tm,),
        in_specs=[pl.BlockSpec((tm, d), lambda i: (i, 0)),
                  pl.BlockSpec((None, tm, pd), lambda i: (layer, i, 0)),
                  _resident((1, d)), _resident((d, f)), _resident((d, f)), _resident((f, d)),
                  _resident((1, d)), _resident((d, d)), _resident((pd, d))],
        out_specs=pl.BlockSpec((tm, d), lambda i: (i, 0)),
        out_shape=jax.ShapeDtypeStruct((n, d), F32),
        compiler_params=_cparams("parallel"),
        name="ffn_ple",
    )(x, pe_all, g, wg, wu, wd, gp, wpg, wpp)


def _t5_bucket_np(rel):
    half = N_BUCKETS // 2
    max_exact = half // 2
    ret = np.where(rel > 0, half, 0)
    n = np.abs(rel)
    nf = np.maximum(n, 1).astype(np.float32)
    large = max_exact + (np.log(nf / np.float32(max_exact)) / np.float32(math.log(MAX_DIST / max_exact))
                         * np.float32(half - max_exact)).astype(np.int32)
    large = np.minimum(large, half - 1)
    return (ret + np.where(n < max_exact, n, large)).astype(np.int32)


def _bias_body(idx_ref, table_ref, o_ref):
    idx = idx_ref[...]
    for h in range(A_HEADS):
        acc = jnp.zeros(idx.shape, F32)
        for b in range(N_BUCKETS):
            acc = jnp.where(idx == b, table_ref[b, h], acc)
        o_ref[h] = acc


def _rel_bias(table, n_q, n_k):
    rel = np.arange(n_k)[:, None] - WINDOW - np.arange(n_q)[None, :]
    idx = jnp.asarray(_t5_bucket_np(rel))
    out = pl.pallas_call(
        _bias_body,
        in_specs=[pl.BlockSpec(memory_space=pltpu.VMEM), pl.BlockSpec(memory_space=pltpu.SMEM)],
        out_specs=pl.BlockSpec(memory_space=pltpu.VMEM),
        out_shape=jax.ShapeDtypeStruct((A_HEADS, n_k, n_q), F32),
        name="rel_bias",
    )(idx, table)
    return jnp.transpose(out, (1, 0, 2)).reshape(n_k, A_HEADS * n_q)


def _head_rms(x, e, g):
    ms = _segsum(x * x, e) * (1.0 / A_HD)
    return x * lax.rsqrt(ms + EPS) * g


def _attend_blocks(q_list, k_list, v_list, bias, sink, valid_list, fill=lambda: None):
    def staged(f, *seqs):
        out = []
        for j, xs in enumerate(zip(*seqs)):
            out.append(f(*xs))
            if j % 8 == 7:
                fill()
        return out

    n = q_list[0].shape[0]
    gcols = A_GROUP * n
    inst = [(b, g) for b in range(len(q_list)) for g in range(A_KV_HEADS)]
    head = lambda x, h: x[:, h * A_HD:(h + 1) * A_HD]
    q_st = staged(lambda bg: jnp.concatenate([head(q_list[bg[0]], bg[1] * A_GROUP + r)
                                              for r in range(A_GROUP)], axis=0), inst)
    kg = staged(lambda bg: head(k_list[bg[0]], bg[1]), inst)
    vg = staged(lambda bg: head(v_list[bg[0]], bg[1]), inst)
    bias_g = [bias[:, g * gcols:(g + 1) * gcols] for g in range(A_KV_HEADS)]
    sink_g = [sink[:, g * gcols:(g + 1) * gcols] for g in range(A_KV_HEADS)]

    def scores(bg, k, qs):
        s = _dot_nt(k, qs) + bias_g[bg[1]]
        valid = valid_list[bg[0]]
        return s if valid is None else jnp.where(valid, s, NEG_INF)

    s = staged(scores, inst, kg, q_st)
    m = staged(lambda bg, x: jnp.maximum(jnp.max(x, axis=0, keepdims=True), sink_g[bg[1]]), inst, s)
    e = staged(lambda x, mm: jnp.exp(x - mm), s, m)
    rinv = [1.0 / (jnp.sum(x, axis=0, keepdims=True) + jnp.exp(sink_g[bg[1]] - mm))
            for bg, x, mm in zip(inst, e, m)]
    p = staged(lambda x, r: x * r, e, rinv)
    og = staged(_dot_tn, p, vg)
    outs = []
    for b in range(len(q_list)):
        pieces = [og[b * A_KV_HEADS + g][r * n:(r + 1) * n] for g in range(A_KV_HEADS)
                  for r in range(A_GROUP)]
        outs.append(jnp.concatenate(pieces, axis=1))
    return outs


def _attn_prompt_block(pa, kv_tail_ref, t_idx, bias_ref, sink_ref, qn_ref, kn_ref, eq_ref, ek_ref,
                       tb, fill):
    qw = A_HEADS * A_HD
    kw = A_KV_HEADS * A_HD
    band = WINDOW + CHUNK
    q = _head_rms(pa[:, 0:qw], eq_ref[...], qn_ref[...]) * (A_HD ** -0.5)
    kf = jnp.concatenate([kv_tail_ref[0], _head_rms(pa[:, qw:qw + kw], ek_ref[...], kn_ref[...])], axis=0)
    vf = jnp.concatenate([kv_tail_ref[1], pa[:, qw + kw:qw + 2 * kw]], axis=0)
    kv_tail_ref[0] = kf[tb:tb + WINDOW, :]
    kv_tail_ref[1] = vf[tb:tb + WINDOW, :]
    kb = kf.astype(BF16)
    vb = vf.astype(BF16)
    kidx = lax.broadcasted_iota(jnp.int32, (band, A_GROUP * CHUNK), 0)
    n_chunks = tb // CHUNK
    valid = [(kidx + (t_idx * tb + c * CHUNK - WINDOW)) >= 0 if c * CHUNK < WINDOW else None
             for c in range(n_chunks)]
    return _attend_blocks([q[c * CHUNK:(c + 1) * CHUNK] for c in range(n_chunks)],
                          [kb[c * CHUNK:c * CHUNK + band] for c in range(n_chunks)],
                          [vb[c * CHUNK:c * CHUNK + band] for c in range(n_chunks)],
                          bias_ref[...], sink_ref[...], valid, fill)


def _attn_sample_body(pa_ref, ck_ref, cv_ref, bias_ref, sink_ref, qn_ref, kn_ref, eq_ref, ek_ref,
                      o_ref, kout_ref, vout_ref, *, s, gb):
    qw = A_HEADS * A_HD
    kw = A_KV_HEADS * A_HD
    q = _head_rms(pa_ref[:, 0:qw], eq_ref[...], qn_ref[...]) * (A_HD ** -0.5)
    kn = _head_rms(pa_ref[:, qw:qw + kw], ek_ref[...], kn_ref[...])
    vn = pa_ref[:, qw + kw:qw + 2 * kw]
    q_list, k_list, v_list = [], [], []
    for b in range(gb):
        kf = jnp.concatenate([ck_ref[b], kn[b * s:(b + 1) * s]], axis=0)
        vf = jnp.concatenate([cv_ref[b], vn[b * s:(b + 1) * s]], axis=0)
        kout_ref[b] = kf[s:s + WINDOW, :]
        vout_ref[b] = vf[s:s + WINDOW, :]
        q_list.append(q[b * s:(b + 1) * s])
        k_list.append(kf.astype(BF16))
        v_list.append(vf.astype(BF16))
    outs = _attend_blocks(q_list, k_list, v_list, bias_ref[...], sink_ref[...], [None] * gb)
    for b in range(gb):
        o_ref[b * s:(b + 1) * s, :] = outs[b]


def _attn_sample(pa, ck, cv, bias, sink, qn, kn, nb, s):
    qw, kw = A_HEADS * A_HD, A_KV_HEADS * A_HD
    width = pa.shape[1]
    gb = _row_tile(nb, 8)
    return pl.pallas_call(
        functools.partial(_attn_sample_body, s=s, gb=gb),
        grid=(nb // gb,),
        in_specs=[pl.BlockSpec((gb * s, width), lambda b: (b, 0)),
                  pl.BlockSpec((gb, WINDOW, kw), lambda b: (b, 0, 0)),
                  pl.BlockSpec((gb, WINDOW, kw), lambda b: (b, 0, 0)),
                  _resident(bias.shape), _resident(sink.shape),
                  _resident((1, qw)), _resident((1, kw)), _resident((HGW, HGW)), _resident((kw, kw))],
        out_specs=[pl.BlockSpec((gb * s, qw), lambda b: (b, 0)),
                   pl.BlockSpec((gb, WINDOW, kw), lambda b: (b, 0, 0)),
                   pl.BlockSpec((gb, WINDOW, kw), lambda b: (b, 0, 0))],
        out_shape=[jax.ShapeDtypeStruct((nb * s, qw), F32),
                   jax.ShapeDtypeStruct((nb, WINDOW, kw), F32),
                   jax.ShapeDtypeStruct((nb, WINDOW, kw), F32)],
        compiler_params=_cparams("parallel"),
        name="attn_sample",
    )(pa, ck, cv, bias, sink, qn, kn, _seg_matrix(HGW, A_HD), _seg_matrix(kw, A_HD))


def _rwkv_chunk_terms(r, lw, k, v, kk, a, fill=lambda: None):
    stage_count = [0]

    def each(f, *seqs):
        out = [f(*xs) for xs in zip(*seqs)]
        stage_count[0] += 1
        if stage_count[0] % 3 == 0:
            fill()
        return out

    cat0 = lambda *xs: jnp.concatenate(xs, axis=0)
    cat1 = lambda *xs: jnp.concatenate(xs, axis=1)
    bd = lambda x: _bd_rows(x, B_HD, HG)
    L = r[0].shape[0]
    tril = _tril_ones(L)
    c = each(lambda x: _split_dot(tril, x, 3), lw)
    c_last = each(lambda x: x[L - 1:L, :], c)
    e_last = each(lambda x, xl: jnp.exp(xl - x), c, c_last)
    beta = each(lambda x, y: x * y, kk, a)
    at = each(lambda x, cc, l: -x * jnp.exp(cc - l), kk, c, lw)
    rt = each(lambda x, cc: x * jnp.exp(cc), r, c)
    enc = each(lambda cc: jnp.exp(-cc), c)
    ar = each(cat0, at, rt)
    mb = each(lambda x, b, e: _dot_nt(x, bd(b * e)), ar, beta, enc)
    mk = each(lambda x, b, e: _dot_nt(x, bd(b * e)), ar, k, enc)
    row = lax.broadcasted_iota(jnp.int32, (L, HG * L), 0)
    col = lax.broadcasted_iota(jnp.int32, (L, HG * L), 1) % L
    strict = col < row
    incl = col <= row
    m_b = each(lambda x: jnp.where(strict, x[:L], 0.0), mb)
    m_k = each(lambda x: jnp.where(strict, x[:L], 0.0), mk)
    n_b = each(lambda x: jnp.where(incl, x[L:], 0.0), mb)
    n_k = each(lambda x: jnp.where(incl, x[L:], 0.0), mk)
    eye = jnp.where(col == row, 1.0, 0.0)
    t_inv = each(lambda x: eye + x, m_b)
    p = m_b
    for lvl in range(1, int(math.log2(L))):
        p_bd = each(lambda x: _bd_rows(x, L, HG), p)
        if lvl == 1:
            p = each(_dot, p, p_bd)
        else:
            tp = each(lambda t, x, xb: _dot(cat0(t, x), xb), t_inv, p, p_bd)
            t_inv = each(lambda t, x: t + x[:L], t_inv, tp)
            p = each(lambda x: x[L:], tp)
    t_inv = each(lambda t, x: t + _dot(t, _bd_rows(x, L, HG)), t_inv, p)
    v_bd = each(bd, v)
    mnv = each(lambda x, y, vb: _dot(cat0(x, y), vb), m_k, n_k, v_bd)
    mkv = each(lambda x: x[:L], mnv)
    wu = each(lambda t, x, y: _dot(t, cat1(bd(x), bd(y))), t_inv, at, mkv)
    w1 = each(lambda x: x[:, :HGW], wu)
    u0 = each(lambda x: x[:, HGW:], wu)
    nbo = each(lambda n, x, y: _dot(n, cat1(bd(x), bd(y))), n_b, w1, u0)
    q1 = each(lambda x, y: x + y[:, :HGW], rt, nbo)
    y0 = each(lambda x, y: x[:, HGW:] + y[L:], nbo, mnv)
    bh = each(lambda x, y: x * y, beta, e_last)
    kh = each(lambda x, y: x * y, k, e_last)
    ri = lax.broadcasted_iota(jnp.int32, (HGW, HGW), 0) // B_HD
    ci = lax.broadcasted_iota(jnp.int32, (HGW, HGW), 1) // B_HD
    diag = ri == ci
    g_mat = each(lambda x, y: jnp.where(diag, _dot_tn(x, y), 0.0), w1, bh)
    h_mat = each(lambda x, y, z, w: jnp.where(diag, _dot_tn(cat0(x, y), cat0(z, w)), 0.0), u0, v, bh, kh)
    gam = each(jnp.exp, c_last)
    return q1, y0, g_mat, h_mat, gam


def _rwkv_body(pb_ref, shift_ref, s0_ref, mu_ref, w0_ref, w2_ref, a0_ref, a2_ref, g2_ref, kk_ref,
               ka_ref, rk_ref, lng_ref, lnb_ref, e_ref,
               o_ref, sout_ref,
               ext_ref, s_ref, *, tb, sub, lc):
    t = pl.program_id(1)
    W = BRANCH_W
    n_groups = W // HGW

    @pl.when(t == 0)
    def _():
        ext_ref[0:SUBLANES, :] = jnp.broadcast_to(shift_ref[0], (SUBLANES, ext_ref.shape[1]))
        for gi in range(n_groups):
            s_ref[gi] = _block_diag_value([s0_ref[0, gi * HG + h] for h in range(HG)])

    p = pb_ref[...]
    ext_ref[SUBLANES:, :] = p
    prev = ext_ref[SUBLANES - 1:SUBLANES - 1 + tb, :]
    xs = p + (prev - p) * mu_ref[...]
    ext_ref[0:SUBLANES, :] = p[tb - SUBLANES:, :]

    e = e_ref[...]
    n_sub = tb // sub

    rows_c = min(sub, lc)
    n_chunks = sub // rows_c

    def prep_stages(lo):
        f = {}
        x = xs[lo:lo + sub]
        f['r'], f['v'] = x[:, 0:W], x[:, 2 * W:3 * W]
        k = x[:, W:2 * W]
        wal = x[:, 3 * W:3 * W + LANES]
        gl = x[:, 3 * W + LANES:3 * W + 2 * LANES]

        def decay():
            w_log = -_softplus_big(-(w0_ref[...] + _dot(jnp.tanh(wal), w2_ref[...]))) - 0.5
            f['lw'] = -jnp.exp(w_log)

        def gates():
            f['a'] = _sigmoid(a0_ref[...] + _dot(wal, a2_ref[...]))
            f['g'] = _dot(_sigmoid(gl), g2_ref[...])

        def keys():
            kkr = k * kk_ref[...]
            f['kk'] = kkr / jnp.maximum(jnp.sqrt(_segsum(kkr * kkr, e)), 1e-12)
            f['k2'] = k * (1.0 + (f['a'] - 1.0) * ka_ref[...])

        return f, [decay, gates, keys]

    def post_stages(lo, f, y):
        t = {}

        def center():
            t['yc'] = y - _segsum(y, e) * (1.0 / B_HD)

        def norm():
            var = _segsum(t['yc'] * t['yc'], e) * (1.0 / B_HD)
            t['yn'] = t['yc'] * lax.rsqrt(var + RWKV_GN_EPS) * lng_ref[...] + lnb_ref[...]

        def out():
            bonus = _segsum(f['r'] * f['k2'] * rk_ref[...], e) * f['v']
            o_ref[lo:lo + sub, :] = (t['yn'] + bonus) * f['g']

        return [center, norm, out]

    def piece(x, ci, gi):
        blk = x[ci * rows_c:(ci + 1) * rows_c, gi * HGW:(gi + 1) * HGW]
        if rows_c < lc:
            blk = jnp.concatenate([blk, jnp.zeros((lc - rows_c, HGW), F32)], axis=0)
        return blk

    feats = [prep_stages(j * sub) for j in range(n_sub)]
    for stage in feats[0][1]:
        stage()
    pending = []

    def fill():
        if pending:
            pending.pop(0)()

    inst = [(ci, gi) for ci in range(n_chunks) for gi in range(n_groups)]
    for j in range(n_sub):
        f = feats[j][0]
        if j + 1 < n_sub:
            pending.extend(feats[j + 1][1])
        terms = _rwkv_chunk_terms(*[[piece(f[name], ci, gi) for ci, gi in inst]
                                    for name in ('r', 'lw', 'k2', 'v', 'kk', 'a')], fill)
        while pending:
            fill()
        y_rows = []
        for ci in range(n_chunks):
            y_cols = []
            for gi in range(n_groups):
                q1, y0, g_mat, h_mat, gam = [tm[ci * n_groups + gi] for tm in terms]
                s = s_ref[gi]
                y_cols.append(_dot_nt(q1, s) + y0)
                s_ref[gi] = s * gam + _dot(s, g_mat) + h_mat
            y_rows.append(jnp.concatenate(y_cols, axis=1)[0:rows_c])
        pending.extend(post_stages(j * sub, f, jnp.concatenate(y_rows, axis=0)))
    while pending:
        fill()

    @pl.when(t == pl.num_programs(1) - 1)
    def _():
        for gi in range(n_groups):
            s = s_ref[gi]
            for h in range(HG):
                sout_ref[0, gi * HG + h] = s[h * B_HD:(h + 1) * B_HD, h * B_HD:(h + 1) * B_HD]


def _rwkv(pb, shift0, s0, wts, nb, t):
    tb = _row_tile(t, 512)
    sub = min(tb, 256)
    lc = CHUNK
    assert sub % lc == 0 or sub < lc
    nt = t // tb
    width = pb.shape[1]
    W = BRANCH_W
    n_groups = W // HGW
    row = lambda n: _resident((1, n))
    return pl.pallas_call(
        functools.partial(_rwkv_body, tb=tb, sub=sub, lc=lc),
        grid=(nb, nt),
        in_specs=[pl.BlockSpec((tb, width), lambda b, i: (b * nt + i, 0)),
                  pl.BlockSpec((1, 1, width), lambda b, i: (b, 0, 0)),
                  pl.BlockSpec((1, B_HEADS, B_HD, B_HD), lambda b, i: (b, 0, 0, 0)),
                  row(width), row(W), _resident((128, W)), row(W), _resident((128, W)),
                  _resident((128, W)), row(W), row(W), row(W), row(W), row(W), _resident((HGW, HGW))],
        out_specs=[pl.BlockSpec((tb, W), lambda b, i: (b * nt + i, 0)),
                   pl.BlockSpec((1, B_HEADS, B_HD, B_HD), lambda b, i: (b, 0, 0, 0))],
        out_shape=[jax.ShapeDtypeStruct((nb * t, W), F32),
                   jax.ShapeDtypeStruct((nb, B_HEADS, B_HD, B_HD), F32)],
        scratch_shapes=[pltpu.VMEM((tb + SUBLANES, width), F32),
                        pltpu.VMEM((n_groups, HGW, HGW), F32)],
        compiler_params=_cparams("parallel", "arbitrary"),
        name="rwkv7",
    )(pb, shift0, s0, *wts, _seg_matrix(HGW, B_HD))


def _gla_chunk_terms(q, k, v, gk):
    each = lambda f, *seqs: [f(*xs) for xs in zip(*seqs)]
    L = q[0].shape[0]
    kw = C_HEADS * C_DK
    vw = C_HEADS * C_DV
    tril = _tril_ones(L)
    b = each(lambda x: _split_dot(tril, x, 3), gk)
    qe = each(lambda x, y: x * jnp.exp(y), q, b)
    ke = each(lambda x, y: x * jnp.exp(-y), k, b)
    a_all = each(lambda x, y: _dot_nt(x, _bd_rows(y, C_DK, C_HEADS)), qe, ke)
    row = lax.broadcasted_iota(jnp.int32, (L, C_HEADS * L), 0)
    col = lax.broadcasted_iota(jnp.int32, (L, C_HEADS * L), 1) % L
    causal = col <= row
    o_intra = each(lambda x, y: _dot(jnp.where(causal, x, 0.0), _bd_rows(y, C_DV, C_HEADS)), a_all, v)
    b_last = each(lambda x: x[L - 1:L, :], b)
    kd = each(lambda x, y, z: x * jnp.exp(z - y), k, b, b_last)
    ri = lax.broadcasted_iota(jnp.int32, (vw, kw), 0) // C_DV
    ci = lax.broadcasted_iota(jnp.int32, (vw, kw), 1) // C_DK
    diag = ri == ci
    upd = each(lambda x, y: jnp.where(diag, _dot_tn(x, y), 0.0), v, kd)
    decay = each(jnp.exp, b_last)
    return qe, o_intra, upd, decay


def _gla_body(pc_ref, s0_ref, g2_ref, gb_ref, norm_ref, o_ref, sout_ref,
              s_ref, *, tb, lc):
    t = pl.program_id(1)
    kw = C_HEADS * C_DK
    vw = C_HEADS * C_DV

    @pl.when(t == 0)
    def _():
        s_ref[...] = _block_diag_value([_transpose_exact(s0_ref[0, h]) for h in range(C_HEADS)])

    pc = pc_ref[...]
    og = pc[:, 2 * kw + vw:2 * kw + 2 * vw]
    gl = pc[:, 2 * kw + 2 * vw:2 * kw + 2 * vw + 128]
    z = _dot(gl, g2_ref[...]) + gb_ref[...]
    q = pc[:, 0:kw] * (C_DK ** -0.5)
    k = pc[:, kw:2 * kw]
    v = pc[:, 2 * kw:2 * kw + vw]
    gk = -_softplus_big(-z) * (1.0 / C_TAU)
    n_chunks = max(tb // lc, 1)

    def piece(x, ci):
        blk = x[ci * lc:min((ci + 1) * lc, tb), :]
        if tb < lc:
            blk = jnp.concatenate([blk, jnp.zeros((lc - tb, x.shape[1]), F32)], axis=0)
        return blk

    qe, o_intra, upd, decay = _gla_chunk_terms(
        *[[piece(x, ci) for ci in range(n_chunks)] for x in (q, k, v, gk)])
    y_rows = []
    for ci in range(n_chunks):
        st = s_ref[...]
        y_rows.append(_dot_nt(qe[ci], st) + o_intra[ci])
        s_ref[...] = st * decay[ci] + upd[ci]
    y = jnp.concatenate(y_rows, axis=0)[0:tb]
    outs = []
    for h in range(C_HEADS):
        yh = y[:, h * C_DV:(h + 1) * C_DV]
        outs.append(yh * lax.rsqrt(jnp.mean(yh * yh, axis=-1, keepdims=True) + EPS) * norm_ref[...])
    yn = jnp.concatenate(outs, axis=1)
    o_ref[...] = yn * (og * _sigmoid(og))

    @pl.when(t == pl.num_programs(1) - 1)
    def _():
        st = s_ref[...]
        for h in range(C_HEADS):
            sout_ref[0, h] = _transpose_exact(st[h * C_DV:(h + 1) * C_DV, h * C_DK:(h + 1) * C_DK])


def _gla(pc, s0, g2, gb, norm, nb, t):
    tb = _row_tile(t, 512)
    lc = CHUNK
    assert tb % lc == 0 or tb < lc
    nt = t // tb
    width = pc.shape[1]
    kw, vw = C_HEADS * C_DK, C_HEADS * C_DV
    return pl.pallas_call(
        functools.partial(_gla_body, tb=tb, lc=lc),
        grid=(nb, nt),
        in_specs=[pl.BlockSpec((tb, width), lambda b, i: (b * nt + i, 0)),
                  pl.BlockSpec((1, C_HEADS, C_DK, C_DV), lambda b, i: (b, 0, 0, 0)),
                  _resident((128, kw)), _resident((1, kw)), _resident((1, C_DV))],
        out_specs=[pl.BlockSpec((tb, vw), lambda b, i: (b * nt + i, 0)),
                   pl.BlockSpec((1, C_HEADS, C_DK, C_DV), lambda b, i: (b, 0, 0, 0))],
        out_shape=[jax.ShapeDtypeStruct((nb * t, vw), F32),
                   jax.ShapeDtypeStruct((nb, C_HEADS, C_DK, C_DV), F32)],
        scratch_shapes=[pltpu.VMEM((vw, kw), F32)],
        compiler_params=_cparams("parallel", "arbitrary"),
        name="gla",
    )(pc, s0, g2, gb, norm)


def _shift_rows(x, d, fill):
    n = x.shape[0]
    if d % SUBLANES == 0:
        head = jnp.full((d, x.shape[1]), fill, x.dtype)
        return jnp.concatenate([head, x[:n - d]], axis=0)
    rolled = pltpu.roll(x, d, 0)
    row = lax.broadcasted_iota(jnp.int32, x.shape, 0)
    return jnp.where(row < d, fill, rolled)


def _lru_block(gate, xr, ext_ref, h_ref, au_ref, cw_ref, cb_ref, wa_ref, ba_ref, wx_ref, bx_ref,
               lam_ref, tb, fill=lambda: None):
    W = BRANCH_W
    ext_ref[SUBLANES:, :] = xr
    xc = cb_ref[...] + xr * cw_ref[CONV_W - 1:CONV_W, :]
    for j in range(CONV_W - 1):
        off = SUBLANES - (CONV_W - 1) + j
        xc = xc + ext_ref[off:off + tb, :] * cw_ref[j:j + 1, :]
    tail = ext_ref[tb:tb + SUBLANES, :]
    ext_ref[0:SUBLANES, :] = tail
    fill()

    def block_dot(w_ref):
        return jnp.concatenate([_dot(xc[:, j:j + MXU_COLS], w_ref[j:j + MXU_COLS, j:j + MXU_COLS])
                                for j in range(0, W, MXU_COLS)], axis=1)

    r = _sigmoid(block_dot(wa_ref) + ba_ref[...])
    fill()
    gi = _sigmoid(block_dot(wx_ref) + bx_ref[...])
    fill()
    log_a = (-LRU_C) * r * _softplus(-lam_ref[...])
    a = jnp.exp(log_a)
    u = jnp.sqrt(1.0 - jnp.exp(2.0 * log_a)) * (gi * xc)
    fill()
    ng = tb // SUBLANES
    row_in_group = lax.broadcasted_iota(jnp.int32, (ng, SUBLANES, W), 1)
    a = a.reshape(ng, SUBLANES, W)
    u = u.reshape(ng, SUBLANES, W)
    d = 1
    while d < SUBLANES:
        inside = row_in_group >= d
        u = u + a * jnp.where(inside, pltpu.roll(u, d, 1), 0.0)
        a = a * jnp.where(inside, pltpu.roll(a, d, 1), 1.0)
        d *= 2
        fill()
    a = a.reshape(tb, W)
    u = u.reshape(tb, W)
    h_in = h_ref[...]
    if ng % SUBLANES == 0:
        ends = pl.ds(SUBLANES - 1, ng, stride=SUBLANES)
        for j in range(W // LANES):
            au_ref[0, j] = a[:, j * LANES:(j + 1) * LANES]
            au_ref[1, j] = u[:, j * LANES:(j + 1) * LANES]
        ae = jnp.concatenate([au_ref[0, j, ends, :] for j in range(W // LANES)], axis=1)
        ue = jnp.concatenate([au_ref[1, j, ends, :] for j in range(W // LANES)], axis=1)
        d = 1
        while d < ng:
            ue = ue + ae * _shift_rows(ue, d, 0.0)
            ae = ae * _shift_rows(ae, d, 1.0)
            d *= 2
        carry = _shift_rows(ue + ae * h_in, 1, h_in)
        carry_rows = jnp.broadcast_to(carry[:, None, :], (ng, SUBLANES, W)).reshape(tb, W)
    else:
        rows = []
        for g in range(ng):
            rows.append(jnp.broadcast_to(h_in, (SUBLANES, W)))
            end = (g + 1) * SUBLANES - 1
            h_in = u[end:end + 1, :] + a[end:end + 1, :] * h_in
        carry_rows = jnp.concatenate(rows, axis=0)
    h = u + a * carry_rows
    h_last = h[tb - 1:tb, :]
    h_ref[...] = h_last
    fill()
    out = h * (0.5 * gate * (1.0 + jnp.tanh(math.sqrt(2.0 / math.pi)
                                            * (gate + 0.044715 * gate * gate * gate))))
    return out, tail, h_last


def _lru_body(pd_ref, conv0_ref, h0_ref, cw_ref, cb_ref, wa_ref, ba_ref, wx_ref, bx_ref, lam_ref,
              o_ref, convout_ref, hout_ref, ext_ref, h_ref, au_ref, *, tb):
    t = pl.program_id(1)
    W = BRANCH_W

    @pl.when(t == 0)
    def _():
        ext_ref[0:SUBLANES, :] = conv0_ref[0]
        h_ref[...] = h0_ref[0]

    out, tail, h_last = _lru_block(pd_ref[:, 0:W], pd_ref[:, W:2 * W], ext_ref, h_ref, au_ref, cw_ref,
                                   cb_ref, wa_ref, ba_ref, wx_ref, bx_ref, lam_ref, tb)
    o_ref[...] = out

    @pl.when(t == pl.num_programs(1) - 1)
    def _():
        convout_ref[0] = tail
        hout_ref[0] = h_last


def _mix_in_body(x_ref, g_ref, wa_ref, wb_ref, wc_ref, wd_ref,
                 bias_ref, sink_ref, qn_ref, kn_ref, eq_ref, ek_ref,
                 conv0_ref, h0_ref, cw_ref, cb_ref, lwa_ref, lba_ref, lwx_ref, lbx_ref, lam_ref,
                 pb_ref, pc_ref, oa_ref, od_ref, kout_ref, vout_ref, convout_ref, hout_ref,
                 kv_tail_ref, ext_ref, h_ref, au_ref, *, tb):
    t = pl.program_id(1)
    W = BRANCH_W

    @pl.when(t == 0)
    def _():
        kv_tail_ref[...] = jnp.zeros(kv_tail_ref.shape, F32)
        ext_ref[0:SUBLANES, :] = conv0_ref[0]
        h_ref[...] = h0_ref[0]

    h = _rms(x_ref[...], g_ref[...]).astype(BF16)
    pd = jnp.dot(h, wd_ref[...], preferred_element_type=F32)
    pa = jnp.dot(h, wa_ref[...], preferred_element_type=F32)

    pending = [(w_ref, p_ref, c0, min(MXU_COLS, w_ref.shape[1] - c0))
               for w_ref, p_ref in ((wb_ref, pb_ref), (wc_ref, pc_ref))
               for c0 in range(0, w_ref.shape[1], MXU_COLS)]

    def fill():
        if pending:
            w_ref, p_ref, c0, width = pending.pop(0)
            p_ref[:, c0:c0 + width] = jnp.dot(h, w_ref[:, c0:c0 + width], preferred_element_type=F32)

    od, tail, h_last = _lru_block(pd[:, 0:W], pd[:, W:2 * W], ext_ref, h_ref, au_ref, cw_ref, cb_ref,
                                  lwa_ref, lba_ref, lwx_ref, lbx_ref, lam_ref, tb, fill)
    od_ref[...] = od
    outs = _attn_prompt_block(pa, kv_tail_ref, t, bias_ref, sink_ref, qn_ref, kn_ref, eq_ref, ek_ref,
                              tb, fill)
    for c, o in enumerate(outs):
        oa_ref[c * CHUNK:(c + 1) * CHUNK, :] = o
    while pending:
        fill()

    @pl.when(t == pl.num_programs(1) - 1)
    def _():
        kout_ref[0] = kv_tail_ref[0]
        vout_ref[0] = kv_tail_ref[1]
        convout_ref[0] = tail
        hout_ref[0] = h_last


def _mix_in(x, g, ws, bias, sink, qn, kn, conv0, h0, lru_wts, nb, t):
    n, d = x.shape
    tb = _row_tile(t, 512)
    assert tb % WINDOW == 0
    nt = t // tb
    W = BRANCH_W
    qw, kw = A_HEADS * A_HD, A_KV_HEADS * A_HD
    wa, wb, wc, wd = ws
    rows = lambda width: pl.BlockSpec((tb, width), lambda b, i: (b * nt + i, 0))
    per_seq = lambda *shape: pl.BlockSpec((1,) + shape, lambda b, i: (b,) + (0,) * len(shape))
    row = lambda width: _resident((1, width))
    return pl.pallas_call(
        functools.partial(_mix_in_body, tb=tb),
        grid=(nb, nt),
        in_specs=[rows(d), row(d)] + [_resident(w.shape) for w in ws]
                 + [_resident(bias.shape), _resident(sink.shape), row(qw), row(kw),
                    _resident((HGW, HGW)), _resident((kw, kw)),
                    per_seq(SUBLANES, W), per_seq(1, W),
                    _resident((CONV_W, W)), row(W), _resident((W, W)), row(W), _resident((W, W)), row(W),
                    row(W)],
        out_specs=[rows(wb.shape[1]), rows(wc.shape[1]), rows(qw), rows(W),
                   per_seq(WINDOW, kw), per_seq(WINDOW, kw), per_seq(SUBLANES, W), per_seq(1, W)],
        out_shape=[jax.ShapeDtypeStruct((n, wb.shape[1]), F32), jax.ShapeDtypeStruct((n, wc.shape[1]), F32),
                   jax.ShapeDtypeStruct((n, qw), F32), jax.ShapeDtypeStruct((n, W), F32),
                   jax.ShapeDtypeStruct((nb, WINDOW, kw), F32), jax.ShapeDtypeStruct((nb, WINDOW, kw), F32),
                   jax.ShapeDtypeStruct((nb, SUBLANES, W), F32), jax.ShapeDtypeStruct((nb, 1, W), F32)],
        scratch_shapes=[pltpu.VMEM((2, WINDOW, kw), F32),
                        pltpu.VMEM((tb + SUBLANES, W), F32), pltpu.VMEM((1, W), F32),
                        pltpu.VMEM((2, W // LANES, tb, LANES), F32)],
        compiler_params=_cparams("parallel", "arbitrary"),
        name="mix_in",
    )(x, g, wa, wb, wc, wd, bias, sink, qn, kn, _seg_matrix(HGW, A_HD), _seg_matrix(kw, A_HD),
      conv0, h0, *lru_wts)


def _lru(pd, conv0, h0, wts, nb, t):
    tb = _row_tile(t, 256)
    nt = t // tb
    W = BRANCH_W
    row = lambda: _resident((1, W))
    return pl.pallas_call(
        functools.partial(_lru_body, tb=tb),
        grid=(nb, nt),
        in_specs=[pl.BlockSpec((tb, 2 * W), lambda b, i: (b * nt + i, 0)),
                  pl.BlockSpec((1, SUBLANES, W), lambda b, i: (b, 0, 0)),
                  pl.BlockSpec((1, 1, W), lambda b, i: (b, 0, 0)),
                  _resident((CONV_W, W)), row(), _resident((W, W)), row(), _resident((W, W)), row(),
                  row()],
        out_specs=[pl.BlockSpec((tb, W), lambda b, i: (b * nt + i, 0)),
                   pl.BlockSpec((1, SUBLANES, W), lambda b, i: (b, 0, 0)),
                   pl.BlockSpec((1, 1, W), lambda b, i: (b, 0, 0))],
        out_shape=[jax.ShapeDtypeStruct((nb * t, W), F32),
                   jax.ShapeDtypeStruct((nb, SUBLANES, W), F32),
                   jax.ShapeDtypeStruct((nb, 1, W), F32)],
        scratch_shapes=[pltpu.VMEM((tb + SUBLANES, W), F32), pltpu.VMEM((1, W), F32),
                        pltpu.VMEM((2, W // LANES, tb, LANES), F32)],
        compiler_params=_cparams("parallel", "arbitrary"),
        name="rglru",
    )(pd, conv0, h0, *wts)


def _block_diag(w):
    n, a, b = w.shape
    eye = jnp.eye(n, dtype=w.dtype)
    return (eye[:, None, :, None] * w[:, :, None, :]).reshape(n * a, n * b)


def _pad_rows_to(w, rows):
    return jnp.pad(w, ((0, rows - w.shape[0]), (0, 0)))


def _layer_weights(i, W):
    bf = lambda a: a.astype(BF16)
    row = lambda a: a.reshape(1, -1).astype(F32)
    bw = BRANCH_W
    a_cols = A_HEADS * A_HD + 2 * A_KV_HEADS * A_HD
    b_cols = 3 * bw + B_W_RANK + B_A_RANK + B_G_RANK
    c_cols = 2 * C_HEADS * C_DK + C_HEADS * C_DV + C_G_RANK + bw
    w_in = W['w_in'][i]
    d = w_in.shape[0]
    wa = w_in[:, :a_cols]
    wb = w_in[:, a_cols:a_cols + b_cols]
    wc = w_in[:, a_cols + b_cols:a_cols + b_cols + c_cols]
    wd = w_in[:, a_cols + b_cols + c_cols:]
    zpad = lambda n: jnp.zeros((d, n), w_in.dtype)
    assert B_W_RANK + B_A_RANK == LANES
    w2 = jnp.pad(W['rwkv_w2'][i], ((0, B_A_RANK), (0, 0)))
    a2 = jnp.pad(W['rwkv_a2'][i], ((B_W_RANK, 0), (0, 0)))
    qkv = 2 * C_HEADS * C_DK + C_HEADS * C_DV
    wc2 = jnp.concatenate([wc[:, :qkv], wc[:, qkv + C_G_RANK:], wc[:, qkv:qkv + C_G_RANK],
                           zpad(128 - C_G_RANK)], axis=1)
    return dict(
        ffn1=(row(W['g_ffn1'][i]), bf(W['w_ffn1_gate'][i]), bf(W['w_ffn1_up'][i]), bf(W['w_ffn1_down'][i])),
        ffn2=(row(W['g_ffn2'][i]), bf(W['w_ffn2_gate'][i]), bf(W['w_ffn2_up'][i]), bf(W['w_ffn2_down'][i])),
        g_mix=row(W['g_mix'][i]),
        w_in=(bf(wa), bf(wb), bf(wc2), bf(wd)),
        q_norm=row(jnp.tile(W['q_norm'][i], A_HEADS)),
        k_norm=row(jnp.tile(W['k_norm'][i], A_KV_HEADS)),
        sink=W['attn_sink'][i].astype(F32),
        rwkv=(row(W['rwkv_mu'][i]), row(W['rwkv_w0'][i]), bf(w2),
              row(W['rwkv_a0'][i]), bf(a2), bf(W['rwkv_g2'][i]),
              row(W['rwkv_k_k'][i]), row(W['rwkv_k_a'][i]), row(W['rwkv_r_k'][i]),
              row(W['rwkv_ln_g'][i]), row(W['rwkv_ln_b'][i])),
        gla=(bf(_pad_rows_to(W['gla_g2'][i], 128)), row(W['gla_gb'][i]), row(W['gla_norm'][i])),
        lru=(W['lru_conv_w'][i].astype(F32), row(W['lru_conv_b'][i]),
             bf(_block_diag(W['lru_wa'][i])), row(W['lru_ba'][i]),
             bf(_block_diag(W['lru_wx'][i])), row(W['lru_bx'][i]), row(W['lru_lambda'][i])),
        merge=(bf(W['w_merge_gate'][i]), bf(W['w_branch'][i]), bf(W['w_out'][i])),
        ple=(row(W['g_ple'][i]), bf(W['w_ple_gate'][i]), bf(W['w_ple_proj'][i])),
    )


def _trunk_layer(x, pe_all, layer, st, lw, bias, first_chunk):
    nb, t, d = x.shape
    n = nb * t
    ck, cv, shift0, s_rwkv0, s_gla0, conv0, lru0 = st
    x2 = x.reshape(n, d)
    x2 = _ffn(x2, *lw['ffn1'])
    kw = A_KV_HEADS * A_HD
    n_q = bias.shape[1] // A_HEADS
    sink2 = jnp.repeat(lw['sink'], n_q).reshape(1, A_HEADS * n_q)
    conv_pad = jnp.pad(conv0, ((0, 0), (SUBLANES - (CONV_W - 1), 0), (0, 0)))
    if first_chunk:
        pb, pc, o_a, o_d, k_win, v_win, conv_out, h_out = _mix_in(
            x2, lw['g_mix'], lw['w_in'], bias, sink2, lw['q_norm'], lw['k_norm'],
            conv_pad, lru0[:, None, :], lw['lru'], nb, t)
    else:
        pa, pb, pc, pd = _inproj(x2, lw['g_mix'], lw['w_in'])
        o_a, k_win, v_win = _attn_sample(pa, ck.reshape(nb, WINDOW, kw), cv.reshape(nb, WINDOW, kw),
                                         bias, sink2, lw['q_norm'], lw['k_norm'], nb, t)
        o_d, conv_out, h_out = _lru(pd, conv_pad, lru0[:, None, :], lw['lru'], nb, t)
    k_win = k_win.reshape(nb, WINDOW, A_KV_HEADS, A_HD)
    v_win = v_win.reshape(nb, WINDOW, A_KV_HEADS, A_HD)

    o_b, s_rwkv1 = _rwkv(pb, shift0[:, None, :], s_rwkv0, lw['rwkv'], nb, t)
    shift1 = pb.reshape(nb, t, -1)[:, -1]
    o_c, s_gla1 = _gla(pc, s_gla0, *lw['gla'], nb, t)
    conv1 = conv_out[:, SUBLANES - (CONV_W - 1):]
    lru1 = h_out[:, 0]

    x2 = _merge(x2, (o_a, o_b, o_c, o_d), lw['g_mix'], *lw['merge'])
    x2 = _ffn_ple(x2, pe_all.reshape(pe_all.shape[0], n, -1), layer, *lw['ffn2'], *lw['ple'])
    return x2.reshape(nb, t, d), (k_win, v_win, shift1, s_rwkv1, s_gla1, conv1, lru1)


def kernel(x_prompt, x_sample, cache_attn_k, cache_attn_v, state_rwkv_shift, state_rwkv, state_gla, state_lru_conv, state_lru, p_prompt, p_sample, rel_bias_table, g_ffn1, w_ffn1_gate, w_ffn1_up, w_ffn1_down, g_mix, w_in, q_norm, k_norm, attn_sink, rwkv_mu, rwkv_w0, rwkv_w2, rwkv_a0, rwkv_a2, rwkv_g2, rwkv_k_k, rwkv_k_a, rwkv_r_k, rwkv_ln_g, rwkv_ln_b, gla_g2, gla_gb, gla_norm, lru_conv_w, lru_conv_b, lru_wa, lru_ba, lru_wx, lru_bx, lru_lambda, w_merge_gate, w_branch, w_out, g_ffn2, w_ffn2_gate, w_ffn2_up, w_ffn2_down, g_ple, w_ple_gate, w_ple_proj):
    W = dict(g_ffn1=g_ffn1, w_ffn1_gate=w_ffn1_gate, w_ffn1_up=w_ffn1_up, w_ffn1_down=w_ffn1_down,
             g_mix=g_mix, w_in=w_in, q_norm=q_norm, k_norm=k_norm, attn_sink=attn_sink,
             rwkv_mu=rwkv_mu, rwkv_w0=rwkv_w0, rwkv_w2=rwkv_w2, rwkv_a0=rwkv_a0, rwkv_a2=rwkv_a2,
             rwkv_g2=rwkv_g2, rwkv_k_k=rwkv_k_k, rwkv_k_a=rwkv_k_a, rwkv_r_k=rwkv_r_k,
             rwkv_ln_g=rwkv_ln_g, rwkv_ln_b=rwkv_ln_b, gla_g2=gla_g2, gla_gb=gla_gb,
             gla_norm=gla_norm, lru_conv_w=lru_conv_w, lru_conv_b=lru_conv_b, lru_wa=lru_wa,
             lru_ba=lru_ba, lru_wx=lru_wx, lru_bx=lru_bx, lru_lambda=lru_lambda,
             w_merge_gate=w_merge_gate, w_branch=w_branch, w_out=w_out, g_ffn2=g_ffn2,
             w_ffn2_gate=w_ffn2_gate, w_ffn2_up=w_ffn2_up, w_ffn2_down=w_ffn2_down,
             g_ple=g_ple, w_ple_gate=w_ple_gate, w_ple_proj=w_ple_proj)
    depth = w_in.shape[0]
    dt = x_prompt.dtype
    bp, tp = x_prompt.shape[:2]
    ts = x_sample.shape[1]
    b_cols = state_rwkv_shift.shape[-1]
    bias_p = _rel_bias(rel_bias_table, CHUNK, WINDOW + CHUNK)
    bias_s = _rel_bias(rel_bias_table, ts, WINDOW + ts)
    yp, ys = x_prompt, x_sample
    st_p, st_s = [], []
    for i in range(depth):
        lw = _layer_weights(i, W)
        zero_st = (None, None,
                   jnp.zeros((bp, b_cols), dt),
                   jnp.zeros((bp, B_HEADS, B_HD, B_HD), dt),
                   jnp.zeros((bp, C_HEADS, C_DK, C_DV), dt),
                   jnp.zeros((bp, CONV_W - 1, BRANCH_W), dt),
                   jnp.zeros((bp, BRANCH_W), dt))
        yp, sp = _trunk_layer(yp, p_prompt, i, zero_st, lw, bias_p, True)
        cache_st = (cache_attn_k[i], cache_attn_v[i], state_rwkv_shift[i], state_rwkv[i],
                    state_gla[i], state_lru_conv[i], state_lru[i])
        ys, ss = _trunk_layer(ys, p_sample, i, cache_st, lw, bias_s, False)
        st_p.append(sp)
        st_s.append(ss)
    stack = lambda states, j: jnp.stack([s[j] for s in states])
    return (yp, ys) + tuple(stack(st_p, j) for j in range(7)) + tuple(stack(st_s, j) for j in range(7))
```

```python
import functools
import math

import numpy as np
import jax
import jax.numpy as jnp
from jax import lax
from jax.experimental import pallas as pl
from jax.experimental.pallas import tpu as pltpu

F32 = jnp.float32
BF16 = jnp.bfloat16

V7X_VMEM_BYTES = 64 * 1024 * 1024
VMEM_LIMIT = V7X_VMEM_BYTES - 8 * 1024 * 1024
SUBLANES = 8
LANES = 128
MXU_COLS = 256

EPS = 1e-6
NEG_INF = -1e30
CHUNK = 64
WINDOW = 128
N_BUCKETS = 32
MAX_DIST = 128
A_HEADS, A_KV_HEADS, A_HD = 8, 2, 64
A_GROUP = A_HEADS // A_KV_HEADS
B_HEADS, B_HD = 8, 64
B_W_RANK, B_A_RANK, B_G_RANK = 64, 64, 128
RWKV_GN_EPS = 64e-5
C_HEADS, C_DK, C_DV = 4, 64, 128
C_G_RANK = 16
C_TAU = 16.0
D_BLOCKS = 8
CONV_W = 4
LRU_C = 8.0
BRANCH_W = 512
HG = 4
HGW = HG * B_HD


def _cparams(*sem):
    return pltpu.CompilerParams(dimension_semantics=sem, vmem_limit_bytes=VMEM_LIMIT)


def _resident(shape):
    nd = len(shape)
    return pl.BlockSpec(shape, lambda *_: (0,) * nd, pipeline_mode=pl.Buffered(1))


def _row_tile(n, cap):
    t = min(n, cap)
    while n % t:
        t //= 2
    return t


def _dot(a, b):
    return jnp.dot(a.astype(BF16), b.astype(BF16), preferred_element_type=F32)


def _dot_nt(a, b):
    return lax.dot_general(a.astype(BF16), b.astype(BF16), (((1,), (1,)), ((), ())),
                           preferred_element_type=F32)


def _dot_tn(a, b):
    return lax.dot_general(a.astype(BF16), b.astype(BF16), (((0,), (0,)), ((), ())),
                           preferred_element_type=F32)


def _rms(x, g):
    return x * lax.rsqrt(jnp.mean(x * x, axis=-1, keepdims=True) + EPS) * g


def _sigmoid(x):
    return 1.0 / (1.0 + jnp.exp(-x))


def _softplus(x):
    return jnp.maximum(x, 0.0) + jnp.log1p(jnp.exp(-jnp.abs(x)))


def _softplus_big(x):
    return jnp.maximum(x, 0.0) + jnp.log(1.0 + jnp.exp(-jnp.abs(x)))


def _split_dot(e_lhs, x, terms):
    acc = None
    rem = x
    for n in range(terms):
        piece = rem.astype(BF16)
        d = jnp.dot(e_lhs, piece, preferred_element_type=F32)
        acc = d if acc is None else acc + d
        if n + 1 < terms:
            rem = rem - piece.astype(F32)
    return acc


def _segsum(x, e):
    blk = e.shape[0]
    xb = x.astype(BF16)
    cols = [jnp.dot(xb[:, j:j + blk], e, preferred_element_type=F32)
            for j in range(0, x.shape[1], blk)]
    return cols[0] if len(cols) == 1 else jnp.concatenate(cols, axis=1)


def _tril_ones(n):
    r = lax.broadcasted_iota(jnp.int32, (n, n), 0)
    c = lax.broadcasted_iota(jnp.int32, (n, n), 1)
    return jnp.where(r >= c, 1.0, 0.0).astype(BF16)


def _bd_rows(x, blk, nblk):
    lane_blk = lax.broadcasted_iota(jnp.int32, x.shape, 1) // blk
    return jnp.concatenate([jnp.where(lane_blk == h, x, 0.0) for h in range(nblk)], axis=0)


def _block_diag_value(blocks):
    n = len(blocks)
    a, b = blocks[0].shape
    rows = []
    for h, blk in enumerate(blocks):
        parts = ([jnp.zeros((a, h * b), F32)] if h else []) + [blk] \
            + ([jnp.zeros((a, (n - 1 - h) * b), F32)] if h < n - 1 else [])
        rows.append(jnp.concatenate(parts, axis=1))
    return jnp.concatenate(rows, axis=0)


def _transpose_exact(x):
    n = x.shape[0]
    r = lax.broadcasted_iota(jnp.int32, (n, n), 0)
    c = lax.broadcasted_iota(jnp.int32, (n, n), 1)
    eye = jnp.where(r == c, 1.0, 0.0).astype(BF16)
    acc = None
    rem = x
    for step in range(3):
        piece = rem.astype(BF16)
        d = lax.dot_general(piece, eye, (((0,), (0,)), ((), ())), preferred_element_type=F32)
        acc = d if acc is None else acc + d
        if step < 2:
            rem = rem - piece.astype(F32)
    return acc


def _seg_matrix(width, seg):
    i = np.arange(width)
    return jnp.asarray((i[:, None] // seg) == (i[None, :] // seg), dtype=BF16)


def _ffn_chunks(f):
    tiles = -(-f // MXU_COLS)
    cut = min(f, (tiles + 1) // 2 * MXU_COLS)
    return [(0, cut)] + ([(cut, f - cut)] if cut < f else [])


def _swiglu_residual(x, g_ref, wg_ref, wu_ref, wd_ref):
    h = _rms(x, g_ref[...]).astype(BF16)
    acc = None
    for start, size in _ffn_chunks(wg_ref.shape[1]):
        sl = pl.ds(start, size)
        gt = jnp.dot(h, wg_ref[:, sl], preferred_element_type=F32)
        up = jnp.dot(h, wu_ref[:, sl], preferred_element_type=F32)
        act = (gt * _sigmoid(gt) * up).astype(BF16)
        d = jnp.dot(act, wd_ref[sl, :], preferred_element_type=F32)
        acc = d if acc is None else acc + d
    return x + 0.5 * acc


def _ffn_body(x_ref, g_ref, wg_ref, wu_ref, wd_ref, o_ref):
    o_ref[...] = _swiglu_residual(x_ref[...], g_ref, wg_ref, wu_ref, wd_ref)


def _ffn_ple_body(x_ref, pe_ref, g_ref, wg_ref, wu_ref, wd_ref, gp_ref, wpg_ref, wpp_ref, o_ref):
    x = _swiglu_residual(x_ref[...], g_ref, wg_ref, wu_ref, wd_ref)
    h = _rms(x, gp_ref[...]).astype(BF16)
    gate = _sigmoid(jnp.dot(h, wpg_ref[...], preferred_element_type=F32))
    o_ref[...] = x + gate * jnp.dot(pe_ref[...].astype(BF16), wpp_ref[...], preferred_element_type=F32)


def _ffn(x, g, wg, wu, wd):
    n, d = x.shape
    f = wg.shape[1]
    tm = _row_tile(n, 512)
    return pl.pallas_call(
        _ffn_body,
        grid=(n // tm,),
        in_specs=[pl.BlockSpec((tm, d), lambda i: (i, 0)),
                  _resident((1, d)), _resident((d, f)), _resident((d, f)), _resident((f, d))],
        out_specs=pl.BlockSpec((tm, d), lambda i: (i, 0)),
        out_shape=jax.ShapeDtypeStruct((n, d), F32),
        compiler_params=_cparams("parallel"),
        name="ffn",
    )(x, g, wg, wu, wd)


def _inproj_body(x_ref, g_ref, wa_ref, wb_ref, wc_ref, wd_ref, pa_ref, pb_ref, pc_ref, pd_ref):
    h = _rms(x_ref[...], g_ref[...]).astype(BF16)
    for w_ref, p_ref in ((wa_ref, pa_ref), (wb_ref, pb_ref), (wc_ref, pc_ref), (wd_ref, pd_ref)):
        p_ref[...] = jnp.dot(h, w_ref[...], preferred_element_type=F32)


def _inproj(x, g, ws):
    n, d = x.shape
    tm = _row_tile(n, 512)
    widths = [w.shape[1] for w in ws]
    return pl.pallas_call(
        _inproj_body,
        grid=(n // tm,),
        in_specs=[pl.BlockSpec((tm, d), lambda i: (i, 0)), _resident((1, d))]
                 + [_resident((d, w)) for w in widths],
        out_specs=[pl.BlockSpec((tm, w), lambda i: (i, 0)) for w in widths],
        out_shape=[jax.ShapeDtypeStruct((n, w), F32) for w in widths],
        compiler_params=_cparams("parallel"),
        name="inproj",
    )(x, g, *ws)


def _merge_body(x_ref, oa_ref, ob_ref, oc_ref, od_ref, g_ref, wmg_ref, wb_ref, wo_ref, o_ref):
    x = x_ref[...]
    h = _rms(x, g_ref[...]).astype(BF16)
    y = None
    for n, b_ref in enumerate((oa_ref, ob_ref, oc_ref, od_ref)):
        gate = _sigmoid(jnp.dot(h, wmg_ref[n], preferred_element_type=F32))
        t = gate * jnp.dot(b_ref[...].astype(BF16), wb_ref[n], preferred_element_type=F32)
        y = t if y is None else y + t
    o_ref[...] = x + jnp.dot(y.astype(BF16), wo_ref[...], preferred_element_type=F32)


def _merge(x, outs, g, wmg, wb, wo):
    n, d = x.shape
    bw = outs[0].shape[1]
    tm = _row_tile(n, 512)
    return pl.pallas_call(
        _merge_body,
        grid=(n // tm,),
        in_specs=[pl.BlockSpec((tm, d), lambda i: (i, 0))]
                 + [pl.BlockSpec((tm, bw), lambda i: (i, 0))] * 4
                 + [_resident((1, d)), _resident(wmg.shape), _resident(wb.shape), _resident(wo.shape)],
        out_specs=pl.BlockSpec((tm, d), lambda i: (i, 0)),
        out_shape=jax.ShapeDtypeStruct((n, d), F32),
        compiler_params=_cparams("parallel"),
        name="merge",
    )(x, *outs, g, wmg, wb, wo)


def _ffn_ple(x, pe_all, layer, g, wg, wu, wd, gp, wpg, wpp):
    n, d = x.shape
    f = wg.shape[1]
    pd = pe_all.shape[2]
    tm = _row_tile(n, 512)
    return pl.pallas_call(
        _ffn_ple_body,
        grid=(n // tm,),
        in_specs=[pl.BlockSpec((tm, d), lambda i: (i, 0)),
                  pl.BlockSpec((None, tm, pd), lambda i: (layer, i, 0)),
                  _resident((1, d)), _resident((d, f)), _resident((d, f)), _resident((f, d)),
                  _resident((1, d)), _resident((d, d)), _resident((pd, d))],
        out_specs=pl.BlockSpec((tm, d), lambda i: (i, 0)),
        out_shape=jax.ShapeDtypeStruct((n, d), F32),
        compiler_params=_cparams("parallel"),
        name="ffn_ple",
    )(x, pe_all, g, wg, wu, wd, gp, wpg, wpp)


def _t5_bucket_np(rel):
    half = N_BUCKETS // 2
    max_exact = half // 2
    ret = np.where(rel > 0, half, 0)
    n = np.abs(rel)
    nf = np.maximum(n, 1).astype(np.float32)
    large = max_exact + (np.log(nf / np.float32(max_exact)) / np.float32(math.log(MAX_DIST / max_exact))
                         * np.float32(half - max_exact)).astype(np.int32)
    large = np.minimum(large, half - 1)
    return (ret + np.where(n < max_exact, n, large)).astype(np.int32)


def _bias_body(idx_ref, table_ref, o_ref):
    idx = idx_ref[...]
    for h in range(A_HEADS):
        acc = jnp.zeros(idx.shape, F32)
        for b in range(N_BUCKETS):
            acc = jnp.where(idx == b, table_ref[b, h], acc)
        o_ref[h] = acc


def _rel_bias(table, n_q, n_k):
    rel = np.arange(n_k)[:, None] - WINDOW - np.arange(n_q)[None, :]
    idx = jnp.asarray(_t5_bucket_np(rel))
    out = pl.pallas_call(
        _bias_body,
        in_specs=[pl.BlockSpec(memory_space=pltpu.VMEM), pl.BlockSpec(memory_space=pltpu.SMEM)],
        out_specs=pl.BlockSpec(memory_space=pltpu.VMEM),
        out_shape=jax.ShapeDtypeStruct((A_HEADS, n_k, n_q), F32),
        name="rel_bias",
    )(idx, table)
    return jnp.transpose(out, (1, 0, 2)).reshape(n_k, A_HEADS * n_q)


def _head_rms(x, e, g):
    ms = _segsum(x * x, e) * (1.0 / A_HD)
    return x * lax.rsqrt(ms + EPS) * g


def _attend_blocks(q_list, k_list, v_list, bias, sink, valid_list, fill=lambda: None):
    def staged(f, *seqs):
        out = []
        for j, xs in enumerate(zip(*seqs)):
            out.append(f(*xs))
            if j % 4 == 3:
                fill()
        return out

    n = q_list[0].shape[0]
    gcols = A_GROUP * n
    inst = [(b, g) for b in range(len(q_list)) for g in range(A_KV_HEADS)]
    head = lambda x, h: x[:, h * A_HD:(h + 1) * A_HD]
    q_st = staged(lambda bg: jnp.concatenate([head(q_list[bg[0]], bg[1] * A_GROUP + r)
                                              for r in range(A_GROUP)], axis=0), inst)
    kg = staged(lambda bg: head(k_list[bg[0]], bg[1]), inst)
    vg = staged(lambda bg: head(v_list[bg[0]], bg[1]), inst)
    bias_g = [bias[:, g * gcols:(g + 1) * gcols] for g in range(A_KV_HEADS)]
    sink_g = [sink[:, g * gcols:(g + 1) * gcols] for g in range(A_KV_HEADS)]

    def scores(bg, k, qs):
        s = _dot_nt(k, qs) + bias_g[bg[1]]
        valid = valid_list[bg[0]]
        return s if valid is None else jnp.where(valid, s, NEG_INF)

    s = staged(scores, inst, kg, q_st)
    m = staged(lambda bg, x: jnp.maximum(jnp.max(x, axis=0, keepdims=True), sink_g[bg[1]]), inst, s)
    e = staged(lambda x, mm: jnp.exp(x - mm), s, m)
    rinv = [1.0 / (jnp.sum(x, axis=0, keepdims=True) + jnp.exp(sink_g[bg[1]] - mm))
            for bg, x, mm in zip(inst, e, m)]
    p = staged(lambda x, r: x * r, e, rinv)
    og = staged(_dot_tn, p, vg)
    outs = []
    for b in range(len(q_list)):
        pieces = [og[b * A_KV_HEADS + g][r * n:(r + 1) * n] for g in range(A_KV_HEADS)
                  for r in range(A_GROUP)]
        outs.append(jnp.concatenate(pieces, axis=1))
    return outs


def _attn_prompt_block(pa, kv_tail_ref, t_idx, bias_ref, sink_ref, qn_ref, kn_ref, eq_ref, ek_ref,
                       tb, fill):
    qw = A_HEADS * A_HD
    kw = A_KV_HEADS * A_HD
    band = WINDOW + CHUNK
    q = _head_rms(pa[:, 0:qw], eq_ref[...], qn_ref[...]) * (A_HD ** -0.5)
    kf = jnp.concatenate([kv_tail_ref[0], _head_rms(pa[:, qw:qw + kw], ek_ref[...], kn_ref[...])], axis=0)
    vf = jnp.concatenate([kv_tail_ref[1], pa[:, qw + kw:qw + 2 * kw]], axis=0)
    kv_tail_ref[0] = kf[tb:tb + WINDOW, :]
    kv_tail_ref[1] = vf[tb:tb + WINDOW, :]
    kb = kf.astype(BF16)
    vb = vf.astype(BF16)
    kidx = lax.broadcasted_iota(jnp.int32, (band, A_GROUP * CHUNK), 0)
    n_chunks = tb // CHUNK
    valid = [(kidx + (t_idx * tb + c * CHUNK - WINDOW)) >= 0 if c * CHUNK < WINDOW else None
             for c in range(n_chunks)]
    return _attend_blocks([q[c * CHUNK:(c + 1) * CHUNK] for c in range(n_chunks)],
                          [kb[c * CHUNK:c * CHUNK + band] for c in range(n_chunks)],
                          [vb[c * CHUNK:c * CHUNK + band] for c in range(n_chunks)],
                          bias_ref[...], sink_ref[...], valid, fill)


def _attn_sample_body(pa_ref, ck_ref, cv_ref, bias_ref, sink_ref, qn_ref, kn_ref, eq_ref, ek_ref,
                      o_ref, kout_ref, vout_ref, *, s, gb):
    qw = A_HEADS * A_HD
    kw = A_KV_HEADS * A_HD
    q = _head_rms(pa_ref[:, 0:qw], eq_ref[...], qn_ref[...]) * (A_HD ** -0.5)
    kn = _head_rms(pa_ref[:, qw:qw + kw], ek_ref[...], kn_ref[...])
    vn = pa_ref[:, qw + kw:qw + 2 * kw]
    q_list, k_list, v_list = [], [], []
    for b in range(gb):
        kf = jnp.concatenate([ck_ref[b], kn[b * s:(b + 1) * s]], axis=0)
        vf = jnp.concatenate([cv_ref[b], vn[b * s:(b + 1) * s]], axis=0)
        kout_ref[b] = kf[s:s + WINDOW, :]
        vout_ref[b] = vf[s:s + WINDOW, :]
        q_list.append(q[b * s:(b + 1) * s])
        k_list.append(kf.astype(BF16))
        v_list.append(vf.astype(BF16))
    outs = _attend_blocks(q_list, k_list, v_list, bias_ref[...], sink_ref[...], [None] * gb)
    for b in range(gb):
        o_ref[b * s:(b + 1) * s, :] = outs[b]


def _attn_sample(pa, ck, cv, bias, sink, qn, kn, nb, s):
    qw, kw = A_HEADS * A_HD, A_KV_HEADS * A_HD
    width = pa.shape[1]
    gb = _row_tile(nb, 8)
    return pl.pallas_call(
        functools.partial(_attn_sample_body, s=s, gb=gb),
        grid=(nb // gb,),
        in_specs=[pl.BlockSpec((gb * s, width), lambda b: (b, 0)),
                  pl.BlockSpec((gb, WINDOW, kw), lambda b: (b, 0, 0)),
                  pl.BlockSpec((gb, WINDOW, kw), lambda b: (b, 0, 0)),
                  _resident(bias.shape), _resident(sink.shape),
                  _resident((1, qw)), _resident((1, kw)), _resident((HGW, HGW)), _resident((kw, kw))],
        out_specs=[pl.BlockSpec((gb * s, qw), lambda b: (b, 0)),
                   pl.BlockSpec((gb, WINDOW, kw), lambda b: (b, 0, 0)),
                   pl.BlockSpec((gb, WINDOW, kw), lambda b: (b, 0, 0))],
        out_shape=[jax.ShapeDtypeStruct((nb * s, qw), F32),
                   jax.ShapeDtypeStruct((nb, WINDOW, kw), F32),
                   jax.ShapeDtypeStruct((nb, WINDOW, kw), F32)],
        compiler_params=_cparams("parallel"),
        name="attn_sample",
    )(pa, ck, cv, bias, sink, qn, kn, _seg_matrix(HGW, A_HD), _seg_matrix(kw, A_HD))


def _rwkv_chunk_terms(r, lw, k, v, kk, a, fill=lambda: None):
    stage_count = [0]

    def each(f, *seqs):
        out = [f(*xs) for xs in zip(*seqs)]
        stage_count[0] += 1
        if stage_count[0] % 3 == 0:
            fill()
        return out

    cat0 = lambda *xs: jnp.concatenate(xs, axis=0)
    cat1 = lambda *xs: jnp.concatenate(xs, axis=1)
    bd = lambda x: _bd_rows(x, B_HD, HG)
    L = r[0].shape[0]
    tril = _tril_ones(L)
    c = each(lambda x: _split_dot(tril, x, 3), lw)
    c_last = each(lambda x: x[L - 1:L, :], c)
    e_last = each(lambda x, xl: jnp.exp(xl - x), c, c_last)
    beta = each(lambda x, y: x * y, kk, a)
    at = each(lambda x, cc, l: -x * jnp.exp(cc - l), kk, c, lw)
    rt = each(lambda x, cc: x * jnp.exp(cc), r, c)
    enc = each(lambda cc: jnp.exp(-cc), c)
    ar = each(cat0, at, rt)
    mb = each(lambda x, b, e: _dot_nt(x, bd(b * e)), ar, beta, enc)
    mk = each(lambda x, b, e: _dot_nt(x, bd(b * e)), ar, k, enc)
    row = lax.broadcasted_iota(jnp.int32, (L, HG * L), 0)
    col = lax.broadcasted_iota(jnp.int32, (L, HG * L), 1) % L
    strict = col < row
    incl = col <= row
    m_b = each(lambda x: jnp.where(strict, x[:L], 0.0), mb)
    m_k = each(lambda x: jnp.where(strict, x[:L], 0.0), mk)
    n_b = each(lambda x: jnp.where(incl, x[L:], 0.0), mb)
    n_k = each(lambda x: jnp.where(incl, x[L:], 0.0), mk)
    eye = jnp.where(col == row, 1.0, 0.0)
    t_inv = each(lambda x: eye + x, m_b)
    p = m_b
    for lvl in range(1, int(math.log2(L))):
        p_bd = each(lambda x: _bd_rows(x, L, HG), p)
        if lvl == 1:
            p = each(_dot, p, p_bd)
        else:
            tp = each(lambda t, x, xb: _dot(cat0(t, x), xb), t_inv, p, p_bd)
            t_inv = each(lambda t, x: t + x[:L], t_inv, tp)
            p = each(lambda x: x[L:], tp)
    t_inv = each(lambda t, x: t + _dot(t, _bd_rows(x, L, HG)), t_inv, p)
    v_bd = each(bd, v)
    mnv = each(lambda x, y, vb: _dot(cat0(x, y), vb), m_k, n_k, v_bd)
    mkv = each(lambda x: x[:L], mnv)
    wu = each(lambda t, x, y: _dot(t, cat1(bd(x), bd(y))), t_inv, at, mkv)
    w1 = each(lambda x: x[:, :HGW], wu)
    u0 = each(lambda x: x[:, HGW:], wu)
    nbo = each(lambda n, x, y: _dot(n, cat1(bd(x), bd(y))), n_b, w1, u0)
    q1 = each(lambda x, y: x + y[:, :HGW], rt, nbo)
    y0 = each(lambda x, y: x[:, HGW:] + y[L:], nbo, mnv)
    bh = each(lambda x, y: x * y, beta, e_last)
    kh = each(lambda x, y: x * y, k, e_last)
    ri = lax.broadcasted_iota(jnp.int32, (HGW, HGW), 0) // B_HD
    ci = lax.broadcasted_iota(jnp.int32, (HGW, HGW), 1) // B_HD
    diag = ri == ci
    g_mat = each(lambda x, y: jnp.where(diag, _dot_tn(x, y), 0.0), w1, bh)
    h_mat = each(lambda x, y, z, w: jnp.where(diag, _dot_tn(cat0(x, y), cat0(z, w)), 0.0), u0, v, bh, kh)
    gam = each(jnp.exp, c_last)
    return q1, y0, g_mat, h_mat, gam


def _rwkv_body(pb_ref, shift_ref, s0_ref, mu_ref, w0_ref, w2_ref, a0_ref, a2_ref, g2_ref, kk_ref,
               ka_ref, rk_ref, lng_ref, lnb_ref, e_ref,
               o_ref, sout_ref,
               ext_ref, s_ref, *, tb, sub, lc):
    t = pl.program_id(1)
    W = BRANCH_W
    n_groups = W // HGW

    @pl.when(t == 0)
    def _():
        ext_ref[0:SUBLANES, :] = jnp.broadcast_to(shift_ref[0], (SUBLANES, ext_ref.shape[1]))
        for gi in range(n_groups):
            s_ref[gi] = _block_diag_value([s0_ref[0, gi * HG + h] for h in range(HG)])

    ext_ref[SUBLANES:, :] = pb_ref[...]

    e = e_ref[...]
    n_sub = tb // sub

    rows_c = min(sub, lc)
    n_chunks = sub // rows_c

    def prep_stages(lo):
        f = {}

        def shift():
            p = pb_ref[lo:lo + sub, :]
            prev = ext_ref[SUBLANES - 1 + lo:SUBLANES - 1 + lo + sub, :]
            f['x'] = p + (prev - p) * mu_ref[...]
            f['r'], f['v'] = f['x'][:, 0:W], f['x'][:, 2 * W:3 * W]

        def decay():
            wal = f['x'][:, 3 * W:3 * W + LANES]
            w_log = -_softplus_big(-(w0_ref[...] + _dot(jnp.tanh(wal), w2_ref[...]))) - 0.5
            f['lw'] = -jnp.exp(w_log)

        def gates():
            wal = f['x'][:, 3 * W:3 * W + LANES]
            gl = f['x'][:, 3 * W + LANES:3 * W + 2 * LANES]
            f['a'] = _sigmoid(a0_ref[...] + _dot(wal, a2_ref[...]))
            f['g'] = _dot(_sigmoid(gl), g2_ref[...])

        def keys():
            k = f['x'][:, W:2 * W]
            kkr = k * kk_ref[...]
            f['kk'] = kkr / jnp.maximum(jnp.sqrt(_segsum(kkr * kkr, e)), 1e-12)
            f['k2'] = k * (1.0 + (f['a'] - 1.0) * ka_ref[...])

        return f, [shift, decay, gates, keys]

    def post_stages(lo, f, y):
        t = {}

        def center():
            t['yc'] = y - _segsum(y, e) * (1.0 / B_HD)

        def norm():
            var = _segsum(t['yc'] * t['yc'], e) * (1.0 / B_HD)
            t['yn'] = t['yc'] * lax.rsqrt(var + RWKV_GN_EPS) * lng_ref[...] + lnb_ref[...]

        def out():
            bonus = _segsum(f['r'] * f['k2'] * rk_ref[...], e) * f['v']
            o_ref[lo:lo + sub, :] = (t['yn'] + bonus) * f['g']

        return [center, norm, out]

    def piece(x, ci, gi):
        blk = x[ci * rows_c:(ci + 1) * rows_c, gi * HGW:(gi + 1) * HGW]
        if rows_c < lc:
            blk = jnp.concatenate([blk, jnp.zeros((lc - rows_c, HGW), F32)], axis=0)
        return blk

    feats = [prep_stages(j * sub) for j in range(n_sub)]
    for stage in feats[0][1]:
        stage()
    pending = []

    def fill():
        if pending:
            pending.pop(0)()

    inst = [(ci, gi) for ci in range(n_chunks) for gi in range(n_groups)]
    for j in range(n_sub):
        f = feats[j][0]
        if j + 1 < n_sub:
            pending.extend(feats[j + 1][1])
        terms = _rwkv_chunk_terms(*[[piece(f[name], ci, gi) for ci, gi in inst]
                                    for name in ('r', 'lw', 'k2', 'v', 'kk', 'a')], fill)
        while pending:
            fill()
        y_rows = []
        for ci in range(n_chunks):
            y_cols = []
            for gi in range(n_groups):
                q1, y0, g_mat, h_mat, gam = [tm[ci * n_groups + gi] for tm in terms]
                s = s_ref[gi]
                y_cols.append(_dot_nt(q1, s) + y0)
                s_ref[gi] = s * gam + _dot(s, g_mat) + h_mat
            y_rows.append(jnp.concatenate(y_cols, axis=1)[0:rows_c])
        pending.extend(post_stages(j * sub, f, jnp.concatenate(y_rows, axis=0)))
    while pending:
        fill()
    ext_ref[0:SUBLANES, :] = pb_ref[tb - SUBLANES:tb, :]

    @pl.when(t == pl.num_programs(1) - 1)
    def _():
        for gi in range(n_groups):
            s = s_ref[gi]
            for h in range(HG):
                sout_ref[0, gi * HG + h] = s[h * B_HD:(h + 1) * B_HD, h * B_HD:(h + 1) * B_HD]


def _rwkv(pb, shift0, s0, wts, nb, t):
    tb = _row_tile(t, 1024)
    sub = min(tb, 256)
    lc = CHUNK
    assert sub % lc == 0 or sub < lc
    nt = t // tb
    width = pb.shape[1]
    W = BRANCH_W
    n_groups = W // HGW
    row = lambda n: _resident((1, n))
    return pl.pallas_call(
        functools.partial(_rwkv_body, tb=tb, sub=sub, lc=lc),
        grid=(nb, nt),
        in_specs=[pl.BlockSpec((tb, width), lambda b, i: (b * nt + i, 0)),
                  pl.BlockSpec((1, 1, width), lambda b, i: (b, 0, 0)),
                  pl.BlockSpec((1, B_HEADS, B_HD, B_HD), lambda b, i: (b, 0, 0, 0)),
                  row(width), row(W), _resident((128, W)), row(W), _resident((128, W)),
                  _resident((128, W)), row(W), row(W), row(W), row(W), row(W), _resident((HGW, HGW))],
        out_specs=[pl.BlockSpec((tb, W), lambda b, i: (b * nt + i, 0)),
                   pl.BlockSpec((1, B_HEADS, B_HD, B_HD), lambda b, i: (b, 0, 0, 0))],
        out_shape=[jax.ShapeDtypeStruct((nb * t, W), F32),
                   jax.ShapeDtypeStruct((nb, B_HEADS, B_HD, B_HD), F32)],
        scratch_shapes=[pltpu.VMEM((tb + SUBLANES, width), F32),
                        pltpu.VMEM((n_groups, HGW, HGW), F32)],
        compiler_params=_cparams("parallel", "arbitrary"),
        name="rwkv7",
    )(pb, shift0, s0, *wts, _seg_matrix(HGW, B_HD))


def _gla_chunk_terms(q, k, v, gk):
    each = lambda f, *seqs: [f(*xs) for xs in zip(*seqs)]
    L = q[0].shape[0]
    kw = C_HEADS * C_DK
    vw = C_HEADS * C_DV
    tril = _tril_ones(L)
    b = each(lambda x: _split_dot(tril, x, 3), gk)
    qe = each(lambda x, y: x * jnp.exp(y), q, b)
    ke = each(lambda x, y: x * jnp.exp(-y), k, b)
    a_all = each(lambda x, y: _dot_nt(x, _bd_rows(y, C_DK, C_HEADS)), qe, ke)
    row = lax.broadcasted_iota(jnp.int32, (L, C_HEADS * L), 0)
    col = lax.broadcasted_iota(jnp.int32, (L, C_HEADS * L), 1) % L
    causal = col <= row
    o_intra = each(lambda x, y: _dot(jnp.where(causal, x, 0.0), _bd_rows(y, C_DV, C_HEADS)), a_all, v)
    b_last = each(lambda x: x[L - 1:L, :], b)
    kd = each(lambda x, y, z: x * jnp.exp(z - y), k, b, b_last)
    ri = lax.broadcasted_iota(jnp.int32, (vw, kw), 0) // C_DV
    ci = lax.broadcasted_iota(jnp.int32, (vw, kw), 1) // C_DK
    diag = ri == ci
    upd = each(lambda x, y: jnp.where(diag, _dot_tn(x, y), 0.0), v, kd)
    decay = each(jnp.exp, b_last)
    return qe, o_intra, upd, decay


def _gla_body(pc_ref, s0_ref, g2_ref, gb_ref, norm_ref, o_ref, sout_ref,
              s_ref, *, tb, lc):
    t = pl.program_id(1)
    kw = C_HEADS * C_DK
    vw = C_HEADS * C_DV

    @pl.when(t == 0)
    def _():
        s_ref[...] = _block_diag_value([_transpose_exact(s0_ref[0, h]) for h in range(C_HEADS)])

    pc = pc_ref[...]
    og = pc[:, 2 * kw + vw:2 * kw + 2 * vw]
    gl = pc[:, 2 * kw + 2 * vw:2 * kw + 2 * vw + 128]
    z = _dot(gl, g2_ref[...]) + gb_ref[...]
    q = pc[:, 0:kw] * (C_DK ** -0.5)
    k = pc[:, kw:2 * kw]
    v = pc[:, 2 * kw:2 * kw + vw]
    gk = -_softplus_big(-z) * (1.0 / C_TAU)
    n_chunks = max(tb // lc, 1)

    def piece(x, ci):
        blk = x[ci * lc:min((ci + 1) * lc, tb), :]
        if tb < lc:
            blk = jnp.concatenate([blk, jnp.zeros((lc - tb, x.shape[1]), F32)], axis=0)
        return blk

    qe, o_intra, upd, decay = _gla_chunk_terms(
        *[[piece(x, ci) for ci in range(n_chunks)] for x in (q, k, v, gk)])
    y_rows = []
    for ci in range(n_chunks):
        st = s_ref[...]
        y_rows.append(_dot_nt(qe[ci], st) + o_intra[ci])
        s_ref[...] = st * decay[ci] + upd[ci]
    y = jnp.concatenate(y_rows, axis=0)[0:tb]
    outs = []
    for h in range(C_HEADS):
        yh = y[:, h * C_DV:(h + 1) * C_DV]
        outs.append(yh * lax.rsqrt(jnp.mean(yh * yh, axis=-1, keepdims=True) + EPS) * norm_ref[...])
    yn = jnp.concatenate(outs, axis=1)
    o_ref[...] = yn * (og * _sigmoid(og))

    @pl.when(t == pl.num_programs(1) - 1)
    def _():
        st = s_ref[...]
        for h in range(C_HEADS):
            sout_ref[0, h] = _transpose_exact(st[h * C_DV:(h + 1) * C_DV, h * C_DK:(h + 1) * C_DK])


def _gla(pc, s0, g2, gb, norm, nb, t):
    tb = _row_tile(t, 1024)
    lc = CHUNK
    assert tb % lc == 0 or tb < lc
    nt = t // tb
    width = pc.shape[1]
    kw, vw = C_HEADS * C_DK, C_HEADS * C_DV
    return pl.pallas_call(
        functools.partial(_gla_body, tb=tb, lc=lc),
        grid=(nb, nt),
        in_specs=[pl.BlockSpec((tb, width), lambda b, i: (b * nt + i, 0)),
                  pl.BlockSpec((1, C_HEADS, C_DK, C_DV), lambda b, i: (b, 0, 0, 0)),
                  _resident((128, kw)), _resident((1, kw)), _resident((1, C_DV))],
        out_specs=[pl.BlockSpec((tb, vw), lambda b, i: (b * nt + i, 0)),
                   pl.BlockSpec((1, C_HEADS, C_DK, C_DV), lambda b, i: (b, 0, 0, 0))],
        out_shape=[jax.ShapeDtypeStruct((nb * t, vw), F32),
                   jax.ShapeDtypeStruct((nb, C_HEADS, C_DK, C_DV), F32)],
        scratch_shapes=[pltpu.VMEM((vw, kw), F32)],
        compiler_params=_cparams("parallel", "arbitrary"),
        name="gla",
    )(pc, s0, g2, gb, norm)


def _shift_rows(x, d, fill):
    n = x.shape[0]
    if d % SUBLANES == 0:
        head = jnp.full((d, x.shape[1]), fill, x.dtype)
        return jnp.concatenate([head, x[:n - d]], axis=0)
    rolled = pltpu.roll(x, d, 0)
    row = lax.broadcasted_iota(jnp.int32, x.shape, 0)
    return jnp.where(row < d, fill, rolled)


def _lru_block(gate, xr, ext_ref, h_ref, au_ref, cw_ref, cb_ref, wa_ref, ba_ref, wx_ref, bx_ref,
               lam_ref, tb, fill=lambda: None):
    W = BRANCH_W
    ext_ref[SUBLANES:, :] = xr
    xc = cb_ref[...] + xr * cw_ref[CONV_W - 1:CONV_W, :]
    for j in range(CONV_W - 1):
        off = SUBLANES - (CONV_W - 1) + j
        xc = xc + ext_ref[off:off + tb, :] * cw_ref[j:j + 1, :]
    tail = ext_ref[tb:tb + SUBLANES, :]
    ext_ref[0:SUBLANES, :] = tail
    fill()

    def block_dot(w_ref):
        return jnp.concatenate([_dot(xc[:, j:j + MXU_COLS], w_ref[j:j + MXU_COLS, j:j + MXU_COLS])
                                for j in range(0, W, MXU_COLS)], axis=1)

    r = _sigmoid(block_dot(wa_ref) + ba_ref[...])
    fill()
    gi = _sigmoid(block_dot(wx_ref) + bx_ref[...])
    fill()
    log_a = (-LRU_C) * r * _softplus(-lam_ref[...])
    a = jnp.exp(log_a)
    u = jnp.sqrt(1.0 - jnp.exp(2.0 * log_a)) * (gi * xc)
    fill()
    ng = tb // SUBLANES
    row_in_group = lax.broadcasted_iota(jnp.int32, (ng, SUBLANES, W), 1)
    a = a.reshape(ng, SUBLANES, W)
    u = u.reshape(ng, SUBLANES, W)
    d = 1
    while d < SUBLANES:
        inside = row_in_group >= d
        u = u + a * jnp.where(inside, pltpu.roll(u, d, 1), 0.0)
        a = a * jnp.where(inside, pltpu.roll(a, d, 1), 1.0)
        d *= 2
        fill()
    a = a.reshape(tb, W)
    u = u.reshape(tb, W)
    h_in = h_ref[...]
    if ng % SUBLANES == 0:
        ends = pl.ds(SUBLANES - 1, ng, stride=SUBLANES)
        for j in range(W // LANES):
            au_ref[0, j] = a[:, j * LANES:(j + 1) * LANES]
            au_ref[1, j] = u[:, j * LANES:(j + 1) * LANES]
        ae = jnp.concatenate([au_ref[0, j, ends, :] for j in range(W // LANES)], axis=1)
        ue = jnp.concatenate([au_ref[1, j, ends, :] for j in range(W // LANES)], axis=1)
        d = 1
        while d < ng:
            ue = ue + ae * _shift_rows(ue, d, 0.0)
            ae = ae * _shift_rows(ae, d, 1.0)
            d *= 2
        carry = _shift_rows(ue + ae * h_in, 1, h_in)
        carry_rows = jnp.broadcast_to(carry[:, None, :], (ng, SUBLANES, W)).reshape(tb, W)
    else:
        rows = []
        for g in range(ng):
            rows.append(jnp.broadcast_to(h_in, (SUBLANES, W)))
            end = (g + 1) * SUBLANES - 1
            h_in = u[end:end + 1, :] + a[end:end + 1, :] * h_in
        carry_rows = jnp.concatenate(rows, axis=0)
    h = u + a * carry_rows
    h_last = h[tb - 1:tb, :]
    h_ref[...] = h_last
    fill()
    out = h * (0.5 * gate * (1.0 + jnp.tanh(math.sqrt(2.0 / math.pi)
                                            * (gate + 0.044715 * gate * gate * gate))))
    return out, tail, h_last


def _lru_body(pd_ref, conv0_ref, h0_ref, cw_ref, cb_ref, wa_ref, ba_ref, wx_ref, bx_ref, lam_ref,
              o_ref, convout_ref, hout_ref, ext_ref, h_ref, au_ref, *, tb):
    t = pl.program_id(1)
    W = BRANCH_W

    @pl.when(t == 0)
    def _():
        ext_ref[0:SUBLANES, :] = conv0_ref[0]
        h_ref[...] = h0_ref[0]

    out, tail, h_last = _lru_block(pd_ref[:, 0:W], pd_ref[:, W:2 * W], ext_ref, h_ref, au_ref, cw_ref,
                                   cb_ref, wa_ref, ba_ref, wx_ref, bx_ref, lam_ref, tb)
    o_ref[...] = out

    @pl.when(t == pl.num_programs(1) - 1)
    def _():
        convout_ref[0] = tail
        hout_ref[0] = h_last


def _mix_in_body(x_ref, g_ref, wa_ref, wb_ref, wc_ref, wd_ref,
                 bias_ref, sink_ref, qn_ref, kn_ref, eq_ref, ek_ref,
                 conv0_ref, h0_ref, cw_ref, cb_ref, lwa_ref, lba_ref, lwx_ref, lbx_ref, lam_ref,
                 pb_ref, pc_ref, oa_ref, od_ref, kout_ref, vout_ref, convout_ref, hout_ref,
                 kv_tail_ref, ext_ref, h_ref, au_ref, *, tb):
    t = pl.program_id(1)
    W = BRANCH_W

    @pl.when(t == 0)
    def _():
        kv_tail_ref[...] = jnp.zeros(kv_tail_ref.shape, F32)
        ext_ref[0:SUBLANES, :] = conv0_ref[0]
        h_ref[...] = h0_ref[0]

    h = _rms(x_ref[...], g_ref[...]).astype(BF16)
    pd = jnp.dot(h, wd_ref[...], preferred_element_type=F32)
    pa = jnp.dot(h, wa_ref[...], preferred_element_type=F32)

    half = tb // 2
    pending = [(w_ref, p_ref, c0, min(MXU_COLS, w_ref.shape[1] - c0), r0)
               for w_ref, p_ref in ((wb_ref, pb_ref), (wc_ref, pc_ref))
               for c0 in range(0, w_ref.shape[1], MXU_COLS) for r0 in (0, half)]

    def fill():
        if pending:
            w_ref, p_ref, c0, width, r0 = pending.pop(0)
            p_ref[r0:r0 + half, c0:c0 + width] = jnp.dot(h[r0:r0 + half], w_ref[:, c0:c0 + width],
                                                         preferred_element_type=F32)

    od, tail, h_last = _lru_block(pd[:, 0:W], pd[:, W:2 * W], ext_ref, h_ref, au_ref, cw_ref, cb_ref,
                                  lwa_ref, lba_ref, lwx_ref, lbx_ref, lam_ref, tb, fill)
    od_ref[...] = od
    outs = _attn_prompt_block(pa, kv_tail_ref, t, bias_ref, sink_ref, qn_ref, kn_ref, eq_ref, ek_ref,
                              tb, fill)
    for c, o in enumerate(outs):
        oa_ref[c * CHUNK:(c + 1) * CHUNK, :] = o
    while pending:
        fill()

    @pl.when(t == pl.num_programs(1) - 1)
    def _():
        kout_ref[0] = kv_tail_ref[0]
        vout_ref[0] = kv_tail_ref[1]
        convout_ref[0] = tail
        hout_ref[0] = h_last


def _mix_in(x, g, ws, bias, sink, qn, kn, conv0, h0, lru_wts, nb, t):
    n, d = x.shape
    tb = _row_tile(t, 512)
    assert tb % WINDOW == 0
    nt = t // tb
    W = BRANCH_W
    qw, kw = A_HEADS * A_HD, A_KV_HEADS * A_HD
    wa, wb, wc, wd = ws
    rows = lambda width: pl.BlockSpec((tb, width), lambda b, i: (b * nt + i, 0))
    per_seq = lambda *shape: pl.BlockSpec((1,) + shape, lambda b, i: (b,) + (0,) * len(shape))
    row = lambda width: _resident((1, width))
    return pl.pallas_call(
        functools.partial(_mix_in_body, tb=tb),
        grid=(nb, nt),
        in_specs=[rows(d), row(d)] + [_resident(w.shape) for w in ws]
                 + [_resident(bias.shape), _resident(sink.shape), row(qw), row(kw),
                    _resident((HGW, HGW)), _resident((kw, kw)),
                    per_seq(SUBLANES, W), per_seq(1, W),
                    _resident((CONV_W, W)), row(W), _resident((W, W)), row(W), _resident((W, W)), row(W),
                    row(W)],
        out_specs=[rows(wb.shape[1]), rows(wc.shape[1]), rows(qw), rows(W),
                   per_seq(WINDOW, kw), per_seq(WINDOW, kw), per_seq(SUBLANES, W), per_seq(1, W)],
        out_shape=[jax.ShapeDtypeStruct((n, wb.shape[1]), F32), jax.ShapeDtypeStruct((n, wc.shape[1]), F32),
                   jax.ShapeDtypeStruct((n, qw), F32), jax.ShapeDtypeStruct((n, W), F32),
                   jax.ShapeDtypeStruct((nb, WINDOW, kw), F32), jax.ShapeDtypeStruct((nb, WINDOW, kw), F32),
                   jax.ShapeDtypeStruct((nb, SUBLANES, W), F32), jax.ShapeDtypeStruct((nb, 1, W), F32)],
        scratch_shapes=[pltpu.VMEM((2, WINDOW, kw), F32),
                        pltpu.VMEM((tb + SUBLANES, W), F32), pltpu.VMEM((1, W), F32),
                        pltpu.VMEM((2, W // LANES, tb, LANES), F32)],
        compiler_params=_cparams("parallel", "arbitrary"),
        name="mix_in",
    )(x, g, wa, wb, wc, wd, bias, sink, qn, kn, _seg_matrix(HGW, A_HD), _seg_matrix(kw, A_HD),
      conv0, h0, *lru_wts)


def _lru(pd, conv0, h0, wts, nb, t):
    tb = _row_tile(t, 256)
    nt = t // tb
    W = BRANCH_W
    row = lambda: _resident((1, W))
    return pl.pallas_call(
        functools.partial(_lru_body, tb=tb),
        grid=(nb, nt),
        in_specs=[pl.BlockSpec((tb, 2 * W), lambda b, i: (b * nt + i, 0)),
                  pl.BlockSpec((1, SUBLANES, W), lambda b, i: (b, 0, 0)),
                  pl.BlockSpec((1, 1, W), lambda b, i: (b, 0, 0)),
                  _resident((CONV_W, W)), row(), _resident((W, W)), row(), _resident((W, W)), row(),
                  row()],
        out_specs=[pl.BlockSpec((tb, W), lambda b, i: (b * nt + i, 0)),
                   pl.BlockSpec((1, SUBLANES, W), lambda b, i: (b, 0, 0)),
                   pl.BlockSpec((1, 1, W), lambda b, i: (b, 0, 0))],
        out_shape=[jax.ShapeDtypeStruct((nb * t, W), F32),
                   jax.ShapeDtypeStruct((nb, SUBLANES, W), F32),
                   jax.ShapeDtypeStruct((nb, 1, W), F32)],
        scratch_shapes=[pltpu.VMEM((tb + SUBLANES, W), F32), pltpu.VMEM((1, W), F32),
                        pltpu.VMEM((2, W // LANES, tb, LANES), F32)],
        compiler_params=_cparams("parallel", "arbitrary"),
        name="rglru",
    )(pd, conv0, h0, *wts)


def _block_diag(w):
    n, a, b = w.shape
    eye = jnp.eye(n, dtype=w.dtype)
    return (eye[:, None, :, None] * w[:, :, None, :]).reshape(n * a, n * b)


def _pad_rows_to(w, rows):
    return jnp.pad(w, ((0, rows - w.shape[0]), (0, 0)))


def _layer_weights(i, W):
    bf = lambda a: a.astype(BF16)
    row = lambda a: a.reshape(1, -1).astype(F32)
    bw = BRANCH_W
    a_cols = A_HEADS * A_HD + 2 * A_KV_HEADS * A_HD
    b_cols = 3 * bw + B_W_RANK + B_A_RANK + B_G_RANK
    c_cols = 2 * C_HEADS * C_DK + C_HEADS * C_DV + C_G_RANK + bw
    w_in = W['w_in'][i]
    d = w_in.shape[0]
    wa = w_in[:, :a_cols]
    wb = w_in[:, a_cols:a_cols + b_cols]
    wc = w_in[:, a_cols + b_cols:a_cols + b_cols + c_cols]
    wd = w_in[:, a_cols + b_cols + c_cols:]
    zpad = lambda n: jnp.zeros((d, n), w_in.dtype)
    assert B_W_RANK + B_A_RANK == LANES
    w2 = jnp.pad(W['rwkv_w2'][i], ((0, B_A_RANK), (0, 0)))
    a2 = jnp.pad(W['rwkv_a2'][i], ((B_W_RANK, 0), (0, 0)))
    qkv = 2 * C_HEADS * C_DK + C_HEADS * C_DV
    wc2 = jnp.concatenate([wc[:, :qkv], wc[:, qkv + C_G_RANK:], wc[:, qkv:qkv + C_G_RANK],
                           zpad(128 - C_G_RANK)], axis=1)
    return dict(
        ffn1=(row(W['g_ffn1'][i]), bf(W['w_ffn1_gate'][i]), bf(W['w_ffn1_up'][i]), bf(W['w_ffn1_down'][i])),
        ffn2=(row(W['g_ffn2'][i]), bf(W['w_ffn2_gate'][i]), bf(W['w_ffn2_up'][i]), bf(W['w_ffn2_down'][i])),
        g_mix=row(W['g_mix'][i]),
        w_in=(bf(wa), bf(wb), bf(wc2), bf(wd)),
        q_norm=row(jnp.tile(W['q_norm'][i], A_HEADS)),
        k_norm=row(jnp.tile(W['k_norm'][i], A_KV_HEADS)),
        sink=W['attn_sink'][i].astype(F32),
        rwkv=(row(W['rwkv_mu'][i]), row(W['rwkv_w0'][i]), bf(w2),
              row(W['rwkv_a0'][i]), bf(a2), bf(W['rwkv_g2'][i]),
              row(W['rwkv_k_k'][i]), row(W['rwkv_k_a'][i]), row(W['rwkv_r_k'][i]),
              row(W['rwkv_ln_g'][i]), row(W['rwkv_ln_b'][i])),
        gla=(bf(_pad_rows_to(W['gla_g2'][i], 128)), row(W['gla_gb'][i]), row(W['gla_norm'][i])),
        lru=(W['lru_conv_w'][i].astype(F32), row(W['lru_conv_b'][i]),
             bf(_block_diag(W['lru_wa'][i])), row(W['lru_ba'][i]),
             bf(_block_diag(W['lru_wx'][i])), row(W['lru_bx'][i]), row(W['lru_lambda'][i])),
        merge=(bf(W['w_merge_gate'][i]), bf(W['w_branch'][i]), bf(W['w_out'][i])),
        ple=(row(W['g_ple'][i]), bf(W['w_ple_gate'][i]), bf(W['w_ple_proj'][i])),
    )


def _trunk_layer(x, pe_all, layer, st, lw, bias, first_chunk):
    nb, t, d = x.shape
    n = nb * t
    ck, cv, shift0, s_rwkv0, s_gla0, conv0, lru0 = st
    x2 = x.reshape(n, d)
    x2 = _ffn(x2, *lw['ffn1'])
    kw = A_KV_HEADS * A_HD
    n_q = bias.shape[1] // A_HEADS
    sink2 = jnp.repeat(lw['sink'], n_q).reshape(1, A_HEADS * n_q)
    conv_pad = jnp.pad(conv0, ((0, 0), (SUBLANES - (CONV_W - 1), 0), (0, 0)))
    if first_chunk:
        pb, pc, o_a, o_d, k_win, v_win, conv_out, h_out = _mix_in(
            x2, lw['g_mix'], lw['w_in'], bias, sink2, lw['q_norm'], lw['k_norm'],
            conv_pad, lru0[:, None, :], lw['lru'], nb, t)
    else:
        pa, pb, pc, pd = _inproj(x2, lw['g_mix'], lw['w_in'])
        o_a, k_win, v_win = _attn_sample(pa, ck.reshape(nb, WINDOW, kw), cv.reshape(nb, WINDOW, kw),
                                         bias, sink2, lw['q_norm'], lw['k_norm'], nb, t)
        o_d, conv_out, h_out = _lru(pd, conv_pad, lru0[:, None, :], lw['lru'], nb, t)
    k_win = k_win.reshape(nb, WINDOW, A_KV_HEADS, A_HD)
    v_win = v_win.reshape(nb, WINDOW, A_KV_HEADS, A_HD)

    o_b, s_rwkv1 = _rwkv(pb, shift0[:, None, :], s_rwkv0, lw['rwkv'], nb, t)
    shift1 = pb.reshape(nb, t, -1)[:, -1]
    o_c, s_gla1 = _gla(pc, s_gla0, *lw['gla'], nb, t)
    conv1 = conv_out[:, SUBLANES - (CONV_W - 1):]
    lru1 = h_out[:, 0]

    x2 = _merge(x2, (o_a, o_b, o_c, o_d), lw['g_mix'], *lw['merge'])
    x2 = _ffn_ple(x2, pe_all.reshape(pe_all.shape[0], n, -1), layer, *lw['ffn2'], *lw['ple'])
    return x2.reshape(nb, t, d), (k_win, v_win, shift1, s_rwkv1, s_gla1, conv1, lru1)


def kernel(x_prompt, x_sample, cache_attn_k, cache_attn_v, state_rwkv_shift, state_rwkv, state_gla, state_lru_conv, state_lru, p_prompt, p_sample, rel_bias_table, g_ffn1, w_ffn1_gate, w_ffn1_up, w_ffn1_down, g_mix, w_in, q_norm, k_norm, attn_sink, rwkv_mu, rwkv_w0, rwkv_w2, rwkv_a0, rwkv_a2, rwkv_g2, rwkv_k_k, rwkv_k_a, rwkv_r_k, rwkv_ln_g, rwkv_ln_b, gla_g2, gla_gb, gla_norm, lru_conv_w, lru_conv_b, lru_wa, lru_ba, lru_wx, lru_bx, lru_lambda, w_merge_gate, w_branch, w_out, g_ffn2, w_ffn2_gate, w_ffn2_up, w_ffn2_down, g_ple, w_ple_gate, w_ple_proj):
    W = dict(g_ffn1=g_ffn1, w_ffn1_gate=w_ffn1_gate, w_ffn1_up=w_ffn1_up, w_ffn1_down=w_ffn1_down,
             g_mix=g_mix, w_in=w_in, q_norm=q_norm, k_norm=k_norm, attn_sink=attn_sink,
             rwkv_mu=rwkv_mu, rwkv_w0=rwkv_w0, rwkv_w2=rwkv_w2, rwkv_a0=rwkv_a0, rwkv_a2=rwkv_a2,
             rwkv_g2=rwkv_g2, rwkv_k_k=rwkv_k_k, rwkv_k_a=rwkv_k_a, rwkv_r_k=rwkv_r_k,
             rwkv_ln_g=rwkv_ln_g, rwkv_ln_b=rwkv_ln_b, gla_g2=gla_g2, gla_gb=gla_gb,
             gla_norm=gla_norm, lru_conv_w=lru_conv_w, lru_conv_b=lru_conv_b, lru_wa=lru_wa,
             lru_ba=lru_ba, lru_wx=lru_wx, lru_bx=lru_bx, lru_lambda=lru_lambda,
             w_merge_gate=w_merge_gate, w_branch=w_branch, w_out=w_out, g_ffn2=g_ffn2,
             w_ffn2_gate=w_ffn2_gate, w_ffn2_up=w_ffn2_up, w_ffn2_down=w_ffn2_down,
             g_ple=g_ple, w_ple_gate=w_ple_gate, w_ple_proj=w_ple_proj)
    depth = w_in.shape[0]
    dt = x_prompt.dtype
    bp, tp = x_prompt.shape[:2]
    ts = x_sample.shape[1]
    b_cols = state_rwkv_shift.shape[-1]
    bias_p = _rel_bias(rel_bias_table, CHUNK, WINDOW + CHUNK)
    bias_s = _rel_bias(rel_bias_table, ts, WINDOW + ts)
    yp, ys = x_prompt, x_sample
    st_p, st_s = [], []
    for i in range(depth):
        lw = _layer_weights(i, W)
        zero_st = (None, None,
                   jnp.zeros((bp, b_cols), dt),
                   jnp.zeros((bp, B_HEADS, B_HD, B_HD), dt),
                   jnp.zeros((bp, C_HEADS, C_DK, C_DV), dt),
                   jnp.zeros((bp, CONV_W - 1, BRANCH_W), dt),
                   jnp.zeros((bp, BRANCH_W), dt))
        yp, sp = _trunk_layer(yp, p_prompt, i, zero_st, lw, bias_p, True)
        cache_st = (cache_attn_k[i], cache_attn_v[i], state_rwkv_shift[i], state_rwkv[i],
                    state_gla[i], state_lru_conv[i], state_lru[i])
        ys, ss = _trunk_layer(ys, p_sample, i, cache_st, lw, bias_s, False)
        st_p.append(sp)
        st_s.append(ss)
    stack = lambda states, j: jnp.stack([s[j] for s in states])
    return (yp, ys) + tuple(stack(st_p, j) for j in range(7)) + tuple(stack(st_s, j) for j in range(7))
```

```python
import functools
import math

import numpy as np
import jax
import jax.numpy as jnp
from jax import lax
from jax.experimental import pallas as pl
from jax.experimental.pallas import tpu as pltpu

F32 = jnp.float32
BF16 = jnp.bfloat16

V7X_VMEM_BYTES = 64 * 1024 * 1024
VMEM_LIMIT = V7X_VMEM_BYTES - 8 * 1024 * 1024
SUBLANES = 8
LANES = 128
MXU_COLS = 256

DENSE_ROWS = 512
SCAN_ROWS = 1024
SCAN_SUB_ROWS = 256
LRU_ROWS = 256
SAMPLE_SEQS = 8

EPS = 1e-6
NEG_INF = -1e30
CHUNK = 64
WINDOW = 128
N_BUCKETS = 32
MAX_DIST = 128
A_HEADS, A_KV_HEADS, A_HD = 8, 2, 64
A_GROUP = A_HEADS // A_KV_HEADS
B_HEADS, B_HD = 8, 64
B_W_RANK, B_A_RANK, B_G_RANK = 64, 64, 128
RWKV_GN_EPS = 64e-5
C_HEADS, C_DK, C_DV = 4, 64, 128
C_G_RANK = 16
C_TAU = 16.0
D_BLOCKS = 8
CONV_W = 4
LRU_C = 8.0
BRANCH_W = 512
HG = 4
HGW = HG * B_HD


def _cparams(*sem):
    return pltpu.CompilerParams(dimension_semantics=sem, vmem_limit_bytes=VMEM_LIMIT)


def _resident(shape):
    nd = len(shape)
    return pl.BlockSpec(shape, lambda *_: (0,) * nd, pipeline_mode=pl.Buffered(1))


def _row_tile(n, cap):
    t = min(n, cap)
    while n % t:
        t //= 2
    return t


def _dot(a, b):
    return jnp.dot(a.astype(BF16), b.astype(BF16), preferred_element_type=F32)


def _dot_nt(a, b):
    return lax.dot_general(a.astype(BF16), b.astype(BF16), (((1,), (1,)), ((), ())),
                           preferred_element_type=F32)


def _dot_tn(a, b):
    return lax.dot_general(a.astype(BF16), b.astype(BF16), (((0,), (0,)), ((), ())),
                           preferred_element_type=F32)


def _rms(x, g):
    return x * lax.rsqrt(jnp.mean(x * x, axis=-1, keepdims=True) + EPS) * g


def _sigmoid(x):
    return 1.0 / (1.0 + jnp.exp(-x))


def _softplus(x):
    return jnp.maximum(x, 0.0) + jnp.log1p(jnp.exp(-jnp.abs(x)))


def _softplus_big(x):
    return jnp.maximum(x, 0.0) + jnp.log(1.0 + jnp.exp(-jnp.abs(x)))


def _split_dot(e_lhs, x, terms):
    acc = None
    rem = x
    for n in range(terms):
        piece = rem.astype(BF16)
        d = jnp.dot(e_lhs, piece, preferred_element_type=F32)
        acc = d if acc is None else acc + d
        if n + 1 < terms:
            rem = rem - piece.astype(F32)
    return acc


def _segsum(x, e):
    blk = e.shape[0]
    xb = x.astype(BF16)
    cols = [jnp.dot(xb[:, j:j + blk], e, preferred_element_type=F32)
            for j in range(0, x.shape[1], blk)]
    return cols[0] if len(cols) == 1 else jnp.concatenate(cols, axis=1)


def _tril_ones(n):
    r = lax.broadcasted_iota(jnp.int32, (n, n), 0)
    c = lax.broadcasted_iota(jnp.int32, (n, n), 1)
    return jnp.where(r >= c, 1.0, 0.0).astype(BF16)


def _bd_rows(x, blk, nblk):
    lane_blk = lax.broadcasted_iota(jnp.int32, x.shape, 1) // blk
    return jnp.concatenate([jnp.where(lane_blk == h, x, 0.0) for h in range(nblk)], axis=0)


def _block_diag_value(blocks):
    n = len(blocks)
    a, b = blocks[0].shape
    rows = []
    for h, blk in enumerate(blocks):
        parts = ([jnp.zeros((a, h * b), F32)] if h else []) + [blk] \
            + ([jnp.zeros((a, (n - 1 - h) * b), F32)] if h < n - 1 else [])
        rows.append(jnp.concatenate(parts, axis=1))
    return jnp.concatenate(rows, axis=0)


def _transpose_exact(x):
    n = x.shape[0]
    r = lax.broadcasted_iota(jnp.int32, (n, n), 0)
    c = lax.broadcasted_iota(jnp.int32, (n, n), 1)
    eye = jnp.where(r == c, 1.0, 0.0).astype(BF16)
    acc = None
    rem = x
    for step in range(3):
        piece = rem.astype(BF16)
        d = lax.dot_general(piece, eye, (((0,), (0,)), ((), ())), preferred_element_type=F32)
        acc = d if acc is None else acc + d
        if step < 2:
            rem = rem - piece.astype(F32)
    return acc


def _seg_matrix(width, seg):
    i = np.arange(width)
    return jnp.asarray((i[:, None] // seg) == (i[None, :] // seg), dtype=BF16)


def _ffn_chunks(f):
    tiles = -(-f // MXU_COLS)
    cut = min(f, (tiles + 1) // 2 * MXU_COLS)
    return [(0, cut)] + ([(cut, f - cut)] if cut < f else [])


def _swiglu_residual(x, g_ref, wg_ref, wu_ref, wd_ref):
    h = _rms(x, g_ref[...]).astype(BF16)
    acc = None
    for start, size in _ffn_chunks(wg_ref.shape[1]):
        sl = pl.ds(start, size)
        gt = jnp.dot(h, wg_ref[:, sl], preferred_element_type=F32)
        up = jnp.dot(h, wu_ref[:, sl], preferred_element_type=F32)
        act = (gt * _sigmoid(gt) * up).astype(BF16)
        d = jnp.dot(act, wd_ref[sl, :], preferred_element_type=F32)
        acc = d if acc is None else acc + d
    return x + 0.5 * acc


def _ffn_body(x_ref, g_ref, wg_ref, wu_ref, wd_ref, o_ref):
    o_ref[...] = _swiglu_residual(x_ref[...], g_ref, wg_ref, wu_ref, wd_ref)


def _ffn_ple_body(x_ref, pe_ref, g_ref, wg_ref, wu_ref, wd_ref, gp_ref, wpg_ref, wpp_ref, o_ref):
    x = _swiglu_residual(x_ref[...], g_ref, wg_ref, wu_ref, wd_ref)
    h = _rms(x, gp_ref[...]).astype(BF16)
    gate = _sigmoid(jnp.dot(h, wpg_ref[...], preferred_element_type=F32))
    o_ref[...] = x + gate * jnp.dot(pe_ref[...].astype(BF16), wpp_ref[...], preferred_element_type=F32)


def _ffn(x, g, wg, wu, wd):
    n, d = x.shape
    f = wg.shape[1]
    tm = _row_tile(n, DENSE_ROWS)
    return pl.pallas_call(
        _ffn_body,
        grid=(n // tm,),
        in_specs=[pl.BlockSpec((tm, d), lambda i: (i, 0)),
                  _resident((1, d)), _resident((d, f)), _resident((d, f)), _resident((f, d))],
        out_specs=pl.BlockSpec((tm, d), lambda i: (i, 0)),
        out_shape=jax.ShapeDtypeStruct((n, d), F32),
        compiler_params=_cparams("parallel"),
        name="ffn",
    )(x, g, wg, wu, wd)


def _inproj_body(x_ref, g_ref, wa_ref, wb_ref, wc_ref, wd_ref, pa_ref, pb_ref, pc_ref, pd_ref):
    h = _rms(x_ref[...], g_ref[...]).astype(BF16)
    for w_ref, p_ref in ((wa_ref, pa_ref), (wb_ref, pb_ref), (wc_ref, pc_ref), (wd_ref, pd_ref)):
        p_ref[...] = jnp.dot(h, w_ref[...], preferred_element_type=F32)


def _inproj(x, g, ws):
    n, d = x.shape
    tm = _row_tile(n, DENSE_ROWS)
    widths = [w.shape[1] for w in ws]
    return pl.pallas_call(
        _inproj_body,
        grid=(n // tm,),
        in_specs=[pl.BlockSpec((tm, d), lambda i: (i, 0)), _resident((1, d))]
                 + [_resident((d, w)) for w in widths],
        out_specs=[pl.BlockSpec((tm, w), lambda i: (i, 0)) for w in widths],
        out_shape=[jax.ShapeDtypeStruct((n, w), F32) for w in widths],
        compiler_params=_cparams("parallel"),
        name="inproj",
    )(x, g, *ws)


def _merge_body(x_ref, oa_ref, ob_ref, oc_ref, od_ref, g_ref, wmg_ref, wb_ref, wo_ref, o_ref):
    x = x_ref[...]
    h = _rms(x, g_ref[...]).astype(BF16)
    y = None
    for n, b_ref in enumerate((oa_ref, ob_ref, oc_ref, od_ref)):
        gate = _sigmoid(jnp.dot(h, wmg_ref[n], preferred_element_type=F32))
        t = gate * jnp.dot(b_ref[...].astype(BF16), wb_ref[n], preferred_element_type=F32)
        y = t if y is None else y + t
    o_ref[...] = x + jnp.dot(y.astype(BF16), wo_ref[...], preferred_element_type=F32)


def _merge(x, outs, g, wmg, wb, wo):
    n, d = x.shape
    bw = outs[0].shape[1]
    tm = _row_tile(n, DENSE_ROWS)
    return pl.pallas_call(
        _merge_body,
        grid=(n // tm,),
        in_specs=[pl.BlockSpec((tm, d), lambda i: (i, 0))]
                 + [pl.BlockSpec((tm, bw), lambda i: (i, 0))] * 4
                 + [_resident((1, d)), _resident(wmg.shape), _resident(wb.shape), _resident(wo.shape)],
        out_specs=pl.BlockSpec((tm, d), lambda i: (i, 0)),
        out_shape=jax.ShapeDtypeStruct((n, d), F32),
        compiler_params=_cparams("parallel"),
        name="merge",
    )(x, *outs, g, wmg, wb, wo)


def _ffn_ple(x, pe_all, layer, g, wg, wu, wd, gp, wpg, wpp):
    n, d = x.shape
    f = wg.shape[1]
    pd = pe_all.shape[2]
    tm = _row_tile(n, DENSE_ROWS)
    return pl.pallas_call(
        _ffn_ple_body,
        grid=(n // tm,),
        in_specs=[pl.BlockSpec((tm, d), lambda i: (i, 0)),
                  pl.BlockSpec((None, tm, pd), lambda i: (layer, i, 0)),
                  _resident((1, d)), _resident((d, f)), _resident((d, f)), _resident((f, d)),
                  _resident((1, d)), _resident((d, d)), _resident((pd, d))],
        out_specs=pl.BlockSpec((tm, d), lambda i: (i, 0)),
        out_shape=jax.ShapeDtypeStruct((n, d), F32),
        compiler_params=_cparams("parallel"),
        name="ffn_ple",
    )(x, pe_all, g, wg, wu, wd, gp, wpg, wpp)


def _t5_bucket_np(rel):
    half = N_BUCKETS // 2
    max_exact = half // 2
    ret = np.where(rel > 0, half, 0)
    n = np.abs(rel)
    nf = np.maximum(n, 1).astype(np.float32)
    large = max_exact + (np.log(nf / np.float32(max_exact)) / np.float32(math.log(MAX_DIST / max_exact))
                         * np.float32(half - max_exact)).astype(np.int32)
    large = np.minimum(large, half - 1)
    return (ret + np.where(n < max_exact, n, large)).astype(np.int32)


def _bias_body(idx_ref, table_ref, o_ref):
    idx = idx_ref[...]
    for h in range(A_HEADS):
        acc = jnp.zeros(idx.shape, F32)
        for b in range(N_BUCKETS):
            acc = jnp.where(idx == b, table_ref[b, h], acc)
        o_ref[h] = acc


def _rel_bias(table, n_q, n_k):
    rel = np.arange(n_k)[:, None] - WINDOW - np.arange(n_q)[None, :]
    idx = jnp.asarray(_t5_bucket_np(rel))
    out = pl.pallas_call(
        _bias_body,
        in_specs=[pl.BlockSpec(memory_space=pltpu.VMEM), pl.BlockSpec(memory_space=pltpu.SMEM)],
        out_specs=pl.BlockSpec(memory_space=pltpu.VMEM),
        out_shape=jax.ShapeDtypeStruct((A_HEADS, n_k, n_q), F32),
        name="rel_bias",
    )(idx, table)
    return jnp.transpose(out, (1, 0, 2)).reshape(n_k, A_HEADS * n_q)


def _head_rms(x, e, g):
    ms = _segsum(x * x, e) * (1.0 / A_HD)
    return x * lax.rsqrt(ms + EPS) * g


def _attend_blocks(q_list, k_list, v_list, bias, sink, valid_list, fill=lambda: None):
    def staged(f, *seqs):
        out = []
        for j, xs in enumerate(zip(*seqs)):
            out.append(f(*xs))
            if j % 4 == 3:
                fill()
        return out

    n = q_list[0].shape[0]
    gcols = A_GROUP * n
    inst = [(b, g) for b in range(len(q_list)) for g in range(A_KV_HEADS)]
    head = lambda x, h: x[:, h * A_HD:(h + 1) * A_HD]
    q_st = staged(lambda bg: jnp.concatenate([head(q_list[bg[0]], bg[1] * A_GROUP + r)
                                              for r in range(A_GROUP)], axis=0), inst)
    kg = staged(lambda bg: head(k_list[bg[0]], bg[1]), inst)
    vg = staged(lambda bg: head(v_list[bg[0]], bg[1]), inst)
    bias_g = [bias[:, g * gcols:(g + 1) * gcols] for g in range(A_KV_HEADS)]
    sink_g = [sink[:, g * gcols:(g + 1) * gcols] for g in range(A_KV_HEADS)]

    def scores(bg, k, qs):
        s = _dot_nt(k, qs) + bias_g[bg[1]]
        valid = valid_list[bg[0]]
        return s if valid is None else jnp.where(valid, s, NEG_INF)

    s = staged(scores, inst, kg, q_st)
    m = staged(lambda bg, x: jnp.maximum(jnp.max(x, axis=0, keepdims=True), sink_g[bg[1]]), inst, s)
    e = staged(lambda x, mm: jnp.exp(x - mm), s, m)
    rinv = [1.0 / (jnp.sum(x, axis=0, keepdims=True) + jnp.exp(sink_g[bg[1]] - mm))
            for bg, x, mm in zip(inst, e, m)]
    p = staged(lambda x, r: x * r, e, rinv)
    og = staged(_dot_tn, p, vg)
    outs = []
    for b in range(len(q_list)):
        pieces = [og[b * A_KV_HEADS + g][r * n:(r + 1) * n] for g in range(A_KV_HEADS)
                  for r in range(A_GROUP)]
        outs.append(jnp.concatenate(pieces, axis=1))
    return outs


def _attn_prompt_block(pa, kv_tail_ref, t_idx, bias_ref, sink_ref, qn_ref, kn_ref, eq_ref, ek_ref,
                       tb, fill):
    qw = A_HEADS * A_HD
    kw = A_KV_HEADS * A_HD
    band = WINDOW + CHUNK
    q = _head_rms(pa[:, 0:qw], eq_ref[...], qn_ref[...]) * (A_HD ** -0.5)
    kf = jnp.concatenate([kv_tail_ref[0], _head_rms(pa[:, qw:qw + kw], ek_ref[...], kn_ref[...])], axis=0)
    vf = jnp.concatenate([kv_tail_ref[1], pa[:, qw + kw:qw + 2 * kw]], axis=0)
    kv_tail_ref[0] = kf[tb:tb + WINDOW, :]
    kv_tail_ref[1] = vf[tb:tb + WINDOW, :]
    kb = kf.astype(BF16)
    vb = vf.astype(BF16)
    kidx = lax.broadcasted_iota(jnp.int32, (band, A_GROUP * CHUNK), 0)
    n_chunks = tb // CHUNK
    valid = [(kidx + (t_idx * tb + c * CHUNK - WINDOW)) >= 0 if c * CHUNK < WINDOW else None
             for c in range(n_chunks)]
    return _attend_blocks([q[c * CHUNK:(c + 1) * CHUNK] for c in range(n_chunks)],
                          [kb[c * CHUNK:c * CHUNK + band] for c in range(n_chunks)],
                          [vb[c * CHUNK:c * CHUNK + band] for c in range(n_chunks)],
                          bias_ref[...], sink_ref[...], valid, fill)


def _attn_sample_body(pa_ref, ck_ref, cv_ref, bias_ref, sink_ref, qn_ref, kn_ref, eq_ref, ek_ref,
                      o_ref, kout_ref, vout_ref, *, s, gb):
    qw = A_HEADS * A_HD
    kw = A_KV_HEADS * A_HD
    q = _head_rms(pa_ref[:, 0:qw], eq_ref[...], qn_ref[...]) * (A_HD ** -0.5)
    kn = _head_rms(pa_ref[:, qw:qw + kw], ek_ref[...], kn_ref[...])
    vn = pa_ref[:, qw + kw:qw + 2 * kw]
    q_list, k_list, v_list = [], [], []
    for b in range(gb):
        kf = jnp.concatenate([ck_ref[b], kn[b * s:(b + 1) * s]], axis=0)
        vf = jnp.concatenate([cv_ref[b], vn[b * s:(b + 1) * s]], axis=0)
        kout_ref[b] = kf[s:s + WINDOW, :]
        vout_ref[b] = vf[s:s + WINDOW, :]
        q_list.append(q[b * s:(b + 1) * s])
        k_list.append(kf.astype(BF16))
        v_list.append(vf.astype(BF16))
    outs = _attend_blocks(q_list, k_list, v_list, bias_ref[...], sink_ref[...], [None] * gb)
    for b in range(gb):
        o_ref[b * s:(b + 1) * s, :] = outs[b]


def _attn_sample(pa, ck, cv, bias, sink, qn, kn, nb, s):
    qw, kw = A_HEADS * A_HD, A_KV_HEADS * A_HD
    width = pa.shape[1]
    gb = _row_tile(nb, SAMPLE_SEQS)
    return pl.pallas_call(
        functools.partial(_attn_sample_body, s=s, gb=gb),
        grid=(nb // gb,),
        in_specs=[pl.BlockSpec((gb * s, width), lambda b: (b, 0)),
                  pl.BlockSpec((gb, WINDOW, kw), lambda b: (b, 0, 0)),
                  pl.BlockSpec((gb, WINDOW, kw), lambda b: (b, 0, 0)),
                  _resident(bias.shape), _resident(sink.shape),
                  _resident((1, qw)), _resident((1, kw)), _resident((HGW, HGW)), _resident((kw, kw))],
        out_specs=[pl.BlockSpec((gb * s, qw), lambda b: (b, 0)),
                   pl.BlockSpec((gb, WINDOW, kw), lambda b: (b, 0, 0)),
                   pl.BlockSpec((gb, WINDOW, kw), lambda b: (b, 0, 0))],
        out_shape=[jax.ShapeDtypeStruct((nb * s, qw), F32),
                   jax.ShapeDtypeStruct((nb, WINDOW, kw), F32),
                   jax.ShapeDtypeStruct((nb, WINDOW, kw), F32)],
        compiler_params=_cparams("parallel"),
        name="attn_sample",
    )(pa, ck, cv, bias, sink, qn, kn, _seg_matrix(HGW, A_HD), _seg_matrix(kw, A_HD))


def _rwkv_chunk_terms(r, lw, k, v, kk, a, fill=lambda: None):
    stage_count = [0]

    def each(f, *seqs):
        out = [f(*xs) for xs in zip(*seqs)]
        stage_count[0] += 1
        if stage_count[0] % 3 == 0:
            fill()
        return out

    cat0 = lambda *xs: jnp.concatenate(xs, axis=0)
    cat1 = lambda *xs: jnp.concatenate(xs, axis=1)
    bd = lambda x: _bd_rows(x, B_HD, HG)
    L = r[0].shape[0]
    tril = _tril_ones(L)
    c = each(lambda x: _split_dot(tril, x, 3), lw)
    c_last = each(lambda x: x[L - 1:L, :], c)
    e_last = each(lambda x, xl: jnp.exp(xl - x), c, c_last)
    beta = each(lambda x, y: x * y, kk, a)
    at = each(lambda x, cc, l: -x * jnp.exp(cc - l), kk, c, lw)
    rt = each(lambda x, cc: x * jnp.exp(cc), r, c)
    enc = each(lambda cc: jnp.exp(-cc), c)
    ar = each(cat0, at, rt)
    mb = each(lambda x, b, e: _dot_nt(x, bd(b * e)), ar, beta, enc)
    mk = each(lambda x, b, e: _dot_nt(x, bd(b * e)), ar, k, enc)
    row = lax.broadcasted_iota(jnp.int32, (L, HG * L), 0)
    col = lax.broadcasted_iota(jnp.int32, (L, HG * L), 1) % L
    strict = col < row
    incl = col <= row
    m_b = each(lambda x: jnp.where(strict, x[:L], 0.0), mb)
    m_k = each(lambda x: jnp.where(strict, x[:L], 0.0), mk)
    n_b = each(lambda x: jnp.where(incl, x[L:], 0.0), mb)
    n_k = each(lambda x: jnp.where(incl, x[L:], 0.0), mk)
    eye = jnp.where(col == row, 1.0, 0.0)
    t_inv = each(lambda x: eye + x, m_b)
    p = m_b
    for lvl in range(1, int(math.log2(L))):
        p_bd = each(lambda x: _bd_rows(x, L, HG), p)
        if lvl == 1:
            p = each(_dot, p, p_bd)
        else:
            tp = each(lambda t, x, xb: _dot(cat0(t, x), xb), t_inv, p, p_bd)
            t_inv = each(lambda t, x: t + x[:L], t_inv, tp)
            p = each(lambda x: x[L:], tp)
    t_inv = each(lambda t, x: t + _dot(t, _bd_rows(x, L, HG)), t_inv, p)
    v_bd = each(bd, v)
    mnv = each(lambda x, y, vb: _dot(cat0(x, y), vb), m_k, n_k, v_bd)
    mkv = each(lambda x: x[:L], mnv)
    wu = each(lambda t, x, y: _dot(t, cat1(bd(x), bd(y))), t_inv, at, mkv)
    w1 = each(lambda x: x[:, :HGW], wu)
    u0 = each(lambda x: x[:, HGW:], wu)
    nbo = each(lambda n, x, y: _dot(n, cat1(bd(x), bd(y))), n_b, w1, u0)
    q1 = each(lambda x, y: x + y[:, :HGW], rt, nbo)
    y0 = each(lambda x, y: x[:, HGW:] + y[L:], nbo, mnv)
    bh = each(lambda x, y: x * y, beta, e_last)
    kh = each(lambda x, y: x * y, k, e_last)
    ri = lax.broadcasted_iota(jnp.int32, (HGW, HGW), 0) // B_HD
    ci = lax.broadcasted_iota(jnp.int32, (HGW, HGW), 1) // B_HD
    diag = ri == ci
    g_mat = each(lambda x, y: jnp.where(diag, _dot_tn(x, y), 0.0), w1, bh)
    h_mat = each(lambda x, y, z, w: jnp.where(diag, _dot_tn(cat0(x, y), cat0(z, w)), 0.0), u0, v, bh, kh)
    gam = each(jnp.exp, c_last)
    return q1, y0, g_mat, h_mat, gam


def _rwkv_body(pb_ref, shift_ref, s0_ref, mu_ref, w0_ref, w2_ref, a0_ref, a2_ref, g2_ref, kk_ref,
               ka_ref, rk_ref, lng_ref, lnb_ref, e_ref,
               o_ref, sout_ref,
               ext_ref, s_ref, *, tb, sub, lc):
    t = pl.program_id(1)
    W = BRANCH_W
    n_groups = W // HGW

    @pl.when(t == 0)
    def _():
        ext_ref[0:SUBLANES, :] = jnp.broadcast_to(shift_ref[0], (SUBLANES, ext_ref.shape[1]))
        for gi in range(n_groups):
            s_ref[gi] = _block_diag_value([s0_ref[0, gi * HG + h] for h in range(HG)])

    ext_ref[SUBLANES:, :] = pb_ref[...]

    e = e_ref[...]
    n_sub = tb // sub

    rows_c = min(sub, lc)
    n_chunks = sub // rows_c

    def prep_stages(lo):
        f = {}

        def shift():
            p = pb_ref[lo:lo + sub, :]
            prev = ext_ref[SUBLANES - 1 + lo:SUBLANES - 1 + lo + sub, :]
            f['x'] = p + (prev - p) * mu_ref[...]
            f['r'], f['v'] = f['x'][:, 0:W], f['x'][:, 2 * W:3 * W]

        def decay():
            wal = f['x'][:, 3 * W:3 * W + LANES]
            w_log = -_softplus_big(-(w0_ref[...] + _dot(jnp.tanh(wal), w2_ref[...]))) - 0.5
            f['lw'] = -jnp.exp(w_log)

        def gates():
            wal = f['x'][:, 3 * W:3 * W + LANES]
            gl = f['x'][:, 3 * W + LANES:3 * W + 2 * LANES]
            f['a'] = _sigmoid(a0_ref[...] + _dot(wal, a2_ref[...]))
            f['g'] = _dot(_sigmoid(gl), g2_ref[...])

        def keys():
            k = f['x'][:, W:2 * W]
            kkr = k * kk_ref[...]
            f['kk'] = kkr / jnp.maximum(jnp.sqrt(_segsum(kkr * kkr, e)), 1e-12)
            f['k2'] = k * (1.0 + (f['a'] - 1.0) * ka_ref[...])

        return f, [shift, decay, gates, keys]

    def post_stages(lo, f, y):
        t = {}

        def center():
            t['yc'] = y - _segsum(y, e) * (1.0 / B_HD)

        def norm():
            var = _segsum(t['yc'] * t['yc'], e) * (1.0 / B_HD)
            t['yn'] = t['yc'] * lax.rsqrt(var + RWKV_GN_EPS) * lng_ref[...] + lnb_ref[...]

        def out():
            bonus = _segsum(f['r'] * f['k2'] * rk_ref[...], e) * f['v']
            o_ref[lo:lo + sub, :] = (t['yn'] + bonus) * f['g']

        return [center, norm, out]

    def piece(x, ci, gi):
        blk = x[ci * rows_c:(ci + 1) * rows_c, gi * HGW:(gi + 1) * HGW]
        if rows_c < lc:
            blk = jnp.concatenate([blk, jnp.zeros((lc - rows_c, HGW), F32)], axis=0)
        return blk

    feats = [prep_stages(j * sub) for j in range(n_sub)]
    for stage in feats[0][1]:
        stage()
    pending = []

    def fill():
        if pending:
            pending.pop(0)()

    inst = [(ci, gi) for ci in range(n_chunks) for gi in range(n_groups)]
    for j in range(n_sub):
        f = feats[j][0]
        if j + 1 < n_sub:
            pending.extend(feats[j + 1][1])
        terms = _rwkv_chunk_terms(*[[piece(f[name], ci, gi) for ci, gi in inst]
                                    for name in ('r', 'lw', 'k2', 'v', 'kk', 'a')], fill)
        while pending:
            fill()
        y_rows = []
        for ci in range(n_chunks):
            y_cols = []
            for gi in range(n_groups):
                q1, y0, g_mat, h_mat, gam = [tm[ci * n_groups + gi] for tm in terms]
                s = s_ref[gi]
                y_cols.append(_dot_nt(q1, s) + y0)
                s_ref[gi] = s * gam + _dot(s, g_mat) + h_mat
            y_rows.append(jnp.concatenate(y_cols, axis=1)[0:rows_c])
        pending.extend(post_stages(j * sub, f, jnp.concatenate(y_rows, axis=0)))
    while pending:
        fill()
    ext_ref[0:SUBLANES, :] = pb_ref[tb - SUBLANES:tb, :]

    @pl.when(t == pl.num_programs(1) - 1)
    def _():
        for gi in range(n_groups):
            s = s_ref[gi]
            for h in range(HG):
                sout_ref[0, gi * HG + h] = s[h * B_HD:(h + 1) * B_HD, h * B_HD:(h + 1) * B_HD]


def _rwkv(pb, shift0, s0, wts, nb, t):
    tb = _row_tile(t, SCAN_ROWS)
    sub = min(tb, SCAN_SUB_ROWS)
    lc = CHUNK
    assert sub % lc == 0 or sub < lc
    nt = t // tb
    width = pb.shape[1]
    W = BRANCH_W
    n_groups = W // HGW
    row = lambda n: _resident((1, n))
    return pl.pallas_call(
        functools.partial(_rwkv_body, tb=tb, sub=sub, lc=lc),
        grid=(nb, nt),
        in_specs=[pl.BlockSpec((tb, width), lambda b, i: (b * nt + i, 0)),
                  pl.BlockSpec((1, 1, width), lambda b, i: (b, 0, 0)),
                  pl.BlockSpec((1, B_HEADS, B_HD, B_HD), lambda b, i: (b, 0, 0, 0)),
                  row(width), row(W), _resident((LANES, W)), row(W), _resident((LANES, W)),
                  _resident((B_G_RANK, W)), row(W), row(W), row(W), row(W), row(W), _resident((HGW, HGW))],
        out_specs=[pl.BlockSpec((tb, W), lambda b, i: (b * nt + i, 0)),
                   pl.BlockSpec((1, B_HEADS, B_HD, B_HD), lambda b, i: (b, 0, 0, 0))],
        out_shape=[jax.ShapeDtypeStruct((nb * t, W), F32),
                   jax.ShapeDtypeStruct((nb, B_HEADS, B_HD, B_HD), F32)],
        scratch_shapes=[pltpu.VMEM((tb + SUBLANES, width), F32),
                        pltpu.VMEM((n_groups, HGW, HGW), F32)],
        compiler_params=_cparams("parallel", "arbitrary"),
        name="rwkv7",
    )(pb, shift0, s0, *wts, _seg_matrix(HGW, B_HD))


def _gla_chunk_terms(q, k, v, gk):
    each = lambda f, *seqs: [f(*xs) for xs in zip(*seqs)]
    L = q[0].shape[0]
    kw = C_HEADS * C_DK
    vw = C_HEADS * C_DV
    tril = _tril_ones(L)
    b = each(lambda x: _split_dot(tril, x, 3), gk)
    qe = each(lambda x, y: x * jnp.exp(y), q, b)
    ke = each(lambda x, y: x * jnp.exp(-y), k, b)
    a_all = each(lambda x, y: _dot_nt(x, _bd_rows(y, C_DK, C_HEADS)), qe, ke)
    row = lax.broadcasted_iota(jnp.int32, (L, C_HEADS * L), 0)
    col = lax.broadcasted_iota(jnp.int32, (L, C_HEADS * L), 1) % L
    causal = col <= row
    o_intra = each(lambda x, y: _dot(jnp.where(causal, x, 0.0), _bd_rows(y, C_DV, C_HEADS)), a_all, v)
    b_last = each(lambda x: x[L - 1:L, :], b)
    kd = each(lambda x, y, z: x * jnp.exp(z - y), k, b, b_last)
    ri = lax.broadcasted_iota(jnp.int32, (vw, kw), 0) // C_DV
    ci = lax.broadcasted_iota(jnp.int32, (vw, kw), 1) // C_DK
    diag = ri == ci
    upd = each(lambda x, y: jnp.where(diag, _dot_tn(x, y), 0.0), v, kd)
    decay = each(jnp.exp, b_last)
    return qe, o_intra, upd, decay


def _gla_body(pc_ref, s0_ref, g2_ref, gb_ref, norm_ref, o_ref, sout_ref,
              s_ref, *, tb, lc):
    t = pl.program_id(1)
    kw = C_HEADS * C_DK
    vw = C_HEADS * C_DV

    @pl.when(t == 0)
    def _():
        s_ref[...] = _block_diag_value([_transpose_exact(s0_ref[0, h]) for h in range(C_HEADS)])

    pc = pc_ref[...]
    og = pc[:, 2 * kw + vw:2 * kw + 2 * vw]
    gl = pc[:, 2 * kw + 2 * vw:2 * kw + 2 * vw + LANES]
    z = _dot(gl, g2_ref[...]) + gb_ref[...]
    q = pc[:, 0:kw] * (C_DK ** -0.5)
    k = pc[:, kw:2 * kw]
    v = pc[:, 2 * kw:2 * kw + vw]
    gk = -_softplus_big(-z) * (1.0 / C_TAU)
    n_chunks = max(tb // lc, 1)

    def piece(x, ci):
        blk = x[ci * lc:min((ci + 1) * lc, tb), :]
        if tb < lc:
            blk = jnp.concatenate([blk, jnp.zeros((lc - tb, x.shape[1]), F32)], axis=0)
        return blk

    qe, o_intra, upd, decay = _gla_chunk_terms(
        *[[piece(x, ci) for ci in range(n_chunks)] for x in (q, k, v, gk)])
    y_rows = []
    for ci in range(n_chunks):
        st = s_ref[...]
        y_rows.append(_dot_nt(qe[ci], st) + o_intra[ci])
        s_ref[...] = st * decay[ci] + upd[ci]
    y = jnp.concatenate(y_rows, axis=0)[0:tb]
    outs = []
    for h in range(C_HEADS):
        yh = y[:, h * C_DV:(h + 1) * C_DV]
        outs.append(yh * lax.rsqrt(jnp.mean(yh * yh, axis=-1, keepdims=True) + EPS) * norm_ref[...])
    yn = jnp.concatenate(outs, axis=1)
    o_ref[...] = yn * (og * _sigmoid(og))

    @pl.when(t == pl.num_programs(1) - 1)
    def _():
        st = s_ref[...]
        for h in range(C_HEADS):
            sout_ref[0, h] = _transpose_exact(st[h * C_DV:(h + 1) * C_DV, h * C_DK:(h + 1) * C_DK])


def _gla(pc, s0, g2, gb, norm, nb, t):
    tb = _row_tile(t, SCAN_ROWS)
    lc = CHUNK
    assert tb % lc == 0 or tb < lc
    nt = t // tb
    width = pc.shape[1]
    kw, vw = C_HEADS * C_DK, C_HEADS * C_DV
    return pl.pallas_call(
        functools.partial(_gla_body, tb=tb, lc=lc),
        grid=(nb, nt),
        in_specs=[pl.BlockSpec((tb, width), lambda b, i: (b * nt + i, 0)),
                  pl.BlockSpec((1, C_HEADS, C_DK, C_DV), lambda b, i: (b, 0, 0, 0)),
                  _resident((LANES, kw)), _resident((1, kw)), _resident((1, C_DV))],
        out_specs=[pl.BlockSpec((tb, vw), lambda b, i: (b * nt + i, 0)),
                   pl.BlockSpec((1, C_HEADS, C_DK, C_DV), lambda b, i: (b, 0, 0, 0))],
        out_shape=[jax.ShapeDtypeStruct((nb * t, vw), F32),
                   jax.ShapeDtypeStruct((nb, C_HEADS, C_DK, C_DV), F32)],
        scratch_shapes=[pltpu.VMEM((vw, kw), F32)],
        compiler_params=_cparams("parallel", "arbitrary"),
        name="gla",
    )(pc, s0, g2, gb, norm)


def _shift_rows(x, d, fill):
    n = x.shape[0]
    if d % SUBLANES == 0:
        head = jnp.full((d, x.shape[1]), fill, x.dtype)
        return jnp.concatenate([head, x[:n - d]], axis=0)
    rolled = pltpu.roll(x, d, 0)
    row = lax.broadcasted_iota(jnp.int32, x.shape, 0)
    return jnp.where(row < d, fill, rolled)


def _lru_block(gate, xr, ext_ref, h_ref, au_ref, cw_ref, cb_ref, wa_ref, ba_ref, wx_ref, bx_ref,
               lam_ref, tb, fill=lambda: None):
    W = BRANCH_W
    ext_ref[SUBLANES:, :] = xr
    xc = cb_ref[...] + xr * cw_ref[CONV_W - 1:CONV_W, :]
    for j in range(CONV_W - 1):
        off = SUBLANES - (CONV_W - 1) + j
        xc = xc + ext_ref[off:off + tb, :] * cw_ref[j:j + 1, :]
    tail = ext_ref[tb:tb + SUBLANES, :]
    ext_ref[0:SUBLANES, :] = tail
    fill()

    def block_dot(w_ref):
        return jnp.concatenate([_dot(xc[:, j:j + MXU_COLS], w_ref[j:j + MXU_COLS, j:j + MXU_COLS])
                                for j in range(0, W, MXU_COLS)], axis=1)

    r = _sigmoid(block_dot(wa_ref) + ba_ref[...])
    fill()
    gi = _sigmoid(block_dot(wx_ref) + bx_ref[...])
    fill()
    log_a = (-LRU_C) * r * _softplus(-lam_ref[...])
    a = jnp.exp(log_a)
    u = jnp.sqrt(1.0 - jnp.exp(2.0 * log_a)) * (gi * xc)
    fill()
    ng = tb // SUBLANES
    row_in_group = lax.broadcasted_iota(jnp.int32, (ng, SUBLANES, W), 1)
    a = a.reshape(ng, SUBLANES, W)
    u = u.reshape(ng, SUBLANES, W)
    d = 1
    while d < SUBLANES:
        inside = row_in_group >= d
        u = u + a * jnp.where(inside, pltpu.roll(u, d, 1), 0.0)
        a = a * jnp.where(inside, pltpu.roll(a, d, 1), 1.0)
        d *= 2
        fill()
    a = a.reshape(tb, W)
    u = u.reshape(tb, W)
    h_in = h_ref[...]
    if ng % SUBLANES == 0:
        ends = pl.ds(SUBLANES - 1, ng, stride=SUBLANES)
        for j in range(W // LANES):
            au_ref[0, j] = a[:, j * LANES:(j + 1) * LANES]
            au_ref[1, j] = u[:, j * LANES:(j + 1) * LANES]
        ae = jnp.concatenate([au_ref[0, j, ends, :] for j in range(W // LANES)], axis=1)
        ue = jnp.concatenate([au_ref[1, j, ends, :] for j in range(W // LANES)], axis=1)
        d = 1
        while d < ng:
            ue = ue + ae * _shift_rows(ue, d, 0.0)
            ae = ae * _shift_rows(ae, d, 1.0)
            d *= 2
        carry = _shift_rows(ue + ae * h_in, 1, h_in)
        carry_rows = jnp.broadcast_to(carry[:, None, :], (ng, SUBLANES, W)).reshape(tb, W)
    else:
        rows = []
        for g in range(ng):
            rows.append(jnp.broadcast_to(h_in, (SUBLANES, W)))
            end = (g + 1) * SUBLANES - 1
            h_in = u[end:end + 1, :] + a[end:end + 1, :] * h_in
        carry_rows = jnp.concatenate(rows, axis=0)
    h = u + a * carry_rows
    h_last = h[tb - 1:tb, :]
    h_ref[...] = h_last
    fill()
    out = h * (0.5 * gate * (1.0 + jnp.tanh(math.sqrt(2.0 / math.pi)
                                            * (gate + 0.044715 * gate * gate * gate))))
    return out, tail, h_last


def _lru_body(pd_ref, conv0_ref, h0_ref, cw_ref, cb_ref, wa_ref, ba_ref, wx_ref, bx_ref, lam_ref,
              o_ref, convout_ref, hout_ref, ext_ref, h_ref, au_ref, *, tb):
    t = pl.program_id(1)
    W = BRANCH_W

    @pl.when(t == 0)
    def _():
        ext_ref[0:SUBLANES, :] = conv0_ref[0]
        h_ref[...] = h0_ref[0]

    out, tail, h_last = _lru_block(pd_ref[:, 0:W], pd_ref[:, W:2 * W], ext_ref, h_ref, au_ref, cw_ref,
                                   cb_ref, wa_ref, ba_ref, wx_ref, bx_ref, lam_ref, tb)
    o_ref[...] = out

    @pl.when(t == pl.num_programs(1) - 1)
    def _():
        convout_ref[0] = tail
        hout_ref[0] = h_last


def _mix_in_body(x_ref, g_ref, wa_ref, wb_ref, wc_ref, wd_ref,
                 bias_ref, sink_ref, qn_ref, kn_ref, eq_ref, ek_ref,
                 conv0_ref, h0_ref, cw_ref, cb_ref, lwa_ref, lba_ref, lwx_ref, lbx_ref, lam_ref,
                 pb_ref, pc_ref, oa_ref, od_ref, kout_ref, vout_ref, convout_ref, hout_ref,
                 kv_tail_ref, ext_ref, h_ref, au_ref, *, tb):
    t = pl.program_id(1)
    W = BRANCH_W

    @pl.when(t == 0)
    def _():
        kv_tail_ref[...] = jnp.zeros(kv_tail_ref.shape, F32)
        ext_ref[0:SUBLANES, :] = conv0_ref[0]
        h_ref[...] = h0_ref[0]

    h = _rms(x_ref[...], g_ref[...]).astype(BF16)
    pd = jnp.dot(h, wd_ref[...], preferred_element_type=F32)
    pa = jnp.dot(h, wa_ref[...], preferred_element_type=F32)

    half = tb // 2
    pending = [(w_ref, p_ref, c0, min(MXU_COLS, w_ref.shape[1] - c0), r0)
               for w_ref, p_ref in ((wb_ref, pb_ref), (wc_ref, pc_ref))
               for c0 in range(0, w_ref.shape[1], MXU_COLS) for r0 in (0, half)]

    def fill():
        if pending:
            w_ref, p_ref, c0, width, r0 = pending.pop(0)
            p_ref[r0:r0 + half, c0:c0 + width] = jnp.dot(h[r0:r0 + half], w_ref[:, c0:c0 + width],
                                                         preferred_element_type=F32)

    od, tail, h_last = _lru_block(pd[:, 0:W], pd[:, W:2 * W], ext_ref, h_ref, au_ref, cw_ref, cb_ref,
                                  lwa_ref, lba_ref, lwx_ref, lbx_ref, lam_ref, tb, fill)
    od_ref[...] = od
    outs = _attn_prompt_block(pa, kv_tail_ref, t, bias_ref, sink_ref, qn_ref, kn_ref, eq_ref, ek_ref,
                              tb, fill)
    for c, o in enumerate(outs):
        oa_ref[c * CHUNK:(c + 1) * CHUNK, :] = o
    while pending:
        fill()

    @pl.when(t == pl.num_programs(1) - 1)
    def _():
        kout_ref[0] = kv_tail_ref[0]
        vout_ref[0] = kv_tail_ref[1]
        convout_ref[0] = tail
        hout_ref[0] = h_last


def _mix_in(x, g, ws, bias, sink, qn, kn, conv0, h0, lru_wts, nb, t):
    n, d = x.shape
    tb = _row_tile(t, DENSE_ROWS)
    assert tb % WINDOW == 0
    nt = t // tb
    W = BRANCH_W
    qw, kw = A_HEADS * A_HD, A_KV_HEADS * A_HD
    wa, wb, wc, wd = ws
    rows = lambda width: pl.BlockSpec((tb, width), lambda b, i: (b * nt + i, 0))
    per_seq = lambda *shape: pl.BlockSpec((1,) + shape, lambda b, i: (b,) + (0,) * len(shape))
    row = lambda width: _resident((1, width))
    return pl.pallas_call(
        functools.partial(_mix_in_body, tb=tb),
        grid=(nb, nt),
        in_specs=[rows(d), row(d)] + [_resident(w.shape) for w in ws]
                 + [_resident(bias.shape), _resident(sink.shape), row(qw), row(kw),
                    _resident((HGW, HGW)), _resident((kw, kw)),
                    per_seq(SUBLANES, W), per_seq(1, W),
                    _resident((CONV_W, W)), row(W), _resident((W, W)), row(W), _resident((W, W)), row(W),
                    row(W)],
        out_specs=[rows(wb.shape[1]), rows(wc.shape[1]), rows(qw), rows(W),
                   per_seq(WINDOW, kw), per_seq(WINDOW, kw), per_seq(SUBLANES, W), per_seq(1, W)],
        out_shape=[jax.ShapeDtypeStruct((n, wb.shape[1]), F32), jax.ShapeDtypeStruct((n, wc.shape[1]), F32),
                   jax.ShapeDtypeStruct((n, qw), F32), jax.ShapeDtypeStruct((n, W), F32),
                   jax.ShapeDtypeStruct((nb, WINDOW, kw), F32), jax.ShapeDtypeStruct((nb, WINDOW, kw), F32),
                   jax.ShapeDtypeStruct((nb, SUBLANES, W), F32), jax.ShapeDtypeStruct((nb, 1, W), F32)],
        scratch_shapes=[pltpu.VMEM((2, WINDOW, kw), F32),
                        pltpu.VMEM((tb + SUBLANES, W), F32), pltpu.VMEM((1, W), F32),
                        pltpu.VMEM((2, W // LANES, tb, LANES), F32)],
        compiler_params=_cparams("parallel", "arbitrary"),
        name="mix_in",
    )(x, g, wa, wb, wc, wd, bias, sink, qn, kn, _seg_matrix(HGW, A_HD), _seg_matrix(kw, A_HD),
      conv0, h0, *lru_wts)


def _lru(pd, conv0, h0, wts, nb, t):
    tb = _row_tile(t, LRU_ROWS)
    nt = t // tb
    W = BRANCH_W
    row = lambda: _resident((1, W))
    return pl.pallas_call(
        functools.partial(_lru_body, tb=tb),
        grid=(nb, nt),
        in_specs=[pl.BlockSpec((tb, 2 * W), lambda b, i: (b * nt + i, 0)),
                  pl.BlockSpec((1, SUBLANES, W), lambda b, i: (b, 0, 0)),
                  pl.BlockSpec((1, 1, W), lambda b, i: (b, 0, 0)),
                  _resident((CONV_W, W)), row(), _resident((W, W)), row(), _resident((W, W)), row(),
                  row()],
        out_specs=[pl.BlockSpec((tb, W), lambda b, i: (b * nt + i, 0)),
                   pl.BlockSpec((1, SUBLANES, W), lambda b, i: (b, 0, 0)),
                   pl.BlockSpec((1, 1, W), lambda b, i: (b, 0, 0))],
        out_shape=[jax.ShapeDtypeStruct((nb * t, W), F32),
                   jax.ShapeDtypeStruct((nb, SUBLANES, W), F32),
                   jax.ShapeDtypeStruct((nb, 1, W), F32)],
        scratch_shapes=[pltpu.VMEM((tb + SUBLANES, W), F32), pltpu.VMEM((1, W), F32),
                        pltpu.VMEM((2, W // LANES, tb, LANES), F32)],
        compiler_params=_cparams("parallel", "arbitrary"),
        name="rglru",
    )(pd, conv0, h0, *wts)


def _block_diag(w):
    n, a, b = w.shape
    eye = jnp.eye(n, dtype=w.dtype)
    return (eye[:, None, :, None] * w[:, :, None, :]).reshape(n * a, n * b)


def _pad_rows_to(w, rows):
    return jnp.pad(w, ((0, rows - w.shape[0]), (0, 0)))


def _layer_weights(i, W):
    bf = lambda a: a.astype(BF16)
    row = lambda a: a.reshape(1, -1).astype(F32)
    bw = BRANCH_W
    a_cols = A_HEADS * A_HD + 2 * A_KV_HEADS * A_HD
    b_cols = 3 * bw + B_W_RANK + B_A_RANK + B_G_RANK
    c_cols = 2 * C_HEADS * C_DK + C_HEADS * C_DV + C_G_RANK + bw
    w_in = W['w_in'][i]
    d = w_in.shape[0]
    wa = w_in[:, :a_cols]
    wb = w_in[:, a_cols:a_cols + b_cols]
    wc = w_in[:, a_cols + b_cols:a_cols + b_cols + c_cols]
    wd = w_in[:, a_cols + b_cols + c_cols:]
    zpad = lambda n: jnp.zeros((d, n), w_in.dtype)
    assert B_W_RANK + B_A_RANK == LANES
    w2 = jnp.pad(W['rwkv_w2'][i], ((0, B_A_RANK), (0, 0)))
    a2 = jnp.pad(W['rwkv_a2'][i], ((B_W_RANK, 0), (0, 0)))
    qkv = 2 * C_HEADS * C_DK + C_HEADS * C_DV
    wc2 = jnp.concatenate([wc[:, :qkv], wc[:, qkv + C_G_RANK:], wc[:, qkv:qkv + C_G_RANK],
                           zpad(LANES - C_G_RANK)], axis=1)
    return dict(
        ffn1=(row(W['g_ffn1'][i]), bf(W['w_ffn1_gate'][i]), bf(W['w_ffn1_up'][i]), bf(W['w_ffn1_down'][i])),
        ffn2=(row(W['g_ffn2'][i]), bf(W['w_ffn2_gate'][i]), bf(W['w_ffn2_up'][i]), bf(W['w_ffn2_down'][i])),
        g_mix=row(W['g_mix'][i]),
        w_in=(bf(wa), bf(wb), bf(wc2), bf(wd)),
        q_norm=row(jnp.tile(W['q_norm'][i], A_HEADS)),
        k_norm=row(jnp.tile(W['k_norm'][i], A_KV_HEADS)),
        sink=W['attn_sink'][i].astype(F32),
        rwkv=(row(W['rwkv_mu'][i]), row(W['rwkv_w0'][i]), bf(w2),
              row(W['rwkv_a0'][i]), bf(a2), bf(W['rwkv_g2'][i]),
              row(W['rwkv_k_k'][i]), row(W['rwkv_k_a'][i]), row(W['rwkv_r_k'][i]),
              row(W['rwkv_ln_g'][i]), row(W['rwkv_ln_b'][i])),
        gla=(bf(_pad_rows_to(W['gla_g2'][i], LANES)), row(W['gla_gb'][i]), row(W['gla_norm'][i])),
        lru=(W['lru_conv_w'][i].astype(F32), row(W['lru_conv_b'][i]),
             bf(_block_diag(W['lru_wa'][i])), row(W['lru_ba'][i]),
             bf(_block_diag(W['lru_wx'][i])), row(W['lru_bx'][i]), row(W['lru_lambda'][i])),
        merge=(bf(W['w_merge_gate'][i]), bf(W['w_branch'][i]), bf(W['w_out'][i])),
        ple=(row(W['g_ple'][i]), bf(W['w_ple_gate'][i]), bf(W['w_ple_proj'][i])),
    )


def _trunk_layer(x, pe_all, layer, st, lw, bias, first_chunk):
    nb, t, d = x.shape
    n = nb * t
    ck, cv, shift0, s_rwkv0, s_gla0, conv0, lru0 = st
    x2 = x.reshape(n, d)
    x2 = _ffn(x2, *lw['ffn1'])
    kw = A_KV_HEADS * A_HD
    n_q = bias.shape[1] // A_HEADS
    sink2 = jnp.repeat(lw['sink'], n_q).reshape(1, A_HEADS * n_q)
    conv_pad = jnp.pad(conv0, ((0, 0), (SUBLANES - (CONV_W - 1), 0), (0, 0)))
    if first_chunk:
        pb, pc, o_a, o_d, k_win, v_win, conv_out, h_out = _mix_in(
            x2, lw['g_mix'], lw['w_in'], bias, sink2, lw['q_norm'], lw['k_norm'],
            conv_pad, lru0[:, None, :], lw['lru'], nb, t)
    else:
        pa, pb, pc, pd = _inproj(x2, lw['g_mix'], lw['w_in'])
        o_a, k_win, v_win = _attn_sample(pa, ck.reshape(nb, WINDOW, kw), cv.reshape(nb, WINDOW, kw),
                                         bias, sink2, lw['q_norm'], lw['k_norm'], nb, t)
        o_d, conv_out, h_out = _lru(pd, conv_pad, lru0[:, None, :], lw['lru'], nb, t)
    k_win = k_win.reshape(nb, WINDOW, A_KV_HEADS, A_HD)
    v_win = v_win.reshape(nb, WINDOW, A_KV_HEADS, A_HD)

    o_b, s_rwkv1 = _rwkv(pb, shift0[:, None, :], s_rwkv0, lw['rwkv'], nb, t)
    shift1 = pb.reshape(nb, t, -1)[:, -1]
    o_c, s_gla1 = _gla(pc, s_gla0, *lw['gla'], nb, t)
    conv1 = conv_out[:, SUBLANES - (CONV_W - 1):]
    lru1 = h_out[:, 0]

    x2 = _merge(x2, (o_a, o_b, o_c, o_d), lw['g_mix'], *lw['merge'])
    x2 = _ffn_ple(x2, pe_all.reshape(pe_all.shape[0], n, -1), layer, *lw['ffn2'], *lw['ple'])
    return x2.reshape(nb, t, d), (k_win, v_win, shift1, s_rwkv1, s_gla1, conv1, lru1)


def kernel(x_prompt, x_sample, cache_attn_k, cache_attn_v, state_rwkv_shift, state_rwkv, state_gla, state_lru_conv, state_lru, p_prompt, p_sample, rel_bias_table, g_ffn1, w_ffn1_gate, w_ffn1_up, w_ffn1_down, g_mix, w_in, q_norm, k_norm, attn_sink, rwkv_mu, rwkv_w0, rwkv_w2, rwkv_a0, rwkv_a2, rwkv_g2, rwkv_k_k, rwkv_k_a, rwkv_r_k, rwkv_ln_g, rwkv_ln_b, gla_g2, gla_gb, gla_norm, lru_conv_w, lru_conv_b, lru_wa, lru_ba, lru_wx, lru_bx, lru_lambda, w_merge_gate, w_branch, w_out, g_ffn2, w_ffn2_gate, w_ffn2_up, w_ffn2_down, g_ple, w_ple_gate, w_ple_proj):
    W = dict(g_ffn1=g_ffn1, w_ffn1_gate=w_ffn1_gate, w_ffn1_up=w_ffn1_up, w_ffn1_down=w_ffn1_down,
             g_mix=g_mix, w_in=w_in, q_norm=q_norm, k_norm=k_norm, attn_sink=attn_sink,
             rwkv_mu=rwkv_mu, rwkv_w0=rwkv_w0, rwkv_w2=rwkv_w2, rwkv_a0=rwkv_a0, rwkv_a2=rwkv_a2,
             rwkv_g2=rwkv_g2, rwkv_k_k=rwkv_k_k, rwkv_k_a=rwkv_k_a, rwkv_r_k=rwkv_r_k,
             rwkv_ln_g=rwkv_ln_g, rwkv_ln_b=rwkv_ln_b, gla_g2=gla_g2, gla_gb=gla_gb,
             gla_norm=gla_norm, lru_conv_w=lru_conv_w, lru_conv_b=lru_conv_b, lru_wa=lru_wa,
             lru_ba=lru_ba, lru_wx=lru_wx, lru_bx=lru_bx, lru_lambda=lru_lambda,
             w_merge_gate=w_merge_gate, w_branch=w_branch, w_out=w_out, g_ffn2=g_ffn2,
             w_ffn2_gate=w_ffn2_gate, w_ffn2_up=w_ffn2_up, w_ffn2_down=w_ffn2_down,
             g_ple=g_ple, w_ple_gate=w_ple_gate, w_ple_proj=w_ple_proj)
    depth = w_in.shape[0]
    dt = x_prompt.dtype
    bp, tp = x_prompt.shape[:2]
    ts = x_sample.shape[1]
    b_cols = state_rwkv_shift.shape[-1]
    bias_p = _rel_bias(rel_bias_table, CHUNK, WINDOW + CHUNK)
    bias_s = _rel_bias(rel_bias_table, ts, WINDOW + ts)
    yp, ys = x_prompt, x_sample
    st_p, st_s = [], []
    for i in range(depth):
        lw = _layer_weights(i, W)
        zero_st = (None, None,
                   jnp.zeros((bp, b_cols), dt),
                   jnp.zeros((bp, B_HEADS, B_HD, B_HD), dt),
                   jnp.zeros((bp, C_HEADS, C_DK, C_DV), dt),
                   jnp.zeros((bp, CONV_W - 1, BRANCH_W), dt),
                   jnp.zeros((bp, BRANCH_W), dt))
        yp, sp = _trunk_layer(yp, p_prompt, i, zero_st, lw, bias_p, True)
        cache_st = (cache_attn_k[i], cache_attn_v[i], state_rwkv_shift[i], state_rwkv[i],
                    state_gla[i], state_lru_conv[i], state_lru[i])
        ys, ss = _trunk_layer(ys, p_sample, i, cache_st, lw, bias_s, False)
        st_p.append(sp)
        st_s.append(ss)
    stack = lambda states, j: jnp.stack([s[j] for s in states])
    return (yp, ys) + tuple(stack(st_p, j) for j in range(7)) + tuple(stack(st_s, j) for j in range(7))
```

```python
import functools
import math

import numpy as np
import jax
import jax.numpy as jnp
from jax import lax
from jax.experimental import pallas as pl
from jax.experimental.pallas import tpu as pltpu

F32 = jnp.float32
BF16 = jnp.bfloat16

V7X_VMEM_BYTES = 64 * 1024 * 1024
VMEM_LIMIT = V7X_VMEM_BYTES - 8 * 1024 * 1024
SUBLANES = 8
LANES = 128
MXU_COLS = 256

DENSE_ROWS = 512
SCAN_ROWS = 1024
SCAN_SUB_ROWS = 256
LRU_ROWS = 256
SAMPLE_SEQS = 8

EPS = 1e-6
NEG_INF = -1e30
CHUNK = 64
WINDOW = 128
N_BUCKETS = 32
MAX_DIST = 128
A_HEADS, A_KV_HEADS, A_HD = 8, 2, 64
A_GROUP = A_HEADS // A_KV_HEADS
B_HEADS, B_HD = 8, 64
B_W_RANK, B_A_RANK, B_G_RANK = 64, 64, 128
RWKV_GN_EPS = 64e-5
C_HEADS, C_DK, C_DV = 4, 64, 128
C_G_RANK = 16
C_TAU = 16.0
D_BLOCKS = 8
CONV_W = 4
LRU_C = 8.0
BRANCH_W = 512
HG = 4
HGW = HG * B_HD


def _cparams(*sem):
    return pltpu.CompilerParams(dimension_semantics=sem, vmem_limit_bytes=VMEM_LIMIT)


def _resident(shape):
    nd = len(shape)
    return pl.BlockSpec(shape, lambda *_: (0,) * nd, pipeline_mode=pl.Buffered(1))


def _row_tile(n, cap):
    t = min(n, cap)
    while n % t:
        t //= 2
    return t


def _dot(a, b):
    return jnp.dot(a.astype(BF16), b.astype(BF16), preferred_element_type=F32)


def _dot_nt(a, b):
    return lax.dot_general(a.astype(BF16), b.astype(BF16), (((1,), (1,)), ((), ())),
                           preferred_element_type=F32)


def _dot_tn(a, b):
    return lax.dot_general(a.astype(BF16), b.astype(BF16), (((0,), (0,)), ((), ())),
                           preferred_element_type=F32)


def _rms(x, g):
    return x * lax.rsqrt(jnp.mean(x * x, axis=-1, keepdims=True) + EPS) * g


def _sigmoid(x):
    return 1.0 / (1.0 + jnp.exp(-x))


def _softplus(x):
    return jnp.maximum(x, 0.0) + jnp.log1p(jnp.exp(-jnp.abs(x)))


def _softplus_big(x):
    return jnp.maximum(x, 0.0) + jnp.log(1.0 + jnp.exp(-jnp.abs(x)))


def _split_dot(e_lhs, x, terms):
    acc = None
    rem = x
    for n in range(terms):
        piece = rem.astype(BF16)
        d = jnp.dot(e_lhs, piece, preferred_element_type=F32)
        acc = d if acc is None else acc + d
        if n + 1 < terms:
            rem = rem - piece.astype(F32)
    return acc


def _segsum(x, e):
    blk = e.shape[0]
    xb = x.astype(BF16)
    cols = [jnp.dot(xb[:, j:j + blk], e, preferred_element_type=F32)
            for j in range(0, x.shape[1], blk)]
    return cols[0] if len(cols) == 1 else jnp.concatenate(cols, axis=1)


def _tril_ones(n):
    r = lax.broadcasted_iota(jnp.int32, (n, n), 0)
    c = lax.broadcasted_iota(jnp.int32, (n, n), 1)
    return jnp.where(r >= c, 1.0, 0.0).astype(BF16)


def _bd_rows(x, blk, nblk):
    lane_blk = lax.broadcasted_iota(jnp.int32, x.shape, 1) // blk
    return jnp.concatenate([jnp.where(lane_blk == h, x, 0.0) for h in range(nblk)], axis=0)


def _block_diag_value(blocks):
    n = len(blocks)
    a, b = blocks[0].shape
    rows = []
    for h, blk in enumerate(blocks):
        parts = ([jnp.zeros((a, h * b), F32)] if h else []) + [blk] \
            + ([jnp.zeros((a, (n - 1 - h) * b), F32)] if h < n - 1 else [])
        rows.append(jnp.concatenate(parts, axis=1))
    return jnp.concatenate(rows, axis=0)


def _transpose_exact(x):
    n = x.shape[0]
    r = lax.broadcasted_iota(jnp.int32, (n, n), 0)
    c = lax.broadcasted_iota(jnp.int32, (n, n), 1)
    eye = jnp.where(r == c, 1.0, 0.0).astype(BF16)
    acc = None
    rem = x
    for step in range(3):
        piece = rem.astype(BF16)
        d = lax.dot_general(piece, eye, (((0,), (0,)), ((), ())), preferred_element_type=F32)
        acc = d if acc is None else acc + d
        if step < 2:
            rem = rem - piece.astype(F32)
    return acc


def _seg_matrix(width, seg):
    i = np.arange(width)
    return jnp.asarray((i[:, None] // seg) == (i[None, :] // seg), dtype=BF16)


def _ffn_chunks(f):
    tiles = -(-f // MXU_COLS)
    cut = min(f, (tiles + 1) // 2 * MXU_COLS)
    return [(0, cut)] + ([(cut, f - cut)] if cut < f else [])


def _swiglu_residual(x, g_ref, wg_ref, wu_ref, wd_ref):
    h = _rms(x, g_ref[...]).astype(BF16)
    acc = None
    for start, size in _ffn_chunks(wg_ref.shape[1]):
        sl = pl.ds(start, size)
        gt = jnp.dot(h, wg_ref[:, sl], preferred_element_type=F32)
        up = jnp.dot(h, wu_ref[:, sl], preferred_element_type=F32)
        act = (gt * _sigmoid(gt) * up).astype(BF16)
        d = jnp.dot(act, wd_ref[sl, :], preferred_element_type=F32)
        acc = d if acc is None else acc + d
    return x + 0.5 * acc


def _ffn_body(x_ref, g_ref, wg_ref, wu_ref, wd_ref, o_ref):
    o_ref[...] = _swiglu_residual(x_ref[...], g_ref, wg_ref, wu_ref, wd_ref)


def _ffn_ple_body(x_ref, pe_ref, g_ref, wg_ref, wu_ref, wd_ref, gp_ref, wpg_ref, wpp_ref, o_ref):
    x = _swiglu_residual(x_ref[...], g_ref, wg_ref, wu_ref, wd_ref)
    h = _rms(x, gp_ref[...]).astype(BF16)
    gate = _sigmoid(jnp.dot(h, wpg_ref[...], preferred_element_type=F32))
    o_ref[...] = x + gate * jnp.dot(pe_ref[...].astype(BF16), wpp_ref[...], preferred_element_type=F32)


def _ffn(x, g, wg, wu, wd):
    n, d = x.shape
    f = wg.shape[1]
    tm = _row_tile(n, DENSE_ROWS)
    return pl.pallas_call(
        _ffn_body,
        grid=(n // tm,),
        in_specs=[pl.BlockSpec((tm, d), lambda i: (i, 0)),
                  _resident((1, d)), _resident((d, f)), _resident((d, f)), _resident((f, d))],
        out_specs=pl.BlockSpec((tm, d), lambda i: (i, 0)),
        out_shape=jax.ShapeDtypeStruct((n, d), F32),
        compiler_params=_cparams("parallel"),
        name="ffn",
    )(x, g, wg, wu, wd)


def _inproj_body(x_ref, g_ref, wa_ref, wb_ref, wc_ref, wd_ref, pa_ref, pb_ref, pc_ref, pd_ref):
    h = _rms(x_ref[...], g_ref[...]).astype(BF16)
    for w_ref, p_ref in ((wa_ref, pa_ref), (wb_ref, pb_ref), (wc_ref, pc_ref), (wd_ref, pd_ref)):
        p_ref[...] = jnp.dot(h, w_ref[...], preferred_element_type=F32)


def _inproj(x, g, ws):
    n, d = x.shape
    tm = _row_tile(n, DENSE_ROWS)
    widths = [w.shape[1] for w in ws]
    return pl.pallas_call(
        _inproj_body,
        grid=(n // tm,),
        in_specs=[pl.BlockSpec((tm, d), lambda i: (i, 0)), _resident((1, d))]
                 + [_resident((d, w)) for w in widths],
        out_specs=[pl.BlockSpec((tm, w), lambda i: (i, 0)) for w in widths],
        out_shape=[jax.ShapeDtypeStruct((n, w), F32) for w in widths],
        compiler_params=_cparams("parallel"),
        name="inproj",
    )(x, g, *ws)


def _merge_body(x_ref, oa_ref, ob_ref, oc_ref, od_ref, g_ref, wmg_ref, wb_ref, wo_ref, o_ref):
    x = x_ref[...]
    h = _rms(x, g_ref[...]).astype(BF16)
    y = None
    for n, b_ref in enumerate((oa_ref, ob_ref, oc_ref, od_ref)):
        gate = _sigmoid(jnp.dot(h, wmg_ref[n], preferred_element_type=F32))
        t = gate * jnp.dot(b_ref[...].astype(BF16), wb_ref[n], preferred_element_type=F32)
        y = t if y is None else y + t
    o_ref[...] = x + jnp.dot(y.astype(BF16), wo_ref[...], preferred_element_type=F32)


def _merge(x, outs, g, wmg, wb, wo):
    n, d = x.shape
    bw = outs[0].shape[1]
    tm = _row_tile(n, DENSE_ROWS)
    return pl.pallas_call(
        _merge_body,
        grid=(n // tm,),
        in_specs=[pl.BlockSpec((tm, d), lambda i: (i, 0))]
                 + [pl.BlockSpec((tm, bw), lambda i: (i, 0))] * 4
                 + [_resident((1, d)), _resident(wmg.shape), _resident(wb.shape), _resident(wo.shape)],
        out_specs=pl.BlockSpec((tm, d), lambda i: (i, 0)),
        out_shape=jax.ShapeDtypeStruct((n, d), F32),
        compiler_params=_cparams("parallel"),
        name="merge",
    )(x, *outs, g, wmg, wb, wo)


def _ffn_ple(x, pe_all, layer, g, wg, wu, wd, gp, wpg, wpp):
    n, d = x.shape
    f = wg.shape[1]
    pd = pe_all.shape[2]
    tm = _row_tile(n, DENSE_ROWS)
    return pl.pallas_call(
        _ffn_ple_body,
        grid=(n // tm,),
        in_specs=[pl.BlockSpec((tm, d), lambda i: (i, 0)),
                  pl.BlockSpec((None, tm, pd), lambda i: (layer, i, 0)),
                  _resident((1, d)), _resident((d, f)), _resident((d, f)), _resident((f, d)),
                  _resident((1, d)), _resident((d, d)), _resident((pd, d))],
        out_specs=pl.BlockSpec((tm, d), lambda i: (i, 0)),
        out_shape=jax.ShapeDtypeStruct((n, d), F32),
        compiler_params=_cparams("parallel"),
        name="ffn_ple",
    )(x, pe_all, g, wg, wu, wd, gp, wpg, wpp)


def _t5_bucket_np(rel):
    half = N_BUCKETS // 2
    max_exact = half // 2
    ret = np.where(rel > 0, half, 0)
    n = np.abs(rel)
    nf = np.maximum(n, 1).astype(np.float32)
    large = max_exact + (np.log(nf / np.float32(max_exact)) / np.float32(math.log(MAX_DIST / max_exact))
                         * np.float32(half - max_exact)).astype(np.int32)
    large = np.minimum(large, half - 1)
    return (ret + np.where(n < max_exact, n, large)).astype(np.int32)


def _bias_body(idx_ref, table_ref, o_ref):
    idx = idx_ref[...]
    for h in range(A_HEADS):
        acc = jnp.zeros(idx.shape, F32)
        for b in range(N_BUCKETS):
            acc = jnp.where(idx == b, table_ref[b, h], acc)
        o_ref[h] = acc


def _rel_bias(table, n_q, n_k):
    rel = np.arange(n_k)[:, None] - WINDOW - np.arange(n_q)[None, :]
    idx = jnp.asarray(_t5_bucket_np(rel))
    out = pl.pallas_call(
        _bias_body,
        in_specs=[pl.BlockSpec(memory_space=pltpu.VMEM), pl.BlockSpec(memory_space=pltpu.SMEM)],
        out_specs=pl.BlockSpec(memory_space=pltpu.VMEM),
        out_shape=jax.ShapeDtypeStruct((A_HEADS, n_k, n_q), F32),
        name="rel_bias",
    )(idx, table)
    return jnp.transpose(out, (1, 0, 2)).reshape(n_k, A_HEADS * n_q)


def _head_rms(x, e, g):
    ms = _segsum(x * x, e) * (1.0 / A_HD)
    return x * lax.rsqrt(ms + EPS) * g


def _attend_blocks(q_list, k_list, v_list, bias, sink, valid_list, fill=lambda: None):
    def staged(f, *seqs):
        out = []
        for j, xs in enumerate(zip(*seqs)):
            out.append(f(*xs))
            if j % 4 == 3:
                fill()
        return out

    n = q_list[0].shape[0]
    gcols = A_GROUP * n
    inst = [(b, g) for b in range(len(q_list)) for g in range(A_KV_HEADS)]
    head = lambda x, h: x[:, h * A_HD:(h + 1) * A_HD]
    q_st = staged(lambda bg: jnp.concatenate([head(q_list[bg[0]], bg[1] * A_GROUP + r)
                                              for r in range(A_GROUP)], axis=0), inst)
    kg = staged(lambda bg: head(k_list[bg[0]], bg[1]), inst)
    vg = staged(lambda bg: head(v_list[bg[0]], bg[1]), inst)
    bias_g = [bias[:, g * gcols:(g + 1) * gcols] for g in range(A_KV_HEADS)]
    sink_g = [sink[:, g * gcols:(g + 1) * gcols] for g in range(A_KV_HEADS)]

    def scores(bg, k, qs):
        s = _dot_nt(k, qs) + bias_g[bg[1]]
        valid = valid_list[bg[0]]
        return s if valid is None else jnp.where(valid, s, NEG_INF)

    s = staged(scores, inst, kg, q_st)
    m = staged(lambda bg, x: jnp.maximum(jnp.max(x, axis=0, keepdims=True), sink_g[bg[1]]), inst, s)
    e = staged(lambda x, mm: jnp.exp(x - mm), s, m)
    rinv = [1.0 / (jnp.sum(x, axis=0, keepdims=True) + jnp.exp(sink_g[bg[1]] - mm))
            for bg, x, mm in zip(inst, e, m)]
    p = staged(lambda x, r: x * r, e, rinv)
    og = staged(_dot_tn, p, vg)
    outs = []
    for b in range(len(q_list)):
        pieces = [og[b * A_KV_HEADS + g][r * n:(r + 1) * n] for g in range(A_KV_HEADS)
                  for r in range(A_GROUP)]
        outs.append(jnp.concatenate(pieces, axis=1))
    return outs


def _attn_prompt_block(pa, kv_tail_ref, t_idx, bias_ref, sink_ref, qn_ref, kn_ref, eq_ref, ek_ref,
                       tb, fill):
    qw = A_HEADS * A_HD
    kw = A_KV_HEADS * A_HD
    band = WINDOW + CHUNK
    q = _head_rms(pa[:, 0:qw], eq_ref[...], qn_ref[...]) * (A_HD ** -0.5)
    kf = jnp.concatenate([kv_tail_ref[0], _head_rms(pa[:, qw:qw + kw], ek_ref[...], kn_ref[...])], axis=0)
    vf = jnp.concatenate([kv_tail_ref[1], pa[:, qw + kw:qw + 2 * kw]], axis=0)
    kv_tail_ref[0] = kf[tb:tb + WINDOW, :]
    kv_tail_ref[1] = vf[tb:tb + WINDOW, :]
    kb = kf.astype(BF16)
    vb = vf.astype(BF16)
    kidx = lax.broadcasted_iota(jnp.int32, (band, A_GROUP * CHUNK), 0)
    n_chunks = tb // CHUNK
    valid = [(kidx + (t_idx * tb + c * CHUNK - WINDOW)) >= 0 if c * CHUNK < WINDOW else None
             for c in range(n_chunks)]
    return _attend_blocks([q[c * CHUNK:(c + 1) * CHUNK] for c in range(n_chunks)],
                          [kb[c * CHUNK:c * CHUNK + band] for c in range(n_chunks)],
                          [vb[c * CHUNK:c * CHUNK + band] for c in range(n_chunks)],
                          bias_ref[...], sink_ref[...], valid, fill)


def _attn_sample_body(pa_ref, ck_ref, cv_ref, bias_ref, sink_ref, qn_ref, kn_ref, eq_ref, ek_ref,
                      o_ref, kout_ref, vout_ref, *, s, gb):
    qw = A_HEADS * A_HD
    kw = A_KV_HEADS * A_HD
    q = _head_rms(pa_ref[:, 0:qw], eq_ref[...], qn_ref[...]) * (A_HD ** -0.5)
    kn = _head_rms(pa_ref[:, qw:qw + kw], ek_ref[...], kn_ref[...])
    vn = pa_ref[:, qw + kw:qw + 2 * kw]
    q_list, k_list, v_list = [], [], []
    for b in range(gb):
        kf = jnp.concatenate([ck_ref[b], kn[b * s:(b + 1) * s]], axis=0)
        vf = jnp.concatenate([cv_ref[b], vn[b * s:(b + 1) * s]], axis=0)
        kout_ref[b] = kf[s:s + WINDOW, :]
        vout_ref[b] = vf[s:s + WINDOW, :]
        q_list.append(q[b * s:(b + 1) * s])
        k_list.append(kf.astype(BF16))
        v_list.append(vf.astype(BF16))
    outs = _attend_blocks(q_list, k_list, v_list, bias_ref[...], sink_ref[...], [None] * gb)
    for b in range(gb):
        o_ref[b * s:(b + 1) * s, :] = outs[b]


def _attn_sample(pa, ck, cv, bias, sink, qn, kn, nb, s):
    qw, kw = A_HEADS * A_HD, A_KV_HEADS * A_HD
    width = pa.shape[1]
    gb = _row_tile(nb, SAMPLE_SEQS)
    return pl.pallas_call(
        functools.partial(_attn_sample_body, s=s, gb=gb),
        grid=(nb // gb,),
        in_specs=[pl.BlockSpec((gb * s, width), lambda b: (b, 0)),
                  pl.BlockSpec((gb, WINDOW, kw), lambda b: (b, 0, 0)),
                  pl.BlockSpec((gb, WINDOW, kw), lambda b: (b, 0, 0)),
                  _resident(bias.shape), _resident(sink.shape),
                  _resident((1, qw)), _resident((1, kw)), _resident((HGW, HGW)), _resident((kw, kw))],
        out_specs=[pl.BlockSpec((gb * s, qw), lambda b: (b, 0)),
                   pl.BlockSpec((gb, WINDOW, kw), lambda b: (b, 0, 0)),
                   pl.BlockSpec((gb, WINDOW, kw), lambda b: (b, 0, 0))],
        out_shape=[jax.ShapeDtypeStruct((nb * s, qw), F32),
                   jax.ShapeDtypeStruct((nb, WINDOW, kw), F32),
                   jax.ShapeDtypeStruct((nb, WINDOW, kw), F32)],
        compiler_params=_cparams("parallel"),
        name="attn_sample",
    )(pa, ck, cv, bias, sink, qn, kn, _seg_matrix(HGW, A_HD), _seg_matrix(kw, A_HD))


def _rwkv_chunk_terms(r, lw, k, v, kk, a, fill=lambda: None):
    stage_count = [0]

    def each(f, *seqs):
        out = [f(*xs) for xs in zip(*seqs)]
        stage_count[0] += 1
        if stage_count[0] % 3 == 0:
            fill()
        return out

    cat0 = lambda *xs: jnp.concatenate(xs, axis=0)
    cat1 = lambda *xs: jnp.concatenate(xs, axis=1)
    bd = lambda x: _bd_rows(x, B_HD, HG)
    L = r[0].shape[0]
    tril = _tril_ones(L)
    c = each(lambda x: _split_dot(tril, x, 3), lw)
    c_last = each(lambda x: x[L - 1:L, :], c)
    e_last = each(lambda x, xl: jnp.exp(xl - x), c, c_last)
    beta = each(lambda x, y: x * y, kk, a)
    at = each(lambda x, cc, l: -x * jnp.exp(cc - l), kk, c, lw)
    rt = each(lambda x, cc: x * jnp.exp(cc), r, c)
    enc = each(lambda cc: jnp.exp(-cc), c)
    ar = each(cat0, at, rt)
    mb = each(lambda x, b, e: _dot_nt(x, bd(b * e)), ar, beta, enc)
    mk = each(lambda x, b, e: _dot_nt(x, bd(b * e)), ar, k, enc)
    row = lax.broadcasted_iota(jnp.int32, (L, HG * L), 0)
    col = lax.broadcasted_iota(jnp.int32, (L, HG * L), 1) % L
    strict = col < row
    incl = col <= row
    m_b = each(lambda x: jnp.where(strict, x[:L], 0.0), mb)
    m_k = each(lambda x: jnp.where(strict, x[:L], 0.0), mk)
    n_b = each(lambda x: jnp.where(incl, x[L:], 0.0), mb)
    n_k = each(lambda x: jnp.where(incl, x[L:], 0.0), mk)
    eye = jnp.where(col == row, 1.0, 0.0)
    t_inv = each(lambda x: eye + x, m_b)
    p = m_b
    for lvl in range(1, int(math.log2(L))):
        p_bd = each(lambda x: _bd_rows(x, L, HG), p)
        if lvl == 1:
            p = each(_dot, p, p_bd)
        else:
            tp = each(lambda t, x, xb: _dot(cat0(t, x), xb), t_inv, p, p_bd)
            t_inv = each(lambda t, x: t + x[:L], t_inv, tp)
            p = each(lambda x: x[L:], tp)
    t_inv = each(lambda t, x: t + _dot(t, _bd_rows(x, L, HG)), t_inv, p)
    v_bd = each(bd, v)
    mnv = each(lambda x, y, vb: _dot(cat0(x, y), vb), m_k, n_k, v_bd)
    mkv = each(lambda x: x[:L], mnv)
    wu = each(lambda t, x, y: _dot(t, cat1(bd(x), bd(y))), t_inv, at, mkv)
    w1 = each(lambda x: x[:, :HGW], wu)
    u0 = each(lambda x: x[:, HGW:], wu)
    nbo = each(lambda n, x, y: _dot(n, cat1(bd(x), bd(y))), n_b, w1, u0)
    q1 = each(lambda x, y: x + y[:, :HGW], rt, nbo)
    y0 = each(lambda x, y: x[:, HGW:] + y[L:], nbo, mnv)
    bh = each(lambda x, y: x * y, beta, e_last)
    kh = each(lambda x, y: x * y, k, e_last)
    ri = lax.broadcasted_iota(jnp.int32, (HGW, HGW), 0) // B_HD
    ci = lax.broadcasted_iota(jnp.int32, (HGW, HGW), 1) // B_HD
    diag = ri == ci
    g_mat = each(lambda x, y: jnp.where(diag, _dot_tn(x, y), 0.0), w1, bh)
    h_mat = each(lambda x, y, z, w: jnp.where(diag, _dot_tn(cat0(x, y), cat0(z, w)), 0.0), u0, v, bh, kh)
    gam = each(jnp.exp, c_last)
    return q1, y0, g_mat, h_mat, gam


def _rwkv_feature_stages(shifted, in_refs, e):
    mu_ref, w0_ref, w2_ref, a0_ref, a2_ref, g2_ref, kk_ref, ka_ref = in_refs
    W = BRANCH_W
    f = {}

    def shift():
        p, prev = shifted()
        f['x'] = p + (prev - p) * mu_ref[...]
        f['r'], f['v'] = f['x'][:, 0:W], f['x'][:, 2 * W:3 * W]

    def decay():
        wal = f['x'][:, 3 * W:3 * W + LANES]
        w_log = -_softplus_big(-(w0_ref[...] + _dot(jnp.tanh(wal), w2_ref[...]))) - 0.5
        f['lw'] = -jnp.exp(w_log)

    def gates():
        wal = f['x'][:, 3 * W:3 * W + LANES]
        gl = f['x'][:, 3 * W + LANES:3 * W + 2 * LANES]
        f['a'] = _sigmoid(a0_ref[...] + _dot(wal, a2_ref[...]))
        f['g'] = _dot(_sigmoid(gl), g2_ref[...])

    def keys():
        k = f['x'][:, W:2 * W]
        kkr = k * kk_ref[...]
        f['kk'] = kkr / jnp.maximum(jnp.sqrt(_segsum(kkr * kkr, e)), 1e-12)
        f['k2'] = k * (1.0 + (f['a'] - 1.0) * ka_ref[...])

    return f, [shift, decay, gates, keys]


def _rwkv_output_stages(f, y, store, out_refs, e):
    rk_ref, lng_ref, lnb_ref = out_refs
    t = {}

    def center():
        t['yc'] = y - _segsum(y, e) * (1.0 / B_HD)

    def norm():
        var = _segsum(t['yc'] * t['yc'], e) * (1.0 / B_HD)
        t['yn'] = t['yc'] * lax.rsqrt(var + RWKV_GN_EPS) * lng_ref[...] + lnb_ref[...]

    def out():
        bonus = _segsum(f['r'] * f['k2'] * rk_ref[...], e) * f['v']
        store((t['yn'] + bonus) * f['g'])

    return [center, norm, out]


def _rwkv_short_body(pb_ref, shift_ref, s0_ref, mu_ref, w0_ref, w2_ref, a0_ref, a2_ref, g2_ref, kk_ref,
                     ka_ref, rk_ref, lng_ref, lnb_ref, e_ref,
                     o_ref, sout_ref, ext_ref, *, t, g):
    n_groups = BRANCH_W // HGW
    e = e_ref[...]
    for q in range(g):
        ext_ref[q, 0:SUBLANES, :] = jnp.broadcast_to(shift_ref[q], (SUBLANES, ext_ref.shape[2]))
        ext_ref[q, SUBLANES:, :] = pb_ref[q * t:(q + 1) * t, :]

    def shifted():
        prev = jnp.concatenate([ext_ref[q, SUBLANES - 1:SUBLANES - 1 + t, :] for q in range(g)], axis=0)
        return pb_ref[...], prev

    f, stages = _rwkv_feature_stages(
        shifted, (mu_ref, w0_ref, w2_ref, a0_ref, a2_ref, g2_ref, kk_ref, ka_ref), e)
    for stage in stages:
        stage()

    def piece(x, q, gi):
        return jnp.concatenate([x[q * t:(q + 1) * t, gi * HGW:(gi + 1) * HGW],
                                jnp.zeros((CHUNK - t, HGW), F32)], axis=0)

    inst = [(q, gi) for q in range(g) for gi in range(n_groups)]
    terms = _rwkv_chunk_terms(*[[piece(f[name], q, gi) for q, gi in inst]
                                for name in ('r', 'lw', 'k2', 'v', 'kk', 'a')])
    y_rows = []
    for q in range(g):
        y_cols = []
        for gi in range(n_groups):
            q1, y0, g_mat, h_mat, gam = [tm[q * n_groups + gi] for tm in terms]
            s = _block_diag_value([s0_ref[q, gi * HG + h] for h in range(HG)])
            y_cols.append(_dot_nt(q1, s) + y0)
            s = s * gam + _dot(s, g_mat) + h_mat
            for h in range(HG):
                sout_ref[q, gi * HG + h] = s[h * B_HD:(h + 1) * B_HD, h * B_HD:(h + 1) * B_HD]
        y_rows.append(jnp.concatenate(y_cols, axis=1)[0:t])

    def store(val):
        o_ref[...] = val

    for stage in _rwkv_output_stages(f, jnp.concatenate(y_rows, axis=0), store,
                                     (rk_ref, lng_ref, lnb_ref), e):
        stage()


def _rwkv_body(pb_ref, shift_ref, s0_ref, mu_ref, w0_ref, w2_ref, a0_ref, a2_ref, g2_ref, kk_ref,
               ka_ref, rk_ref, lng_ref, lnb_ref, e_ref,
               o_ref, sout_ref,
               ext_ref, s_ref, *, tb, sub, lc):
    t = pl.program_id(1)
    W = BRANCH_W
    n_groups = W // HGW

    @pl.when(t == 0)
    def _():
        ext_ref[0:SUBLANES, :] = jnp.broadcast_to(shift_ref[0], (SUBLANES, ext_ref.shape[1]))
        for gi in range(n_groups):
            s_ref[gi] = _block_diag_value([s0_ref[0, gi * HG + h] for h in range(HG)])

    ext_ref[SUBLANES:, :] = pb_ref[...]

    e = e_ref[...]
    n_sub = tb // sub

    rows_c = lc
    n_chunks = sub // rows_c

    in_refs = (mu_ref, w0_ref, w2_ref, a0_ref, a2_ref, g2_ref, kk_ref, ka_ref)
    out_refs = (rk_ref, lng_ref, lnb_ref)

    def prep_stages(lo):
        def shifted():
            p = pb_ref[lo:lo + sub, :]
            return p, ext_ref[SUBLANES - 1 + lo:SUBLANES - 1 + lo + sub, :]
        return _rwkv_feature_stages(shifted, in_refs, e)

    def post_stages(lo, f, y):
        def store(val):
            o_ref[lo:lo + sub, :] = val
        return _rwkv_output_stages(f, y, store, out_refs, e)

    def piece(x, ci, gi):
        return x[ci * rows_c:(ci + 1) * rows_c, gi * HGW:(gi + 1) * HGW]

    feats = [prep_stages(j * sub) for j in range(n_sub)]
    for stage in feats[0][1]:
        stage()
    pending = []

    def fill():
        if pending:
            pending.pop(0)()

    inst = [(ci, gi) for ci in range(n_chunks) for gi in range(n_groups)]
    for j in range(n_sub):
        f = feats[j][0]
        if j + 1 < n_sub:
            pending.extend(feats[j + 1][1])
        terms = _rwkv_chunk_terms(*[[piece(f[name], ci, gi) for ci, gi in inst]
                                    for name in ('r', 'lw', 'k2', 'v', 'kk', 'a')], fill)
        while pending:
            fill()
        y_rows = []
        for ci in range(n_chunks):
            y_cols = []
            for gi in range(n_groups):
                q1, y0, g_mat, h_mat, gam = [tm[ci * n_groups + gi] for tm in terms]
                s = s_ref[gi]
                y_cols.append(_dot_nt(q1, s) + y0)
                s_ref[gi] = s * gam + _dot(s, g_mat) + h_mat
            y_rows.append(jnp.concatenate(y_cols, axis=1))
        pending.extend(post_stages(j * sub, f, jnp.concatenate(y_rows, axis=0)))
    while pending:
        fill()
    ext_ref[0:SUBLANES, :] = pb_ref[tb - SUBLANES:tb, :]

    @pl.when(t == pl.num_programs(1) - 1)
    def _():
        for gi in range(n_groups):
            s = s_ref[gi]
            for h in range(HG):
                sout_ref[0, gi * HG + h] = s[h * B_HD:(h + 1) * B_HD, h * B_HD:(h + 1) * B_HD]


def _rwkv(pb, shift0, s0, wts, nb, t):
    width = pb.shape[1]
    W = BRANCH_W
    n_groups = W // HGW
    row = lambda n: _resident((1, n))
    weight_specs = [row(width), row(W), _resident((LANES, W)), row(W), _resident((LANES, W)),
                    _resident((B_G_RANK, W)), row(W), row(W), row(W), row(W), row(W),
                    _resident((HGW, HGW))]
    if t < CHUNK:
        g = _row_tile(nb, SAMPLE_SEQS)
        assert t % SUBLANES == 0
        return pl.pallas_call(
            functools.partial(_rwkv_short_body, t=t, g=g),
            grid=(nb // g,),
            in_specs=[pl.BlockSpec((g * t, width), lambda b: (b, 0)),
                      pl.BlockSpec((g, 1, width), lambda b: (b, 0, 0)),
                      pl.BlockSpec((g, B_HEADS, B_HD, B_HD), lambda b: (b, 0, 0, 0))] + weight_specs,
            out_specs=[pl.BlockSpec((g * t, W), lambda b: (b, 0)),
                       pl.BlockSpec((g, B_HEADS, B_HD, B_HD), lambda b: (b, 0, 0, 0))],
            out_shape=[jax.ShapeDtypeStruct((nb * t, W), F32),
                       jax.ShapeDtypeStruct((nb, B_HEADS, B_HD, B_HD), F32)],
            scratch_shapes=[pltpu.VMEM((g, t + SUBLANES, width), F32)],
            compiler_params=_cparams("parallel"),
            name="rwkv7_short",
        )(pb, shift0, s0, *wts, _seg_matrix(HGW, B_HD))
    tb = _row_tile(t, SCAN_ROWS)
    sub = min(tb, SCAN_SUB_ROWS)
    lc = CHUNK
    assert sub % lc == 0
    nt = t // tb
    return pl.pallas_call(
        functools.partial(_rwkv_body, tb=tb, sub=sub, lc=lc),
        grid=(nb, nt),
        in_specs=[pl.BlockSpec((tb, width), lambda b, i: (b * nt + i, 0)),
                  pl.BlockSpec((1, 1, width), lambda b, i: (b, 0, 0)),
                  pl.BlockSpec((1, B_HEADS, B_HD, B_HD), lambda b, i: (b, 0, 0, 0))] + weight_specs,
        out_specs=[pl.BlockSpec((tb, W), lambda b, i: (b * nt + i, 0)),
                   pl.BlockSpec((1, B_HEADS, B_HD, B_HD), lambda b, i: (b, 0, 0, 0))],
        out_shape=[jax.ShapeDtypeStruct((nb * t, W), F32),
                   jax.ShapeDtypeStruct((nb, B_HEADS, B_HD, B_HD), F32)],
        scratch_shapes=[pltpu.VMEM((tb + SUBLANES, width), F32),
                        pltpu.VMEM((n_groups, HGW, HGW), F32)],
        compiler_params=_cparams("parallel", "arbitrary"),
        name="rwkv7",
    )(pb, shift0, s0, *wts, _seg_matrix(HGW, B_HD))


def _gla_chunk_terms(q, k, v, gk):
    each = lambda f, *seqs: [f(*xs) for xs in zip(*seqs)]
    L = q[0].shape[0]
    kw = C_HEADS * C_DK
    vw = C_HEADS * C_DV
    tril = _tril_ones(L)
    b = each(lambda x: _split_dot(tril, x, 3), gk)
    qe = each(lambda x, y: x * jnp.exp(y), q, b)
    ke = each(lambda x, y: x * jnp.exp(-y), k, b)
    a_all = each(lambda x, y: _dot_nt(x, _bd_rows(y, C_DK, C_HEADS)), qe, ke)
    row = lax.broadcasted_iota(jnp.int32, (L, C_HEADS * L), 0)
    col = lax.broadcasted_iota(jnp.int32, (L, C_HEADS * L), 1) % L
    causal = col <= row
    o_intra = each(lambda x, y: _dot(jnp.where(causal, x, 0.0), _bd_rows(y, C_DV, C_HEADS)), a_all, v)
    b_last = each(lambda x: x[L - 1:L, :], b)
    kd = each(lambda x, y, z: x * jnp.exp(z - y), k, b, b_last)
    ri = lax.broadcasted_iota(jnp.int32, (vw, kw), 0) // C_DV
    ci = lax.broadcasted_iota(jnp.int32, (vw, kw), 1) // C_DK
    diag = ri == ci
    upd = each(lambda x, y: jnp.where(diag, _dot_tn(x, y), 0.0), v, kd)
    decay = each(jnp.exp, b_last)
    return qe, o_intra, upd, decay


def _gla_body(pc_ref, s0_ref, g2_ref, gb_ref, norm_ref, o_ref, sout_ref,
              s_ref, *, tb, lc, g):
    t = pl.program_id(1)
    kw = C_HEADS * C_DK
    vw = C_HEADS * C_DV

    def state_in(qi):
        return _block_diag_value([_transpose_exact(s0_ref[qi, h]) for h in range(C_HEADS)])

    def state_out(qi, st):
        for h in range(C_HEADS):
            sout_ref[qi, h] = _transpose_exact(st[h * C_DV:(h + 1) * C_DV, h * C_DK:(h + 1) * C_DK])

    if g == 1:
        @pl.when(t == 0)
        def _():
            s_ref[...] = state_in(0)

    pc = pc_ref[...]
    og = pc[:, 2 * kw + vw:2 * kw + 2 * vw]
    gl = pc[:, 2 * kw + 2 * vw:2 * kw + 2 * vw + LANES]
    z = _dot(gl, g2_ref[...]) + gb_ref[...]
    q = pc[:, 0:kw] * (C_DK ** -0.5)
    k = pc[:, kw:2 * kw]
    v = pc[:, 2 * kw:2 * kw + vw]
    gk = -_softplus_big(-z) * (1.0 / C_TAU)
    rows_q = tb // g
    n_chunks = max(rows_q // lc, 1)

    def piece(x, qi, ci):
        lo = qi * rows_q + ci * lc
        blk = x[lo:lo + min(lc, rows_q), :]
        if rows_q < lc:
            blk = jnp.concatenate([blk, jnp.zeros((lc - rows_q, x.shape[1]), F32)], axis=0)
        return blk

    inst = [(qi, ci) for qi in range(g) for ci in range(n_chunks)]
    qe, o_intra, upd, decay = _gla_chunk_terms(
        *[[piece(x, qi, ci) for qi, ci in inst] for x in (q, k, v, gk)])
    y_rows = []
    for qi in range(g):
        st = s_ref[...] if g == 1 else state_in(qi)
        for ci in range(n_chunks):
            j = qi * n_chunks + ci
            y_rows.append((_dot_nt(qe[j], st) + o_intra[j])[0:min(lc, rows_q)])
            st = st * decay[j] + upd[j]
        if g == 1:
            s_ref[...] = st
        else:
            state_out(qi, st)
    y = jnp.concatenate(y_rows, axis=0)
    outs = []
    for h in range(C_HEADS):
        yh = y[:, h * C_DV:(h + 1) * C_DV]
        outs.append(yh * lax.rsqrt(jnp.mean(yh * yh, axis=-1, keepdims=True) + EPS) * norm_ref[...])
    yn = jnp.concatenate(outs, axis=1)
    o_ref[...] = yn * (og * _sigmoid(og))

    if g == 1:
        @pl.when(t == pl.num_programs(1) - 1)
        def _():
            state_out(0, s_ref[...])


def _gla(pc, s0, g2, gb, norm, nb, t):
    lc = CHUNK
    g = _row_tile(nb, SAMPLE_SEQS) if t < lc else 1
    tb = g * t if t < lc else _row_tile(t, SCAN_ROWS)
    assert tb % lc == 0 or t < lc
    nt = (g * t) // tb
    width = pc.shape[1]
    kw, vw = C_HEADS * C_DK, C_HEADS * C_DV
    return pl.pallas_call(
        functools.partial(_gla_body, tb=tb, lc=lc, g=g),
        grid=(nb // g, nt),
        in_specs=[pl.BlockSpec((tb, width), lambda b, i: (b * nt + i, 0)),
                  pl.BlockSpec((g, C_HEADS, C_DK, C_DV), lambda b, i: (b, 0, 0, 0)),
                  _resident((LANES, kw)), _resident((1, kw)), _resident((1, C_DV))],
        out_specs=[pl.BlockSpec((tb, vw), lambda b, i: (b * nt + i, 0)),
                   pl.BlockSpec((g, C_HEADS, C_DK, C_DV), lambda b, i: (b, 0, 0, 0))],
        out_shape=[jax.ShapeDtypeStruct((nb * t, vw), F32),
                   jax.ShapeDtypeStruct((nb, C_HEADS, C_DK, C_DV), F32)],
        scratch_shapes=[pltpu.VMEM((vw, kw), F32)],
        compiler_params=_cparams("parallel", "arbitrary"),
        name="gla",
    )(pc, s0, g2, gb, norm)


def _shift_rows(x, d, fill):
    n = x.shape[0]
    if d % SUBLANES == 0:
        head = jnp.full((d, x.shape[1]), fill, x.dtype)
        return jnp.concatenate([head, x[:n - d]], axis=0)
    rolled = pltpu.roll(x, d, 0)
    row = lax.broadcasted_iota(jnp.int32, x.shape, 0)
    return jnp.where(row < d, fill, rolled)


def _lru_block(gate, xr, ext_ref, h_ref, au_ref, cw_ref, cb_ref, wa_ref, ba_ref, wx_ref, bx_ref,
               lam_ref, tb, fill=lambda: None):
    W = BRANCH_W
    ext_ref[SUBLANES:, :] = xr
    xc = cb_ref[...] + xr * cw_ref[CONV_W - 1:CONV_W, :]
    for j in range(CONV_W - 1):
        off = SUBLANES - (CONV_W - 1) + j
        xc = xc + ext_ref[off:off + tb, :] * cw_ref[j:j + 1, :]
    tail = ext_ref[tb:tb + SUBLANES, :]
    ext_ref[0:SUBLANES, :] = tail
    fill()

    def block_dot(w_ref):
        return jnp.concatenate([_dot(xc[:, j:j + MXU_COLS], w_ref[j:j + MXU_COLS, j:j + MXU_COLS])
                                for j in range(0, W, MXU_COLS)], axis=1)

    r = _sigmoid(block_dot(wa_ref) + ba_ref[...])
    fill()
    gi = _sigmoid(block_dot(wx_ref) + bx_ref[...])
    fill()
    log_a = (-LRU_C) * r * _softplus(-lam_ref[...])
    a = jnp.exp(log_a)
    u = jnp.sqrt(1.0 - jnp.exp(2.0 * log_a)) * (gi * xc)
    fill()
    ng = tb // SUBLANES
    row_in_group = lax.broadcasted_iota(jnp.int32, (ng, SUBLANES, W), 1)
    a = a.reshape(ng, SUBLANES, W)
    u = u.reshape(ng, SUBLANES, W)
    d = 1
    while d < SUBLANES:
        inside = row_in_group >= d
        u = u + a * jnp.where(inside, pltpu.roll(u, d, 1), 0.0)
        a = a * jnp.where(inside, pltpu.roll(a, d, 1), 1.0)
        d *= 2
        fill()
    a = a.reshape(tb, W)
    u = u.reshape(tb, W)
    h_in = h_ref[...]
    if ng % SUBLANES == 0:
        ends = pl.ds(SUBLANES - 1, ng, stride=SUBLANES)
        for j in range(W // LANES):
            au_ref[0, j] = a[:, j * LANES:(j + 1) * LANES]
            au_ref[1, j] = u[:, j * LANES:(j + 1) * LANES]
        ae = jnp.concatenate([au_ref[0, j, ends, :] for j in range(W // LANES)], axis=1)
        ue = jnp.concatenate([au_ref[1, j, ends, :] for j in range(W // LANES)], axis=1)
        d = 1
        while d < ng:
            ue = ue + ae * _shift_rows(ue, d, 0.0)
            ae = ae * _shift_rows(ae, d, 1.0)
            d *= 2
        carry = _shift_rows(ue + ae * h_in, 1, h_in)
        carry_rows = jnp.broadcast_to(carry[:, None, :], (ng, SUBLANES, W)).reshape(tb, W)
    else:
        rows = []
        for g in range(ng):
            rows.append(jnp.broadcast_to(h_in, (SUBLANES, W)))
            end = (g + 1) * SUBLANES - 1
            h_in = u[end:end + 1, :] + a[end:end + 1, :] * h_in
        carry_rows = jnp.concatenate(rows, axis=0)
    h = u + a * carry_rows
    h_last = h[tb - 1:tb, :]
    h_ref[...] = h_last
    fill()
    out = h * (0.5 * gate * (1.0 + jnp.tanh(math.sqrt(2.0 / math.pi)
                                            * (gate + 0.044715 * gate * gate * gate))))
    return out, tail, h_last


def _lru_body(pd_ref, conv0_ref, h0_ref, cw_ref, cb_ref, wa_ref, ba_ref, wx_ref, bx_ref, lam_ref,
              o_ref, convout_ref, hout_ref, ext_ref, h_ref, au_ref, *, tb):
    t = pl.program_id(1)
    W = BRANCH_W

    @pl.when(t == 0)
    def _():
        ext_ref[0:SUBLANES, :] = conv0_ref[0]
        h_ref[...] = h0_ref[0]

    out, tail, h_last = _lru_block(pd_ref[:, 0:W], pd_ref[:, W:2 * W], ext_ref, h_ref, au_ref, cw_ref,
                                   cb_ref, wa_ref, ba_ref, wx_ref, bx_ref, lam_ref, tb)
    o_ref[...] = out

    @pl.when(t == pl.num_programs(1) - 1)
    def _():
        convout_ref[0] = tail
        hout_ref[0] = h_last


def _mix_in_body(x_ref, g_ref, wa_ref, wb_ref, wc_ref, wd_ref,
                 bias_ref, sink_ref, qn_ref, kn_ref, eq_ref, ek_ref,
                 conv0_ref, h0_ref, cw_ref, cb_ref, lwa_ref, lba_ref, lwx_ref, lbx_ref, lam_ref,
                 pb_ref, pc_ref, oa_ref, od_ref, kout_ref, vout_ref, convout_ref, hout_ref,
                 kv_tail_ref, ext_ref, h_ref, au_ref, *, tb):
    t = pl.program_id(1)
    W = BRANCH_W

    @pl.when(t == 0)
    def _():
        kv_tail_ref[...] = jnp.zeros(kv_tail_ref.shape, F32)
        ext_ref[0:SUBLANES, :] = conv0_ref[0]
        h_ref[...] = h0_ref[0]

    h = _rms(x_ref[...], g_ref[...]).astype(BF16)
    pd = jnp.dot(h, wd_ref[...], preferred_element_type=F32)
    pa = jnp.dot(h, wa_ref[...], preferred_element_type=F32)

    half = tb // 2
    pending = [(w_ref, p_ref, c0, min(MXU_COLS, w_ref.shape[1] - c0), r0)
               for w_ref, p_ref in ((wb_ref, pb_ref), (wc_ref, pc_ref))
               for c0 in range(0, w_ref.shape[1], MXU_COLS) for r0 in (0, half)]

    def fill():
        if pending:
            w_ref, p_ref, c0, width, r0 = pending.pop(0)
            p_ref[r0:r0 + half, c0:c0 + width] = jnp.dot(h[r0:r0 + half], w_ref[:, c0:c0 + width],
                                                         preferred_element_type=F32)

    od, tail, h_last = _lru_block(pd[:, 0:W], pd[:, W:2 * W], ext_ref, h_ref, au_ref, cw_ref, cb_ref,
                                  lwa_ref, lba_ref, lwx_ref, lbx_ref, lam_ref, tb, fill)
    od_ref[...] = od
    outs = _attn_prompt_block(pa, kv_tail_ref, t, bias_ref, sink_ref, qn_ref, kn_ref, eq_ref, ek_ref,
                              tb, fill)
    for c, o in enumerate(outs):
        oa_ref[c * CHUNK:(c + 1) * CHUNK, :] = o
    while pending:
        fill()

    @pl.when(t == pl.num_programs(1) - 1)
    def _():
        kout_ref[0] = kv_tail_ref[0]
        vout_ref[0] = kv_tail_ref[1]
        convout_ref[0] = tail
        hout_ref[0] = h_last


def _mix_in(x, g, ws, bias, sink, qn, kn, conv0, h0, lru_wts, nb, t):
    n, d = x.shape
    tb = _row_tile(t, DENSE_ROWS)
    assert tb % WINDOW == 0
    nt = t // tb
    W = BRANCH_W
    qw, kw = A_HEADS * A_HD, A_KV_HEADS * A_HD
    wa, wb, wc, wd = ws
    rows = lambda width: pl.BlockSpec((tb, width), lambda b, i: (b * nt + i, 0))
    per_seq = lambda *shape: pl.BlockSpec((1,) + shape, lambda b, i: (b,) + (0,) * len(shape))
    row = lambda width: _resident((1, width))
    return pl.pallas_call(
        functools.partial(_mix_in_body, tb=tb),
        grid=(nb, nt),
        in_specs=[rows(d), row(d)] + [_resident(w.shape) for w in ws]
                 + [_resident(bias.shape), _resident(sink.shape), row(qw), row(kw),
                    _resident((HGW, HGW)), _resident((kw, kw)),
                    per_seq(SUBLANES, W), per_seq(1, W),
                    _resident((CONV_W, W)), row(W), _resident((W, W)), row(W), _resident((W, W)), row(W),
                    row(W)],
        out_specs=[rows(wb.shape[1]), rows(wc.shape[1]), rows(qw), rows(W),
                   per_seq(WINDOW, kw), per_seq(WINDOW, kw), per_seq(SUBLANES, W), per_seq(1, W)],
        out_shape=[jax.ShapeDtypeStruct((n, wb.shape[1]), F32), jax.ShapeDtypeStruct((n, wc.shape[1]), F32),
                   jax.ShapeDtypeStruct((n, qw), F32), jax.ShapeDtypeStruct((n, W), F32),
                   jax.ShapeDtypeStruct((nb, WINDOW, kw), F32), jax.ShapeDtypeStruct((nb, WINDOW, kw), F32),
                   jax.ShapeDtypeStruct((nb, SUBLANES, W), F32), jax.ShapeDtypeStruct((nb, 1, W), F32)],
        scratch_shapes=[pltpu.VMEM((2, WINDOW, kw), F32),
                        pltpu.VMEM((tb + SUBLANES, W), F32), pltpu.VMEM((1, W), F32),
                        pltpu.VMEM((2, W // LANES, tb, LANES), F32)],
        compiler_params=_cparams("parallel", "arbitrary"),
        name="mix_in",
    )(x, g, wa, wb, wc, wd, bias, sink, qn, kn, _seg_matrix(HGW, A_HD), _seg_matrix(kw, A_HD),
      conv0, h0, *lru_wts)


def _lru(pd, conv0, h0, wts, nb, t):
    tb = _row_tile(t, LRU_ROWS)
    nt = t // tb
    W = BRANCH_W
    row = lambda: _resident((1, W))
    return pl.pallas_call(
        functools.partial(_lru_body, tb=tb),
        grid=(nb, nt),
        in_specs=[pl.BlockSpec((tb, 2 * W), lambda b, i: (b * nt + i, 0)),
                  pl.BlockSpec((1, SUBLANES, W), lambda b, i: (b, 0, 0)),
                  pl.BlockSpec((1, 1, W), lambda b, i: (b, 0, 0)),
                  _resident((CONV_W, W)), row(), _resident((W, W)), row(), _resident((W, W)), row(),
                  row()],
        out_specs=[pl.BlockSpec((tb, W), lambda b, i: (b * nt + i, 0)),
                   pl.BlockSpec((1, SUBLANES, W), lambda b, i: (b, 0, 0)),
                   pl.BlockSpec((1, 1, W), lambda b, i: (b, 0, 0))],
        out_shape=[jax.ShapeDtypeStruct((nb * t, W), F32),
                   jax.ShapeDtypeStruct((nb, SUBLANES, W), F32),
                   jax.ShapeDtypeStruct((nb, 1, W), F32)],
        scratch_shapes=[pltpu.VMEM((tb + SUBLANES, W), F32), pltpu.VMEM((1, W), F32),
                        pltpu.VMEM((2, W // LANES, tb, LANES), F32)],
        compiler_params=_cparams("parallel", "arbitrary"),
        name="rglru",
    )(pd, conv0, h0, *wts)


def _block_diag(w):
    n, a, b = w.shape
    eye = jnp.eye(n, dtype=w.dtype)
    return (eye[:, None, :, None] * w[:, :, None, :]).reshape(n * a, n * b)


def _pad_rows_to(w, rows):
    return jnp.pad(w, ((0, rows - w.shape[0]), (0, 0)))


def _layer_weights(i, W):
    bf = lambda a: a.astype(BF16)
    row = lambda a: a.reshape(1, -1).astype(F32)
    bw = BRANCH_W
    a_cols = A_HEADS * A_HD + 2 * A_KV_HEADS * A_HD
    b_cols = 3 * bw + B_W_RANK + B_A_RANK + B_G_RANK
    c_cols = 2 * C_HEADS * C_DK + C_HEADS * C_DV + C_G_RANK + bw
    w_in = W['w_in'][i]
    d = w_in.shape[0]
    wa = w_in[:, :a_cols]
    wb = w_in[:, a_cols:a_cols + b_cols]
    wc = w_in[:, a_cols + b_cols:a_cols + b_cols + c_cols]
    wd = w_in[:, a_cols + b_cols + c_cols:]
    zpad = lambda n: jnp.zeros((d, n), w_in.dtype)
    assert B_W_RANK + B_A_RANK == LANES
    w2 = jnp.pad(W['rwkv_w2'][i], ((0, B_A_RANK), (0, 0)))
    a2 = jnp.pad(W['rwkv_a2'][i], ((B_W_RANK, 0), (0, 0)))
    qkv = 2 * C_HEADS * C_DK + C_HEADS * C_DV
    wc2 = jnp.concatenate([wc[:, :qkv], wc[:, qkv + C_G_RANK:], wc[:, qkv:qkv + C_G_RANK],
                           zpad(LANES - C_G_RANK)], axis=1)
    return dict(
        ffn1=(row(W['g_ffn1'][i]), bf(W['w_ffn1_gate'][i]), bf(W['w_ffn1_up'][i]), bf(W['w_ffn1_down'][i])),
        ffn2=(row(W['g_ffn2'][i]), bf(W['w_ffn2_gate'][i]), bf(W['w_ffn2_up'][i]), bf(W['w_ffn2_down'][i])),
        g_mix=row(W['g_mix'][i]),
        w_in=(bf(wa), bf(wb), bf(wc2), bf(wd)),
        q_norm=row(jnp.tile(W['q_norm'][i], A_HEADS)),
        k_norm=row(jnp.tile(W['k_norm'][i], A_KV_HEADS)),
        sink=W['attn_sink'][i].astype(F32),
        rwkv=(row(W['rwkv_mu'][i]), row(W['rwkv_w0'][i]), bf(w2),
              row(W['rwkv_a0'][i]), bf(a2), bf(W['rwkv_g2'][i]),
              row(W['rwkv_k_k'][i]), row(W['rwkv_k_a'][i]), row(W['rwkv_r_k'][i]),
              row(W['rwkv_ln_g'][i]), row(W['rwkv_ln_b'][i])),
        gla=(bf(_pad_rows_to(W['gla_g2'][i], LANES)), row(W['gla_gb'][i]), row(W['gla_norm'][i])),
        lru=(W['lru_conv_w'][i].astype(F32), row(W['lru_conv_b'][i]),
             bf(_block_diag(W['lru_wa'][i])), row(W['lru_ba'][i]),
             bf(_block_diag(W['lru_wx'][i])), row(W['lru_bx'][i]), row(W['lru_lambda'][i])),
        merge=(bf(W['w_merge_gate'][i]), bf(W['w_branch'][i]), bf(W['w_out'][i])),
        ple=(row(W['g_ple'][i]), bf(W['w_ple_gate'][i]), bf(W['w_ple_proj'][i])),
    )


def _trunk_layer(x, pe_all, layer, st, lw, bias, first_chunk):
    nb, t, d = x.shape
    n = nb * t
    ck, cv, shift0, s_rwkv0, s_gla0, conv0, lru0 = st
    x2 = x.reshape(n, d)
    x2 = _ffn(x2, *lw['ffn1'])
    kw = A_KV_HEADS * A_HD
    n_q = bias.shape[1] // A_HEADS
    sink2 = jnp.repeat(lw['sink'], n_q).reshape(1, A_HEADS * n_q)
    conv_pad = jnp.pad(conv0, ((0, 0), (SUBLANES - (CONV_W - 1), 0), (0, 0)))
    if first_chunk:
        pb, pc, o_a, o_d, k_win, v_win, conv_out, h_out = _mix_in(
            x2, lw['g_mix'], lw['w_in'], bias, sink2, lw['q_norm'], lw['k_norm'],
            conv_pad, lru0[:, None, :], lw['lru'], nb, t)
    else:
        pa, pb, pc, pd = _inproj(x2, lw['g_mix'], lw['w_in'])
        o_a, k_win, v_win = _attn_sample(pa, ck.reshape(nb, WINDOW, kw), cv.reshape(nb, WINDOW, kw),
                                         bias, sink2, lw['q_norm'], lw['k_norm'], nb, t)
        o_d, conv_out, h_out = _lru(pd, conv_pad, lru0[:, None, :], lw['lru'], nb, t)
    k_win = k_win.reshape(nb, WINDOW, A_KV_HEADS, A_HD)
    v_win = v_win.reshape(nb, WINDOW, A_KV_HEADS, A_HD)

    o_b, s_rwkv1 = _rwkv(pb, shift0[:, None, :], s_rwkv0, lw['rwkv'], nb, t)
    shift1 = pb.reshape(nb, t, -1)[:, -1]
    o_c, s_gla1 = _gla(pc, s_gla0, *lw['gla'], nb, t)
    conv1 = conv_out[:, SUBLANES - (CONV_W - 1):]
    lru1 = h_out[:, 0]

    x2 = _merge(x2, (o_a, o_b, o_c, o_d), lw['g_mix'], *lw['merge'])
    x2 = _ffn_ple(x2, pe_all.reshape(pe_all.shape[0], n, -1), layer, *lw['ffn2'], *lw['ple'])
    return x2.reshape(nb, t, d), (k_win, v_win, shift1, s_rwkv1, s_gla1, conv1, lru1)


def kernel(x_prompt, x_sample, cache_attn_k, cache_attn_v, state_rwkv_shift, state_rwkv, state_gla, state_lru_conv, state_lru, p_prompt, p_sample, rel_bias_table, g_ffn1, w_ffn1_gate, w_ffn1_up, w_ffn1_down, g_mix, w_in, q_norm, k_norm, attn_sink, rwkv_mu, rwkv_w0, rwkv_w2, rwkv_a0, rwkv_a2, rwkv_g2, rwkv_k_k, rwkv_k_a, rwkv_r_k, rwkv_ln_g, rwkv_ln_b, gla_g2, gla_gb, gla_norm, lru_conv_w, lru_conv_b, lru_wa, lru_ba, lru_wx, lru_bx, lru_lambda, w_merge_gate, w_branch, w_out, g_ffn2, w_ffn2_gate, w_ffn2_up, w_ffn2_down, g_ple, w_ple_gate, w_ple_proj):
    W = dict(g_ffn1=g_ffn1, w_ffn1_gate=w_ffn1_gate, w_ffn1_up=w_ffn1_up, w_ffn1_down=w_ffn1_down,
             g_mix=g_mix, w_in=w_in, q_norm=q_norm, k_norm=k_norm, attn_sink=attn_sink,
             rwkv_mu=rwkv_mu, rwkv_w0=rwkv_w0, rwkv_w2=rwkv_w2, rwkv_a0=rwkv_a0, rwkv_a2=rwkv_a2,
             rwkv_g2=rwkv_g2, rwkv_k_k=rwkv_k_k, rwkv_k_a=rwkv_k_a, rwkv_r_k=rwkv_r_k,
             rwkv_ln_g=rwkv_ln_g, rwkv_ln_b=rwkv_ln_b, gla_g2=gla_g2, gla_gb=gla_gb,
             gla_norm=gla_norm, lru_conv_w=lru_conv_w, lru_conv_b=lru_conv_b, lru_wa=lru_wa,
             lru_ba=lru_ba, lru_wx=lru_wx, lru_bx=lru_bx, lru_lambda=lru_lambda,
             w_merge_gate=w_merge_gate, w_branch=w_branch, w_out=w_out, g_ffn2=g_ffn2,
             w_ffn2_gate=w_ffn2_gate, w_ffn2_up=w_ffn2_up, w_ffn2_down=w_ffn2_down,
             g_ple=g_ple, w_ple_gate=w_ple_gate, w_ple_proj=w_ple_proj)
    depth = w_in.shape[0]
    dt = x_prompt.dtype
    bp, tp = x_prompt.shape[:2]
    ts = x_sample.shape[1]
    b_cols = state_rwkv_shift.shape[-1]
    bias_p = _rel_bias(rel_bias_table, CHUNK, WINDOW + CHUNK)
    bias_s = _rel_bias(rel_bias_table, ts, WINDOW + ts)
    yp, ys = x_prompt, x_sample
    st_p, st_s = [], []
    for i in range(depth):
        lw = _layer_weights(i, W)
        zero_st = (None, None,
                   jnp.zeros((bp, b_cols), dt),
                   jnp.zeros((bp, B_HEADS, B_HD, B_HD), dt),
                   jnp.zeros((bp, C_HEADS, C_DK, C_DV), dt),
                   jnp.zeros((bp, CONV_W - 1, BRANCH_W), dt),
                   jnp.zeros((bp, BRANCH_W), dt))
        yp, sp = _trunk_layer(yp, p_prompt, i, zero_st, lw, bias_p, True)
        cache_st = (cache_attn_k[i], cache_attn_v[i], state_rwkv_shift[i], state_rwkv[i],
                    state_gla[i], state_lru_conv[i], state_lru[i])
        ys, ss = _trunk_layer(ys, p_sample, i, cache_st, lw, bias_s, False)
        st_p.append(sp)
        st_s.append(ss)
    stack = lambda states, j: jnp.stack([s[j] for s in states])
    return (yp, ys) + tuple(stack(st_p, j) for j in range(7)) + tuple(stack(st_s, j) for j in range(7))
```

```python
import functools
import math

import numpy as np
import jax
import jax.numpy as jnp
from jax import lax
from jax.experimental import pallas as pl
from jax.experimental.pallas import tpu as pltpu

F32 = jnp.float32
BF16 = jnp.bfloat16

V7X_VMEM_BYTES = 64 * 1024 * 1024
VMEM_LIMIT = V7X_VMEM_BYTES - 8 * 1024 * 1024
SUBLANES = 8
LANES = 128
MXU_COLS = 256

DENSE_ROWS = 512
SCAN_ROWS = 1024
SCAN_SUB_ROWS = 256
LRU_ROWS = 256
SAMPLE_SEQS = 8

EPS = 1e-6
NEG_INF = -1e30
CHUNK = 64
WINDOW = 128
N_BUCKETS = 32
MAX_DIST = 128
A_HEADS, A_KV_HEADS, A_HD = 8, 2, 64
A_GROUP = A_HEADS // A_KV_HEADS
B_HEADS, B_HD = 8, 64
B_W_RANK, B_A_RANK, B_G_RANK = 64, 64, 128
RWKV_GN_EPS = 64e-5
C_HEADS, C_DK, C_DV = 4, 64, 128
C_G_RANK = 16
C_TAU = 16.0
D_BLOCKS = 8
CONV_W = 4
LRU_C = 8.0
BRANCH_W = 512
HG = 4
HGW = HG * B_HD


def _cparams(*sem):
    return pltpu.CompilerParams(dimension_semantics=sem, vmem_limit_bytes=VMEM_LIMIT)


def _resident(shape):
    nd = len(shape)
    return pl.BlockSpec(shape, lambda *_: (0,) * nd, pipeline_mode=pl.Buffered(1))


def _row_tile(n, cap):
    t = min(n, cap)
    while n % t:
        t //= 2
    return t


def _dot(a, b):
    return jnp.dot(a.astype(BF16), b.astype(BF16), preferred_element_type=F32)


def _dot_nt(a, b):
    return lax.dot_general(a.astype(BF16), b.astype(BF16), (((1,), (1,)), ((), ())),
                           preferred_element_type=F32)


def _dot_tn(a, b):
    return lax.dot_general(a.astype(BF16), b.astype(BF16), (((0,), (0,)), ((), ())),
                           preferred_element_type=F32)


def _rms(x, g):
    return x * lax.rsqrt(jnp.mean(x * x, axis=-1, keepdims=True) + EPS) * g


def _sigmoid(x):
    return 1.0 / (1.0 + jnp.exp(-x))


def _softplus(x):
    return jnp.maximum(x, 0.0) + jnp.log1p(jnp.exp(-jnp.abs(x)))


def _softplus_big(x):
    return jnp.maximum(x, 0.0) + jnp.log(1.0 + jnp.exp(-jnp.abs(x)))


def _split_dot(e_lhs, x, terms):
    acc = None
    rem = x
    for n in range(terms):
        piece = rem.astype(BF16)
        d = jnp.dot(e_lhs, piece, preferred_element_type=F32)
        acc = d if acc is None else acc + d
        if n + 1 < terms:
            rem = rem - piece.astype(F32)
    return acc


def _segsum(x, e):
    blk = e.shape[0]
    xb = x.astype(BF16)
    cols = [jnp.dot(xb[:, j:j + blk], e, preferred_element_type=F32)
            for j in range(0, x.shape[1], blk)]
    return cols[0] if len(cols) == 1 else jnp.concatenate(cols, axis=1)


def _tril_ones(n):
    r = lax.broadcasted_iota(jnp.int32, (n, n), 0)
    c = lax.broadcasted_iota(jnp.int32, (n, n), 1)
    return jnp.where(r >= c, 1.0, 0.0).astype(BF16)


def _bd_rows(x, blk, nblk):
    lane_blk = lax.broadcasted_iota(jnp.int32, x.shape, 1) // blk
    return jnp.concatenate([jnp.where(lane_blk == h, x, 0.0) for h in range(nblk)], axis=0)


def _block_diag_value(blocks):
    n = len(blocks)
    a, b = blocks[0].shape
    rows = []
    for h, blk in enumerate(blocks):
        parts = ([jnp.zeros((a, h * b), F32)] if h else []) + [blk] \
            + ([jnp.zeros((a, (n - 1 - h) * b), F32)] if h < n - 1 else [])
        rows.append(jnp.concatenate(parts, axis=1))
    return jnp.concatenate(rows, axis=0)


def _transpose_exact(x):
    n = x.shape[0]
    r = lax.broadcasted_iota(jnp.int32, (n, n), 0)
    c = lax.broadcasted_iota(jnp.int32, (n, n), 1)
    eye = jnp.where(r == c, 1.0, 0.0).astype(BF16)
    acc = None
    rem = x
    for step in range(3):
        piece = rem.astype(BF16)
        d = lax.dot_general(piece, eye, (((0,), (0,)), ((), ())), preferred_element_type=F32)
        acc = d if acc is None else acc + d
        if step < 2:
            rem = rem - piece.astype(F32)
    return acc


def _seg_matrix(width, seg):
    i = np.arange(width)
    return jnp.asarray((i[:, None] // seg) == (i[None, :] // seg), dtype=BF16)


def _ffn_chunks(f):
    tiles = -(-f // MXU_COLS)
    cut = min(f, (tiles + 1) // 2 * MXU_COLS)
    return [(0, cut)] + ([(cut, f - cut)] if cut < f else [])


def _swiglu_residual(x, g_ref, wg_ref, wu_ref, wd_ref):
    h = _rms(x, g_ref[...]).astype(BF16)
    acc = None
    for start, size in _ffn_chunks(wg_ref.shape[1]):
        sl = pl.ds(start, size)
        gt = jnp.dot(h, wg_ref[:, sl], preferred_element_type=F32)
        up = jnp.dot(h, wu_ref[:, sl], preferred_element_type=F32)
        act = (gt * _sigmoid(gt) * up).astype(BF16)
        d = jnp.dot(act, wd_ref[sl, :], preferred_element_type=F32)
        acc = d if acc is None else acc + d
    return x + 0.5 * acc


def _ffn_body(x_ref, g_ref, wg_ref, wu_ref, wd_ref, o_ref):
    o_ref[...] = _swiglu_residual(x_ref[...], g_ref, wg_ref, wu_ref, wd_ref)


def _ffn_ple_body(x_ref, pe_ref, g_ref, wg_ref, wu_ref, wd_ref, gp_ref, wpg_ref, wpp_ref, o_ref):
    x = _swiglu_residual(x_ref[...], g_ref, wg_ref, wu_ref, wd_ref)
    h = _rms(x, gp_ref[...]).astype(BF16)
    gate = _sigmoid(jnp.dot(h, wpg_ref[...], preferred_element_type=F32))
    o_ref[...] = x + gate * jnp.dot(pe_ref[...].astype(BF16), wpp_ref[...], preferred_element_type=F32)


def _ffn(x, g, wg, wu, wd):
    n, d = x.shape
    f = wg.shape[1]
    tm = _row_tile(n, DENSE_ROWS)
    return pl.pallas_call(
        _ffn_body,
        grid=(n // tm,),
        in_specs=[pl.BlockSpec((tm, d), lambda i: (i, 0)),
                  _resident((1, d)), _resident((d, f)), _resident((d, f)), _resident((f, d))],
        out_specs=pl.BlockSpec((tm, d), lambda i: (i, 0)),
        out_shape=jax.ShapeDtypeStruct((n, d), F32),
        compiler_params=_cparams("parallel"),
        name="ffn",
    )(x, g, wg, wu, wd)


def _inproj_body(x_ref, g_ref, wa_ref, wb_ref, wc_ref, wd_ref, pa_ref, pb_ref, pc_ref, pd_ref):
    h = _rms(x_ref[...], g_ref[...]).astype(BF16)
    for w_ref, p_ref in ((wa_ref, pa_ref), (wb_ref, pb_ref), (wc_ref, pc_ref), (wd_ref, pd_ref)):
        p_ref[...] = jnp.dot(h, w_ref[...], preferred_element_type=F32)


def _inproj(x, g, ws):
    n, d = x.shape
    tm = _row_tile(n, DENSE_ROWS)
    widths = [w.shape[1] for w in ws]
    return pl.pallas_call(
        _inproj_body,
        grid=(n // tm,),
        in_specs=[pl.BlockSpec((tm, d), lambda i: (i, 0)), _resident((1, d))]
                 + [_resident((d, w)) for w in widths],
        out_specs=[pl.BlockSpec((tm, w), lambda i: (i, 0)) for w in widths],
        out_shape=[jax.ShapeDtypeStruct((n, w), F32) for w in widths],
        compiler_params=_cparams("parallel"),
        name="inproj",
    )(x, g, *ws)


def _merge_body(x_ref, oa_ref, ob_ref, oc_ref, od_ref, g_ref, wmg_ref, wb_ref, wo_ref, o_ref):
    x = x_ref[...]
    h = _rms(x, g_ref[...]).astype(BF16)
    y = None
    for n, b_ref in enumerate((oa_ref, ob_ref, oc_ref, od_ref)):
        gate = _sigmoid(jnp.dot(h, wmg_ref[n], preferred_element_type=F32))
        t = gate * jnp.dot(b_ref[...].astype(BF16), wb_ref[n], preferred_element_type=F32)
        y = t if y is None else y + t
    o_ref[...] = x + jnp.dot(y.astype(BF16), wo_ref[...], preferred_element_type=F32)


def _merge(x, outs, g, wmg, wb, wo):
    n, d = x.shape
    bw = outs[0].shape[1]
    tm = _row_tile(n, DENSE_ROWS)
    return pl.pallas_call(
        _merge_body,
        grid=(n // tm,),
        in_specs=[pl.BlockSpec((tm, d), lambda i: (i, 0))]
                 + [pl.BlockSpec((tm, bw), lambda i: (i, 0))] * 4
                 + [_resident((1, d)), _resident(wmg.shape), _resident(wb.shape), _resident(wo.shape)],
        out_specs=pl.BlockSpec((tm, d), lambda i: (i, 0)),
        out_shape=jax.ShapeDtypeStruct((n, d), F32),
        compiler_params=_cparams("parallel"),
        name="merge",
    )(x, *outs, g, wmg, wb, wo)


def _ffn_ple(x, pe_all, layer, g, wg, wu, wd, gp, wpg, wpp):
    n, d = x.shape
    f = wg.shape[1]
    pd = pe_all.shape[2]
    tm = _row_tile(n, DENSE_ROWS)
    return pl.pallas_call(
        _ffn_ple_body,
        grid=(n // tm,),
        in_specs=[pl.BlockSpec((tm, d), lambda i: (i, 0)),
                  pl.BlockSpec((None, tm, pd), lambda i: (layer, i, 0)),
                  _resident((1, d)), _resident((d, f)), _resident((d, f)), _resident((f, d)),
                  _resident((1, d)), _resident((d, d)), _resident((pd, d))],
        out_specs=pl.BlockSpec((tm, d), lambda i: (i, 0)),
        out_shape=jax.ShapeDtypeStruct((n, d), F32),
        compiler_params=_cparams("parallel"),
        name="ffn_ple",
    )(x, pe_all, g, wg, wu, wd, gp, wpg, wpp)


def _t5_bucket_np(rel):
    half = N_BUCKETS // 2
    max_exact = half // 2
    ret = np.where(rel > 0, half, 0)
    n = np.abs(rel)
    nf = np.maximum(n, 1).astype(np.float32)
    large = max_exact + (np.log(nf / np.float32(max_exact)) / np.float32(math.log(MAX_DIST / max_exact))
                         * np.float32(half - max_exact)).astype(np.int32)
    large = np.minimum(large, half - 1)
    return (ret + np.where(n < max_exact, n, large)).astype(np.int32)


def _bias_body(idx_ref, table_ref, o_ref):
    idx = idx_ref[...]
    for h in range(A_HEADS):
        acc = jnp.zeros(idx.shape, F32)
        for b in range(N_BUCKETS):
            acc = jnp.where(idx == b, table_ref[b, h], acc)
        o_ref[h] = acc


def _rel_bias(table, n_q, n_k):
    rel = np.arange(n_k)[:, None] - WINDOW - np.arange(n_q)[None, :]
    idx = jnp.asarray(_t5_bucket_np(rel))
    out = pl.pallas_call(
        _bias_body,
        in_specs=[pl.BlockSpec(memory_space=pltpu.VMEM), pl.BlockSpec(memory_space=pltpu.SMEM)],
        out_specs=pl.BlockSpec(memory_space=pltpu.VMEM),
        out_shape=jax.ShapeDtypeStruct((A_HEADS, n_k, n_q), F32),
        name="rel_bias",
    )(idx, table)
    return jnp.transpose(out, (1, 0, 2)).reshape(n_k, A_HEADS * n_q)


def _head_rms(x, e, g):
    ms = _segsum(x * x, e) * (1.0 / A_HD)
    return x * lax.rsqrt(ms + EPS) * g


def _attend_blocks(q_list, k_list, v_list, bias, sink, valid_list, fill=lambda: None):
    def staged(f, *seqs):
        out = []
        for j, xs in enumerate(zip(*seqs)):
            out.append(f(*xs))
            if j % 4 == 3:
                fill()
        return out

    n = q_list[0].shape[0]
    gcols = A_GROUP * n
    inst = [(b, g) for b in range(len(q_list)) for g in range(A_KV_HEADS)]
    head = lambda x, h: x[:, h * A_HD:(h + 1) * A_HD]
    q_st = staged(lambda bg: jnp.concatenate([head(q_list[bg[0]], bg[1] * A_GROUP + r)
                                              for r in range(A_GROUP)], axis=0), inst)
    kg = staged(lambda bg: head(k_list[bg[0]], bg[1]), inst)
    vg = staged(lambda bg: head(v_list[bg[0]], bg[1]), inst)
    bias_g = [bias[:, g * gcols:(g + 1) * gcols] for g in range(A_KV_HEADS)]
    sink_g = [sink[:, g * gcols:(g + 1) * gcols] for g in range(A_KV_HEADS)]

    def scores(bg, k, qs):
        s = _dot_nt(k, qs) + bias_g[bg[1]]
        valid = valid_list[bg[0]]
        return s if valid is None else jnp.where(valid, s, NEG_INF)

    s = staged(scores, inst, kg, q_st)
    m = staged(lambda bg, x: jnp.maximum(jnp.max(x, axis=0, keepdims=True), sink_g[bg[1]]), inst, s)
    e = staged(lambda x, mm: jnp.exp(x - mm), s, m)
    rinv = [1.0 / (jnp.sum(x, axis=0, keepdims=True) + jnp.exp(sink_g[bg[1]] - mm))
            for bg, x, mm in zip(inst, e, m)]
    p = staged(lambda x, r: x * r, e, rinv)
    og = staged(_dot_tn, p, vg)
    outs = []
    for b in range(len(q_list)):
        pieces = [og[b * A_KV_HEADS + g][r * n:(r + 1) * n] for g in range(A_KV_HEADS)
                  for r in range(A_GROUP)]
        outs.append(jnp.concatenate(pieces, axis=1))
    return outs


def _attn_prompt_block(pa, kv_tail_ref, t_idx, bias_ref, sink_ref, qn_ref, kn_ref, eq_ref, ek_ref,
                       tb, fill):
    qw = A_HEADS * A_HD
    kw = A_KV_HEADS * A_HD
    band = WINDOW + CHUNK
    q = _head_rms(pa[:, 0:qw], eq_ref[...], qn_ref[...]) * (A_HD ** -0.5)
    kf = jnp.concatenate([kv_tail_ref[0], _head_rms(pa[:, qw:qw + kw], ek_ref[...], kn_ref[...])], axis=0)
    vf = jnp.concatenate([kv_tail_ref[1], pa[:, qw + kw:qw + 2 * kw]], axis=0)
    kv_tail_ref[0] = kf[tb:tb + WINDOW, :]
    kv_tail_ref[1] = vf[tb:tb + WINDOW, :]
    kb = kf.astype(BF16)
    vb = vf.astype(BF16)
    kidx = lax.broadcasted_iota(jnp.int32, (band, A_GROUP * CHUNK), 0)
    n_chunks = tb // CHUNK
    valid = [(kidx + (t_idx * tb + c * CHUNK - WINDOW)) >= 0 if c * CHUNK < WINDOW else None
             for c in range(n_chunks)]
    return _attend_blocks([q[c * CHUNK:(c + 1) * CHUNK] for c in range(n_chunks)],
                          [kb[c * CHUNK:c * CHUNK + band] for c in range(n_chunks)],
                          [vb[c * CHUNK:c * CHUNK + band] for c in range(n_chunks)],
                          bias_ref[...], sink_ref[...], valid, fill)


def _attn_sample_body(pa_ref, ck_ref, cv_ref, bias_ref, sink_ref, qn_ref, kn_ref, eq_ref, ek_ref,
                      o_ref, kout_ref, vout_ref, *, s, gb):
    qw = A_HEADS * A_HD
    kw = A_KV_HEADS * A_HD
    q = _head_rms(pa_ref[:, 0:qw], eq_ref[...], qn_ref[...]) * (A_HD ** -0.5)
    kn = _head_rms(pa_ref[:, qw:qw + kw], ek_ref[...], kn_ref[...])
    vn = pa_ref[:, qw + kw:qw + 2 * kw]
    q_list, k_list, v_list = [], [], []
    for b in range(gb):
        kf = jnp.concatenate([ck_ref[b], kn[b * s:(b + 1) * s]], axis=0)
        vf = jnp.concatenate([cv_ref[b], vn[b * s:(b + 1) * s]], axis=0)
        kout_ref[b] = kf[s:s + WINDOW, :]
        vout_ref[b] = vf[s:s + WINDOW, :]
        q_list.append(q[b * s:(b + 1) * s])
        k_list.append(kf.astype(BF16))
        v_list.append(vf.astype(BF16))
    outs = _attend_blocks(q_list, k_list, v_list, bias_ref[...], sink_ref[...], [None] * gb)
    for b in range(gb):
        o_ref[b * s:(b + 1) * s, :] = outs[b]


def _attn_sample(pa, ck, cv, bias, sink, qn, kn, nb, s):
    qw, kw = A_HEADS * A_HD, A_KV_HEADS * A_HD
    width = pa.shape[1]
    gb = _row_tile(nb, SAMPLE_SEQS)
    return pl.pallas_call(
        functools.partial(_attn_sample_body, s=s, gb=gb),
        grid=(nb // gb,),
        in_specs=[pl.BlockSpec((gb * s, width), lambda b: (b, 0)),
                  pl.BlockSpec((gb, WINDOW, kw), lambda b: (b, 0, 0)),
                  pl.BlockSpec((gb, WINDOW, kw), lambda b: (b, 0, 0)),
                  _resident(bias.shape), _resident(sink.shape),
                  _resident((1, qw)), _resident((1, kw)), _resident((HGW, HGW)), _resident((kw, kw))],
        out_specs=[pl.BlockSpec((gb * s, qw), lambda b: (b, 0)),
                   pl.BlockSpec((gb, WINDOW, kw), lambda b: (b, 0, 0)),
                   pl.BlockSpec((gb, WINDOW, kw), lambda b: (b, 0, 0))],
        out_shape=[jax.ShapeDtypeStruct((nb * s, qw), F32),
                   jax.ShapeDtypeStruct((nb, WINDOW, kw), F32),
                   jax.ShapeDtypeStruct((nb, WINDOW, kw), F32)],
        compiler_params=_cparams("parallel"),
        name="attn_sample",
    )(pa, ck, cv, bias, sink, qn, kn, _seg_matrix(HGW, A_HD), _seg_matrix(kw, A_HD))


def _rwkv_chunk_terms(r, lw, k, v, kk, a, fill=lambda: None):
    stage_count = [0]

    def each(f, *seqs):
        out = [f(*xs) for xs in zip(*seqs)]
        stage_count[0] += 1
        if stage_count[0] % 3 == 0:
            fill()
        return out

    cat0 = lambda *xs: jnp.concatenate(xs, axis=0)
    cat1 = lambda *xs: jnp.concatenate(xs, axis=1)
    bd = lambda x: _bd_rows(x, B_HD, HG)
    L = r[0].shape[0]
    tril = _tril_ones(L)
    c = each(lambda x: _split_dot(tril, x, 3), lw)
    c_last = each(lambda x: x[L - 1:L, :], c)
    e_last = each(lambda x, xl: jnp.exp(xl - x), c, c_last)
    beta = each(lambda x, y: x * y, kk, a)
    at = each(lambda x, cc, l: -x * jnp.exp(cc - l), kk, c, lw)
    rt = each(lambda x, cc: x * jnp.exp(cc), r, c)
    enc = each(lambda cc: jnp.exp(-cc), c)
    ar = each(cat0, at, rt)
    mb = each(lambda x, b, e: _dot_nt(x, bd(b * e)), ar, beta, enc)
    mk = each(lambda x, b, e: _dot_nt(x, bd(b * e)), ar, k, enc)
    row = lax.broadcasted_iota(jnp.int32, (L, HG * L), 0)
    col = lax.broadcasted_iota(jnp.int32, (L, HG * L), 1) % L
    strict = col < row
    incl = col <= row
    m_b = each(lambda x: jnp.where(strict, x[:L], 0.0), mb)
    m_k = each(lambda x: jnp.where(strict, x[:L], 0.0), mk)
    n_b = each(lambda x: jnp.where(incl, x[L:], 0.0), mb)
    n_k = each(lambda x: jnp.where(incl, x[L:], 0.0), mk)
    eye = jnp.where(col == row, 1.0, 0.0)
    t_inv = each(lambda x: eye + x, m_b)
    p = m_b
    for lvl in range(1, int(math.log2(L))):
        p_bd = each(lambda x: _bd_rows(x, L, HG), p)
        if lvl == 1:
            p = each(_dot, p, p_bd)
        else:
            tp = each(lambda t, x, xb: _dot(cat0(t, x), xb), t_inv, p, p_bd)
            t_inv = each(lambda t, x: t + x[:L], t_inv, tp)
            p = each(lambda x: x[L:], tp)
    t_inv = each(lambda t, x: t + _dot(t, _bd_rows(x, L, HG)), t_inv, p)
    v_bd = each(bd, v)
    mnv = each(lambda x, y, vb: _dot(cat0(x, y), vb), m_k, n_k, v_bd)
    mkv = each(lambda x: x[:L], mnv)
    wu = each(lambda t, x, y: _dot(t, cat1(bd(x), bd(y))), t_inv, at, mkv)
    w1 = each(lambda x: x[:, :HGW], wu)
    u0 = each(lambda x: x[:, HGW:], wu)
    nbo = each(lambda n, x, y: _dot(n, cat1(bd(x), bd(y))), n_b, w1, u0)
    q1 = each(lambda x, y: x + y[:, :HGW], rt, nbo)
    y0 = each(lambda x, y: x[:, HGW:] + y[L:], nbo, mnv)
    bh = each(lambda x, y: x * y, beta, e_last)
    kh = each(lambda x, y: x * y, k, e_last)
    ri = lax.broadcasted_iota(jnp.int32, (HGW, HGW), 0) // B_HD
    ci = lax.broadcasted_iota(jnp.int32, (HGW, HGW), 1) // B_HD
    diag = ri == ci
    g_mat = each(lambda x, y: jnp.where(diag, _dot_tn(x, y), 0.0), w1, bh)
    h_mat = each(lambda x, y, z, w: jnp.where(diag, _dot_tn(cat0(x, y), cat0(z, w)), 0.0), u0, v, bh, kh)
    gam = each(jnp.exp, c_last)
    return q1, y0, g_mat, h_mat, gam


def _rwkv_feature_stages(shifted, in_refs, e):
    mu_ref, w0_ref, w2_ref, a0_ref, a2_ref, g2_ref, kk_ref, ka_ref = in_refs
    W = BRANCH_W
    f = {}

    def shift():
        p, prev = shifted()
        f['x'] = p + (prev - p) * mu_ref[...]
        f['r'], f['v'] = f['x'][:, 0:W], f['x'][:, 2 * W:3 * W]

    def decay():
        wal = f['x'][:, 3 * W:3 * W + LANES]
        w_log = -_softplus_big(-(w0_ref[...] + _dot(jnp.tanh(wal), w2_ref[...]))) - 0.5
        f['lw'] = -jnp.exp(w_log)

    def gates():
        wal = f['x'][:, 3 * W:3 * W + LANES]
        gl = f['x'][:, 3 * W + LANES:3 * W + 2 * LANES]
        f['a'] = _sigmoid(a0_ref[...] + _dot(wal, a2_ref[...]))
        f['g'] = _dot(_sigmoid(gl), g2_ref[...])

    def keys():
        k = f['x'][:, W:2 * W]
        kkr = k * kk_ref[...]
        f['kk'] = kkr / jnp.maximum(jnp.sqrt(_segsum(kkr * kkr, e)), 1e-12)
        f['k2'] = k * (1.0 + (f['a'] - 1.0) * ka_ref[...])

    return f, [shift, decay, gates, keys]


def _rwkv_output_stages(f, y, store, out_refs, e):
    rk_ref, lng_ref, lnb_ref = out_refs
    t = {}

    def center():
        t['yc'] = y - _segsum(y, e) * (1.0 / B_HD)

    def norm():
        var = _segsum(t['yc'] * t['yc'], e) * (1.0 / B_HD)
        t['yn'] = t['yc'] * lax.rsqrt(var + RWKV_GN_EPS) * lng_ref[...] + lnb_ref[...]

    def out():
        bonus = _segsum(f['r'] * f['k2'] * rk_ref[...], e) * f['v']
        store((t['yn'] + bonus) * f['g'])

    return [center, norm, out]


def _rwkv_short_body(pb_ref, shift_ref, s0_ref, mu_ref, w0_ref, w2_ref, a0_ref, a2_ref, g2_ref, kk_ref,
                     ka_ref, rk_ref, lng_ref, lnb_ref, e_ref,
                     o_ref, sout_ref, ext_ref, *, t, g):
    n_groups = BRANCH_W // HGW
    e = e_ref[...]
    for q in range(g):
        ext_ref[q, 0:SUBLANES, :] = jnp.broadcast_to(shift_ref[q], (SUBLANES, ext_ref.shape[2]))
        ext_ref[q, SUBLANES:, :] = pb_ref[q * t:(q + 1) * t, :]

    def shifted():
        prev = jnp.concatenate([ext_ref[q, SUBLANES - 1:SUBLANES - 1 + t, :] for q in range(g)], axis=0)
        return pb_ref[...], prev

    f, stages = _rwkv_feature_stages(
        shifted, (mu_ref, w0_ref, w2_ref, a0_ref, a2_ref, g2_ref, kk_ref, ka_ref), e)
    for stage in stages:
        stage()

    def piece(x, q, gi):
        return jnp.concatenate([x[q * t:(q + 1) * t, gi * HGW:(gi + 1) * HGW],
                                jnp.zeros((CHUNK - t, HGW), F32)], axis=0)

    inst = [(q, gi) for q in range(g) for gi in range(n_groups)]
    terms = _rwkv_chunk_terms(*[[piece(f[name], q, gi) for q, gi in inst]
                                for name in ('r', 'lw', 'k2', 'v', 'kk', 'a')])
    y_rows = []
    for q in range(g):
        y_cols = []
        for gi in range(n_groups):
            q1, y0, g_mat, h_mat, gam = [tm[q * n_groups + gi] for tm in terms]
            s = _block_diag_value([s0_ref[q, gi * HG + h] for h in range(HG)])
            y_cols.append(_dot_nt(q1, s) + y0)
            s = s * gam + _dot(s, g_mat) + h_mat
            for h in range(HG):
                sout_ref[q, gi * HG + h] = s[h * B_HD:(h + 1) * B_HD, h * B_HD:(h + 1) * B_HD]
        y_rows.append(jnp.concatenate(y_cols, axis=1)[0:t])

    def store(val):
        o_ref[...] = val

    for stage in _rwkv_output_stages(f, jnp.concatenate(y_rows, axis=0), store,
                                     (rk_ref, lng_ref, lnb_ref), e):
        stage()


def _rwkv_body(pb_ref, shift_ref, s0_ref, mu_ref, w0_ref, w2_ref, a0_ref, a2_ref, g2_ref, kk_ref,
               ka_ref, rk_ref, lng_ref, lnb_ref, e_ref,
               o_ref, sout_ref,
               ext_ref, s_ref, *, tb, sub, lc):
    t = pl.program_id(1)
    W = BRANCH_W
    n_groups = W // HGW

    @pl.when(t == 0)
    def _():
        ext_ref[0:SUBLANES, :] = jnp.broadcast_to(shift_ref[0], (SUBLANES, ext_ref.shape[1]))
        for gi in range(n_groups):
            s_ref[gi] = _block_diag_value([s0_ref[0, gi * HG + h] for h in range(HG)])

    ext_ref[SUBLANES:, :] = pb_ref[...]

    e = e_ref[...]
    n_sub = tb // sub

    rows_c = lc
    n_chunks = sub // rows_c

    in_refs = (mu_ref, w0_ref, w2_ref, a0_ref, a2_ref, g2_ref, kk_ref, ka_ref)
    out_refs = (rk_ref, lng_ref, lnb_ref)

    def prep_stages(lo):
        def shifted():
            p = pb_ref[lo:lo + sub, :]
            return p, ext_ref[SUBLANES - 1 + lo:SUBLANES - 1 + lo + sub, :]
        return _rwkv_feature_stages(shifted, in_refs, e)

    def post_stages(lo, f, y):
        def store(val):
            o_ref[lo:lo + sub, :] = val
        return _rwkv_output_stages(f, y, store, out_refs, e)

    def piece(x, ci, gi):
        return x[ci * rows_c:(ci + 1) * rows_c, gi * HGW:(gi + 1) * HGW]

    feats = [prep_stages(j * sub) for j in range(n_sub)]
    for stage in feats[0][1]:
        stage()
    pending = []

    def fill():
        if pending:
            pending.pop(0)()

    inst = [(ci, gi) for ci in range(n_chunks) for gi in range(n_groups)]
    for j in range(n_sub):
        f = feats[j][0]
        if j + 1 < n_sub:
            pending.extend(feats[j + 1][1])
        terms = _rwkv_chunk_terms(*[[piece(f[name], ci, gi) for ci, gi in inst]
                                    for name in ('r', 'lw', 'k2', 'v', 'kk', 'a')], fill)
        while pending:
            fill()
        y_rows = []
        for ci in range(n_chunks):
            y_cols = []
            for gi in range(n_groups):
                q1, y0, g_mat, h_mat, gam = [tm[ci * n_groups + gi] for tm in terms]
                s = s_ref[gi]
                y_cols.append(_dot_nt(q1, s) + y0)
                s_ref[gi] = s * gam + _dot(s, g_mat) + h_mat
            y_rows.append(jnp.concatenate(y_cols, axis=1))
        pending.extend(post_stages(j * sub, f, jnp.concatenate(y_rows, axis=0)))
    while pending:
        fill()
    ext_ref[0:SUBLANES, :] = pb_ref[tb - SUBLANES:tb, :]

    @pl.when(t == pl.num_programs(1) - 1)
    def _():
        for gi in range(n_groups):
            s = s_ref[gi]
            for h in range(HG):
                sout_ref[0, gi * HG + h] = s[h * B_HD:(h + 1) * B_HD, h * B_HD:(h + 1) * B_HD]


def _rwkv(pb, shift0, s0, wts, nb, t):
    width = pb.shape[1]
    W = BRANCH_W
    n_groups = W // HGW
    row = lambda n: _resident((1, n))
    weight_specs = [row(width), row(W), _resident((LANES, W)), row(W), _resident((LANES, W)),
                    _resident((B_G_RANK, W)), row(W), row(W), row(W), row(W), row(W),
                    _resident((HGW, HGW))]
    if t < CHUNK:
        g = _row_tile(nb, SAMPLE_SEQS)
        assert t % SUBLANES == 0
        return pl.pallas_call(
            functools.partial(_rwkv_short_body, t=t, g=g),
            grid=(nb // g,),
            in_specs=[pl.BlockSpec((g * t, width), lambda b: (b, 0)),
                      pl.BlockSpec((g, 1, width), lambda b: (b, 0, 0)),
                      pl.BlockSpec((g, B_HEADS, B_HD, B_HD), lambda b: (b, 0, 0, 0))] + weight_specs,
            out_specs=[pl.BlockSpec((g * t, W), lambda b: (b, 0)),
                       pl.BlockSpec((g, B_HEADS, B_HD, B_HD), lambda b: (b, 0, 0, 0))],
            out_shape=[jax.ShapeDtypeStruct((nb * t, W), F32),
                       jax.ShapeDtypeStruct((nb, B_HEADS, B_HD, B_HD), F32)],
            scratch_shapes=[pltpu.VMEM((g, t + SUBLANES, width), F32)],
            compiler_params=_cparams("parallel"),
            name="rwkv7_short",
        )(pb, shift0, s0, *wts, _seg_matrix(HGW, B_HD))
    tb = _row_tile(t, SCAN_ROWS)
    sub = min(tb, SCAN_SUB_ROWS)
    lc = CHUNK
    assert sub % lc == 0
    nt = t // tb
    return pl.pallas_call(
        functools.partial(_rwkv_body, tb=tb, sub=sub, lc=lc),
        grid=(nb, nt),
        in_specs=[pl.BlockSpec((tb, width), lambda b, i: (b * nt + i, 0)),
                  pl.BlockSpec((1, 1, width), lambda b, i: (b, 0, 0)),
                  pl.BlockSpec((1, B_HEADS, B_HD, B_HD), lambda b, i: (b, 0, 0, 0))] + weight_specs,
        out_specs=[pl.BlockSpec((tb, W), lambda b, i: (b * nt + i, 0)),
                   pl.BlockSpec((1, B_HEADS, B_HD, B_HD), lambda b, i: (b, 0, 0, 0))],
        out_shape=[jax.ShapeDtypeStruct((nb * t, W), F32),
                   jax.ShapeDtypeStruct((nb, B_HEADS, B_HD, B_HD), F32)],
        scratch_shapes=[pltpu.VMEM((tb + SUBLANES, width), F32),
                        pltpu.VMEM((n_groups, HGW, HGW), F32)],
        compiler_params=_cparams("parallel", "arbitrary"),
        name="rwkv7",
    )(pb, shift0, s0, *wts, _seg_matrix(HGW, B_HD))


def _gla_chunk_terms(q, k, v, gk):
    each = lambda f, *seqs: [f(*xs) for xs in zip(*seqs)]
    L = q[0].shape[0]
    kw = C_HEADS * C_DK
    vw = C_HEADS * C_DV
    tril = _tril_ones(L)
    b = each(lambda x: _split_dot(tril, x, 3), gk)
    qe = each(lambda x, y: x * jnp.exp(y), q, b)
    ke = each(lambda x, y: x * jnp.exp(-y), k, b)
    a_all = each(lambda x, y: _dot_nt(x, _bd_rows(y, C_DK, C_HEADS)), qe, ke)
    row = lax.broadcasted_iota(jnp.int32, (L, C_HEADS * L), 0)
    col = lax.broadcasted_iota(jnp.int32, (L, C_HEADS * L), 1) % L
    causal = col <= row
    o_intra = each(lambda x, y: _dot(jnp.where(causal, x, 0.0), _bd_rows(y, C_DV, C_HEADS)), a_all, v)
    b_last = each(lambda x: x[L - 1:L, :], b)
    kd = each(lambda x, y, z: x * jnp.exp(z - y), k, b, b_last)
    ri = lax.broadcasted_iota(jnp.int32, (vw, kw), 0) // C_DV
    ci = lax.broadcasted_iota(jnp.int32, (vw, kw), 1) // C_DK
    diag = ri == ci
    upd = each(lambda x, y: jnp.where(diag, _dot_tn(x, y), 0.0), v, kd)
    decay = each(jnp.exp, b_last)
    return qe, o_intra, upd, decay


def _gla_body(pc_ref, s0_ref, g2_ref, gb_ref, norm_ref, o_ref, sout_ref,
              s_ref, *, tb, lc, g):
    t = pl.program_id(1)
    kw = C_HEADS * C_DK
    vw = C_HEADS * C_DV

    def state_in(qi):
        return _block_diag_value([_transpose_exact(s0_ref[qi, h]) for h in range(C_HEADS)])

    def state_out(qi, st):
        for h in range(C_HEADS):
            sout_ref[qi, h] = _transpose_exact(st[h * C_DV:(h + 1) * C_DV, h * C_DK:(h + 1) * C_DK])

    if g == 1:
        @pl.when(t == 0)
        def _():
            s_ref[...] = state_in(0)

    pc = pc_ref[...]
    og = pc[:, 2 * kw + vw:2 * kw + 2 * vw]
    gl = pc[:, 2 * kw + 2 * vw:2 * kw + 2 * vw + LANES]
    z = _dot(gl, g2_ref[...]) + gb_ref[...]
    q = pc[:, 0:kw] * (C_DK ** -0.5)
    k = pc[:, kw:2 * kw]
    v = pc[:, 2 * kw:2 * kw + vw]
    gk = -_softplus_big(-z) * (1.0 / C_TAU)
    rows_q = tb // g
    n_chunks = max(rows_q // lc, 1)

    def piece(x, qi, ci):
        lo = qi * rows_q + ci * lc
        blk = x[lo:lo + min(lc, rows_q), :]
        if rows_q < lc:
            blk = jnp.concatenate([blk, jnp.zeros((lc - rows_q, x.shape[1]), F32)], axis=0)
        return blk

    inst = [(qi, ci) for qi in range(g) for ci in range(n_chunks)]
    qe, o_intra, upd, decay = _gla_chunk_terms(
        *[[piece(x, qi, ci) for qi, ci in inst] for x in (q, k, v, gk)])
    y_rows = []
    for qi in range(g):
        st = s_ref[...] if g == 1 else state_in(qi)
        for ci in range(n_chunks):
            j = qi * n_chunks + ci
            y_rows.append((_dot_nt(qe[j], st) + o_intra[j])[0:min(lc, rows_q)])
            st = st * decay[j] + upd[j]
        if g == 1:
            s_ref[...] = st
        else:
            state_out(qi, st)
    y = jnp.concatenate(y_rows, axis=0)
    outs = []
    for h in range(C_HEADS):
        yh = y[:, h * C_DV:(h + 1) * C_DV]
        outs.append(yh * lax.rsqrt(jnp.mean(yh * yh, axis=-1, keepdims=True) + EPS) * norm_ref[...])
    yn = jnp.concatenate(outs, axis=1)
    o_ref[...] = yn * (og * _sigmoid(og))

    if g == 1:
        @pl.when(t == pl.num_programs(1) - 1)
        def _():
            state_out(0, s_ref[...])


def _gla(pc, s0, g2, gb, norm, nb, t):
    lc = CHUNK
    g = _row_tile(nb, SAMPLE_SEQS) if t < lc else 1
    tb = g * t if t < lc else _row_tile(t, SCAN_ROWS)
    assert tb % lc == 0 or t < lc
    nt = (g * t) // tb
    width = pc.shape[1]
    kw, vw = C_HEADS * C_DK, C_HEADS * C_DV
    return pl.pallas_call(
        functools.partial(_gla_body, tb=tb, lc=lc, g=g),
        grid=(nb // g, nt),
        in_specs=[pl.BlockSpec((tb, width), lambda b, i: (b * nt + i, 0)),
                  pl.BlockSpec((g, C_HEADS, C_DK, C_DV), lambda b, i: (b, 0, 0, 0)),
                  _resident((LANES, kw)), _resident((1, kw)), _resident((1, C_DV))],
        out_specs=[pl.BlockSpec((tb, vw), lambda b, i: (b * nt + i, 0)),
                   pl.BlockSpec((g, C_HEADS, C_DK, C_DV), lambda b, i: (b, 0, 0, 0))],
        out_shape=[jax.ShapeDtypeStruct((nb * t, vw), F32),
                   jax.ShapeDtypeStruct((nb, C_HEADS, C_DK, C_DV), F32)],
        scratch_shapes=[pltpu.VMEM((vw, kw), F32)],
        compiler_params=_cparams("parallel", "arbitrary"),
        name="gla",
    )(pc, s0, g2, gb, norm)


def _shift_rows(x, d, fill):
    n = x.shape[0]
    if d % SUBLANES == 0:
        head = jnp.full((d, x.shape[1]), fill, x.dtype)
        return jnp.concatenate([head, x[:n - d]], axis=0)
    rolled = pltpu.roll(x, d, 0)
    row = lax.broadcasted_iota(jnp.int32, x.shape, 0)
    return jnp.where(row < d, fill, rolled)


def _lru_block(gate, xr, ext_ref, h_ref, au_ref, cw_ref, cb_ref, wa_ref, ba_ref, wx_ref, bx_ref,
               lam_ref, tb, fill=lambda: None):
    W = BRANCH_W
    ext_ref[SUBLANES:, :] = xr
    xc = cb_ref[...] + xr * cw_ref[CONV_W - 1:CONV_W, :]
    for j in range(CONV_W - 1):
        off = SUBLANES - (CONV_W - 1) + j
        xc = xc + ext_ref[off:off + tb, :] * cw_ref[j:j + 1, :]
    tail = ext_ref[tb:tb + SUBLANES, :]
    ext_ref[0:SUBLANES, :] = tail
    fill()

    def block_dot(w_ref):
        return jnp.concatenate([_dot(xc[:, j:j + MXU_COLS], w_ref[j:j + MXU_COLS, j:j + MXU_COLS])
                                for j in range(0, W, MXU_COLS)], axis=1)

    r = _sigmoid(block_dot(wa_ref) + ba_ref[...])
    fill()
    gi = _sigmoid(block_dot(wx_ref) + bx_ref[...])
    fill()
    log_a = (-LRU_C) * r * _softplus(-lam_ref[...])
    a = jnp.exp(log_a)
    u = jnp.sqrt(1.0 - jnp.exp(2.0 * log_a)) * (gi * xc)
    fill()
    ng = tb // SUBLANES
    row_in_group = lax.broadcasted_iota(jnp.int32, (ng, SUBLANES, W), 1)
    a = a.reshape(ng, SUBLANES, W)
    u = u.reshape(ng, SUBLANES, W)
    d = 1
    while d < SUBLANES:
        inside = row_in_group >= d
        u = u + a * jnp.where(inside, pltpu.roll(u, d, 1), 0.0)
        a = a * jnp.where(inside, pltpu.roll(a, d, 1), 1.0)
        d *= 2
        fill()
    a = a.reshape(tb, W)
    u = u.reshape(tb, W)
    h_in = h_ref[...]
    if ng % SUBLANES == 0:
        ends = pl.ds(SUBLANES - 1, ng, stride=SUBLANES)
        for j in range(W // LANES):
            au_ref[0, j] = a[:, j * LANES:(j + 1) * LANES]
            au_ref[1, j] = u[:, j * LANES:(j + 1) * LANES]
        ae = jnp.concatenate([au_ref[0, j, ends, :] for j in range(W // LANES)], axis=1)
        ue = jnp.concatenate([au_ref[1, j, ends, :] for j in range(W // LANES)], axis=1)
        d = 1
        while d < ng:
            ue = ue + ae * _shift_rows(ue, d, 0.0)
            ae = ae * _shift_rows(ae, d, 1.0)
            d *= 2
        carry = _shift_rows(ue + ae * h_in, 1, h_in)
        carry_rows = jnp.broadcast_to(carry[:, None, :], (ng, SUBLANES, W)).reshape(tb, W)
    else:
        rows = []
        for g in range(ng):
            rows.append(jnp.broadcast_to(h_in, (SUBLANES, W)))
            end = (g + 1) * SUBLANES - 1
            h_in = u[end:end + 1, :] + a[end:end + 1, :] * h_in
        carry_rows = jnp.concatenate(rows, axis=0)
    h = u + a * carry_rows
    h_last = h[tb - 1:tb, :]
    h_ref[...] = h_last
    fill()
    out = h * (0.5 * gate * (1.0 + jnp.tanh(math.sqrt(2.0 / math.pi)
                                            * (gate + 0.044715 * gate * gate * gate))))
    return out, tail, h_last


def _lru_body(pd_ref, conv0_ref, h0_ref, cw_ref, cb_ref, wa_ref, ba_ref, wx_ref, bx_ref, lam_ref,
              o_ref, convout_ref, hout_ref, ext_ref, h_ref, au_ref, *, tb):
    t = pl.program_id(1)
    W = BRANCH_W

    @pl.when(t == 0)
    def _():
        ext_ref[0:SUBLANES, :] = conv0_ref[0]
        h_ref[...] = h0_ref[0]

    out, tail, h_last = _lru_block(pd_ref[:, 0:W], pd_ref[:, W:2 * W], ext_ref, h_ref, au_ref, cw_ref,
                                   cb_ref, wa_ref, ba_ref, wx_ref, bx_ref, lam_ref, tb)
    o_ref[...] = out

    @pl.when(t == pl.num_programs(1) - 1)
    def _():
        convout_ref[0] = tail
        hout_ref[0] = h_last


def _mix_in_body(x_ref, g_ref, wa_ref, wb_ref, wc_ref, wd_ref,
                 bias_ref, sink_ref, qn_ref, kn_ref, eq_ref, ek_ref,
                 conv0_ref, h0_ref, cw_ref, cb_ref, lwa_ref, lba_ref, lwx_ref, lbx_ref, lam_ref,
                 pb_ref, pc_ref, oa_ref, od_ref, kout_ref, vout_ref, convout_ref, hout_ref,
                 kv_tail_ref, ext_ref, h_ref, au_ref, *, tb):
    t = pl.program_id(1)
    W = BRANCH_W

    @pl.when(t == 0)
    def _():
        kv_tail_ref[...] = jnp.zeros(kv_tail_ref.shape, F32)
        ext_ref[0:SUBLANES, :] = conv0_ref[0]
        h_ref[...] = h0_ref[0]

    h = _rms(x_ref[...], g_ref[...]).astype(BF16)
    pd = jnp.dot(h, wd_ref[...], preferred_element_type=F32)
    pa = jnp.dot(h, wa_ref[...], preferred_element_type=F32)

    half = tb // 2
    pending = [(w_ref, p_ref, c0, min(MXU_COLS, w_ref.shape[1] - c0), r0)
               for w_ref, p_ref in ((wb_ref, pb_ref), (wc_ref, pc_ref))
               for c0 in range(0, w_ref.shape[1], MXU_COLS) for r0 in (0, half)]

    def fill():
        if pending:
            w_ref, p_ref, c0, width, r0 = pending.pop(0)
            p_ref[r0:r0 + half, c0:c0 + width] = jnp.dot(h[r0:r0 + half], w_ref[:, c0:c0 + width],
                                                         preferred_element_type=F32)

    hb = tb // 4
    for r0 in range(0, tb, hb):
        od, tail, h_last = _lru_block(pd[r0:r0 + hb, 0:W], pd[r0:r0 + hb, W:2 * W],
                                      ext_ref.at[0:hb + SUBLANES], h_ref, au_ref.at[:, :, 0:hb],
                                      cw_ref, cb_ref, lwa_ref, lba_ref, lwx_ref, lbx_ref, lam_ref, hb, fill)
        od_ref[r0:r0 + hb, :] = od
    outs = _attn_prompt_block(pa, kv_tail_ref, t, bias_ref, sink_ref, qn_ref, kn_ref, eq_ref, ek_ref,
                              tb, fill)
    for c, o in enumerate(outs):
        oa_ref[c * CHUNK:(c + 1) * CHUNK, :] = o
    while pending:
        fill()

    @pl.when(t == pl.num_programs(1) - 1)
    def _():
        kout_ref[0] = kv_tail_ref[0]
        vout_ref[0] = kv_tail_ref[1]
        convout_ref[0] = tail
        hout_ref[0] = h_last


def _mix_in(x, g, ws, bias, sink, qn, kn, conv0, h0, lru_wts, nb, t):
    n, d = x.shape
    tb = _row_tile(t, DENSE_ROWS)
    assert tb % WINDOW == 0
    nt = t // tb
    W = BRANCH_W
    qw, kw = A_HEADS * A_HD, A_KV_HEADS * A_HD
    wa, wb, wc, wd = ws
    rows = lambda width: pl.BlockSpec((tb, width), lambda b, i: (b * nt + i, 0))
    per_seq = lambda *shape: pl.BlockSpec((1,) + shape, lambda b, i: (b,) + (0,) * len(shape))
    row = lambda width: _resident((1, width))
    return pl.pallas_call(
        functools.partial(_mix_in_body, tb=tb),
        grid=(nb, nt),
        in_specs=[rows(d), row(d)] + [_resident(w.shape) for w in ws]
                 + [_resident(bias.shape), _resident(sink.shape), row(qw), row(kw),
                    _resident((HGW, HGW)), _resident((kw, kw)),
                    per_seq(SUBLANES, W), per_seq(1, W),
                    _resident((CONV_W, W)), row(W), _resident((W, W)), row(W), _resident((W, W)), row(W),
                    row(W)],
        out_specs=[rows(wb.shape[1]), rows(wc.shape[1]), rows(qw), rows(W),
                   per_seq(WINDOW, kw), per_seq(WINDOW, kw), per_seq(SUBLANES, W), per_seq(1, W)],
        out_shape=[jax.ShapeDtypeStruct((n, wb.shape[1]), F32), jax.ShapeDtypeStruct((n, wc.shape[1]), F32),
                   jax.ShapeDtypeStruct((n, qw), F32), jax.ShapeDtypeStruct((n, W), F32),
                   jax.ShapeDtypeStruct((nb, WINDOW, kw), F32), jax.ShapeDtypeStruct((nb, WINDOW, kw), F32),
                   jax.ShapeDtypeStruct((nb, SUBLANES, W), F32), jax.ShapeDtypeStruct((nb, 1, W), F32)],
        scratch_shapes=[pltpu.VMEM((2, WINDOW, kw), F32),
                        pltpu.VMEM((tb + SUBLANES, W), F32), pltpu.VMEM((1, W), F32),
                        pltpu.VMEM((2, W // LANES, tb, LANES), F32)],
        compiler_params=_cparams("parallel", "arbitrary"),
        name="mix_in",
    )(x, g, wa, wb, wc, wd, bias, sink, qn, kn, _seg_matrix(HGW, A_HD), _seg_matrix(kw, A_HD),
      conv0, h0, *lru_wts)


def _lru(pd, conv0, h0, wts, nb, t):
    tb = _row_tile(t, LRU_ROWS)
    nt = t // tb
    W = BRANCH_W
    row = lambda: _resident((1, W))
    return pl.pallas_call(
        functools.partial(_lru_body, tb=tb),
        grid=(nb, nt),
        in_specs=[pl.BlockSpec((tb, 2 * W), lambda b, i: (b * nt + i, 0)),
                  pl.BlockSpec((1, SUBLANES, W), lambda b, i: (b, 0, 0)),
                  pl.BlockSpec((1, 1, W), lambda b, i: (b, 0, 0)),
                  _resident((CONV_W, W)), row(), _resident((W, W)), row(), _resident((W, W)), row(),
                  row()],
        out_specs=[pl.BlockSpec((tb, W), lambda b, i: (b * nt + i, 0)),
                   pl.BlockSpec((1, SUBLANES, W), lambda b, i: (b, 0, 0)),
                   pl.BlockSpec((1, 1, W), lambda b, i: (b, 0, 0))],
        out_shape=[jax.ShapeDtypeStruct((nb * t, W), F32),
                   jax.ShapeDtypeStruct((nb, SUBLANES, W), F32),
                   jax.ShapeDtypeStruct((nb, 1, W), F32)],
        scratch_shapes=[pltpu.VMEM((tb + SUBLANES, W), F32), pltpu.VMEM((1, W), F32),
                        pltpu.VMEM((2, W // LANES, tb, LANES), F32)],
        compiler_params=_cparams("parallel", "arbitrary"),
        name="rglru",
    )(pd, conv0, h0, *wts)


def _block_diag(w):
    n, a, b = w.shape
    eye = jnp.eye(n, dtype=w.dtype)
    return (eye[:, None, :, None] * w[:, :, None, :]).reshape(n * a, n * b)


def _pad_rows_to(w, rows):
    return jnp.pad(w, ((0, rows - w.shape[0]), (0, 0)))


def _layer_weights(i, W):
    bf = lambda a: a.astype(BF16)
    row = lambda a: a.reshape(1, -1).astype(F32)
    bw = BRANCH_W
    a_cols = A_HEADS * A_HD + 2 * A_KV_HEADS * A_HD
    b_cols = 3 * bw + B_W_RANK + B_A_RANK + B_G_RANK
    c_cols = 2 * C_HEADS * C_DK + C_HEADS * C_DV + C_G_RANK + bw
    w_in = W['w_in'][i]
    d = w_in.shape[0]
    wa = w_in[:, :a_cols]
    wb = w_in[:, a_cols:a_cols + b_cols]
    wc = w_in[:, a_cols + b_cols:a_cols + b_cols + c_cols]
    wd = w_in[:, a_cols + b_cols + c_cols:]
    zpad = lambda n: jnp.zeros((d, n), w_in.dtype)
    assert B_W_RANK + B_A_RANK == LANES
    w2 = jnp.pad(W['rwkv_w2'][i], ((0, B_A_RANK), (0, 0)))
    a2 = jnp.pad(W['rwkv_a2'][i], ((B_W_RANK, 0), (0, 0)))
    qkv = 2 * C_HEADS * C_DK + C_HEADS * C_DV
    wc2 = jnp.concatenate([wc[:, :qkv], wc[:, qkv + C_G_RANK:], wc[:, qkv:qkv + C_G_RANK],
                           zpad(LANES - C_G_RANK)], axis=1)
    return dict(
        ffn1=(row(W['g_ffn1'][i]), bf(W['w_ffn1_gate'][i]), bf(W['w_ffn1_up'][i]), bf(W['w_ffn1_down'][i])),
        ffn2=(row(W['g_ffn2'][i]), bf(W['w_ffn2_gate'][i]), bf(W['w_ffn2_up'][i]), bf(W['w_ffn2_down'][i])),
        g_mix=row(W['g_mix'][i]),
        w_in=(bf(wa), bf(wb), bf(wc2), bf(wd)),
        q_norm=row(jnp.tile(W['q_norm'][i], A_HEADS)),
        k_norm=row(jnp.tile(W['k_norm'][i], A_KV_HEADS)),
        sink=W['attn_sink'][i].astype(F32),
        rwkv=(row(W['rwkv_mu'][i]), row(W['rwkv_w0'][i]), bf(w2),
              row(W['rwkv_a0'][i]), bf(a2), bf(W['rwkv_g2'][i]),
              row(W['rwkv_k_k'][i]), row(W['rwkv_k_a'][i]), row(W['rwkv_r_k'][i]),
              row(W['rwkv_ln_g'][i]), row(W['rwkv_ln_b'][i])),
        gla=(bf(_pad_rows_to(W['gla_g2'][i], LANES)), row(W['gla_gb'][i]), row(W['gla_norm'][i])),
        lru=(W['lru_conv_w'][i].astype(F32), row(W['lru_conv_b'][i]),
             bf(_block_diag(W['lru_wa'][i])), row(W['lru_ba'][i]),
             bf(_block_diag(W['lru_wx'][i])), row(W['lru_bx'][i]), row(W['lru_lambda'][i])),
        merge=(bf(W['w_merge_gate'][i]), bf(W['w_branch'][i]), bf(W['w_out'][i])),
        ple=(row(W['g_ple'][i]), bf(W['w_ple_gate'][i]), bf(W['w_ple_proj'][i])),
    )


def _trunk_layer(x, pe_all, layer, st, lw, bias, first_chunk):
    nb, t, d = x.shape
    n = nb * t
    ck, cv, shift0, s_rwkv0, s_gla0, conv0, lru0 = st
    x2 = x.reshape(n, d)
    x2 = _ffn(x2, *lw['ffn1'])
    kw = A_KV_HEADS * A_HD
    n_q = bias.shape[1] // A_HEADS
    sink2 = jnp.repeat(lw['sink'], n_q).reshape(1, A_HEADS * n_q)
    conv_pad = jnp.pad(conv0, ((0, 0), (SUBLANES - (CONV_W - 1), 0), (0, 0)))
    if first_chunk:
        pb, pc, o_a, o_d, k_win, v_win, conv_out, h_out = _mix_in(
            x2, lw['g_mix'], lw['w_in'], bias, sink2, lw['q_norm'], lw['k_norm'],
            conv_pad, lru0[:, None, :], lw['lru'], nb, t)
    else:
        pa, pb, pc, pd = _inproj(x2, lw['g_mix'], lw['w_in'])
        o_a, k_win, v_win = _attn_sample(pa, ck.reshape(nb, WINDOW, kw), cv.reshape(nb, WINDOW, kw),
                                         bias, sink2, lw['q_norm'], lw['k_norm'], nb, t)
        o_d, conv_out, h_out = _lru(pd, conv_pad, lru0[:, None, :], lw['lru'], nb, t)
    k_win = k_win.reshape(nb, WINDOW, A_KV_HEADS, A_HD)
    v_win = v_win.reshape(nb, WINDOW, A_KV_HEADS, A_HD)

    o_b, s_rwkv1 = _rwkv(pb, shift0[:, None, :], s_rwkv0, lw['rwkv'], nb, t)
    shift1 = pb.reshape(nb, t, -1)[:, -1]
    o_c, s_gla1 = _gla(pc, s_gla0, *lw['gla'], nb, t)
    conv1 = conv_out[:, SUBLANES - (CONV_W - 1):]
    lru1 = h_out[:, 0]

    x2 = _merge(x2, (o_a, o_b, o_c, o_d), lw['g_mix'], *lw['merge'])
    x2 = _ffn_ple(x2, pe_all.reshape(pe_all.shape[0], n, -1), layer, *lw['ffn2'], *lw['ple'])
    return x2.reshape(nb, t, d), (k_win, v_win, shift1, s_rwkv1, s_gla1, conv1, lru1)


def kernel(x_prompt, x_sample, cache_attn_k, cache_attn_v, state_rwkv_shift, state_rwkv, state_gla, state_lru_conv, state_lru, p_prompt, p_sample, rel_bias_table, g_ffn1, w_ffn1_gate, w_ffn1_up, w_ffn1_down, g_mix, w_in, q_norm, k_norm, attn_sink, rwkv_mu, rwkv_w0, rwkv_w2, rwkv_a0, rwkv_a2, rwkv_g2, rwkv_k_k, rwkv_k_a, rwkv_r_k, rwkv_ln_g, rwkv_ln_b, gla_g2, gla_gb, gla_norm, lru_conv_w, lru_conv_b, lru_wa, lru_ba, lru_wx, lru_bx, lru_lambda, w_merge_gate, w_branch, w_out, g_ffn2, w_ffn2_gate, w_ffn2_up, w_ffn2_down, g_ple, w_ple_gate, w_ple_proj):
    W = dict(g_ffn1=g_ffn1, w_ffn1_gate=w_ffn1_gate, w_ffn1_up=w_ffn1_up, w_ffn1_down=w_ffn1_down,
             g_mix=g_mix, w_in=w_in, q_norm=q_norm, k_norm=k_norm, attn_sink=attn_sink,
             rwkv_mu=rwkv_mu, rwkv_w0=rwkv_w0, rwkv_w2=rwkv_w2, rwkv_a0=rwkv_a0, rwkv_a2=rwkv_a2,
             rwkv_g2=rwkv_g2, rwkv_k_k=rwkv_k_k, rwkv_k_a=rwkv_k_a, rwkv_r_k=rwkv_r_k,
             rwkv_ln_g=rwkv_ln_g, rwkv_ln_b=rwkv_ln_b, gla_g2=gla_g2, gla_gb=gla_gb,
             gla_norm=gla_norm, lru_conv_w=lru_conv_w, lru_conv_b=lru_conv_b, lru_wa=lru_wa,
             lru_ba=lru_ba, lru_wx=lru_wx, lru_bx=lru_bx, lru_lambda=lru_lambda,
             w_merge_gate=w_merge_gate, w_branch=w_branch, w_out=w_out, g_ffn2=g_ffn2,
             w_ffn2_gate=w_ffn2_gate, w_ffn2_up=w_ffn2_up, w_ffn2_down=w_ffn2_down,
             g_ple=g_ple, w_ple_gate=w_ple_gate, w_ple_proj=w_ple_proj)
    depth = w_in.shape[0]
    dt = x_prompt.dtype
    bp, tp = x_prompt.shape[:2]
    ts = x_sample.shape[1]
    b_cols = state_rwkv_shift.shape[-1]
    bias_p = _rel_bias(rel_bias_table, CHUNK, WINDOW + CHUNK)
    bias_s = _rel_bias(rel_bias_table, ts, WINDOW + ts)
    yp, ys = x_prompt, x_sample
    st_p, st_s = [], []
    for i in range(depth):
        lw = _layer_weights(i, W)
        zero_st = (None, None,
                   jnp.zeros((bp, b_cols), dt),
                   jnp.zeros((bp, B_HEADS, B_HD, B_HD), dt),
                   jnp.zeros((bp, C_HEADS, C_DK, C_DV), dt),
                   jnp.zeros((bp, CONV_W - 1, BRANCH_W), dt),
                   jnp.zeros((bp, BRANCH_W), dt))
        yp, sp = _trunk_layer(yp, p_prompt, i, zero_st, lw, bias_p, True)
        cache_st = (cache_attn_k[i], cache_attn_v[i], state_rwkv_shift[i], state_rwkv[i],
                    state_gla[i], state_lru_conv[i], state_lru[i])
        ys, ss = _trunk_layer(ys, p_sample, i, cache_st, lw, bias_s, False)
        st_p.append(sp)
        st_s.append(ss)
    stack = lambda states, j: jnp.stack([s[j] for s in states])
    return (yp, ys) + tuple(stack(st_p, j) for j in range(7)) + tuple(stack(st_s, j) for j in range(7))
```
